```python
import math
import jax, jax.numpy as jnp
from jax import lax
import numpy as np

D_MODEL = 2048
BATCH = 8
SEQ = 4096
DEPTH = 2

POOL_WINDOWS = (2, 4, 8, 16)
N_POOL_GROUPS = len(POOL_WINDOWS)
POOL_GROUP_DIM = D_MODEL // N_POOL_GROUPS
HEAD_DIM = 128
N_HEADS = D_MODEL // HEAD_DIM
DILATED_BRANCHES = ((128, 1), (512, 4), (2048, 16))
ATTN_BLOCK = 128
D_FF = 128 * ((8 * D_MODEL // 3 + 127) // 128)
CONV_WIDTH = 3
N_A_LAYERS = DEPTH // 2
N_B_LAYERS = DEPTH - N_A_LAYERS
DEEPNORM_ALPHA = (2.0 * DEPTH) ** 0.25
DEEPNORM_BETA = (8.0 * DEPTH) ** -0.25
LN_EPS = 1e-5
NEG_INF = -1e30

kernel_name = "yoco_pool_dilated_attn_convffn_deepnorm"


def layer_norm(x, g, b):
    xf = x.astype(jnp.float32)
    mu = jnp.mean(xf, axis=-1, keepdims=True)
    var = jnp.mean(jnp.square(xf - mu), axis=-1, keepdims=True)
    y = (xf - mu) * lax.rsqrt(var + LN_EPS) * g.astype(jnp.float32) + b.astype(jnp.float32)
    return y.astype(x.dtype)


def pool_mixer(h, w_in, w_grp, scale, w_out):
    B_, S, D = h.shape
    p = (h @ w_in).reshape(B_, S, N_POOL_GROUPS, POOL_GROUP_DIM).astype(jnp.float32)
    cs = jnp.cumsum(p, axis=1)
    t = jnp.arange(S)
    outs = []
    for g, w in enumerate(POOL_WINDOWS):
        c = cs[:, :, g]
        lag = jnp.pad(c[:, :S - w], ((0, 0), (w, 0), (0, 0)))
        cnt = jnp.minimum(t + 1, w).astype(jnp.float32)[None, :, None]
        outs.append((c - lag) / cnt - p[:, :, g])
    pooled = jnp.stack(outs, axis=2).astype(h.dtype)
    mixed = jnp.einsum('bsgc,gcd->bsgd', pooled, w_grp).reshape(B_, S, D) * scale
    return mixed @ w_out


def _dilated_branch(q, k, v, window, dilation):
    B_, S, H, Dh = q.shape
    L = S // dilation
    wr = window // dilation
    assert wr <= ATTN_BLOCK
    N = B_ * dilation

    def to_res(a):
        return a.reshape(B_, L, dilation, H, Dh).transpose(0, 2, 3, 1, 4).reshape(N, H, L, Dh)

    nb = -(-L // ATTN_BLOCK)
    Lp = nb * ATTN_BLOCK
    qr = jnp.pad(to_res(q), ((0, 0), (0, 0), (0, Lp - L), (0, 0)))
    kr = jnp.pad(to_res(k), ((0, 0), (0, 0), (ATTN_BLOCK, Lp - L), (0, 0)))
    vr = jnp.pad(to_res(v), ((0, 0), (0, 0), (ATTN_BLOCK, Lp - L), (0, 0)))
    qb = qr.reshape(N, H, nb, ATTN_BLOCK, Dh)
    kb = kr.reshape(N, H, nb + 1, ATTN_BLOCK, Dh)
    vb = vr.reshape(N, H, nb + 1, ATTN_BLOCK, Dh)
    kw = jnp.concatenate([kb[:, :, :-1], kb[:, :, 1:]], axis=3)
    vw = jnp.concatenate([vb[:, :, :-1], vb[:, :, 1:]], axis=3)

    s = jnp.einsum('nhbqd,nhbkd->nhbqk', qb, kw).astype(jnp.float32) * (1.0 / math.sqrt(Dh))
    qi = jnp.arange(ATTN_BLOCK)[:, None]
    kj = jnp.arange(2 * ATTN_BLOCK)[None, :]
    dist = ATTN_BLOCK + qi - kj
    band = (dist >= 0) & (dist <= wr)
    key_pos = jnp.arange(nb)[:, None, None] * ATTN_BLOCK - ATTN_BLOCK + kj[None]
    mask = band[None] & (key_pos >= 0)
    s = jnp.where(mask, s, NEG_INF)
    m = jnp.max(s, axis=-1, keepdims=True)
    pr = jnp.exp(s - m)
    den = jnp.sum(pr, axis=-1)
    o = jnp.einsum('nhbqk,nhbkd->nhbqd', pr, vw.astype(jnp.float32)) / den[..., None]
    lse = m[..., 0] + jnp.log(den)

    o = o.reshape(N, H, Lp, Dh)[:, :, :L].reshape(B_, dilation, H, L, Dh)
    o = o.transpose(0, 3, 1, 2, 4).reshape(B_, S, H, Dh)
    lse = lse.reshape(N, H, Lp)[:, :, :L].reshape(B_, dilation, H, L)
    lse = lse.transpose(0, 3, 1, 2).reshape(B_, S, H)
    return o, lse


def dilated_attention(h, k, v, w_q, w_o):
    B_, S, _ = h.shape
    q = (h @ w_q).reshape(B_, S, N_HEADS, HEAD_DIM)
    outs, lses = [], []
    for window, dil in DILATED_BRANCHES:
        o, lse = _dilated_branch(q, k, v, window, dil)
        outs.append(o)
        lses.append(lse)
    wts = jax.nn.softmax(jnp.stack(lses, axis=0), axis=0)
    o = jnp.sum(wts[..., None] * jnp.stack(outs, axis=0), axis=0)
    return o.reshape(B_, S, D_MODEL).astype(h.dtype) @ w_o


def conv_ffn(h, w_up, conv_w, conv_b, w_down):
    S = h.shape[1]
    u = h @ w_up
    up = jnp.pad(u, ((0, 0), (CONV_WIDTH - 1, 0), (0, 0)))
    c = conv_b + up[:, 0:S] * conv_w[0]
    for j in range(1, CONV_WIDTH):
        c = c + up[:, j:j + S] * conv_w[j]
    gate, val = jnp.split(c, 2, axis=-1)
    return (jax.nn.silu(gate) * val) @ w_down


def _fwd_setup_inputs(seed: int = 0) -> dict:
    key = jax.random.key(seed)
    ks = jax.random.split(key, 20)
    f32 = jnp.float32
    D, G, C, F = D_MODEL, N_POOL_GROUPS, POOL_GROUP_DIM, D_FF

    def nrm(k, shape, scale):
        return jax.random.normal(k, shape, f32) * scale

    return {
        "x": nrm(ks[0], (BATCH, SEQ, D), 1.0),
        "pool_w_in": nrm(ks[1], (N_A_LAYERS, D, D), D ** -0.5),
        "pool_w_grp": nrm(ks[2], (N_A_LAYERS, G, C, C), C ** -0.5),
        "pool_scale": 1.0 + nrm(ks[3], (N_A_LAYERS, D), 0.1),
        "pool_w_out": nrm(ks[4], (N_A_LAYERS, D, D), D ** -0.5 * DEEPNORM_BETA),
        "attn_w_q": nrm(ks[5], (N_B_LAYERS, D, D), D ** -0.5),
        "attn_w_o": nrm(ks[6], (N_B_LAYERS, D, D), D ** -0.5 * DEEPNORM_BETA),
        "shared_w_k": nrm(ks[7], (D, D), D ** -0.5),
        "shared_w_v": nrm(ks[8], (D, D), D ** -0.5 * DEEPNORM_BETA),
        "ffn_w_up": nrm(ks[9], (DEPTH, D, 2 * F), D ** -0.5 * DEEPNORM_BETA),
        "ffn_conv_w": nrm(ks[10], (DEPTH, CONV_WIDTH, 2 * F), CONV_WIDTH ** -0.5),
        "ffn_conv_b": nrm(ks[11], (DEPTH, 2 * F), 0.02),
        "ffn_w_down": nrm(ks[12], (DEPTH, F, D), F ** -0.5 * DEEPNORM_BETA),
        "ln1_g": 1.0 + nrm(ks[13], (DEPTH, D), 0.02),
        "ln1_b": nrm(ks[14], (DEPTH, D), 0.02),
        "ln2_g": 1.0 + nrm(ks[15], (DEPTH, D), 0.02),
        "ln2_b": nrm(ks[16], (DEPTH, D), 0.02),
    }


def _fwd_reference(x, pool_w_in, pool_w_grp, pool_scale, pool_w_out, attn_w_q, attn_w_o,
              shared_w_k, shared_w_v, ffn_w_up, ffn_conv_w, ffn_conv_b, ffn_w_down,
              ln1_g, ln1_b, ln2_g, ln2_b):
    B_, S, _ = x.shape
    h = x
    k_shared = None
    v_shared = None
    for i in range(DEPTH):
        if i < N_A_LAYERS:
            mix = pool_mixer(h, pool_w_in[i], pool_w_grp[i], pool_scale[i], pool_w_out[i])
        else:
            if i == N_A_LAYERS:
                k_shared = (h @ shared_w_k).reshape(B_, S, N_HEADS, HEAD_DIM)
                v_shared = (h @ shared_w_v).reshape(B_, S, N_HEADS, HEAD_DIM)
            j = i - N_A_LAYERS
            mix = dilated_attention(h, k_shared, v_shared, attn_w_q[j], attn_w_o[j])
        h = layer_norm(DEEPNORM_ALPHA * h + mix, ln1_g[i], ln1_b[i])
        ff = conv_ffn(h, ffn_w_up[i], ffn_conv_w[i], ffn_conv_b[i], ffn_w_down[i])
        h = layer_norm(DEEPNORM_ALPHA * h + ff, ln2_g[i], ln2_b[i])
    return h


import jax as _jax
import jax.numpy as _jnp

TWIN_FORMAT = 'train_step'
FWD_PARAMS = ['x', 'pool_w_in', 'pool_w_grp', 'pool_scale', 'pool_w_out', 'attn_w_q', 'attn_w_o', 'shared_w_k', 'shared_w_v', 'ffn_w_up', 'ffn_conv_w', 'ffn_conv_b', 'ffn_w_down', 'ln1_g', 'ln1_b', 'ln2_g', 'ln2_b']
TWIN_WEIGHTS = ['pool_w_in', 'pool_w_grp', 'pool_scale', 'pool_w_out', 'attn_w_q', 'attn_w_o', 'shared_w_k', 'shared_w_v', 'ffn_w_up', 'ffn_conv_w', 'ffn_conv_b', 'ffn_w_down', 'ln1_g', 'ln1_b', 'ln2_g', 'ln2_b']
TWIN_DIFF_INPUT = 'x'
TWIN_INPUTS = ['x', 'pool_w_in', 'pool_w_grp', 'pool_scale', 'pool_w_out', 'attn_w_q', 'attn_w_o', 'shared_w_k', 'shared_w_v', 'ffn_w_up', 'ffn_conv_w', 'ffn_conv_b', 'ffn_w_down', 'ln1_g', 'ln1_b', 'ln2_g', 'ln2_b', 'loss_target', 'm_pool_w_in', 'm_pool_w_grp', 'm_pool_scale', 'm_pool_w_out', 'm_attn_w_q', 'm_attn_w_o', 'm_shared_w_k', 'm_shared_w_v', 'm_ffn_w_up', 'm_ffn_conv_w', 'm_ffn_conv_b', 'm_ffn_w_down', 'm_ln1_g', 'm_ln1_b', 'm_ln2_g', 'm_ln2_b', 'v_pool_w_in', 'v_pool_w_grp', 'v_pool_scale', 'v_pool_w_out', 'v_attn_w_q', 'v_attn_w_o', 'v_shared_w_k', 'v_shared_w_v', 'v_ffn_w_up', 'v_ffn_conv_w', 'v_ffn_conv_b', 'v_ffn_w_down', 'v_ln1_g', 'v_ln1_b', 'v_ln2_g', 'v_ln2_b']
TWIN_OUTPUTS = ['loss', 'grad_x', 'grad_pool_w_in', 'grad_pool_w_grp', 'grad_pool_scale', 'grad_pool_w_out', 'grad_attn_w_q', 'grad_attn_w_o', 'grad_shared_w_k', 'grad_shared_w_v', 'grad_ffn_w_up', 'grad_ffn_conv_w', 'grad_ffn_conv_b', 'grad_ffn_w_down', 'grad_ln1_g', 'grad_ln1_b', 'grad_ln2_g', 'grad_ln2_b', 'delta_pool_w_in', 'delta_pool_w_grp', 'delta_pool_scale', 'delta_pool_w_out', 'delta_attn_w_q', 'delta_attn_w_o', 'delta_shared_w_k', 'delta_shared_w_v', 'delta_ffn_w_up', 'delta_ffn_conv_w', 'delta_ffn_conv_b', 'delta_ffn_w_down', 'delta_ln1_g', 'delta_ln1_b', 'delta_ln2_g', 'delta_ln2_b', 'new_m_pool_w_in', 'new_m_pool_w_grp', 'new_m_pool_scale', 'new_m_pool_w_out', 'new_m_attn_w_q', 'new_m_attn_w_o', 'new_m_shared_w_k', 'new_m_shared_w_v', 'new_m_ffn_w_up', 'new_m_ffn_conv_w', 'new_m_ffn_conv_b', 'new_m_ffn_w_down', 'new_m_ln1_g', 'new_m_ln1_b', 'new_m_ln2_g', 'new_m_ln2_b', 'new_v_pool_w_in', 'new_v_pool_w_grp', 'new_v_pool_scale', 'new_v_pool_w_out', 'new_v_attn_w_q', 'new_v_attn_w_o', 'new_v_shared_w_k', 'new_v_shared_w_v', 'new_v_ffn_w_up', 'new_v_ffn_conv_w', 'new_v_ffn_conv_b', 'new_v_ffn_w_down', 'new_v_ln1_g', 'new_v_ln1_b', 'new_v_ln2_g', 'new_v_ln2_b']
TWIN_LEAF_KINDS = {'loss': 'loss', 'grad_x': 'grad_x', 'grad_pool_w_in': 'grad_w', 'grad_pool_w_grp': 'grad_w', 'grad_pool_scale': 'grad_w', 'grad_pool_w_out': 'grad_w', 'grad_attn_w_q': 'grad_w', 'grad_attn_w_o': 'grad_w', 'grad_shared_w_k': 'grad_w', 'grad_shared_w_v': 'grad_w', 'grad_ffn_w_up': 'grad_w', 'grad_ffn_conv_w': 'grad_w', 'grad_ffn_conv_b': 'grad_w', 'grad_ffn_w_down': 'grad_w', 'grad_ln1_g': 'grad_w', 'grad_ln1_b': 'grad_w', 'grad_ln2_g': 'grad_w', 'grad_ln2_b': 'grad_w', 'delta_pool_w_in': 'delta_w', 'delta_pool_w_grp': 'delta_w', 'delta_pool_scale': 'delta_w', 'delta_pool_w_out': 'delta_w', 'delta_attn_w_q': 'delta_w', 'delta_attn_w_o': 'delta_w', 'delta_shared_w_k': 'delta_w', 'delta_shared_w_v': 'delta_w', 'delta_ffn_w_up': 'delta_w', 'delta_ffn_conv_w': 'delta_w', 'delta_ffn_conv_b': 'delta_w', 'delta_ffn_w_down': 'delta_w', 'delta_ln1_g': 'delta_w', 'delta_ln1_b': 'delta_w', 'delta_ln2_g': 'delta_w', 'delta_ln2_b': 'delta_w', 'new_m_pool_w_in': 'new_m', 'new_m_pool_w_grp': 'new_m', 'new_m_pool_scale': 'new_m', 'new_m_pool_w_out': 'new_m', 'new_m_attn_w_q': 'new_m', 'new_m_attn_w_o': 'new_m', 'new_m_shared_w_k': 'new_m', 'new_m_shared_w_v': 'new_m', 'new_m_ffn_w_up': 'new_m', 'new_m_ffn_conv_w': 'new_m', 'new_m_ffn_conv_b': 'new_m', 'new_m_ffn_w_down': 'new_m', 'new_m_ln1_g': 'new_m', 'new_m_ln1_b': 'new_m', 'new_m_ln2_g': 'new_m', 'new_m_ln2_b': 'new_m', 'new_v_pool_w_in': 'new_v', 'new_v_pool_w_grp': 'new_v', 'new_v_pool_scale': 'new_v', 'new_v_pool_w_out': 'new_v', 'new_v_attn_w_q': 'new_v', 'new_v_attn_w_o': 'new_v', 'new_v_shared_w_k': 'new_v', 'new_v_shared_w_v': 'new_v', 'new_v_ffn_w_up': 'new_v', 'new_v_ffn_conv_w': 'new_v', 'new_v_ffn_conv_b': 'new_v', 'new_v_ffn_w_down': 'new_v', 'new_v_ln1_g': 'new_v', 'new_v_ln1_b': 'new_v', 'new_v_ln2_g': 'new_v', 'new_v_ln2_b': 'new_v'}


def _forward(args):
    return _fwd_reference(*[args[k] for k in FWD_PARAMS])


def _output_shape():
    def fwd():
        inp = _fwd_setup_inputs(0)
        return _fwd_reference(*[inp[k] for k in FWD_PARAMS])
    out = _jax.eval_shape(fwd)
    return out.shape, out.dtype

N_MICROBATCH = 1
ADAM_LR = 0.001
ADAM_B1 = 0.9
ADAM_B2 = 0.999
ADAM_EPS = 1e-08
ADAM_WD = 0.01
ADAM_STEP = 10
PER_EXAMPLE_BATCH_AXIS = {'x': 0, 'loss_target': 0}
SHARED_INPUTS = []
_WEIGHT_DTYPES = {'pool_w_in': _jnp.float32, 'pool_w_grp': _jnp.float32, 'pool_scale': _jnp.float32, 'pool_w_out': _jnp.float32, 'attn_w_q': _jnp.float32, 'attn_w_o': _jnp.float32, 'shared_w_k': _jnp.float32, 'shared_w_v': _jnp.float32, 'ffn_w_up': _jnp.float32, 'ffn_conv_w': _jnp.float32, 'ffn_conv_b': _jnp.float32, 'ffn_w_down': _jnp.float32, 'ln1_g': _jnp.float32, 'ln1_b': _jnp.float32, 'ln2_g': _jnp.float32, 'ln2_b': _jnp.float32}
MOMENT_SCALE = {'pool_w_in': 2.643179e-02, 'pool_w_grp': 2.648655e-02, 'pool_scale': 2.650393e-02, 'pool_w_out': 5.292720e-02, 'attn_w_q': 2.937266e-03, 'attn_w_o': 7.340031e-03, 'shared_w_k': 2.951494e-03, 'shared_w_v': 7.319869e-03, 'ffn_w_up': 5.384197e-03, 'ffn_conv_w': 2.670123e-03, 'ffn_conv_b': 5.883793e-03, 'ffn_w_down': 8.743170e-03, 'ln1_g': 5.988117e-01, 'ln1_b': 2.794453e-01, 'ln2_g': 1.134195e+01, 'ln2_b': 4.605364e-01}


def _to_microbatches(a, axis):
    t = _jnp.moveaxis(a, axis, 0)
    t = t.reshape((N_MICROBATCH, t.shape[0] // N_MICROBATCH) + t.shape[1:])
    return _jnp.moveaxis(t, 1, axis + 1)


def setup_inputs(seed: int = 0) -> dict:
    inp = _fwd_setup_inputs(seed)
    key = _jax.random.fold_in(_jax.random.key(seed), 7919)
    shape, _ = _output_shape()
    out = dict(inp)
    out["loss_target"] = _jax.random.normal(_jax.random.fold_in(key, 0), shape, _jnp.float32)
    for i, name in enumerate(TWIN_WEIGHTS):
        w = inp[name].astype(_jnp.float32)
        if MOMENT_SCALE is None:
            s = _jnp.sqrt(_jnp.mean(_jnp.square(w)) + 1e-30)
        else:
            s = MOMENT_SCALE[name]
        km, kv = _jax.random.split(_jax.random.fold_in(key, i + 1))
        out[name] = w
        out["m_" + name] = s * _jax.random.normal(km, w.shape, _jnp.float32)
        out["v_" + name] = (s * s) * _jax.random.uniform(kv, w.shape, _jnp.float32, 0.5, 1.5)
    if N_MICROBATCH > 1:
        for name, axis in PER_EXAMPLE_BATCH_AXIS.items():
            out[name] = _to_microbatches(out[name], axis)
    return {'x': out['x'], 'pool_w_in': out['pool_w_in'], 'pool_w_grp': out['pool_w_grp'], 'pool_scale': out['pool_scale'], 'pool_w_out': out['pool_w_out'], 'attn_w_q': out['attn_w_q'], 'attn_w_o': out['attn_w_o'], 'shared_w_k': out['shared_w_k'], 'shared_w_v': out['shared_w_v'], 'ffn_w_up': out['ffn_w_up'], 'ffn_conv_w': out['ffn_conv_w'], 'ffn_conv_b': out['ffn_conv_b'], 'ffn_w_down': out['ffn_w_down'], 'ln1_g': out['ln1_g'], 'ln1_b': out['ln1_b'], 'ln2_g': out['ln2_g'], 'ln2_b': out['ln2_b'], 'loss_target': out['loss_target'], 'm_pool_w_in': out['m_pool_w_in'], 'm_pool_w_grp': out['m_pool_w_grp'], 'm_pool_scale': out['m_pool_scale'], 'm_pool_w_out': out['m_pool_w_out'], 'm_attn_w_q': out['m_attn_w_q'], 'm_attn_w_o': out['m_attn_w_o'], 'm_shared_w_k': out['m_shared_w_k'], 'm_shared_w_v': out['m_shared_w_v'], 'm_ffn_w_up': out['m_ffn_w_up'], 'm_ffn_conv_w': out['m_ffn_conv_w'], 'm_ffn_conv_b': out['m_ffn_conv_b'], 'm_ffn_w_down': out['m_ffn_w_down'], 'm_ln1_g': out['m_ln1_g'], 'm_ln1_b': out['m_ln1_b'], 'm_ln2_g': out['m_ln2_g'], 'm_ln2_b': out['m_ln2_b'], 'v_pool_w_in': out['v_pool_w_in'], 'v_pool_w_grp': out['v_pool_w_grp'], 'v_pool_scale': out['v_pool_scale'], 'v_pool_w_out': out['v_pool_w_out'], 'v_attn_w_q': out['v_attn_w_q'], 'v_attn_w_o': out['v_attn_w_o'], 'v_shared_w_k': out['v_shared_w_k'], 'v_shared_w_v': out['v_shared_w_v'], 'v_ffn_w_up': out['v_ffn_w_up'], 'v_ffn_conv_w': out['v_ffn_conv_w'], 'v_ffn_conv_b': out['v_ffn_conv_b'], 'v_ffn_w_down': out['v_ffn_w_down'], 'v_ln1_g': out['v_ln1_g'], 'v_ln1_b': out['v_ln1_b'], 'v_ln2_g': out['v_ln2_g'], 'v_ln2_b': out['v_ln2_b']}


def _loss(weights, diff, rest, loss_target):
    with _jax.named_scope("forward"):
        args = {**rest, TWIN_DIFF_INPUT: diff, **{k: w.astype(_WEIGHT_DTYPES[k]) for k, w in weights.items()}}
        y = _forward(args)
    with _jax.named_scope("loss_head"):
        err = _jnp.square(y.astype(_jnp.float32) - loss_target)
        return 0.5 * _jnp.sum(_jnp.mean(err, axis=-1)) if err.ndim else 0.5 * err


def _adamw(w, g, m, v):
    m = ADAM_B1 * m + (1.0 - ADAM_B1) * g
    v = ADAM_B2 * v + (1.0 - ADAM_B2) * _jnp.square(g)
    m_hat = m / (1.0 - ADAM_B1 ** ADAM_STEP)
    v_hat = v / (1.0 - ADAM_B2 ** ADAM_STEP)
    delta = -ADAM_LR * (m_hat / (_jnp.sqrt(v_hat) + ADAM_EPS) + ADAM_WD * w)
    return delta, m, v


def reference(x, pool_w_in, pool_w_grp, pool_scale, pool_w_out, attn_w_q, attn_w_o, shared_w_k, shared_w_v, ffn_w_up, ffn_conv_w, ffn_conv_b, ffn_w_down, ln1_g, ln1_b, ln2_g, ln2_b, loss_target, m_pool_w_in, m_pool_w_grp, m_pool_scale, m_pool_w_out, m_attn_w_q, m_attn_w_o, m_shared_w_k, m_shared_w_v, m_ffn_w_up, m_ffn_conv_w, m_ffn_conv_b, m_ffn_w_down, m_ln1_g, m_ln1_b, m_ln2_g, m_ln2_b, v_pool_w_in, v_pool_w_grp, v_pool_scale, v_pool_w_out, v_attn_w_q, v_attn_w_o, v_shared_w_k, v_shared_w_v, v_ffn_w_up, v_ffn_conv_w, v_ffn_conv_b, v_ffn_w_down, v_ln1_g, v_ln1_b, v_ln2_g, v_ln2_b):
    given = dict(x=x, pool_w_in=pool_w_in, pool_w_grp=pool_w_grp, pool_scale=pool_scale, pool_w_out=pool_w_out, attn_w_q=attn_w_q, attn_w_o=attn_w_o, shared_w_k=shared_w_k, shared_w_v=shared_w_v, ffn_w_up=ffn_w_up, ffn_conv_w=ffn_conv_w, ffn_conv_b=ffn_conv_b, ffn_w_down=ffn_w_down, ln1_g=ln1_g, ln1_b=ln1_b, ln2_g=ln2_g, ln2_b=ln2_b, loss_target=loss_target, m_pool_w_in=m_pool_w_in, m_pool_w_grp=m_pool_w_grp, m_pool_scale=m_pool_scale, m_pool_w_out=m_pool_w_out, m_attn_w_q=m_attn_w_q, m_attn_w_o=m_attn_w_o, m_shared_w_k=m_shared_w_k, m_shared_w_v=m_shared_w_v, m_ffn_w_up=m_ffn_w_up, m_ffn_conv_w=m_ffn_conv_w, m_ffn_conv_b=m_ffn_conv_b, m_ffn_w_down=m_ffn_w_down, m_ln1_g=m_ln1_g, m_ln1_b=m_ln1_b, m_ln2_g=m_ln2_g, m_ln2_b=m_ln2_b, v_pool_w_in=v_pool_w_in, v_pool_w_grp=v_pool_w_grp, v_pool_scale=v_pool_scale, v_pool_w_out=v_pool_w_out, v_attn_w_q=v_attn_w_q, v_attn_w_o=v_attn_w_o, v_shared_w_k=v_shared_w_k, v_shared_w_v=v_shared_w_v, v_ffn_w_up=v_ffn_w_up, v_ffn_conv_w=v_ffn_conv_w, v_ffn_conv_b=v_ffn_conv_b, v_ffn_w_down=v_ffn_w_down, v_ln1_g=v_ln1_g, v_ln1_b=v_ln1_b, v_ln2_g=v_ln2_g, v_ln2_b=v_ln2_b)
    weights = {n: given[n] for n in TWIN_WEIGHTS}
    shared = {n: given[n] for n in SHARED_INPUTS}
    per_example = {n: given[n] for n in ['x']}
    grad_fn = _jax.value_and_grad(_loss, argnums=(0, 1))

    def one_microbatch(ex, loss_target):
        ex = dict(ex)
        diff = ex.pop(TWIN_DIFF_INPUT)
        return grad_fn(weights, diff, {**shared, **ex}, loss_target)

    if N_MICROBATCH == 1:
        loss, (grad_w, grad_x) = one_microbatch(per_example, given["loss_target"])
    else:
        def body(carry, xs):
            loss_sum, grad_sum = carry
            l_k, (gw_k, gx_k) = one_microbatch(xs[0], xs[1])
            with _jax.named_scope("update"):
                return (loss_sum + l_k, _jax.tree.map(_jnp.add, grad_sum, gw_k)), gx_k

        init = (_jnp.zeros((), _jnp.float32), _jax.tree.map(_jnp.zeros_like, weights))
        (loss, grad_w), grad_x = _jax.lax.scan(body, init, (per_example, given["loss_target"]))
    with _jax.named_scope("update"):
        delta_w, new_m, new_v = {}, {}, {}
        for n in TWIN_WEIGHTS:
            delta_w[n], new_m[n], new_v[n] = _adamw(weights[n], grad_w[n], given["m_" + n], given["v_" + n])
    return (loss, grad_x, *[grad_w[n] for n in TWIN_WEIGHTS], *[delta_w[n] for n in TWIN_WEIGHTS],
            *[new_m[n] for n in TWIN_WEIGHTS], *[new_v[n] for n in TWIN_WEIGHTS])
```

```python
import functools
import math

import jax
import jax.numpy as jnp
from jax import lax
from jax.experimental import pallas as pl
from jax.experimental.pallas import tpu as pltpu

F32 = jnp.float32
BF16 = jnp.bfloat16

LANES = 128
HEAD_DIM = 128
ATTN_BLOCK = 128
DILATIONS = (1, 4, 16)
POOL_WINDOWS = (2, 4, 8, 16)
POOL_HALO = 16
CONV_HALO = 8
DEPTH = 2
ALPHA = (2.0 * DEPTH) ** 0.25
LN_EPS = 1e-5
NEG_INF = -1e30
ADAM_LR = 0.001
ADAM_B1 = 0.9
ADAM_B2 = 0.999
ADAM_EPS = 1e-08
ADAM_WD = 0.01
ADAM_STEP = 10
N_CHIPS = 4
VMEM_LIMIT = 56 * 1024 * 1024
ANY = pl.BlockSpec(memory_space=pl.ANY)
MESH = pl.DeviceIdType.MESH

IW_IN, IW_OUT, IW_Q, IW_K, IW_V, IW_O = range(6)


def _cparams(n_grid):
    return pltpu.CompilerParams(dimension_semantics=("arbitrary",) * n_grid, vmem_limit_bytes=VMEM_LIMIT)


def _tile(dim, pref, align=LANES):
    if dim <= pref:
        return dim
    t = (pref // align) * align
    while t >= align:
        if dim % t == 0:
            return t
        t -= align
    return dim


def _perm(q):
    return (q % 2) * 2 + q // 2


_DIMS = {"nn": (((1,), (0,)), ((), ())), "nt": (((1,), (1,)), ((), ())), "tn": (((0,), (0,)), ((), ()))}


def _mm(a, b, *, mode, name, out_dtype, n_q, tm, tn, tk, qa, qb, qo=lambda q: q, n_qr=1, out_q=None, into=None):
    if mode == "nn":
        m, kdim, n = a.shape[1], a.shape[2], b.shape[2]
    elif mode == "nt":
        m, kdim, n = a.shape[1], a.shape[2], b.shape[1]
    else:
        kdim, m, n = a.shape[1], a.shape[2], b.shape[2]
    assert m % tm == 0 and n % tn == 0 and kdim % tk == 0, (name, m, n, kdim, tm, tn, tk)
    kr_n = kdim // tk
    nr = n_qr * kr_n
    out_q = n_q if out_q is None else out_q

    def split(r):
        return (r // kr_n, r % kr_n) if n_qr > 1 else (0, r)

    if mode == "tn":
        a_spec = pl.BlockSpec((None, tk, tm), lambda q, i, j, r: (qa(q, split(r)[0]), split(r)[1], i))
    else:
        a_spec = pl.BlockSpec((None, tm, tk), lambda q, i, j, r: (qa(q, split(r)[0]), i, split(r)[1]))
    if mode == "nt":
        b_spec = pl.BlockSpec((None, tn, tk), lambda q, i, j, r: (qb(q, split(r)[0]), j, split(r)[1]))
    else:
        b_spec = pl.BlockSpec((None, tk, tn), lambda q, i, j, r: (qb(q, split(r)[0]), split(r)[1], j))
    o_spec = pl.BlockSpec((None, tm, tn), lambda q, i, j, r: (qo(q), i, j))
    dims = _DIMS[mode]

    def body(*refs):
        a_ref, b_ref = refs[0], refs[1]
        o_ref = refs[3] if into is not None else refs[2]
        lhs, rhs = a_ref[...], b_ref[...]
        if lhs.dtype != BF16:
            lhs = lhs.astype(BF16)
        if rhs.dtype != BF16:
            rhs = rhs.astype(BF16)
        part = lax.dot_general(lhs, rhs, dims, preferred_element_type=F32)
        if nr == 1:
            o_ref[...] = part.astype(o_ref.dtype)
        else:
            acc_ref = refs[-1]
            r = pl.program_id(3)

            @pl.when(r == 0)
            def _():
                acc_ref[...] = part

            @pl.when(r > 0)
            def _():
                acc_ref[...] += part

            @pl.when(r == nr - 1)
            def _():
                o_ref[...] = acc_ref[...].astype(o_ref.dtype)

    in_specs = [a_spec, b_spec]
    args = [a, b]
    aliases = {}
    if into is not None:
        assert into.shape == (out_q, m, n) and into.dtype == out_dtype
        in_specs.append(ANY)
        args.append(into)
        aliases = {2: 0}
    return pl.pallas_call(
        body, name=name, grid=(n_q, m // tm, n // tn, nr),
        in_specs=in_specs, out_specs=o_spec,
        out_shape=jax.ShapeDtypeStruct((out_q, m, n), out_dtype),
        scratch_shapes=[pltpu.VMEM((tm, tn), F32)] if nr > 1 else [],
        input_output_aliases=aliases,
        compiler_params=_cparams(4),
    )(*args)


def _q0(q, qr):
    return 0


def _qq(q, qr):
    return q


def _qr(q, qr):
    return qr


def _mm_dd(a3, w6, widx, *, mode, name, out_dtype, n_q=1, n_qr=1, qa=_q0):
    d = w6.shape[1]
    f32_in = a3.dtype != BF16
    tm = _tile(a3.shape[1] if mode != "tn" else a3.shape[2], 512 if f32_in else 1024)
    if n_qr > 1:
        qb = lambda q, qr: widx + qr
    elif n_q > 1:
        qb = lambda q, qr: widx + q
    else:
        qb = lambda q, qr: widx
    return _mm(a3, w6, mode=mode, name=name, out_dtype=out_dtype, n_q=n_q, n_qr=n_qr,
               tm=tm, tn=_tile(d, 1024), tk=d, qa=qa, qb=qb)


def _mm_wgrad(a3, b3, *, name, n_q=1, qa=_q0, qb=_q0):
    s, d = a3.shape[1], a3.shape[2]
    f32_in = a3.dtype != BF16 or b3.dtype != BF16
    return _mm(a3, b3, mode="tn", name=name, out_dtype=BF16, n_q=n_q, tm=_tile(d, 1024), tn=b3.shape[2],
               tk=_tile(s, 512 if f32_in else 1024), qa=qa, qb=qb)


def _grp_mm(a, b, *, mode, name, out_dtype, tm, w=None):
    s = a.shape[0]
    if mode == "tn":
        cg = b.shape[1] // len(POOL_WINDOWS)
        g_n = len(POOL_WINDOWS)
        ts = _tile(s, tm)
        nr = s // ts

        def body(a_ref, b_ref, o_ref, acc_ref):
            r = pl.program_id(1)
            part = lax.dot_general(a_ref[...], b_ref[...], _DIMS["tn"], preferred_element_type=F32)

            @pl.when(r == 0)
            def _():
                acc_ref[...] = part

            @pl.when(r > 0)
            def _():
                acc_ref[...] += part

            @pl.when(r == nr - 1)
            def _():
                o_ref[...] = acc_ref[...].astype(o_ref.dtype)

        return pl.pallas_call(
            body, name=name, grid=(g_n, nr),
            in_specs=[pl.BlockSpec((ts, cg), lambda g, r: (r, g)), pl.BlockSpec((ts, cg), lambda g, r: (r, g))],
            out_specs=pl.BlockSpec((None, cg, cg), lambda g, r: (g, 0, 0)),
            out_shape=jax.ShapeDtypeStruct((g_n, cg, cg), out_dtype),
            scratch_shapes=[pltpu.VMEM((cg, cg), F32)],
            compiler_params=_cparams(2),
        )(a, b)
    g_n, cg = w.shape[0], w.shape[1]
    ts = _tile(s, tm)
    dims = _DIMS[mode]

    def body(a_ref, w_ref, o_ref):
        o_ref[...] = lax.dot_general(a_ref[...], w_ref[...], dims, preferred_element_type=F32).astype(o_ref.dtype)

    return pl.pallas_call(
        body, name=name, grid=(g_n, s // ts),
        in_specs=[pl.BlockSpec((ts, cg), lambda g, i: (i, g)), pl.BlockSpec((None, cg, cg), lambda g, i: (g, 0, 0))],
        out_specs=pl.BlockSpec((ts, cg), lambda g, i: (i, g)),
        out_shape=jax.ShapeDtypeStruct((s, g_n * cg), out_dtype),
        compiler_params=_cparams(2),
    )(a, w)


def _causal_ext(load, r0, rows, halo):
    cur = load(r0, rows)
    prev = load(pl.multiple_of(jnp.maximum(r0 - halo, 0), halo), halo)
    prev = jnp.where(r0 > 0, prev, jnp.zeros_like(prev))
    return jnp.concatenate([prev, cur], axis=0)


def _anti_ext(load, r0, rows, halo, s):
    cur = load(r0, rows)
    nxt = load(pl.multiple_of(jnp.minimum(r0 + rows, s - halo), halo), halo)
    nxt = jnp.where(r0 + rows < s, nxt, jnp.zeros_like(nxt))
    return jnp.concatenate([cur, nxt], axis=0)


def _down(ext, k):
    return pltpu.roll(ext, k, axis=0)


def _up(ext, k):
    return pltpu.roll(ext, ext.shape[0] - k, axis=0)


def _fold8(x):
    return jnp.sum(x.reshape(x.shape[0] // 8, 8, x.shape[1]), axis=0)


def _sigmoid(x):
    return 1.0 / (1.0 + jnp.exp(-x))


def _pool(p, *, backward, name, rows=64):
    s, d = p.shape
    strips_per_group = (d // len(POOL_WINDOWS)) // LANES
    assert strips_per_group * LANES * len(POOL_WINDOWS) == d and s % rows == 0

    def body(p_ref, o_ref):
        g = pl.program_id(0) // strips_per_group
        win = jnp.left_shift(2, g).astype(F32)

        def load(r0, n):
            return p_ref[pl.ds(r0, n), :]

        def pick(levels):
            return jnp.where(g == 0, levels[0], jnp.where(g == 1, levels[1], jnp.where(g == 2, levels[2], levels[3])))

        def chunk(c, carry):
            r0 = pl.multiple_of(c * rows, rows)
            if not backward:
                ext = _causal_ext(load, r0, rows, POOL_HALO)
                levels, acc = [], ext
                for k in (1, 2, 4, 8):
                    acc = acc + _down(acc, k)
                    levels.append(acc)
                t = (r0 + lax.broadcasted_iota(jnp.int32, (rows, LANES), 0)).astype(F32)
                cnt = jnp.minimum(t + 1.0, win)
                out = pick(levels)[POOL_HALO:] / cnt - ext[POOL_HALO:]
            else:
                ext = _anti_ext(load, r0, rows, POOL_HALO, s)
                t = (r0 + lax.broadcasted_iota(jnp.int32, (rows + POOL_HALO, LANES), 0)).astype(F32)
                e = ext / jnp.minimum(t + 1.0, win)
                levels, acc = [], e
                for k in (1, 2, 4, 8):
                    acc = acc + _up(acc, k)
                    levels.append(acc)
                out = pick(levels)[:rows] - ext[:rows]
            o_ref[pl.ds(r0, rows), :] = out.astype(o_ref.dtype)
            return carry

        lax.fori_loop(0, s // rows, chunk, 0)

    return pl.pallas_call(
        body, name=name, grid=(d // LANES,),
        in_specs=[pl.BlockSpec((s, LANES), lambda j: (0, j))],
        out_specs=pl.BlockSpec((s, LANES), lambda j: (0, j)),
        out_shape=jax.ShapeDtypeStruct((s, d), BF16),
        compiler_params=_cparams(1),
    )(p)


def _grp_fwd(pooled, w_grp, scale, *, name):
    s, d = pooled.shape
    g_n, cg = w_grp.shape[0], w_grp.shape[1]
    ts = _tile(s, 1024)

    def body(a_ref, w_ref, sc_ref, mg_ref, mx_ref):
        mg = jnp.dot(a_ref[...], w_ref[...], preferred_element_type=F32)
        mg_ref[...] = mg.astype(BF16)
        mx_ref[...] = (mg * sc_ref[...]).astype(BF16)

    blk = pl.BlockSpec((ts, cg), lambda g, i: (i, g))
    return pl.pallas_call(
        body, name=name, grid=(g_n, s // ts),
        in_specs=[blk, pl.BlockSpec((None, cg, cg), lambda g, i: (g, 0, 0)), pl.BlockSpec((1, cg), lambda g, i: (0, g))],
        out_specs=[blk, blk],
        out_shape=[jax.ShapeDtypeStruct((s, d), BF16)] * 2,
        compiler_params=_cparams(2),
    )(pooled, w_grp, scale)


def _grp_bwd_pre(dmixed, mg, scale, *, name):
    s, d = dmixed.shape
    ts = _tile(s, 256, 16)

    def body(dm_ref, mg_ref, sc_ref, dmg_ref, dsc_ref):
        dm = dm_ref[...]
        dmg_ref[...] = (dm * sc_ref[...]).astype(BF16)
        part = jnp.sum(dm * mg_ref[...].astype(F32), axis=0, keepdims=True)

        @pl.when(pl.program_id(0) == 0)
        def _():
            dsc_ref[...] = part

        @pl.when(pl.program_id(0) > 0)
        def _():
            dsc_ref[...] += part

    blk = pl.BlockSpec((ts, d), lambda i: (i, 0))
    vec = pl.BlockSpec((1, d), lambda i: (0, 0))
    return pl.pallas_call(
        body, name=name, grid=(s // ts,),
        in_specs=[blk, blk, vec], out_specs=[blk, vec],
        out_shape=[jax.ShapeDtypeStruct((s, d), BF16), jax.ShapeDtypeStruct((1, d), F32)],
        compiler_params=_cparams(1),
    )(dmixed, mg, scale)


def _ln_fwd(res, mm, g, b, *, name):
    s, d = res.shape
    ts = _tile(s, 128, 16)

    def body(res_ref, mm_ref, g_ref, b_ref, h_ref, hb_ref, xh_ref, rs_ref):
        z = ALPHA * res_ref[...] + mm_ref[...]
        mu = jnp.mean(z, axis=-1, keepdims=True)
        zc = z - mu
        var = jnp.mean(zc * zc, axis=-1, keepdims=True)
        rstd = lax.rsqrt(var + LN_EPS)
        xhat = zc * rstd
        h = xhat * g_ref[...] + b_ref[...]
        h_ref[...] = h
        hb_ref[...] = h.astype(BF16)
        xh_ref[...] = xhat
        rs_ref[...] = rstd

    blk = pl.BlockSpec((ts, d), lambda i: (i, 0))
    vec = pl.BlockSpec((1, d), lambda i: (0, 0))
    return pl.pallas_call(
        body, name=name, grid=(s // ts,),
        in_specs=[blk, blk, vec, vec],
        out_specs=[blk, blk, blk, pl.BlockSpec((ts, 1), lambda i: (i, 0))],
        out_shape=[jax.ShapeDtypeStruct((s, d), F32), jax.ShapeDtypeStruct((s, d), BF16),
                   jax.ShapeDtypeStruct((s, d), F32), jax.ShapeDtypeStruct((s, 1), F32)],
        compiler_params=_cparams(1),
    )(res, mm, g, b)


def _ln_bwd(dres, dmm, xhat, rstd, g, *, name):
    s, d = dmm.shape
    ts = _tile(s, 128, 16)
    has_res = dres is not None

    def body(*refs):
        if has_res:
            dres_ref, dmm_ref, xh_ref, rs_ref, g_ref, dz_ref, dzb_ref, dg_ref, db_ref = refs
            dh = ALPHA * dres_ref[...] + dmm_ref[...]
        else:
            dmm_ref, xh_ref, rs_ref, g_ref, dz_ref, dzb_ref, dg_ref, db_ref = refs
            dh = dmm_ref[...]
        xhat_ = xh_ref[...]
        dxh = dh * g_ref[...]
        c1 = jnp.mean(dxh, axis=-1, keepdims=True)
        c2 = jnp.mean(dxh * xhat_, axis=-1, keepdims=True)
        dz = rs_ref[...] * (dxh - c1 - xhat_ * c2)
        dz_ref[...] = dz
        dzb_ref[...] = dz.astype(BF16)
        dg_part = jnp.sum(dh * xhat_, axis=0, keepdims=True)
        db_part = jnp.sum(dh, axis=0, keepdims=True)

        @pl.when(pl.program_id(0) == 0)
        def _():
            dg_ref[...] = dg_part
            db_ref[...] = db_part

        @pl.when(pl.program_id(0) > 0)
        def _():
            dg_ref[...] += dg_part
            db_ref[...] += db_part

    blk = pl.BlockSpec((ts, d), lambda i: (i, 0))
    vec = pl.BlockSpec((1, d), lambda i: (0, 0))
    col = pl.BlockSpec((ts, 1), lambda i: (i, 0))
    ins = ([dres] if has_res else []) + [dmm, xhat, rstd, g]
    in_specs = ([blk] if has_res else []) + [blk, blk, col, vec]
    return pl.pallas_call(
        body, name=name, grid=(s // ts,),
        in_specs=in_specs, out_specs=[blk, blk, vec, vec],
        out_shape=[jax.ShapeDtypeStruct((s, d), F32), jax.ShapeDtypeStruct((s, d), BF16),
                   jax.ShapeDtypeStruct((1, d), F32), jax.ShapeDtypeStruct((1, d), F32)],
        compiler_params=_cparams(1),
    )(*ins)


def _loss_head(h, tgt, *, name):
    s, d = h.shape
    ts = _tile(s, 256, 16)

    def body(h_ref, t_ref, dh_ref, loss_ref):
        err = h_ref[...] - t_ref[...]
        dh_ref[...] = err * (1.0 / d)
        part = 0.5 * jnp.sum(jnp.mean(err * err, axis=-1, keepdims=True), axis=0, keepdims=True)

        @pl.when(pl.program_id(0) == 0)
        def _():
            loss_ref[...] = part

        @pl.when(pl.program_id(0) > 0)
        def _():
            loss_ref[...] += part

    blk = pl.BlockSpec((ts, d), lambda i: (i, 0))
    return pl.pallas_call(
        body, name=name, grid=(s // ts,),
        in_specs=[blk, blk], out_specs=[blk, pl.BlockSpec((1, 1), lambda i: (0, 0))],
        out_shape=[jax.ShapeDtypeStruct((s, d), F32), jax.ShapeDtypeStruct((1, 1), F32)],
        compiler_params=_cparams(1),
    )(h, tgt)


def _conv(ext, w, bias):
    c = bias + _down(ext, 2) * w[0:1] + _down(ext, 1) * w[1:2] + ext * w[2:3]
    return c[CONV_HALO:]


def _act_specs(s, n2):
    n_strips = pl.cdiv(n2, LANES)
    u_spec = pl.BlockSpec((None, 2, s, LANES), lambda hh, j: (hh, 0, 0, j))
    cwg = pl.BlockSpec((None, 3, LANES), lambda hh, j: (hh, 0, j))
    cwv = pl.BlockSpec((None, 3, LANES), lambda hh, j: (hh + 2, 0, j))
    cbg = pl.BlockSpec((None, 1, LANES), lambda hh, j: (hh, 0, j))
    cbv = pl.BlockSpec((None, 1, LANES), lambda hh, j: (hh + 2, 0, j))
    return n_strips, u_spec, cwg, cwv, cbg, cbv


def _act_fwd(u, cw, cb, *, name, rows=64):
    _, _, s, n2 = u.shape
    n_strips, u_spec, cwg, cwv, cbg, cbv = _act_specs(s, n2)

    def body(u_ref, wg_ref, wv_ref, bg_ref, bv_ref, a_ref):
        wg, wv, bg, bv = wg_ref[...], wv_ref[...], bg_ref[...], bv_ref[...]

        def chunk(c, carry):
            r0 = pl.multiple_of(c * rows, rows)
            cg = _conv(_causal_ext(lambda r, n: u_ref[0, pl.ds(r, n), :], r0, rows, CONV_HALO), wg, bg)
            cv = _conv(_causal_ext(lambda r, n: u_ref[1, pl.ds(r, n), :], r0, rows, CONV_HALO), wv, bv)
            a_ref[pl.ds(r0, rows), :] = (cg * _sigmoid(cg) * cv).astype(BF16)
            return carry

        lax.fori_loop(0, s // rows, chunk, 0)

    return pl.pallas_call(
        body, name=name, grid=(2, n_strips),
        in_specs=[u_spec, cwg, cwv, cbg, cbv],
        out_specs=pl.BlockSpec((None, s, LANES), lambda hh, j: (hh, 0, j)),
        out_shape=jax.ShapeDtypeStruct((2, s, n2), BF16),
        compiler_params=_cparams(2),
    )(u, cw, cw, cb, cb)


def _act_bwd(u, da, cw, cb, *, name, rows=64):
    _, _, s, n2 = u.shape
    n_strips, u_spec, cwg, cwv, cbg, cbv = _act_specs(s, n2)
    n_chunks = s // rows

    def body(u_ref, da_ref, wg_ref, wv_ref, bg_ref, bv_ref, du_ref, dwg_ref, dwv_ref, dbg_ref, dbv_ref, dg_s, dv_s):
        wg, wv, bg, bv = wg_ref[...], wv_ref[...], bg_ref[...], bv_ref[...]

        def first(c, sums):
            r0 = pl.multiple_of(c * rows, rows)
            eg = _causal_ext(lambda r, n: u_ref[0, pl.ds(r, n), :], r0, rows, CONV_HALO)
            ev = _causal_ext(lambda r, n: u_ref[1, pl.ds(r, n), :], r0, rows, CONV_HALO)
            cg, cv = _conv(eg, wg, bg), _conv(ev, wv, bv)
            sg = _sigmoid(cg)
            dact = da_ref[pl.ds(r0, rows), :]
            dval = dact * (cg * sg)
            dgate = dact * cv * (sg * (1.0 + cg * (1.0 - sg)))
            dg_s[pl.ds(r0, rows), :] = dgate
            dv_s[pl.ds(r0, rows), :] = dval
            new = []
            for dc, ext in ((dgate, eg), (dval, ev)):
                new += [_fold8(dc * _down(ext, 2)[CONV_HALO:]), _fold8(dc * _down(ext, 1)[CONV_HALO:]),
                        _fold8(dc * ext[CONV_HALO:]), _fold8(dc)]
            return tuple(acc + x for acc, x in zip(sums, new))

        zero = jnp.zeros((8, LANES), F32)
        sums = lax.fori_loop(0, n_chunks, first, (zero,) * 8)
        red = [jnp.sum(x, axis=0, keepdims=True) for x in sums]
        dwg_ref[...] = jnp.concatenate(red[0:3], axis=0)
        dbg_ref[...] = red[3]
        dwv_ref[...] = jnp.concatenate(red[4:7], axis=0)
        dbv_ref[...] = red[7]

        def second(c, carry):
            r0 = pl.multiple_of(c * rows, rows)
            for gv, (src, w) in enumerate(((dg_s, wg), (dv_s, wv))):
                ext = _anti_ext(lambda r, n: src[pl.ds(r, n), :], r0, rows, CONV_HALO, s)
                du = ext * w[2:3] + _up(ext, 1) * w[1:2] + _up(ext, 2) * w[0:1]
                du_ref[gv, pl.ds(r0, rows), :] = du[:rows].astype(BF16)
            return carry

        lax.fori_loop(0, n_chunks, second, 0)

    w_out = pl.BlockSpec((None, 3, LANES), lambda hh, j: (hh, 0, j))
    b_out = pl.BlockSpec((None, 1, LANES), lambda hh, j: (hh, 0, j))
    return pl.pallas_call(
        body, name=name, grid=(2, n_strips),
        in_specs=[u_spec, pl.BlockSpec((None, s, LANES), lambda hh, j: (hh, 0, j)), cwg, cwv, cbg, cbv],
        out_specs=[u_spec, w_out, w_out, b_out, b_out],
        out_shape=[jax.ShapeDtypeStruct((2, 2, s, n2), BF16),
                   jax.ShapeDtypeStruct((2, 3, n2), F32), jax.ShapeDtypeStruct((2, 3, n2), F32),
                   jax.ShapeDtypeStruct((2, 1, n2), F32), jax.ShapeDtypeStruct((2, 1, n2), F32)],
        scratch_shapes=[pltpu.VMEM((s, LANES), F32), pltpu.VMEM((s, LANES), F32)],
        compiler_params=_cparams(2),
    )(u, da, cw, cw, cb, cb)


def _dot_nt(a, b):
    return lax.dot_general(a, b, _DIMS["nt"], preferred_element_type=F32)


def _dot_tn(a, b):
    return lax.dot_general(a, b, _DIMS["tn"], preferred_element_type=F32)


def _band_masks(b):
    qi = lax.broadcasted_iota(jnp.int32, (ATTN_BLOCK, ATTN_BLOCK), 0)
    kj = lax.broadcasted_iota(jnp.int32, (ATTN_BLOCK, ATTN_BLOCK), 1)
    return kj <= qi, jnp.logical_and(kj >= qi, b > 0)


def _attn_fwd(q, k, v, d, *, name):
    s, dm = q.shape
    heads, seq, scale = dm // HEAD_DIM, s // d, 1.0 / math.sqrt(HEAD_DIM)
    nb = seq // ATTN_BLOCK
    assert nb * ATTN_BLOCK * d == s and heads <= LANES

    def body(q_ref, kc_ref, kp_ref, vc_ref, vp_ref, o_ref, l_ref):
        mask_c, mask_p = _band_masks(pl.program_id(1))
        lane = lax.broadcasted_iota(jnp.int32, (ATTN_BLOCK, LANES), 1)
        lse_all = jnp.zeros((ATTN_BLOCK, LANES), F32)
        for h in range(heads):
            hs = slice(h * HEAD_DIM, (h + 1) * HEAD_DIM)
            qh = q_ref[:, hs]
            sc = jnp.where(mask_c, _dot_nt(qh, kc_ref[:, hs]) * scale, NEG_INF)
            sp = jnp.where(mask_p, _dot_nt(qh, kp_ref[:, hs]) * scale, NEG_INF)
            m = jnp.maximum(jnp.max(sc, axis=-1, keepdims=True), jnp.max(sp, axis=-1, keepdims=True))
            pc, pp = jnp.exp(sc - m), jnp.exp(sp - m)
            den = jnp.sum(pc, axis=-1, keepdims=True) + jnp.sum(pp, axis=-1, keepdims=True)
            acc = (jnp.dot(pc.astype(BF16), vc_ref[:, hs], preferred_element_type=F32)
                   + jnp.dot(pp.astype(BF16), vp_ref[:, hs], preferred_element_type=F32))
            o_ref[:, hs] = acc / den
            lse_all = jnp.where(lane == h, m + jnp.log(den), lse_all)
        l_ref[...] = lse_all

    cur = pl.BlockSpec((ATTN_BLOCK, dm), lambda r, b: (b, r))
    prev = pl.BlockSpec((ATTN_BLOCK, dm), lambda r, b: (jnp.maximum(b - 1, 0), r))
    views = [t.reshape(seq, d * dm) for t in (q, k, k, v, v)]
    o, lse = pl.pallas_call(
        body, name=name, grid=(d, nb),
        in_specs=[cur, cur, prev, cur, prev],
        out_specs=[cur, pl.BlockSpec((ATTN_BLOCK, LANES), lambda r, b: (b, r))],
        out_shape=[jax.ShapeDtypeStruct((seq, d * dm), F32), jax.ShapeDtypeStruct((seq, d * LANES), F32)],
        compiler_params=_cparams(2),
    )(*views)
    return o.reshape(s, dm), lse.reshape(s, LANES)


def _attn_merge(outs, lses, *, name):
    s, dm = outs[0].shape
    heads = dm // HEAD_DIM
    ts = _tile(s, 128, 16)

    def body(o1, o2, o3, l1, l2, l3, o_ref, ob_ref, lt_ref):
        ls = [l1[...], l2[...], l3[...]]
        m = jnp.maximum(jnp.maximum(ls[0], ls[1]), ls[2])
        ws = [jnp.exp(x - m) for x in ls]
        tot = ws[0] + ws[1] + ws[2]
        lt_ref[...] = m + jnp.log(tot)
        ws = [w / tot for w in ws]
        for h in range(heads):
            hs = slice(h * HEAD_DIM, (h + 1) * HEAD_DIM)
            o = (ws[0][:, h:h + 1] * o1[:, hs] + ws[1][:, h:h + 1] * o2[:, hs]) + ws[2][:, h:h + 1] * o3[:, hs]
            o_ref[:, hs] = o
            ob_ref[:, hs] = o.astype(BF16)

    blk = pl.BlockSpec((ts, dm), lambda i: (i, 0))
    st = pl.BlockSpec((ts, LANES), lambda i: (i, 0))
    return pl.pallas_call(
        body, name=name, grid=(s // ts,),
        in_specs=[blk] * 3 + [st] * 3, out_specs=[blk, blk, st],
        out_shape=[jax.ShapeDtypeStruct((s, dm), F32), jax.ShapeDtypeStruct((s, dm), BF16),
                   jax.ShapeDtypeStruct((s, LANES), F32)],
        compiler_params=_cparams(1),
    )(*outs, *lses)


def _attn_delta(do, o, *, name):
    s, dm = o.shape
    heads = dm // HEAD_DIM
    ts = _tile(s, 256, 16)

    def body(do_ref, o_ref, dl_ref):
        lane = lax.broadcasted_iota(jnp.int32, (ts, LANES), 1)
        acc = jnp.zeros((ts, LANES), F32)
        for h in range(heads):
            hs = slice(h * HEAD_DIM, (h + 1) * HEAD_DIM)
            row = jnp.sum(do_ref[:, hs].astype(F32) * o_ref[:, hs], axis=-1, keepdims=True)
            acc = jnp.where(lane == h, row, acc)
        dl_ref[...] = acc

    blk = pl.BlockSpec((ts, dm), lambda i: (i, 0))
    return pl.pallas_call(
        body, name=name, grid=(s // ts,),
        in_specs=[blk, blk], out_specs=pl.BlockSpec((ts, LANES), lambda i: (i, 0)),
        out_shape=jax.ShapeDtypeStruct((s, LANES), F32),
        compiler_params=_cparams(1),
    )(do, o)


def _attn_bwd(q, k, v, do, lse, delta, acc_in, d, *, name):
    s, dm = q.shape
    heads, seq, scale = dm // HEAD_DIM, s // d, 1.0 / math.sqrt(HEAD_DIM)
    nb = seq // ATTN_BLOCK
    has_acc = acc_in is not None

    def body(*refs):
        q_ref, do_ref, l_ref, dl_ref, kc_ref, kp_ref, vc_ref, vp_ref = refs[:8]
        acc_ref = refs[8] if has_acc else None
        out_ref, carry, new, prevc = refs[-4:]
        b = pl.program_id(1)

        @pl.when(b < nb)
        def _():
            mask_c, mask_p = _band_masks(b)
            lse_blk, dl_blk = l_ref[...], dl_ref[...]
            for h in range(heads):
                hs = slice(h * HEAD_DIM, (h + 1) * HEAD_DIM)
                qh, doh = q_ref[:, hs], do_ref[:, hs]
                kc, kp, vc, vp = kc_ref[:, hs], kp_ref[:, hs], vc_ref[:, hs], vp_ref[:, hs]
                lse_h, dl_h = lse_blk[:, h:h + 1], dl_blk[:, h:h + 1]
                sc = jnp.where(mask_c, _dot_nt(qh, kc) * scale, NEG_INF)
                sp = jnp.where(mask_p, _dot_nt(qh, kp) * scale, NEG_INF)
                pc, pp = jnp.exp(sc - lse_h), jnp.exp(sp - lse_h)
                dsc = (pc * (_dot_nt(doh, vc) - dl_h) * scale).astype(BF16)
                dsp = (pp * (_dot_nt(doh, vp) - dl_h) * scale).astype(BF16)
                new[0, :, hs] = (jnp.dot(dsc, kc, preferred_element_type=F32)
                                 + jnp.dot(dsp, kp, preferred_element_type=F32))
                new[1, :, hs] = _dot_tn(dsc, qh)
                new[2, :, hs] = _dot_tn(pc.astype(BF16), doh)
                prevc[0, :, hs] = _dot_tn(dsp, qh)
                prevc[1, :, hs] = _dot_tn(pp.astype(BF16), doh)

        @pl.when(b == nb)
        def _():
            prevc[...] = jnp.zeros_like(prevc)

        @pl.when(b > 0)
        def _():
            for i in range(3):
                val = carry[i]
                if i > 0:
                    val = val + prevc[i - 1]
                if has_acc:
                    val = val + acc_ref[i]
                out_ref[i] = val

        @pl.when(b < nb)
        def _():
            carry[...] = new[...]

    def cb(b):
        return jnp.minimum(b, nb - 1)

    cur = pl.BlockSpec((ATTN_BLOCK, dm), lambda r, b: (cb(b), r))
    prev = pl.BlockSpec((ATTN_BLOCK, dm), lambda r, b: (jnp.maximum(cb(b) - 1, 0), r))
    stat = pl.BlockSpec((ATTN_BLOCK, LANES), lambda r, b: (cb(b), r))
    out = pl.BlockSpec((3, ATTN_BLOCK, dm), lambda r, b: (0, jnp.maximum(b - 1, 0), r))
    wide = lambda t: t.reshape(seq, d * dm)
    ins = [wide(q), wide(do), lse.reshape(seq, d * LANES), delta.reshape(seq, d * LANES),
           wide(k), wide(k), wide(v), wide(v)]
    in_specs = [cur, cur, stat, stat, cur, prev, cur, prev]
    if has_acc:
        ins.append(acc_in.reshape(3, seq, d * dm))
        in_specs.append(out)
    res = pl.pallas_call(
        body, name=name, grid=(d, nb + 1),
        in_specs=in_specs, out_specs=out,
        out_shape=jax.ShapeDtypeStruct((3, seq, d * dm), F32),
        scratch_shapes=[pltpu.VMEM((3, ATTN_BLOCK, dm), F32), pltpu.VMEM((3, ATTN_BLOCK, dm), F32),
                        pltpu.VMEM((2, ATTN_BLOCK, dm), F32)],
        compiler_params=_cparams(2),
    )(*ins)
    return res.reshape(3, s, dm)


def _ew(fn, ins, out_dtypes, *, name, tile_bytes=1 << 20):
    first = ins[0][0] if isinstance(ins[0], tuple) else ins[0]
    rows, cols = first.shape[-2], first.shape[-1]
    tr = _tile(rows, max(16, tile_bytes // (4 * cols)), 16)
    n_in = len(ins)

    def body(*refs):
        outs = fn(*[r[...] for r in refs[:n_in]])
        for o_ref, val in zip(refs[n_in:], outs):
            o_ref[...] = val.astype(o_ref.dtype)

    in_specs, args = [], []
    for item in ins:
        if isinstance(item, tuple):
            arr, lead = item
            in_specs.append(pl.BlockSpec((None, tr, cols), lambda i, lead=lead: (lead, i, 0)))
            args.append(arr)
        else:
            in_specs.append(pl.BlockSpec((tr, cols), lambda i: (i, 0)))
            args.append(item)
    blk = pl.BlockSpec((tr, cols), lambda i: (i, 0))
    return pl.pallas_call(
        body, name=name, grid=(rows // tr,),
        in_specs=in_specs, out_specs=[blk] * len(out_dtypes),
        out_shape=[jax.ShapeDtypeStruct((rows, cols), dt) for dt in out_dtypes],
        compiler_params=_cparams(1),
    )(*args)


def _adamw_math(g, w, m, v):
    m2 = ADAM_B1 * m + (1.0 - ADAM_B1) * g
    v2 = ADAM_B2 * v + (1.0 - ADAM_B2) * (g * g)
    m_hat = m2 / (1.0 - ADAM_B1 ** ADAM_STEP)
    v_hat = v2 / (1.0 - ADAM_B2 ** ADAM_STEP)
    delta = -ADAM_LR * (m_hat / (jnp.sqrt(v_hat) + ADAM_EPS) + ADAM_WD * w)
    return g, delta, m2, v2


def _adamw(g, w, m, v, *, name):
    flat = lambda t: t.reshape(-1, t.shape[-1])
    outs = _ew(_adamw_math, [flat(g), flat(w), flat(m), flat(v)], [F32] * 4, name=name, tile_bytes=1 << 19)
    return [o.reshape(w.shape) for o in outs]


def _place():
    x, y, c = lax.axis_index("x"), lax.axis_index("y"), lax.axis_index("c")
    chips = [(1 - x, y), (x, 1 - y), (1 - x, 1 - y)]
    return x, y, c, chips


def _remote(src, dst, send_sem, recv_sem, dev):
    return pltpu.make_async_remote_copy(src_ref=src, dst_ref=dst, send_sem=send_sem, recv_sem=recv_sem,
                                        device_id=dev, device_id_type=MESH)


def _all_gather(shards, *, name):
    n = len(shards)
    out_shape = [jax.ShapeDtypeStruct((t.shape[0], N_CHIPS) + t.shape[1:], t.dtype) for t in shards]

    def body(*refs):
        src, dst = refs[:n], refs[n:2 * n]
        send_sems, recv_sems, local_sems = refs[2 * n:]
        x, y, c, chips = _place()
        q_me = 2 * x + y

        def half(t, q, core):
            rh = shards[t].shape[1] // 2
            return dst[t].at[:, q, pl.ds(core * rh, rh), :]

        def src_half(t):
            rh = shards[t].shape[1] // 2
            return src[t].at[:, pl.ds(c * rh, rh), :]

        own = [pltpu.make_async_copy(src[t], dst[t].at[:, q_me], local_sems.at[t]) for t in range(n)]
        for cp in own:
            cp.start()
        sent = []
        for j, (cx, cy) in enumerate(chips):
            for t in range(n):
                cp = _remote(src_half(t), half(t, q_me, c), send_sems.at[6 * t + j], recv_sems.at[6 * t + j], (cx, cy, c))
                cp.start()
                sent.append(cp)
        for j, (cx, cy) in enumerate(chips):
            q_j = 2 * cx + cy
            for t in range(n):
                k = 6 * t + j
                _remote(src_half(t), half(t, q_j, c), send_sems.at[k], recv_sems.at[k], (cx, cy, c)).wait_recv()
                cp = _remote(half(t, q_j, c), half(t, q_j, c), send_sems.at[k + 3], recv_sems.at[k + 3], (x, y, 1 - c))
                cp.start()
                sent.append(cp)
        for j, (cx, cy) in enumerate(chips):
            q_j = 2 * cx + cy
            for t in range(n):
                k = 6 * t + j + 3
                _remote(half(t, q_j, 1 - c), half(t, q_j, 1 - c), send_sems.at[k], recv_sems.at[k], (x, y, 1 - c)).wait_recv()
        for cp in sent:
            cp.wait_send()
        for cp in own:
            cp.wait()

    return pl.pallas_call(
        body, name=name, in_specs=[ANY] * n, out_specs=[ANY] * n, out_shape=out_shape,
        scratch_shapes=[pltpu.SemaphoreType.DMA((6 * n,)), pltpu.SemaphoreType.DMA((6 * n,)),
                        pltpu.SemaphoreType.DMA((n,))],
    )(*shards)


def _rs_sibling(grads, *, name):
    n = len(grads)
    halves = [jax.ShapeDtypeStruct(g.shape[:2] + (g.shape[2] // 2, g.shape[3]), g.dtype) for g in grads]

    def body(*refs):
        src, mine, theirs = refs[:n], refs[n:2 * n], refs[2 * n:3 * n]
        send_sems, recv_sems, local_sems = refs[3 * n:]
        x, y, c, _ = _place()
        ops = []
        for t in range(n):
            rh = grads[t].shape[2] // 2
            keep = pltpu.make_async_copy(src[t].at[:, :, pl.ds(c * rh, rh), :], mine[t], local_sems.at[t])
            give = _remote(src[t].at[:, :, pl.ds((1 - c) * rh, rh), :], theirs[t], send_sems.at[t], recv_sems.at[t],
                           (x, y, 1 - c))
            keep.start()
            give.start()
            ops += [keep, give]
        for op in ops:
            op.wait()

    outs = pl.pallas_call(
        body, name=name, in_specs=[ANY] * n, out_specs=[ANY] * (2 * n), out_shape=halves + halves,
        scratch_shapes=[pltpu.SemaphoreType.DMA((n,)), pltpu.SemaphoreType.DMA((n,)), pltpu.SemaphoreType.DMA((n,))],
    )(*grads)
    return outs[:n], outs[n:]


def _rs_chips(parts, *, name):
    n = len(parts)
    out_shape = [jax.ShapeDtypeStruct((N_CHIPS, p.shape[0]) + p.shape[2:], p.dtype) for p in parts]

    def body(*refs):
        src, dst = refs[:n], refs[n:2 * n]
        send_sems, recv_sems, local_sems = refs[2 * n:]
        x, y, c, chips = _place()
        q_me = 2 * x + y
        own = [pltpu.make_async_copy(src[t].at[:, q_me], dst[t].at[q_me], local_sems.at[t]) for t in range(n)]
        for cp in own:
            cp.start()
        sent = []
        for j, (cx, cy) in enumerate(chips):
            for t in range(n):
                cp = _remote(src[t].at[:, 2 * cx + cy], dst[t].at[q_me], send_sems.at[3 * t + j], recv_sems.at[3 * t + j],
                             (cx, cy, c))
                cp.start()
                sent.append(cp)
        for j, (cx, cy) in enumerate(chips):
            for t in range(n):
                _remote(src[t].at[:, q_me], dst[t].at[2 * cx + cy], send_sems.at[3 * t + j], recv_sems.at[3 * t + j],
                        (cx, cy, c)).wait_recv()
        for cp in sent:
            cp.wait_send()
        for cp in own:
            cp.wait()

    return pl.pallas_call(
        body, name=name, in_specs=[ANY] * n, out_specs=[ANY] * n, out_shape=out_shape,
        scratch_shapes=[pltpu.SemaphoreType.DMA((3 * n,)), pltpu.SemaphoreType.DMA((3 * n,)),
                        pltpu.SemaphoreType.DMA((n,))],
    )(*parts)


def _rs_finish(halves, *, name):
    n = len(halves)
    out_shape = [jax.ShapeDtypeStruct((h.shape[0], 2 * h.shape[1], h.shape[2]), h.dtype) for h in halves]

    def body(*refs):
        src, dst = refs[:n], refs[n:2 * n]
        send_sems, recv_sems, local_sems = refs[2 * n:]
        x, y, c, _ = _place()
        ops = []
        for t in range(n):
            rh = halves[t].shape[1]
            rows = dst[t].at[:, pl.ds(c * rh, rh), :]
            keep = pltpu.make_async_copy(src[t], rows, local_sems.at[t])
            give = _remote(src[t], rows, send_sems.at[t], recv_sems.at[t], (x, y, 1 - c))
            keep.start()
            give.start()
            ops += [keep, give]
        for t in range(n):
            rh = halves[t].shape[1]
            ops[2 * t].wait()
            ops[2 * t + 1].wait_send()
            _remote(src[t], dst[t].at[:, pl.ds((1 - c) * rh, rh), :], send_sems.at[t], recv_sems.at[t],
                    (x, y, 1 - c)).wait_recv()

    return pl.pallas_call(
        body, name=name, in_specs=[ANY] * n, out_specs=[ANY] * n, out_shape=out_shape,
        scratch_shapes=[pltpu.SemaphoreType.DMA((n,)), pltpu.SemaphoreType.DMA((n,)), pltpu.SemaphoreType.DMA((n,))],
    )(*halves)


def _all_reduce_small(vec, *, name):
    rows = vec.shape[0]

    def body(v_ref, o_ref, land, send_sems, recv_sems):
        x, y, c, _ = _place()
        me = 4 * x + 2 * y + c
        land[me] = v_ref[...]
        flips = [(fx, fy, fc) for fx in (0, 1) for fy in (0, 1) for fc in (0, 1)][1:]
        sent = []
        for k, (fx, fy, fc) in enumerate(flips):
            cp = _remote(v_ref, land.at[me], send_sems.at[k], recv_sems.at[k], (x ^ fx, y ^ fy, c ^ fc))
            cp.start()
            sent.append(cp)
        for k, (fx, fy, fc) in enumerate(flips):
            peer = 4 * (x ^ fx) + 2 * (y ^ fy) + (c ^ fc)
            _remote(v_ref, land.at[peer], send_sems.at[k], recv_sems.at[k], (x ^ fx, y ^ fy, c ^ fc)).wait_recv()
        for cp in sent:
            cp.wait_send()
        total = land[0]
        for dev in range(1, 8):
            total = total + land[dev]
        o_ref[...] = total

    whole = pl.BlockSpec(memory_space=pltpu.VMEM)
    return pl.pallas_call(
        body, name=name, in_specs=[whole], out_specs=whole,
        out_shape=jax.ShapeDtypeStruct(vec.shape, F32),
        scratch_shapes=[pltpu.VMEM((8, rows, LANES), F32), pltpu.SemaphoreType.DMA((7,)), pltpu.SemaphoreType.DMA((7,))],
        compiler_params=pltpu.CompilerParams(vmem_limit_bytes=VMEM_LIMIT),
    )(vec)


def _pack(parts, mult=16):
    flat = jnp.concatenate([p.reshape(-1).astype(F32) for p in parts])
    rows = -(-flat.shape[0] // (LANES * mult)) * mult
    return jnp.pad(flat, (0, rows * LANES - flat.shape[0])).reshape(rows, LANES)


def _unpack(vec, shapes):
    flat, out, pos = vec.reshape(-1), [], 0
    for shp in shapes:
        size = math.prod(shp)
        out.append(flat[pos:pos + size].reshape(shp))
        pos += size
    return out


def kernel(x, pool_w_in, pool_w_grp, pool_scale, pool_w_out, attn_w_q, attn_w_o, shared_w_k, shared_w_v, ffn_w_up, ffn_conv_w, ffn_conv_b, ffn_w_down, ln1_g, ln1_b, ln2_g, ln2_b, loss_target, m_pool_w_in, m_pool_w_grp, m_pool_scale, m_pool_w_out, m_attn_w_q, m_attn_w_o, m_shared_w_k, m_shared_w_v, m_ffn_w_up, m_ffn_conv_w, m_ffn_conv_b, m_ffn_w_down, m_ln1_g, m_ln1_b, m_ln2_g, m_ln2_b, v_pool_w_in, v_pool_w_grp, v_pool_scale, v_pool_w_out, v_attn_w_q, v_attn_w_o, v_shared_w_k, v_shared_w_v, v_ffn_w_up, v_ffn_conv_w, v_ffn_conv_b, v_ffn_w_down, v_ln1_g, v_ln1_b, v_ln2_g, v_ln2_b):
    s, d = x.shape[1], x.shape[2]
    n2 = ffn_w_up.shape[2]
    fq = ffn_w_down.shape[1]
    assert 2 * fq == n2 and d % N_CHIPS == 0
    g_n, cg = pool_w_grp.shape[1], pool_w_grp.shape[3]
    xs, tgt = x[0], loss_target[0]
    q_me = 2 * lax.axis_index("x") + lax.axis_index("y")

    six = jnp.stack([pool_w_in[0], pool_w_out[0], attn_w_q[0], shared_w_k, shared_w_v, attn_w_o[0]]).astype(BF16)
    small = _pack([ffn_conv_w, pool_scale])
    six_g, grp_g, up_g, dn_g, small_g = _all_gather(
        [six, pool_w_grp[0].astype(BF16), ffn_w_up.astype(BF16), ffn_w_down.astype(BF16), small[None]],
        name="gather_weights")
    w6 = six_g.reshape(6, d, d)
    w_grp = grp_g.reshape(g_n, cg, cg)
    up8 = up_g.reshape(2 * N_CHIPS, d, n2)
    dn4 = dn_g.reshape(4, n2, d)
    small_q = small_g.reshape(N_CHIPS, -1)
    n_cw = 2 * 3 * n2
    conv_w = small_q[:, :n_cw].reshape(N_CHIPS, 2, 3, n2).transpose(1, 0, 2, 3)
    scale_full = small_q[:, n_cw:n_cw + d // N_CHIPS].reshape(1, d)
    conv_b = ffn_conv_b.reshape(2, N_CHIPS, 1, n2)

    def ffn_fwd(l, hb):
        u = _mm(hb[None], up8, mode="nn", name=f"ffn{l}_up", out_dtype=F32, n_q=4, tm=_tile(s, 512), tn=n2,
                tk=_tile(d, 1024), qa=_q0, qb=lambda q, qr: 4 * l + _perm(q)).reshape(2, 2, s, n2)
        act = _act_fwd(u, conv_w[l], conv_b[l], name=f"ffn{l}_act")
        ff = _mm(act, dn4, mode="nn", name=f"ffn{l}_down", out_dtype=F32, n_q=1, n_qr=2, tm=_tile(s, 512),
                 tn=_tile(d, 1024), tk=n2, qa=_qr, qb=lambda q, qr: 2 * l + qr, qo=lambda q: 0)[0]
        return u, act, ff

    p = _mm_dd(xs[None], w6, IW_IN, mode="nn", name="pool_in", out_dtype=F32)[0]
    pooled = _pool(p, backward=False, name="pool_fwd")
    mg, mixed = _grp_fwd(pooled, w_grp, scale_full, name="pool_grp")
    mix0 = _mm_dd(mixed[None], w6, IW_OUT, mode="nn", name="pool_out", out_dtype=F32)[0]
    h1, h1b, xh1, rs1 = _ln_fwd(xs, mix0, ln1_g[0:1], ln1_b[0:1], name="ln1_0")
    u0, act0, ff0 = ffn_fwd(0, h1b)
    h2, h2b, xh2, rs2 = _ln_fwd(h1, ff0, ln2_g[0:1], ln2_b[0:1], name="ln2_0")

    qkv = _mm_dd(h2b[None], w6, IW_Q, mode="nn", name="attn_qkv", out_dtype=BF16, n_q=3)
    branch = [_attn_fwd(qkv[0], qkv[1], qkv[2], dil, name=f"attn_fwd_d{dil}") for dil in DILATIONS]
    o, ob, lse = _attn_merge([br[0] for br in branch], [br[1] for br in branch], name="attn_merge")
    mix1 = _mm_dd(ob[None], w6, IW_O, mode="nn", name="attn_out", out_dtype=F32)[0]
    h3, h3b, xh3, rs3 = _ln_fwd(h2, mix1, ln1_g[1:2], ln1_b[1:2], name="ln1_1")
    u1, act1, ff1 = ffn_fwd(1, h3b)
    h4, _, xh4, rs4 = _ln_fwd(h3, ff1, ln2_g[1:2], ln2_b[1:2], name="ln2_1")
    dh4, loss_local = _loss_head(h4, tgt, name="loss_head")

    g_up = g_dn = None
    d_conv_w, d_conv_b = [None, None], [None, None]

    def ffn_bwd(l, dzb, u, act, hb):
        nonlocal g_up, g_dn
        da = _mm(dzb[None], dn4, mode="nt", name=f"ffn{l}_dact", out_dtype=F32, n_q=2, tm=_tile(s, 512), tn=n2,
                 tk=_tile(d, 512), qa=_q0, qb=lambda q, qr: 2 * l + q)
        du, dwg, dwv, dbg, dbv = _act_bwd(u, da, conv_w[l], conv_b[l], name=f"ffn{l}_dconv")
        d_conv_w[l] = jnp.concatenate([dwg, dwv], axis=0)
        d_conv_b[l] = jnp.concatenate([dbg, dbv], axis=0)
        du4 = du.reshape(4, s, n2)
        g_dn = _mm(act, dzb[None], mode="tn", name=f"ffn{l}_gdown", out_dtype=BF16, n_q=2, tm=n2, tn=_tile(d, 1024),
                   tk=_tile(s, 512), qa=_qq, qb=_q0, qo=lambda q: 2 * l + q, out_q=4, into=g_dn)
        g_up = _mm(hb[None], du4, mode="tn", name=f"ffn{l}_gup", out_dtype=BF16, n_q=4, tm=_tile(d, 512), tn=n2,
                   tk=_tile(s, 1024), qa=_q0, qb=lambda q, qr: _perm(q), qo=lambda q: 4 * l + q, out_q=8, into=g_up)
        return _mm(du4, up8, mode="nt", name=f"ffn{l}_dh", out_dtype=F32, n_q=1, n_qr=4, tm=_tile(s, 512),
                   tn=_tile(d, 1024), tk=n2, qa=_qr, qb=lambda q, qr: 4 * l + _perm(qr), qo=lambda q: 0)[0]

    dz4, dz4b, dg_ln2_1, db_ln2_1 = _ln_bwd(None, dh4, xh4, rs4, ln2_g[1:2], name="dln2_1")
    dh3 = ffn_bwd(1, dz4b, u1, act1, h3b)
    dz3, dz3b, dg_ln1_1, db_ln1_1 = _ln_bwd(dz4, dh3, xh3, rs3, ln1_g[1:2], name="dln1_1")
    g_wo = _mm_wgrad(ob[None], dz3b[None], name="attn_gwo")
    do = _mm_dd(dz3b[None], w6, IW_O, mode="nt", name="attn_do", out_dtype=BF16)[0]
    delta = _attn_delta(do, o, name="attn_delta")
    dqkv = None
    for dil in DILATIONS:
        dqkv = _attn_bwd(qkv[0], qkv[1], qkv[2], do, lse, delta, dqkv, dil, name=f"attn_bwd_d{dil}")
    g_wq = _mm_wgrad(h2b[None], dqkv, name="attn_gwq", qb=lambda q, qr: 0)
    g_wk = _mm_wgrad(h2b[None], dqkv, name="attn_gwk", qb=lambda q, qr: 1)
    g_wv = _mm_wgrad(h2b[None], dqkv, name="attn_gwv", qb=lambda q, qr: 2)
    dh2 = _mm_dd(dqkv, w6, IW_Q, mode="nt", name="attn_dh", out_dtype=F32, n_qr=3, qa=_qr)[0]
    dz2, dz2b, dg_ln2_0, db_ln2_0 = _ln_bwd(dz3, dh2, xh2, rs2, ln2_g[0:1], name="dln2_0")
    dh1 = ffn_bwd(0, dz2b, u0, act0, h1b)
    dz1, dz1b, dg_ln1_0, db_ln1_0 = _ln_bwd(dz2, dh1, xh1, rs1, ln1_g[0:1], name="dln1_0")
    g_wout = _mm_wgrad(mixed[None], dz1b[None], name="pool_gwout")
    dmixed = _mm_dd(dz1b[None], w6, IW_OUT, mode="nt", name="pool_dmixed", out_dtype=F32)[0]
    dmg, d_scale = _grp_bwd_pre(dmixed, mg, scale_full, name="pool_dscale")
    g_wgrp = _grp_mm(pooled, dmg, mode="tn", name="pool_gwgrp", out_dtype=BF16, tm=1024)
    dpooled = _grp_mm(dmg, None, mode="nt", name="pool_dpooled", out_dtype=F32, tm=1024, w=w_grp)
    dp = _pool(dpooled, backward=True, name="pool_bwd")
    g_win = _mm_wgrad(xs[None], dp[None], name="pool_gwin")
    dx_mm = _mm_dd(dp[None], w6, IW_IN, mode="nt", name="pool_dx", out_dtype=F32)[0]
    (grad_x,) = _ew(lambda a, b: (ALPHA * a + b,), [dz1, dx_mm], [F32], name="grad_x")

    rq = d // N_CHIPS
    big = [
        ("pool_w_in", g_win.reshape(1, N_CHIPS, rq, d), pool_w_in, m_pool_w_in, v_pool_w_in),
        ("pool_w_grp", g_wgrp.reshape(g_n, N_CHIPS, cg // N_CHIPS, cg), pool_w_grp, m_pool_w_grp, v_pool_w_grp),
        ("pool_w_out", g_wout.reshape(1, N_CHIPS, rq, d), pool_w_out, m_pool_w_out, v_pool_w_out),
        ("attn_w_q", g_wq.reshape(1, N_CHIPS, rq, d), attn_w_q, m_attn_w_q, v_attn_w_q),
        ("attn_w_o", g_wo.reshape(1, N_CHIPS, rq, d), attn_w_o, m_attn_w_o, v_attn_w_o),
        ("shared_w_k", g_wk.reshape(1, N_CHIPS, rq, d), shared_w_k, m_shared_w_k, v_shared_w_k),
        ("shared_w_v", g_wv.reshape(1, N_CHIPS, rq, d), shared_w_v, m_shared_w_v, v_shared_w_v),
        ("ffn_w_up", g_up.reshape(2, N_CHIPS, d, n2), ffn_w_up, m_ffn_w_up, v_ffn_w_up),
        ("ffn_w_down", g_dn.reshape(2, N_CHIPS, fq, d), ffn_w_down, m_ffn_w_down, v_ffn_w_down),
    ]
    mine, theirs = _rs_sibling([t[1] for t in big], name="rs_sibling")
    parts = []
    for (nm, *_), a, b in zip(big, mine, theirs):
        flat = lambda t: t.reshape(-1, t.shape[-1])
        (psum,) = _ew(lambda u_, v_: (u_.astype(F32) + v_.astype(F32),), [flat(a), flat(b)], [BF16], name=f"rs_add_{nm}")
        parts.append(psum.reshape(a.shape))
    landed = _rs_chips(parts, name="rs_chips")
    finished = []
    for (nm, *_), y4 in zip(big, landed):
        y3 = y4.reshape(N_CHIPS, -1, y4.shape[-1])
        (tot,) = _ew(lambda a0, a1, a2, a3: (((a0.astype(F32) + a1.astype(F32)) + a2.astype(F32)) + a3.astype(F32),),
                     [(y3, 0), (y3, 1), (y3, 2), (y3, 3)], [F32], name=f"rs_sum_{nm}")
        finished.append(tot.reshape(y4.shape[1:]))
    shards = _rs_finish(finished, name="rs_finish")
    results = {}
    for (nm, _, w, m, v), gsh in zip(big, shards):
        results[nm] = _adamw(gsh.reshape(w.shape), w, m, v, name=f"adamw_{nm}")

    ln_grads = [jnp.concatenate([a, b], axis=0) for a, b in
                ((dg_ln1_0, dg_ln1_1), (db_ln1_0, db_ln1_1), (dg_ln2_0, dg_ln2_1), (db_ln2_0, db_ln2_1))]
    small_shapes = [(2, N_CHIPS, 3, n2), (2, N_CHIPS, n2)] + [(2, d)] * 4 + [(1, d)]
    vec = _pack([jnp.stack(d_conv_w), jnp.stack(d_conv_b)] + ln_grads + [d_scale], mult=8)
    tot = _unpack(_all_reduce_small(vec, name="allreduce_small"), small_shapes)
    g_cw = lax.dynamic_index_in_dim(tot[0], q_me, axis=1, keepdims=False)
    g_cb = tot[1].reshape(2, N_CHIPS * n2)
    g_scale = lax.dynamic_slice_in_dim(tot[6], q_me * rq, rq, axis=1)
    small_names = ["ffn_conv_w", "ffn_conv_b", "ln1_g", "ln1_b", "ln2_g", "ln2_b", "pool_scale"]
    small_g = [g_cw, g_cb, tot[2], tot[3], tot[4], tot[5], g_scale]
    small_w = [ffn_conv_w, ffn_conv_b, ln1_g, ln1_b, ln2_g, ln2_b, pool_scale]
    small_m = [m_ffn_conv_w, m_ffn_conv_b, m_ln1_g, m_ln1_b, m_ln2_g, m_ln2_b, m_pool_scale]
    small_v = [v_ffn_conv_w, v_ffn_conv_b, v_ln1_g, v_ln1_b, v_ln2_g, v_ln2_b, v_pool_scale]
    packed = _ew(_adamw_math, [_pack(small_g, 8), _pack(small_w, 8), _pack(small_m, 8), _pack(small_v, 8)], [F32] * 4,
                 name="adamw_small")
    shapes = [w.shape for w in small_w]
    unpacked = [_unpack(pk, shapes) for pk in packed]
    for i, nm in enumerate(small_names):
        results[nm] = [unpacked[k][i] for k in range(4)]

    loss = lax.psum(loss_local[0, 0], ("x", "y", "c"))
    order = ["pool_w_in", "pool_w_grp", "pool_scale", "pool_w_out", "attn_w_q", "attn_w_o", "shared_w_k", "shared_w_v",
             "ffn_w_up", "ffn_conv_w", "ffn_conv_b", "ffn_w_down", "ln1_g", "ln1_b", "ln2_g", "ln2_b"]
    outs = [loss, grad_x[None]]
    for k in range(4):
        outs += [results[nm][k] for nm in order]
    return tuple(outs)
```

```python
import functools
import math

import jax
import jax.numpy as jnp
from jax import lax
from jax.experimental import pallas as pl
from jax.experimental.pallas import tpu as pltpu

F32 = jnp.float32
BF16 = jnp.bfloat16

LANES = 128
HEAD_DIM = 128
ATTN_BLOCK = 128
DILATIONS = (1, 4, 16)
POOL_WINDOWS = (2, 4, 8, 16)
POOL_HALO = 16
CONV_HALO = 8
DEPTH = 2
ALPHA = (2.0 * DEPTH) ** 0.25
LN_EPS = 1e-5
NEG_INF = -1e30
ADAM_LR = 0.001
ADAM_B1 = 0.9
ADAM_B2 = 0.999
ADAM_EPS = 1e-08
ADAM_WD = 0.01
ADAM_STEP = 10
N_CHIPS = 4
VMEM_LIMIT = 56 * 1024 * 1024
ANY = pl.BlockSpec(memory_space=pl.ANY)
MESH = pl.DeviceIdType.MESH

IW_IN, IW_OUT, IW_Q, IW_K, IW_V, IW_O = range(6)


def _cparams(n_grid):
    return pltpu.CompilerParams(dimension_semantics=("arbitrary",) * n_grid, vmem_limit_bytes=VMEM_LIMIT)


def _tile(dim, pref, align=LANES):
    if dim <= pref:
        return dim
    t = (pref // align) * align
    while t >= align:
        if dim % t == 0:
            return t
        t -= align
    return dim


def _perm(q):
    return (q % 2) * 2 + q // 2


_DIMS = {"nn": (((1,), (0,)), ((), ())), "nt": (((1,), (1,)), ((), ())), "tn": (((0,), (0,)), ((), ()))}


def _mm(a, b, *, mode, name, out_dtype, n_q, tm, tn, tk, qa, qb, qo=lambda q: q, n_qr=1, out_q=None, into=None):
    if mode == "nn":
        m, kdim, n = a.shape[1], a.shape[2], b.shape[2]
    elif mode == "nt":
        m, kdim, n = a.shape[1], a.shape[2], b.shape[1]
    else:
        kdim, m, n = a.shape[1], a.shape[2], b.shape[2]
    assert m % tm == 0 and n % tn == 0 and kdim % tk == 0, (name, m, n, kdim, tm, tn, tk)
    kr_n = kdim // tk
    nr = n_qr * kr_n
    out_q = n_q if out_q is None else out_q

    def split(r):
        return (r // kr_n, r % kr_n) if n_qr > 1 else (0, r)

    if mode == "tn":
        a_spec = pl.BlockSpec((None, tk, tm), lambda q, i, j, r: (qa(q, split(r)[0]), split(r)[1], i))
    else:
        a_spec = pl.BlockSpec((None, tm, tk), lambda q, i, j, r: (qa(q, split(r)[0]), i, split(r)[1]))
    if mode == "nt":
        b_spec = pl.BlockSpec((None, tn, tk), lambda q, i, j, r: (qb(q, split(r)[0]), j, split(r)[1]))
    else:
        b_spec = pl.BlockSpec((None, tk, tn), lambda q, i, j, r: (qb(q, split(r)[0]), split(r)[1], j))
    o_spec = pl.BlockSpec((None, tm, tn), lambda q, i, j, r: (qo(q), i, j))
    dims = _DIMS[mode]

    def body(*refs):
        a_ref, b_ref = refs[0], refs[1]
        o_ref = refs[3] if into is not None else refs[2]
        lhs, rhs = a_ref[...], b_ref[...]
        if lhs.dtype != BF16:
            lhs = lhs.astype(BF16)
        if rhs.dtype != BF16:
            rhs = rhs.astype(BF16)
        part = lax.dot_general(lhs, rhs, dims, preferred_element_type=F32)
        if nr == 1:
            o_ref[...] = part.astype(o_ref.dtype)
        else:
            acc_ref = refs[-1]
            r = pl.program_id(3)

            @pl.when(r == 0)
            def _():
                acc_ref[...] = part

            @pl.when(r > 0)
            def _():
                acc_ref[...] += part

            @pl.when(r == nr - 1)
            def _():
                o_ref[...] = acc_ref[...].astype(o_ref.dtype)

    in_specs = [a_spec, b_spec]
    args = [a, b]
    aliases = {}
    if into is not None:
        assert into.shape == (out_q, m, n) and into.dtype == out_dtype
        in_specs.append(ANY)
        args.append(into)
        aliases = {2: 0}
    return pl.pallas_call(
        body, name=name, grid=(n_q, m // tm, n // tn, nr),
        in_specs=in_specs, out_specs=o_spec,
        out_shape=jax.ShapeDtypeStruct((out_q, m, n), out_dtype),
        scratch_shapes=[pltpu.VMEM((tm, tn), F32)] if nr > 1 else [],
        input_output_aliases=aliases,
        compiler_params=_cparams(4),
    )(*args)


def _q0(q, qr):
    return 0


def _qq(q, qr):
    return q


def _qr(q, qr):
    return qr


def _mm_dd(a3, w6, widx, *, mode, name, out_dtype, n_q=1, n_qr=1, qa=_q0):
    d = w6.shape[1]
    f32_in = a3.dtype != BF16
    tm = _tile(a3.shape[1] if mode != "tn" else a3.shape[2], 512 if f32_in else 1024)
    if n_qr > 1:
        qb = lambda q, qr: widx + qr
    elif n_q > 1:
        qb = lambda q, qr: widx + q
    else:
        qb = lambda q, qr: widx
    return _mm(a3, w6, mode=mode, name=name, out_dtype=out_dtype, n_q=n_q, n_qr=n_qr,
               tm=tm, tn=_tile(d, 1024), tk=d, qa=qa, qb=qb)


def _mm_wgrad(a3, b3, *, name, n_q=1, qa=_q0, qb=_q0):
    s, d = a3.shape[1], a3.shape[2]
    f32_in = a3.dtype != BF16 or b3.dtype != BF16
    return _mm(a3, b3, mode="tn", name=name, out_dtype=BF16, n_q=n_q, tm=_tile(d, 1024), tn=b3.shape[2],
               tk=_tile(s, 512 if f32_in else 1024), qa=qa, qb=qb)


def _grp_mm(a, b, *, mode, name, out_dtype, tm, w=None):
    s = a.shape[0]
    if mode == "tn":
        cg = b.shape[1] // len(POOL_WINDOWS)
        g_n = len(POOL_WINDOWS)
        ts = _tile(s, tm)
        nr = s // ts

        def body(a_ref, b_ref, o_ref, acc_ref):
            r = pl.program_id(1)
            part = lax.dot_general(a_ref[...], b_ref[...], _DIMS["tn"], preferred_element_type=F32)

            @pl.when(r == 0)
            def _():
                acc_ref[...] = part

            @pl.when(r > 0)
            def _():
                acc_ref[...] += part

            @pl.when(r == nr - 1)
            def _():
                o_ref[...] = acc_ref[...].astype(o_ref.dtype)

        return pl.pallas_call(
            body, name=name, grid=(g_n, nr),
            in_specs=[pl.BlockSpec((ts, cg), lambda g, r: (r, g)), pl.BlockSpec((ts, cg), lambda g, r: (r, g))],
            out_specs=pl.BlockSpec((None, cg, cg), lambda g, r: (g, 0, 0)),
            out_shape=jax.ShapeDtypeStruct((g_n, cg, cg), out_dtype),
            scratch_shapes=[pltpu.VMEM((cg, cg), F32)],
            compiler_params=_cparams(2),
        )(a, b)
    g_n, cg = w.shape[0], w.shape[1]
    ts = _tile(s, tm)
    dims = _DIMS[mode]

    def body(a_ref, w_ref, o_ref):
        o_ref[...] = lax.dot_general(a_ref[...], w_ref[...], dims, preferred_element_type=F32).astype(o_ref.dtype)

    return pl.pallas_call(
        body, name=name, grid=(g_n, s // ts),
        in_specs=[pl.BlockSpec((ts, cg), lambda g, i: (i, g)), pl.BlockSpec((None, cg, cg), lambda g, i: (g, 0, 0))],
        out_specs=pl.BlockSpec((ts, cg), lambda g, i: (i, g)),
        out_shape=jax.ShapeDtypeStruct((s, g_n * cg), out_dtype),
        compiler_params=_cparams(2),
    )(a, w)


def _causal_ext(load, r0, rows, halo):
    cur = load(r0, rows)
    prev = load(pl.multiple_of(jnp.maximum(r0 - halo, 0), halo), halo)
    prev = jnp.where(r0 > 0, prev, jnp.zeros_like(prev))
    return jnp.concatenate([prev, cur], axis=0)


def _anti_ext(load, r0, rows, halo, s):
    cur = load(r0, rows)
    nxt = load(pl.multiple_of(jnp.minimum(r0 + rows, s - halo), halo), halo)
    nxt = jnp.where(r0 + rows < s, nxt, jnp.zeros_like(nxt))
    return jnp.concatenate([cur, nxt], axis=0)


def _down(ext, k):
    return pltpu.roll(ext, k, axis=0)


def _up(ext, k):
    return pltpu.roll(ext, ext.shape[0] - k, axis=0)


def _fold8(x):
    return jnp.sum(x.reshape(x.shape[0] // 8, 8, x.shape[1]), axis=0)


def _sigmoid(x):
    return 1.0 / (1.0 + jnp.exp(-x))


def _pool(p, *, backward, name, rows=64):
    s, d = p.shape
    strips_per_group = (d // len(POOL_WINDOWS)) // LANES
    assert strips_per_group * LANES * len(POOL_WINDOWS) == d and s % rows == 0

    def body(p_ref, o_ref):
        g = pl.program_id(0) // strips_per_group
        win = jnp.left_shift(2, g).astype(F32)

        def load(r0, n):
            return p_ref[pl.ds(r0, n), :]

        def pick(levels):
            return jnp.where(g == 0, levels[0], jnp.where(g == 1, levels[1], jnp.where(g == 2, levels[2], levels[3])))

        def chunk(c, carry):
            r0 = pl.multiple_of(c * rows, rows)
            if not backward:
                ext = _causal_ext(load, r0, rows, POOL_HALO)
                levels, acc = [], ext
                for k in (1, 2, 4, 8):
                    acc = acc + _down(acc, k)
                    levels.append(acc)
                t = (r0 + lax.broadcasted_iota(jnp.int32, (rows, LANES), 0)).astype(F32)
                cnt = jnp.minimum(t + 1.0, win)
                out = pick(levels)[POOL_HALO:] / cnt - ext[POOL_HALO:]
            else:
                ext = _anti_ext(load, r0, rows, POOL_HALO, s)
                t = (r0 + lax.broadcasted_iota(jnp.int32, (rows + POOL_HALO, LANES), 0)).astype(F32)
                e = ext / jnp.minimum(t + 1.0, win)
                levels, acc = [], e
                for k in (1, 2, 4, 8):
                    acc = acc + _up(acc, k)
                    levels.append(acc)
                out = pick(levels)[:rows] - ext[:rows]
            o_ref[pl.ds(r0, rows), :] = out.astype(o_ref.dtype)
            return carry

        lax.fori_loop(0, s // rows, chunk, 0)

    return pl.pallas_call(
        body, name=name, grid=(d // LANES,),
        in_specs=[pl.BlockSpec((s, LANES), lambda j: (0, j))],
        out_specs=pl.BlockSpec((s, LANES), lambda j: (0, j)),
        out_shape=jax.ShapeDtypeStruct((s, d), BF16),
        compiler_params=_cparams(1),
    )(p)


def _grp_fwd(pooled, w_grp, scale, *, name):
    s, d = pooled.shape
    g_n, cg = w_grp.shape[0], w_grp.shape[1]
    ts = _tile(s, 1024)

    def body(a_ref, w_ref, sc_ref, mg_ref, mx_ref):
        mg = jnp.dot(a_ref[...], w_ref[...], preferred_element_type=F32)
        mg_ref[...] = mg.astype(BF16)
        mx_ref[...] = (mg * sc_ref[...]).astype(BF16)

    blk = pl.BlockSpec((ts, cg), lambda g, i: (i, g))
    return pl.pallas_call(
        body, name=name, grid=(g_n, s // ts),
        in_specs=[blk, pl.BlockSpec((None, cg, cg), lambda g, i: (g, 0, 0)), pl.BlockSpec((1, cg), lambda g, i: (0, g))],
        out_specs=[blk, blk],
        out_shape=[jax.ShapeDtypeStruct((s, d), BF16)] * 2,
        compiler_params=_cparams(2),
    )(pooled, w_grp, scale)


def _grp_bwd_pre(dmixed, mg, scale, *, name):
    s, d = dmixed.shape
    ts = _tile(s, 256, 16)

    def body(dm_ref, mg_ref, sc_ref, dmg_ref, dsc_ref):
        dm = dm_ref[...]
        dmg_ref[...] = (dm * sc_ref[...]).astype(BF16)
        part = jnp.sum(dm * mg_ref[...].astype(F32), axis=0, keepdims=True)

        @pl.when(pl.program_id(0) == 0)
        def _():
            dsc_ref[...] = part

        @pl.when(pl.program_id(0) > 0)
        def _():
            dsc_ref[...] += part

    blk = pl.BlockSpec((ts, d), lambda i: (i, 0))
    vec = pl.BlockSpec((1, d), lambda i: (0, 0))
    return pl.pallas_call(
        body, name=name, grid=(s // ts,),
        in_specs=[blk, blk, vec], out_specs=[blk, vec],
        out_shape=[jax.ShapeDtypeStruct((s, d), BF16), jax.ShapeDtypeStruct((1, d), F32)],
        compiler_params=_cparams(1),
    )(dmixed, mg, scale)


def _ln_fwd(res, mm, g, b, *, name):
    s, d = res.shape
    ts = _tile(s, 128, 16)

    def body(res_ref, mm_ref, g_ref, b_ref, h_ref, hb_ref, xh_ref, rs_ref):
        z = ALPHA * res_ref[...] + mm_ref[...]
        mu = jnp.mean(z, axis=-1, keepdims=True)
        zc = z - mu
        var = jnp.mean(zc * zc, axis=-1, keepdims=True)
        rstd = lax.rsqrt(var + LN_EPS)
        xhat = zc * rstd
        h = xhat * g_ref[...] + b_ref[...]
        h_ref[...] = h
        hb_ref[...] = h.astype(BF16)
        xh_ref[...] = xhat
        rs_ref[...] = rstd

    blk = pl.BlockSpec((ts, d), lambda i: (i, 0))
    vec = pl.BlockSpec((1, d), lambda i: (0, 0))
    return pl.pallas_call(
        body, name=name, grid=(s // ts,),
        in_specs=[blk, blk, vec, vec],
        out_specs=[blk, blk, blk, pl.BlockSpec((ts, 1), lambda i: (i, 0))],
        out_shape=[jax.ShapeDtypeStruct((s, d), F32), jax.ShapeDtypeStruct((s, d), BF16),
                   jax.ShapeDtypeStruct((s, d), F32), jax.ShapeDtypeStruct((s, 1), F32)],
        compiler_params=_cparams(1),
    )(res, mm, g, b)


def _ln_bwd(dres, dmm, xhat, rstd, g, *, name):
    s, d = dmm.shape
    ts = _tile(s, 128, 16)
    has_res = dres is not None

    def body(*refs):
        if has_res:
            dres_ref, dmm_ref, xh_ref, rs_ref, g_ref, dz_ref, dzb_ref, dg_ref, db_ref = refs
            dh = ALPHA * dres_ref[...] + dmm_ref[...]
        else:
            dmm_ref, xh_ref, rs_ref, g_ref, dz_ref, dzb_ref, dg_ref, db_ref = refs
            dh = dmm_ref[...]
        xhat_ = xh_ref[...]
        dxh = dh * g_ref[...]
        c1 = jnp.mean(dxh, axis=-1, keepdims=True)
        c2 = jnp.mean(dxh * xhat_, axis=-1, keepdims=True)
        dz = rs_ref[...] * (dxh - c1 - xhat_ * c2)
        dz_ref[...] = dz
        dzb_ref[...] = dz.astype(BF16)
        dg_part = jnp.sum(dh * xhat_, axis=0, keepdims=True)
        db_part = jnp.sum(dh, axis=0, keepdims=True)

        @pl.when(pl.program_id(0) == 0)
        def _():
            dg_ref[...] = dg_part
            db_ref[...] = db_part

        @pl.when(pl.program_id(0) > 0)
        def _():
            dg_ref[...] += dg_part
            db_ref[...] += db_part

    blk = pl.BlockSpec((ts, d), lambda i: (i, 0))
    vec = pl.BlockSpec((1, d), lambda i: (0, 0))
    col = pl.BlockSpec((ts, 1), lambda i: (i, 0))
    ins = ([dres] if has_res else []) + [dmm, xhat, rstd, g]
    in_specs = ([blk] if has_res else []) + [blk, blk, col, vec]
    return pl.pallas_call(
        body, name=name, grid=(s // ts,),
        in_specs=in_specs, out_specs=[blk, blk, vec, vec],
        out_shape=[jax.ShapeDtypeStruct((s, d), F32), jax.ShapeDtypeStruct((s, d), BF16),
                   jax.ShapeDtypeStruct((1, d), F32), jax.ShapeDtypeStruct((1, d), F32)],
        compiler_params=_cparams(1),
    )(*ins)


def _loss_head(h, tgt, *, name):
    s, d = h.shape
    ts = _tile(s, 256, 16)

    def body(h_ref, t_ref, dh_ref, loss_ref):
        err = h_ref[...] - t_ref[...]
        dh_ref[...] = err * (1.0 / d)
        part = 0.5 * jnp.sum(jnp.mean(err * err, axis=-1, keepdims=True), axis=0, keepdims=True)

        @pl.when(pl.program_id(0) == 0)
        def _():
            loss_ref[...] = part

        @pl.when(pl.program_id(0) > 0)
        def _():
            loss_ref[...] += part

    blk = pl.BlockSpec((ts, d), lambda i: (i, 0))
    return pl.pallas_call(
        body, name=name, grid=(s // ts,),
        in_specs=[blk, blk], out_specs=[blk, pl.BlockSpec((1, 1), lambda i: (0, 0))],
        out_shape=[jax.ShapeDtypeStruct((s, d), F32), jax.ShapeDtypeStruct((1, 1), F32)],
        compiler_params=_cparams(1),
    )(h, tgt)


def _conv(ext, w, bias):
    c = bias + _down(ext, 2) * w[0:1] + _down(ext, 1) * w[1:2] + ext * w[2:3]
    return c[CONV_HALO:]


def _act_specs(s, n2):
    n_strips = pl.cdiv(n2, LANES)
    u_spec = pl.BlockSpec((None, 2, s, LANES), lambda hh, j: (hh, 0, 0, j))
    cwg = pl.BlockSpec((None, 3, LANES), lambda hh, j: (hh, 0, j))
    cwv = pl.BlockSpec((None, 3, LANES), lambda hh, j: (hh + 2, 0, j))
    cbg = pl.BlockSpec((None, 1, LANES), lambda hh, j: (hh, 0, j))
    cbv = pl.BlockSpec((None, 1, LANES), lambda hh, j: (hh + 2, 0, j))
    return n_strips, u_spec, cwg, cwv, cbg, cbv


def _act_fwd(u, cw, cb, *, name, rows=64):
    _, _, s, n2 = u.shape
    n_strips, u_spec, cwg, cwv, cbg, cbv = _act_specs(s, n2)

    def body(u_ref, wg_ref, wv_ref, bg_ref, bv_ref, a_ref):
        wg, wv, bg, bv = wg_ref[...], wv_ref[...], bg_ref[...], bv_ref[...]

        def chunk(c, carry):
            r0 = pl.multiple_of(c * rows, rows)
            cg = _conv(_causal_ext(lambda r, n: u_ref[0, pl.ds(r, n), :], r0, rows, CONV_HALO), wg, bg)
            cv = _conv(_causal_ext(lambda r, n: u_ref[1, pl.ds(r, n), :], r0, rows, CONV_HALO), wv, bv)
            a_ref[pl.ds(r0, rows), :] = (cg * _sigmoid(cg) * cv).astype(BF16)
            return carry

        lax.fori_loop(0, s // rows, chunk, 0)

    return pl.pallas_call(
        body, name=name, grid=(2, n_strips),
        in_specs=[u_spec, cwg, cwv, cbg, cbv],
        out_specs=pl.BlockSpec((None, s, LANES), lambda hh, j: (hh, 0, j)),
        out_shape=jax.ShapeDtypeStruct((2, s, n2), BF16),
        compiler_params=_cparams(2),
    )(u, cw, cw, cb, cb)


def _act_bwd(u, da, cw, cb, *, name, rows=64):
    _, _, s, n2 = u.shape
    n_strips, u_spec, cwg, cwv, cbg, cbv = _act_specs(s, n2)
    n_chunks = s // rows

    def body(u_ref, da_ref, wg_ref, wv_ref, bg_ref, bv_ref, du_ref, dwg_ref, dwv_ref, dbg_ref, dbv_ref, dg_s, dv_s):
        wg, wv, bg, bv = wg_ref[...], wv_ref[...], bg_ref[...], bv_ref[...]

        def first(c, sums):
            r0 = pl.multiple_of(c * rows, rows)
            eg = _causal_ext(lambda r, n: u_ref[0, pl.ds(r, n), :], r0, rows, CONV_HALO)
            ev = _causal_ext(lambda r, n: u_ref[1, pl.ds(r, n), :], r0, rows, CONV_HALO)
            cg, cv = _conv(eg, wg, bg), _conv(ev, wv, bv)
            sg = _sigmoid(cg)
            dact = da_ref[pl.ds(r0, rows), :]
            dval = dact * (cg * sg)
            dgate = dact * cv * (sg * (1.0 + cg * (1.0 - sg)))
            dg_s[pl.ds(r0, rows), :] = dgate
            dv_s[pl.ds(r0, rows), :] = dval
            new = []
            for dc, ext in ((dgate, eg), (dval, ev)):
                new += [_fold8(dc * _down(ext, 2)[CONV_HALO:]), _fold8(dc * _down(ext, 1)[CONV_HALO:]),
                        _fold8(dc * ext[CONV_HALO:]), _fold8(dc)]
            return tuple(acc + x for acc, x in zip(sums, new))

        zero = jnp.zeros((8, LANES), F32)
        sums = lax.fori_loop(0, n_chunks, first, (zero,) * 8)
        red = [jnp.sum(x, axis=0, keepdims=True) for x in sums]
        dwg_ref[...] = jnp.concatenate(red[0:3], axis=0)
        dbg_ref[...] = red[3]
        dwv_ref[...] = jnp.concatenate(red[4:7], axis=0)
        dbv_ref[...] = red[7]

        def second(c, carry):
            r0 = pl.multiple_of(c * rows, rows)
            for gv, (src, w) in enumerate(((dg_s, wg), (dv_s, wv))):
                ext = _anti_ext(lambda r, n: src[pl.ds(r, n), :], r0, rows, CONV_HALO, s)
                du = ext * w[2:3] + _up(ext, 1) * w[1:2] + _up(ext, 2) * w[0:1]
                du_ref[gv, pl.ds(r0, rows), :] = du[:rows].astype(BF16)
            return carry

        lax.fori_loop(0, n_chunks, second, 0)

    w_out = pl.BlockSpec((None, 3, LANES), lambda hh, j: (hh, 0, j))
    b_out = pl.BlockSpec((None, 1, LANES), lambda hh, j: (hh, 0, j))
    return pl.pallas_call(
        body, name=name, grid=(2, n_strips),
        in_specs=[u_spec, pl.BlockSpec((None, s, LANES), lambda hh, j: (hh, 0, j)), cwg, cwv, cbg, cbv],
        out_specs=[u_spec, w_out, w_out, b_out, b_out],
        out_shape=[jax.ShapeDtypeStruct((2, 2, s, n2), BF16),
                   jax.ShapeDtypeStruct((2, 3, n2), F32), jax.ShapeDtypeStruct((2, 3, n2), F32),
                   jax.ShapeDtypeStruct((2, 1, n2), F32), jax.ShapeDtypeStruct((2, 1, n2), F32)],
        scratch_shapes=[pltpu.VMEM((s, LANES), F32), pltpu.VMEM((s, LANES), F32)],
        compiler_params=_cparams(2),
    )(u, da, cw, cw, cb, cb)


def _dot_nt(a, b):
    return lax.dot_general(a, b, _DIMS["nt"], preferred_element_type=F32)


def _dot_tn(a, b):
    return lax.dot_general(a, b, _DIMS["tn"], preferred_element_type=F32)


def _band_masks(b):
    qi = lax.broadcasted_iota(jnp.int32, (ATTN_BLOCK, ATTN_BLOCK), 0)
    kj = lax.broadcasted_iota(jnp.int32, (ATTN_BLOCK, ATTN_BLOCK), 1)
    return kj <= qi, jnp.logical_and(kj >= qi, b > 0)


def _attn_fwd(q, k, v, d, *, name):
    s, dm = q.shape
    heads, seq, scale = dm // HEAD_DIM, s // d, 1.0 / math.sqrt(HEAD_DIM)
    nb = seq // ATTN_BLOCK
    assert nb * ATTN_BLOCK * d == s and heads <= LANES

    def body(q_ref, kc_ref, kp_ref, vc_ref, vp_ref, o_ref, l_ref):
        mask_c, mask_p = _band_masks(pl.program_id(1))
        lane = lax.broadcasted_iota(jnp.int32, (ATTN_BLOCK, LANES), 1)
        lse_all = jnp.zeros((ATTN_BLOCK, LANES), F32)
        for h in range(heads):
            hs = slice(h * HEAD_DIM, (h + 1) * HEAD_DIM)
            qh = q_ref[:, hs]
            sc = jnp.where(mask_c, _dot_nt(qh, kc_ref[:, hs]) * scale, NEG_INF)
            sp = jnp.where(mask_p, _dot_nt(qh, kp_ref[:, hs]) * scale, NEG_INF)
            m = jnp.maximum(jnp.max(sc, axis=-1, keepdims=True), jnp.max(sp, axis=-1, keepdims=True))
            pc, pp = jnp.exp(sc - m), jnp.exp(sp - m)
            den = jnp.sum(pc, axis=-1, keepdims=True) + jnp.sum(pp, axis=-1, keepdims=True)
            acc = (jnp.dot(pc.astype(BF16), vc_ref[:, hs], preferred_element_type=F32)
                   + jnp.dot(pp.astype(BF16), vp_ref[:, hs], preferred_element_type=F32))
            o_ref[:, hs] = acc / den
            lse_all = jnp.where(lane == h, m + jnp.log(den), lse_all)
        l_ref[...] = lse_all

    cur = pl.BlockSpec((ATTN_BLOCK, dm), lambda r, b: (b, r))
    prev = pl.BlockSpec((ATTN_BLOCK, dm), lambda r, b: (jnp.maximum(b - 1, 0), r))
    views = [t.reshape(seq, d * dm) for t in (q, k, k, v, v)]
    o, lse = pl.pallas_call(
        body, name=name, grid=(d, nb),
        in_specs=[cur, cur, prev, cur, prev],
        out_specs=[cur, pl.BlockSpec((ATTN_BLOCK, LANES), lambda r, b: (b, r))],
        out_shape=[jax.ShapeDtypeStruct((seq, d * dm), F32), jax.ShapeDtypeStruct((seq, d * LANES), F32)],
        compiler_params=_cparams(2),
    )(*views)
    return o.reshape(s, dm), lse.reshape(s, LANES)


def _attn_merge(outs, lses, *, name):
    s, dm = outs[0].shape
    heads = dm // HEAD_DIM
    ts = _tile(s, 128, 16)

    def body(o1, o2, o3, l1, l2, l3, o_ref, ob_ref, lt_ref):
        ls = [l1[...], l2[...], l3[...]]
        m = jnp.maximum(jnp.maximum(ls[0], ls[1]), ls[2])
        ws = [jnp.exp(x - m) for x in ls]
        tot = ws[0] + ws[1] + ws[2]
        lt_ref[...] = m + jnp.log(tot)
        ws = [w / tot for w in ws]
        for h in range(heads):
            hs = slice(h * HEAD_DIM, (h + 1) * HEAD_DIM)
            o = (ws[0][:, h:h + 1] * o1[:, hs] + ws[1][:, h:h + 1] * o2[:, hs]) + ws[2][:, h:h + 1] * o3[:, hs]
            o_ref[:, hs] = o
            ob_ref[:, hs] = o.astype(BF16)

    blk = pl.BlockSpec((ts, dm), lambda i: (i, 0))
    st = pl.BlockSpec((ts, LANES), lambda i: (i, 0))
    return pl.pallas_call(
        body, name=name, grid=(s // ts,),
        in_specs=[blk] * 3 + [st] * 3, out_specs=[blk, blk, st],
        out_shape=[jax.ShapeDtypeStruct((s, dm), F32), jax.ShapeDtypeStruct((s, dm), BF16),
                   jax.ShapeDtypeStruct((s, LANES), F32)],
        compiler_params=_cparams(1),
    )(*outs, *lses)


def _attn_delta(do, o, *, name):
    s, dm = o.shape
    heads = dm // HEAD_DIM
    ts = _tile(s, 256, 16)

    def body(do_ref, o_ref, dl_ref):
        lane = lax.broadcasted_iota(jnp.int32, (ts, LANES), 1)
        acc = jnp.zeros((ts, LANES), F32)
        for h in range(heads):
            hs = slice(h * HEAD_DIM, (h + 1) * HEAD_DIM)
            row = jnp.sum(do_ref[:, hs].astype(F32) * o_ref[:, hs], axis=-1, keepdims=True)
            acc = jnp.where(lane == h, row, acc)
        dl_ref[...] = acc

    blk = pl.BlockSpec((ts, dm), lambda i: (i, 0))
    return pl.pallas_call(
        body, name=name, grid=(s // ts,),
        in_specs=[blk, blk], out_specs=pl.BlockSpec((ts, LANES), lambda i: (i, 0)),
        out_shape=jax.ShapeDtypeStruct((s, LANES), F32),
        compiler_params=_cparams(1),
    )(do, o)


def _attn_bwd(q, k, v, do, lse, delta, acc_in, d, *, name):
    s, dm = q.shape
    heads, seq, scale = dm // HEAD_DIM, s // d, 1.0 / math.sqrt(HEAD_DIM)
    nb = seq // ATTN_BLOCK
    has_acc = acc_in is not None

    def body(*refs):
        q_ref, do_ref, l_ref, dl_ref, kc_ref, kp_ref, vc_ref, vp_ref = refs[:8]
        acc_ref = refs[8] if has_acc else None
        out_ref, carry, new, prevc = refs[-4:]
        b = pl.program_id(1)

        @pl.when(b < nb)
        def _():
            mask_c, mask_p = _band_masks(b)
            lse_blk, dl_blk = l_ref[...], dl_ref[...]
            for h in range(heads):
                hs = slice(h * HEAD_DIM, (h + 1) * HEAD_DIM)
                qh, doh = q_ref[:, hs], do_ref[:, hs]
                kc, kp, vc, vp = kc_ref[:, hs], kp_ref[:, hs], vc_ref[:, hs], vp_ref[:, hs]
                lse_h, dl_h = lse_blk[:, h:h + 1], dl_blk[:, h:h + 1]
                sc = jnp.where(mask_c, _dot_nt(qh, kc) * scale, NEG_INF)
                sp = jnp.where(mask_p, _dot_nt(qh, kp) * scale, NEG_INF)
                pc, pp = jnp.exp(sc - lse_h), jnp.exp(sp - lse_h)
                dsc = (pc * (_dot_nt(doh, vc) - dl_h) * scale).astype(BF16)
                dsp = (pp * (_dot_nt(doh, vp) - dl_h) * scale).astype(BF16)
                new[0, :, hs] = (jnp.dot(dsc, kc, preferred_element_type=F32)
                                 + jnp.dot(dsp, kp, preferred_element_type=F32))
                new[1, :, hs] = _dot_tn(dsc, qh)
                new[2, :, hs] = _dot_tn(pc.astype(BF16), doh)
                prevc[0, :, hs] = _dot_tn(dsp, qh)
                prevc[1, :, hs] = _dot_tn(pp.astype(BF16), doh)

        @pl.when(b == nb)
        def _():
            prevc[...] = jnp.zeros_like(prevc)

        @pl.when(b > 0)
        def _():
            for i in range(3):
                val = carry[i]
                if i > 0:
                    val = val + prevc[i - 1]
                if has_acc:
                    val = val + acc_ref[i]
                out_ref[i] = val

        @pl.when(b < nb)
        def _():
            carry[...] = new[...]

    def cb(b):
        return jnp.minimum(b, nb - 1)

    cur = pl.BlockSpec((ATTN_BLOCK, dm), lambda r, b: (cb(b), r))
    prev = pl.BlockSpec((ATTN_BLOCK, dm), lambda r, b: (jnp.maximum(cb(b) - 1, 0), r))
    stat = pl.BlockSpec((ATTN_BLOCK, LANES), lambda r, b: (cb(b), r))
    out = pl.BlockSpec((3, ATTN_BLOCK, dm), lambda r, b: (0, jnp.maximum(b - 1, 0), r))
    wide = lambda t: t.reshape(seq, d * dm)
    ins = [wide(q), wide(do), lse.reshape(seq, d * LANES), delta.reshape(seq, d * LANES),
           wide(k), wide(k), wide(v), wide(v)]
    in_specs = [cur, cur, stat, stat, cur, prev, cur, prev]
    if has_acc:
        ins.append(acc_in.reshape(3, seq, d * dm))
        in_specs.append(out)
    res = pl.pallas_call(
        body, name=name, grid=(d, nb + 1),
        in_specs=in_specs, out_specs=out,
        out_shape=jax.ShapeDtypeStruct((3, seq, d * dm), F32),
        scratch_shapes=[pltpu.VMEM((3, ATTN_BLOCK, dm), F32), pltpu.VMEM((3, ATTN_BLOCK, dm), F32),
                        pltpu.VMEM((2, ATTN_BLOCK, dm), F32)],
        compiler_params=_cparams(2),
    )(*ins)
    return res.reshape(3, s, dm)


def _ew(fn, ins, out_dtypes, *, name, tile_bytes=1 << 20):
    first = ins[0][0] if isinstance(ins[0], tuple) else ins[0]
    rows, cols = first.shape[-2], first.shape[-1]
    tr = _tile(rows, max(16, tile_bytes // (4 * cols)), 16)
    n_in = len(ins)

    def body(*refs):
        outs = fn(*[r[...] for r in refs[:n_in]])
        for o_ref, val in zip(refs[n_in:], outs):
            o_ref[...] = val.astype(o_ref.dtype)

    in_specs, args = [], []
    for item in ins:
        if isinstance(item, tuple):
            arr, lead = item
            in_specs.append(pl.BlockSpec((None, tr, cols), lambda i, lead=lead: (lead, i, 0)))
            args.append(arr)
        else:
            in_specs.append(pl.BlockSpec((tr, cols), lambda i: (i, 0)))
            args.append(item)
    blk = pl.BlockSpec((tr, cols), lambda i: (i, 0))
    return pl.pallas_call(
        body, name=name, grid=(rows // tr,),
        in_specs=in_specs, out_specs=[blk] * len(out_dtypes),
        out_shape=[jax.ShapeDtypeStruct((rows, cols), dt) for dt in out_dtypes],
        compiler_params=_cparams(1),
    )(*args)


def _adamw_math(g, w, m, v):
    m2 = ADAM_B1 * m + (1.0 - ADAM_B1) * g
    v2 = ADAM_B2 * v + (1.0 - ADAM_B2) * (g * g)
    m_hat = m2 / (1.0 - ADAM_B1 ** ADAM_STEP)
    v_hat = v2 / (1.0 - ADAM_B2 ** ADAM_STEP)
    delta = -ADAM_LR * (m_hat / (jnp.sqrt(v_hat) + ADAM_EPS) + ADAM_WD * w)
    return g, delta, m2, v2


def _scalars(*vals):
    return jnp.stack([jnp.asarray(v, jnp.int32) for v in vals])


def _adamw_halves(mine, theirs, core, w, m, v, *, name):
    shape = w.shape
    a_n, rh, cols = mine.shape
    w3, m3, v3 = (t.reshape(a_n, 2 * rh, cols) for t in (w, m, v))
    tr = _tile(rh, max(16, (1 << 19) // (4 * cols)), 8)
    n_i = rh // tr

    def body(c_ref, mine_ref, theirs_ref, w_ref, m_ref, v_ref, g_ref, d_ref, m2_ref, v2_ref):
        g = jnp.where(pl.program_id(1) == c_ref[0], mine_ref[...], theirs_ref[...])
        outs = _adamw_math(g, w_ref[...], m_ref[...], v_ref[...])
        for ref, val in zip((g_ref, d_ref, m2_ref, v2_ref), outs):
            ref[...] = val

    half = pl.BlockSpec((None, tr, cols), lambda a, h, i, c_ref: (a, i, 0))
    full = pl.BlockSpec((None, tr, cols), lambda a, h, i, c_ref: (a, h * n_i + i, 0))
    outs = pl.pallas_call(
        body, name=name,
        grid_spec=pltpu.PrefetchScalarGridSpec(
            num_scalar_prefetch=1, grid=(a_n, 2, n_i),
            in_specs=[half, half, full, full, full], out_specs=[full] * 4),
        out_shape=[jax.ShapeDtypeStruct(w3.shape, F32)] * 4,
        compiler_params=_cparams(3),
    )(_scalars(core), mine, theirs, w3, m3, v3)
    return [o.reshape(shape) for o in outs]


def _cast_into(src, buf, lead, slot, *, name, buf_shape=None, dtype=BF16):
    a_n, rows, cols = src.shape
    tr = _tile(rows, max(16, (1 << 21) // (4 * cols)), 16)

    def body(slot_ref, src_ref, *rest):
        rest[-1][...] = src_ref[...].astype(rest[-1].dtype)

    in_specs = [pl.BlockSpec((None, tr, cols), lambda a, i, slot_ref: (a, i, 0))]
    args = [src]
    aliases = {}
    if buf is not None:
        in_specs.append(ANY)
        args.append(buf)
        aliases = {2: 0}
        buf_shape, dtype = buf.shape, buf.dtype
    return pl.pallas_call(
        body, name=name,
        grid_spec=pltpu.PrefetchScalarGridSpec(
            num_scalar_prefetch=1, grid=(a_n, rows // tr), in_specs=in_specs,
            out_specs=pl.BlockSpec((None, None, tr, cols), lambda a, i, slot_ref: (lead + a, slot_ref[0], i, 0))),
        out_shape=jax.ShapeDtypeStruct(buf_shape, dtype),
        input_output_aliases=aliases,
        compiler_params=_cparams(2),
    )(_scalars(slot), *args)


def _place():
    x, y, c = lax.axis_index("x"), lax.axis_index("y"), lax.axis_index("c")
    chips = [(1 - x, y), (x, 1 - y), (1 - x, 1 - y)]
    return x, y, c, chips


def _remote(src, dst, send_sem, recv_sem, dev):
    return pltpu.make_async_remote_copy(src_ref=src, dst_ref=dst, send_sem=send_sem, recv_sem=recv_sem,
                                        device_id=dev, device_id_type=MESH)


def _all_gather(bufs, *, name):
    n = len(bufs)

    def body(*refs):
        dst = refs[n:2 * n]
        send_sems, recv_sems = refs[2 * n:]
        x, y, c, chips = _place()
        q_me = 2 * x + y

        def half(t, q, core):
            rh = bufs[t].shape[2] // 2
            return dst[t].at[:, q, pl.ds(core * rh, rh), :]

        sent = []
        for j, (cx, cy) in enumerate(chips):
            for t in range(n):
                cp = _remote(half(t, q_me, c), half(t, q_me, c), send_sems.at[6 * t + j], recv_sems.at[6 * t + j],
                             (cx, cy, c))
                cp.start()
                sent.append(cp)
        for j, (cx, cy) in enumerate(chips):
            q_j = 2 * cx + cy
            for t in range(n):
                k = 6 * t + j
                _remote(half(t, q_j, c), half(t, q_j, c), send_sems.at[k], recv_sems.at[k], (cx, cy, c)).wait_recv()
                cp = _remote(half(t, q_j, c), half(t, q_j, c), send_sems.at[k + 3], recv_sems.at[k + 3], (x, y, 1 - c))
                cp.start()
                sent.append(cp)
        for j, (cx, cy) in enumerate(chips):
            q_j = 2 * cx + cy
            for t in range(n):
                k = 6 * t + j + 3
                _remote(half(t, q_j, 1 - c), half(t, q_j, 1 - c), send_sems.at[k], recv_sems.at[k], (x, y, 1 - c)).wait_recv()
        for cp in sent:
            cp.wait_send()

    return pl.pallas_call(
        body, name=name, in_specs=[ANY] * n, out_specs=[ANY] * n,
        out_shape=[jax.ShapeDtypeStruct(b.shape, b.dtype) for b in bufs],
        input_output_aliases={t: t for t in range(n)},
        scratch_shapes=[pltpu.SemaphoreType.DMA((6 * n,)), pltpu.SemaphoreType.DMA((6 * n,))],
    )(*bufs)


def _rs_sibling(grads, *, name):
    n = len(grads)
    halves = [jax.ShapeDtypeStruct(g.shape[:2] + (g.shape[2] // 2, g.shape[3]), g.dtype) for g in grads]

    def body(*refs):
        src, theirs = refs[:n], refs[n:2 * n]
        send_sems, recv_sems = refs[2 * n:]
        x, y, c, _ = _place()
        ops = []
        for t in range(n):
            rh = grads[t].shape[2] // 2
            give = _remote(src[t].at[:, :, pl.ds((1 - c) * rh, rh), :], theirs[t], send_sems.at[t], recv_sems.at[t],
                           (x, y, 1 - c))
            give.start()
            ops.append(give)
        for op in ops:
            op.wait()

    return pl.pallas_call(
        body, name=name, in_specs=[ANY] * n, out_specs=[ANY] * n, out_shape=halves,
        scratch_shapes=[pltpu.SemaphoreType.DMA((n,)), pltpu.SemaphoreType.DMA((n,))],
    )(*grads)


def _rs_add(grad, theirs, core, slot, *, name):
    a_n, _, rh, cols = theirs.shape
    tr = _tile(rh, max(16, (1 << 20) // (4 * cols)), 16)
    n_i = rh // tr

    def body(s_ref, g_ref, t_ref, p_ref, y_ref):
        part = (g_ref[...].astype(F32) + t_ref[...].astype(F32)).astype(BF16)
        p_ref[...] = part

        @pl.when(pl.program_id(2) == s_ref[1])
        def _():
            y_ref[...] = part

    blk = (None, None, tr, cols)
    return pl.pallas_call(
        body, name=name,
        grid_spec=pltpu.PrefetchScalarGridSpec(
            num_scalar_prefetch=1, grid=(a_n, n_i, N_CHIPS),
            in_specs=[pl.BlockSpec(blk, lambda a, i, q, s: (a, q, s[0] * n_i + i, 0)),
                      pl.BlockSpec(blk, lambda a, i, q, s: (a, q, i, 0))],
            out_specs=[pl.BlockSpec(blk, lambda a, i, q, s: (a, q, i, 0)),
                       pl.BlockSpec(blk, lambda a, i, q, s: (s[1], a, i, 0))]),
        out_shape=[jax.ShapeDtypeStruct(theirs.shape, BF16),
                   jax.ShapeDtypeStruct((N_CHIPS, a_n, rh, cols), BF16)],
        compiler_params=_cparams(3),
    )(_scalars(core, slot), grad, theirs)


def _rs_chips(parts, lands, *, name):
    n = len(parts)

    def body(*refs):
        src, dst = refs[:n], refs[2 * n:3 * n]
        send_sems, recv_sems = refs[3 * n:]
        x, y, c, chips = _place()
        q_me = 2 * x + y
        sent = []
        for j, (cx, cy) in enumerate(chips):
            for t in range(n):
                cp = _remote(src[t].at[:, 2 * cx + cy], dst[t].at[q_me], send_sems.at[3 * t + j], recv_sems.at[3 * t + j],
                             (cx, cy, c))
                cp.start()
                sent.append(cp)
        for j, (cx, cy) in enumerate(chips):
            for t in range(n):
                _remote(src[t].at[:, q_me], dst[t].at[2 * cx + cy], send_sems.at[3 * t + j], recv_sems.at[3 * t + j],
                        (cx, cy, c)).wait_recv()
        for cp in sent:
            cp.wait_send()

    return pl.pallas_call(
        body, name=name, in_specs=[ANY] * (2 * n), out_specs=[ANY] * n,
        out_shape=[jax.ShapeDtypeStruct(t.shape, t.dtype) for t in lands],
        input_output_aliases={n + t: t for t in range(n)},
        scratch_shapes=[pltpu.SemaphoreType.DMA((3 * n,)), pltpu.SemaphoreType.DMA((3 * n,))],
    )(*parts, *lands)


def _rs_finish(halves, *, name):
    n = len(halves)

    def body(*refs):
        src, dst = refs[:n], refs[n:2 * n]
        send_sems, recv_sems = refs[2 * n:]
        x, y, c, _ = _place()
        ops = []
        for t in range(n):
            give = _remote(src[t], dst[t], send_sems.at[t], recv_sems.at[t], (x, y, 1 - c))
            give.start()
            ops.append(give)
        for op in ops:
            op.wait()

    return pl.pallas_call(
        body, name=name, in_specs=[ANY] * n, out_specs=[ANY] * n,
        out_shape=[jax.ShapeDtypeStruct(h.shape, h.dtype) for h in halves],
        scratch_shapes=[pltpu.SemaphoreType.DMA((n,)), pltpu.SemaphoreType.DMA((n,))],
    )(*halves)


def _all_reduce_small(vec, *, name):
    rows = vec.shape[0]

    def body(v_ref, o_ref, land, send_sems, recv_sems):
        x, y, c, _ = _place()
        me = 4 * x + 2 * y + c
        land[me] = v_ref[...]
        flips = [(fx, fy, fc) for fx in (0, 1) for fy in (0, 1) for fc in (0, 1)][1:]
        sent = []
        for k, (fx, fy, fc) in enumerate(flips):
            cp = _remote(v_ref, land.at[me], send_sems.at[k], recv_sems.at[k], (x ^ fx, y ^ fy, c ^ fc))
            cp.start()
            sent.append(cp)
        for k, (fx, fy, fc) in enumerate(flips):
            peer = 4 * (x ^ fx) + 2 * (y ^ fy) + (c ^ fc)
            _remote(v_ref, land.at[peer], send_sems.at[k], recv_sems.at[k], (x ^ fx, y ^ fy, c ^ fc)).wait_recv()
        for cp in sent:
            cp.wait_send()
        total = land[0]
        for dev in range(1, 8):
            total = total + land[dev]
        o_ref[...] = total

    whole = pl.BlockSpec(memory_space=pltpu.VMEM)
    return pl.pallas_call(
        body, name=name, in_specs=[whole], out_specs=whole,
        out_shape=jax.ShapeDtypeStruct(vec.shape, F32),
        scratch_shapes=[pltpu.VMEM((8, rows, LANES), F32), pltpu.SemaphoreType.DMA((7,)), pltpu.SemaphoreType.DMA((7,))],
        compiler_params=pltpu.CompilerParams(vmem_limit_bytes=VMEM_LIMIT),
    )(vec)


def _pack(parts, mult=16):
    flat = jnp.concatenate([p.reshape(-1).astype(F32) for p in parts])
    rows = -(-flat.shape[0] // (LANES * mult)) * mult
    return jnp.pad(flat, (0, rows * LANES - flat.shape[0])).reshape(rows, LANES)


def _unpack(vec, shapes):
    flat, out, pos = vec.reshape(-1), [], 0
    for shp in shapes:
        size = math.prod(shp)
        out.append(flat[pos:pos + size].reshape(shp))
        pos += size
    return out


def kernel(x, pool_w_in, pool_w_grp, pool_scale, pool_w_out, attn_w_q, attn_w_o, shared_w_k, shared_w_v, ffn_w_up, ffn_conv_w, ffn_conv_b, ffn_w_down, ln1_g, ln1_b, ln2_g, ln2_b, loss_target, m_pool_w_in, m_pool_w_grp, m_pool_scale, m_pool_w_out, m_attn_w_q, m_attn_w_o, m_shared_w_k, m_shared_w_v, m_ffn_w_up, m_ffn_conv_w, m_ffn_conv_b, m_ffn_w_down, m_ln1_g, m_ln1_b, m_ln2_g, m_ln2_b, v_pool_w_in, v_pool_w_grp, v_pool_scale, v_pool_w_out, v_attn_w_q, v_attn_w_o, v_shared_w_k, v_shared_w_v, v_ffn_w_up, v_ffn_conv_w, v_ffn_conv_b, v_ffn_w_down, v_ln1_g, v_ln1_b, v_ln2_g, v_ln2_b):
    s, d = x.shape[1], x.shape[2]
    n2 = ffn_w_up.shape[2]
    fq = ffn_w_down.shape[1]
    assert 2 * fq == n2 and d % N_CHIPS == 0
    g_n, cg = pool_w_grp.shape[1], pool_w_grp.shape[3]
    xs, tgt = x[0], loss_target[0]
    q_me = 2 * lax.axis_index("x") + lax.axis_index("y")
    core = lax.axis_index("c")
    rq = d // N_CHIPS

    six_g = None
    for i, w_ in enumerate((pool_w_in[0], pool_w_out[0], attn_w_q[0], shared_w_k, shared_w_v, attn_w_o[0])):
        six_g = _cast_into(w_[None], six_g, i, q_me, name=f"cast_w{i}", buf_shape=(6, N_CHIPS, rq, d))
    grp_g = _cast_into(pool_w_grp[0], None, 0, q_me, name="cast_grp", buf_shape=(g_n, N_CHIPS, cg // N_CHIPS, cg))
    up_g = _cast_into(ffn_w_up, None, 0, q_me, name="cast_up", buf_shape=(2, N_CHIPS, d, n2))
    dn_g = _cast_into(ffn_w_down, None, 0, q_me, name="cast_down", buf_shape=(2, N_CHIPS, fq, d))
    small = _pack([ffn_conv_w, pool_scale])
    small_g = _cast_into(small[None], None, 0, q_me, name="cast_small", buf_shape=(1, N_CHIPS) + small.shape, dtype=F32)
    six_g, grp_g, up_g, dn_g, small_g = _all_gather([six_g, grp_g, up_g, dn_g, small_g], name="gather_weights")
    w6 = six_g.reshape(6, d, d)
    w_grp = grp_g.reshape(g_n, cg, cg)
    up8 = up_g.reshape(2 * N_CHIPS, d, n2)
    dn4 = dn_g.reshape(4, n2, d)
    small_q = small_g.reshape(N_CHIPS, -1)
    n_cw = 2 * 3 * n2
    conv_w = small_q[:, :n_cw].reshape(N_CHIPS, 2, 3, n2).transpose(1, 0, 2, 3)
    scale_full = small_q[:, n_cw:n_cw + d // N_CHIPS].reshape(1, d)
    conv_b = ffn_conv_b.reshape(2, N_CHIPS, 1, n2)

    def ffn_fwd(l, hb):
        u = _mm(hb[None], up8, mode="nn", name=f"ffn{l}_up", out_dtype=F32, n_q=4, tm=_tile(s, 512), tn=n2,
                tk=_tile(d, 1024), qa=_q0, qb=lambda q, qr: 4 * l + _perm(q)).reshape(2, 2, s, n2)
        act = _act_fwd(u, conv_w[l], conv_b[l], name=f"ffn{l}_act")
        ff = _mm(act, dn4, mode="nn", name=f"ffn{l}_down", out_dtype=F32, n_q=1, n_qr=2, tm=_tile(s, 512),
                 tn=_tile(d, 1024), tk=n2, qa=_qr, qb=lambda q, qr: 2 * l + qr, qo=lambda q: 0)[0]
        return u, act, ff

    p = _mm_dd(xs[None], w6, IW_IN, mode="nn", name="pool_in", out_dtype=F32)[0]
    pooled = _pool(p, backward=False, name="pool_fwd")
    mg, mixed = _grp_fwd(pooled, w_grp, scale_full, name="pool_grp")
    mix0 = _mm_dd(mixed[None], w6, IW_OUT, mode="nn", name="pool_out", out_dtype=F32)[0]
    h1, h1b, xh1, rs1 = _ln_fwd(xs, mix0, ln1_g[0:1], ln1_b[0:1], name="ln1_0")
    u0, act0, ff0 = ffn_fwd(0, h1b)
    h2, h2b, xh2, rs2 = _ln_fwd(h1, ff0, ln2_g[0:1], ln2_b[0:1], name="ln2_0")

    qkv = _mm_dd(h2b[None], w6, IW_Q, mode="nn", name="attn_qkv", out_dtype=BF16, n_q=3)
    branch = [_attn_fwd(qkv[0], qkv[1], qkv[2], dil, name=f"attn_fwd_d{dil}") for dil in DILATIONS]
    o, ob, lse = _attn_merge([br[0] for br in branch], [br[1] for br in branch], name="attn_merge")
    mix1 = _mm_dd(ob[None], w6, IW_O, mode="nn", name="attn_out", out_dtype=F32)[0]
    h3, h3b, xh3, rs3 = _ln_fwd(h2, mix1, ln1_g[1:2], ln1_b[1:2], name="ln1_1")
    u1, act1, ff1 = ffn_fwd(1, h3b)
    h4, _, xh4, rs4 = _ln_fwd(h3, ff1, ln2_g[1:2], ln2_b[1:2], name="ln2_1")
    dh4, loss_local = _loss_head(h4, tgt, name="loss_head")

    g_up = g_dn = None
    d_conv_w, d_conv_b = [None, None], [None, None]

    def ffn_bwd(l, dzb, u, act, hb):
        nonlocal g_up, g_dn
        da = _mm(dzb[None], dn4, mode="nt", name=f"ffn{l}_dact", out_dtype=F32, n_q=2, tm=_tile(s, 512), tn=n2,
                 tk=_tile(d, 512), qa=_q0, qb=lambda q, qr: 2 * l + q)
        du, dwg, dwv, dbg, dbv = _act_bwd(u, da, conv_w[l], conv_b[l], name=f"ffn{l}_dconv")
        d_conv_w[l] = jnp.concatenate([dwg, dwv], axis=0)
        d_conv_b[l] = jnp.concatenate([dbg, dbv], axis=0)
        du4 = du.reshape(4, s, n2)
        g_dn = _mm(act, dzb[None], mode="tn", name=f"ffn{l}_gdown", out_dtype=BF16, n_q=2, tm=n2, tn=_tile(d, 1024),
                   tk=_tile(s, 512), qa=_qq, qb=_q0, qo=lambda q: 2 * l + q, out_q=4, into=g_dn)
        g_up = _mm(hb[None], du4, mode="tn", name=f"ffn{l}_gup", out_dtype=BF16, n_q=4, tm=_tile(d, 512), tn=n2,
                   tk=_tile(s, 1024), qa=_q0, qb=lambda q, qr: _perm(q), qo=lambda q: 4 * l + q, out_q=8, into=g_up)
        return _mm(du4, up8, mode="nt", name=f"ffn{l}_dh", out_dtype=F32, n_q=1, n_qr=4, tm=_tile(s, 512),
                   tn=_tile(d, 1024), tk=n2, qa=_qr, qb=lambda q, qr: 4 * l + _perm(qr), qo=lambda q: 0)[0]

    dz4, dz4b, dg_ln2_1, db_ln2_1 = _ln_bwd(None, dh4, xh4, rs4, ln2_g[1:2], name="dln2_1")
    dh3 = ffn_bwd(1, dz4b, u1, act1, h3b)
    dz3, dz3b, dg_ln1_1, db_ln1_1 = _ln_bwd(dz4, dh3, xh3, rs3, ln1_g[1:2], name="dln1_1")
    g_wo = _mm_wgrad(ob[None], dz3b[None], name="attn_gwo")
    do = _mm_dd(dz3b[None], w6, IW_O, mode="nt", name="attn_do", out_dtype=BF16)[0]
    delta = _attn_delta(do, o, name="attn_delta")
    dqkv = None
    for dil in DILATIONS:
        dqkv = _attn_bwd(qkv[0], qkv[1], qkv[2], do, lse, delta, dqkv, dil, name=f"attn_bwd_d{dil}")
    g_wq = _mm_wgrad(h2b[None], dqkv, name="attn_gwq", qb=lambda q, qr: 0)
    g_wk = _mm_wgrad(h2b[None], dqkv, name="attn_gwk", qb=lambda q, qr: 1)
    g_wv = _mm_wgrad(h2b[None], dqkv, name="attn_gwv", qb=lambda q, qr: 2)
    dh2 = _mm_dd(dqkv, w6, IW_Q, mode="nt", name="attn_dh", out_dtype=F32, n_qr=3, qa=_qr)[0]
    dz2, dz2b, dg_ln2_0, db_ln2_0 = _ln_bwd(dz3, dh2, xh2, rs2, ln2_g[0:1], name="dln2_0")
    dh1 = ffn_bwd(0, dz2b, u0, act0, h1b)
    dz1, dz1b, dg_ln1_0, db_ln1_0 = _ln_bwd(dz2, dh1, xh1, rs1, ln1_g[0:1], name="dln1_0")
    g_wout = _mm_wgrad(mixed[None], dz1b[None], name="pool_gwout")
    dmixed = _mm_dd(dz1b[None], w6, IW_OUT, mode="nt", name="pool_dmixed", out_dtype=F32)[0]
    dmg, d_scale = _grp_bwd_pre(dmixed, mg, scale_full, name="pool_dscale")
    g_wgrp = _grp_mm(pooled, dmg, mode="tn", name="pool_gwgrp", out_dtype=BF16, tm=1024)
    dpooled = _grp_mm(dmg, None, mode="nt", name="pool_dpooled", out_dtype=F32, tm=1024, w=w_grp)
    dp = _pool(dpooled, backward=True, name="pool_bwd")
    g_win = _mm_wgrad(xs[None], dp[None], name="pool_gwin")
    dx_mm = _mm_dd(dp[None], w6, IW_IN, mode="nt", name="pool_dx", out_dtype=F32)[0]
    (grad_x,) = _ew(lambda a, b: (ALPHA * a + b,), [dz1, dx_mm], [F32], name="grad_x")

    big = [
        ("pool_w_in", g_win.reshape(1, N_CHIPS, rq, d), pool_w_in, m_pool_w_in, v_pool_w_in),
        ("pool_w_grp", g_wgrp.reshape(g_n, N_CHIPS, cg // N_CHIPS, cg), pool_w_grp, m_pool_w_grp, v_pool_w_grp),
        ("pool_w_out", g_wout.reshape(1, N_CHIPS, rq, d), pool_w_out, m_pool_w_out, v_pool_w_out),
        ("attn_w_q", g_wq.reshape(1, N_CHIPS, rq, d), attn_w_q, m_attn_w_q, v_attn_w_q),
        ("attn_w_o", g_wo.reshape(1, N_CHIPS, rq, d), attn_w_o, m_attn_w_o, v_attn_w_o),
        ("shared_w_k", g_wk.reshape(1, N_CHIPS, rq, d), shared_w_k, m_shared_w_k, v_shared_w_k),
        ("shared_w_v", g_wv.reshape(1, N_CHIPS, rq, d), shared_w_v, m_shared_w_v, v_shared_w_v),
        ("ffn_w_up", g_up.reshape(2, N_CHIPS, d, n2), ffn_w_up, m_ffn_w_up, v_ffn_w_up),
        ("ffn_w_down", g_dn.reshape(2, N_CHIPS, fq, d), ffn_w_down, m_ffn_w_down, v_ffn_w_down),
    ]
    theirs = _rs_sibling([t[1] for t in big], name="rs_sibling")
    parts, lands = [], []
    for (nm, g_, *_), t_ in zip(big, theirs):
        part, land = _rs_add(g_, t_, core, q_me, name=f"rs_add_{nm}")
        parts.append(part)
        lands.append(land)
    landed = _rs_chips(parts, lands, name="rs_chips")
    finished = []
    for (nm, *_), y4 in zip(big, landed):
        y3 = y4.reshape(N_CHIPS, -1, y4.shape[-1])
        (tot,) = _ew(lambda a0, a1, a2, a3: (((a0.astype(F32) + a1.astype(F32)) + a2.astype(F32)) + a3.astype(F32),),
                     [(y3, 0), (y3, 1), (y3, 2), (y3, 3)], [F32], name=f"rs_sum_{nm}")
        finished.append(tot.reshape(y4.shape[1:]))
    others = _rs_finish(finished, name="rs_finish")
    results = {}
    for (nm, _, w, m, v), mine_, other_ in zip(big, finished, others):
        results[nm] = _adamw_halves(mine_, other_, core, w, m, v, name=f"adamw_{nm}")

    ln_grads = [jnp.concatenate([a, b], axis=0) for a, b in
                ((dg_ln1_0, dg_ln1_1), (db_ln1_0, db_ln1_1), (dg_ln2_0, dg_ln2_1), (db_ln2_0, db_ln2_1))]
    small_shapes = [(2, N_CHIPS, 3, n2), (2, N_CHIPS, n2)] + [(2, d)] * 4 + [(1, d)]
    vec = _pack([jnp.stack(d_conv_w), jnp.stack(d_conv_b)] + ln_grads + [d_scale], mult=8)
    tot = _unpack(_all_reduce_small(vec, name="allreduce_small"), small_shapes)
    g_cw = lax.dynamic_index_in_dim(tot[0], q_me, axis=1, keepdims=False)
    g_cb = tot[1].reshape(2, N_CHIPS * n2)
    g_scale = lax.dynamic_slice_in_dim(tot[6], q_me * rq, rq, axis=1)
    small_names = ["ffn_conv_w", "ffn_conv_b", "ln1_g", "ln1_b", "ln2_g", "ln2_b", "pool_scale"]
    small_g = [g_cw, g_cb, tot[2], tot[3], tot[4], tot[5], g_scale]
    small_w = [ffn_conv_w, ffn_conv_b, ln1_g, ln1_b, ln2_g, ln2_b, pool_scale]
    small_m = [m_ffn_conv_w, m_ffn_conv_b, m_ln1_g, m_ln1_b, m_ln2_g, m_ln2_b, m_pool_scale]
    small_v = [v_ffn_conv_w, v_ffn_conv_b, v_ln1_g, v_ln1_b, v_ln2_g, v_ln2_b, v_pool_scale]
    packed = _ew(_adamw_math, [_pack(small_g, 8), _pack(small_w, 8), _pack(small_m, 8), _pack(small_v, 8)], [F32] * 4,
                 name="adamw_small")
    shapes = [w.shape for w in small_w]
    unpacked = [_unpack(pk, shapes) for pk in packed]
    for i, nm in enumerate(small_names):
        results[nm] = [unpacked[k][i] for k in range(4)]

    loss = lax.psum(loss_local[0, 0], ("x", "y", "c"))
    order = ["pool_w_in", "pool_w_grp", "pool_scale", "pool_w_out", "attn_w_q", "attn_w_o", "shared_w_k", "shared_w_v",
             "ffn_w_up", "ffn_conv_w", "ffn_conv_b", "ffn_w_down", "ln1_g", "ln1_b", "ln2_g", "ln2_b"]
    outs = [loss, grad_x[None]]
    for k in range(4):
        outs += [results[nm][k] for nm in order]
    return tuple(outs)
```

```python
import functools
import math

import jax
import jax.numpy as jnp
from jax import lax
from jax.experimental import pallas as pl
from jax.experimental.pallas import tpu as pltpu

F32 = jnp.float32
BF16 = jnp.bfloat16

LANES = 128
HEAD_DIM = 128
ATTN_BLOCK = 128
DILATIONS = (1, 4, 16)
POOL_WINDOWS = (2, 4, 8, 16)
POOL_HALO = 16
CONV_HALO = 16
DEPTH = 2
ALPHA = (2.0 * DEPTH) ** 0.25
LN_EPS = 1e-5
NEG_INF = -1e30
ADAM_LR = 0.001
ADAM_B1 = 0.9
ADAM_B2 = 0.999
ADAM_EPS = 1e-08
ADAM_WD = 0.01
ADAM_STEP = 10
N_CHIPS = 4
VMEM_LIMIT = 56 * 1024 * 1024
ANY = pl.BlockSpec(memory_space=pl.ANY)
MESH = pl.DeviceIdType.MESH

IW_IN, IW_OUT, IW_Q, IW_K, IW_V, IW_O = range(6)


def _cparams(n_grid):
    return pltpu.CompilerParams(dimension_semantics=("arbitrary",) * n_grid, vmem_limit_bytes=VMEM_LIMIT)


def _tile(dim, pref, align=LANES):
    if dim <= pref:
        return dim
    t = (pref // align) * align
    while t >= align:
        if dim % t == 0:
            return t
        t -= align
    return dim


def _tile2(rows, cols, budget, row_align):
    best = None
    for tc in [cols] + [c for c in range(LANES, cols, LANES) if cols % c == 0]:
        for tr in range(row_align, rows + 1, row_align):
            if rows % tr == 0 and tr * tc <= budget:
                if best is None or (tr * tc, tc) > (best[0] * best[1], best[1]):
                    best = (tr, tc)
    return best if best is not None else (rows, cols)


def _perm(q):
    return (q % 2) * 2 + q // 2


_DIMS = {"nn": (((1,), (0,)), ((), ())), "nt": (((1,), (1,)), ((), ())), "tn": (((0,), (0,)), ((), ()))}


class _Comm:
    def __init__(self, operands, n_sems, start, finish):
        self.operands, self.n_sems, self.start, self.finish = list(operands), n_sems, start, finish


def _mm(a, b, *, mode, name, out_dtype, n_q, tm, tn, tk, qa, qb, qo=lambda q: q, n_qr=1, out_q=None, comm=None):
    if mode == "nn":
        m, kdim, n = a.shape[1], a.shape[2], b.shape[2]
    elif mode == "nt":
        m, kdim, n = a.shape[1], a.shape[2], b.shape[1]
    else:
        kdim, m, n = a.shape[1], a.shape[2], b.shape[2]
    assert m % tm == 0 and n % tn == 0 and kdim % tk == 0, (name, m, n, kdim, tm, tn, tk)
    kr_n = kdim // tk
    nr = n_qr * kr_n
    out_q = n_q if out_q is None else out_q

    def split(r):
        return (r // kr_n, r % kr_n) if n_qr > 1 else (0, r)

    if mode == "tn":
        a_spec = pl.BlockSpec((None, tk, tm), lambda q, i, j, r: (qa(q, split(r)[0]), split(r)[1], i))
    else:
        a_spec = pl.BlockSpec((None, tm, tk), lambda q, i, j, r: (qa(q, split(r)[0]), i, split(r)[1]))
    if mode == "nt":
        b_spec = pl.BlockSpec((None, tn, tk), lambda q, i, j, r: (qb(q, split(r)[0]), j, split(r)[1]))
    else:
        b_spec = pl.BlockSpec((None, tk, tn), lambda q, i, j, r: (qb(q, split(r)[0]), split(r)[1], j))
    o_spec = pl.BlockSpec((None, tm, tn), lambda q, i, j, r: (qo(q), i, j))
    dims = _DIMS[mode]

    n_comm = len(comm.operands) if comm is not None else 0
    grid = (n_q, m // tm, n // tn, nr)

    def body(*refs):
        a_ref, b_ref = refs[0], refs[1]
        o_ref = refs[2 + n_comm]
        if comm is not None:
            comm_refs = refs[3 + n_comm:3 + 2 * n_comm]
            send_sems, recv_sems = refs[-2:]
            ids = [pl.program_id(ax) for ax in range(4)]
            first = functools.reduce(jnp.logical_and, [i == 0 for i in ids])
            last = functools.reduce(jnp.logical_and, [i == g - 1 for i, g in zip(ids, grid)])

            @pl.when(first)
            def _():
                comm.start(comm_refs, send_sems, recv_sems)
        lhs, rhs = a_ref[...], b_ref[...]
        if lhs.dtype != BF16:
            lhs = lhs.astype(BF16)
        if rhs.dtype != BF16:
            rhs = rhs.astype(BF16)
        part = lax.dot_general(lhs, rhs, dims, preferred_element_type=F32)
        if nr == 1:
            o_ref[...] = part.astype(o_ref.dtype)
        else:
            acc_ref = refs[3 + 2 * n_comm]
            r = pl.program_id(3)

            @pl.when(r == 0)
            def _():
                acc_ref[...] = part

            @pl.when(r > 0)
            def _():
                acc_ref[...] += part

            @pl.when(r == nr - 1)
            def _():
                o_ref[...] = acc_ref[...].astype(o_ref.dtype)
        if comm is not None:
            @pl.when(last)
            def _():
                comm.finish(comm_refs, send_sems, recv_sems)

    scratch = [pltpu.VMEM((tm, tn), F32)] if nr > 1 else []
    out_shape = [jax.ShapeDtypeStruct((out_q, m, n), out_dtype)]
    args = [a, b]
    if comm is not None:
        args += comm.operands
        out_shape += [jax.ShapeDtypeStruct(t.shape, t.dtype) for t in comm.operands]
        scratch += [pltpu.SemaphoreType.DMA((comm.n_sems,)), pltpu.SemaphoreType.DMA((comm.n_sems,))]
    outs = pl.pallas_call(
        body, name=name, grid=grid,
        in_specs=[a_spec, b_spec] + [ANY] * n_comm, out_specs=[o_spec] + [ANY] * n_comm,
        out_shape=out_shape, scratch_shapes=scratch,
        input_output_aliases={2 + i: 1 + i for i in range(n_comm)},
        compiler_params=_cparams(4),
    )(*args)
    return (outs[0], outs[1:]) if comm is not None else outs[0]


def _q0(q, qr):
    return 0


def _qq(q, qr):
    return q


def _qr(q, qr):
    return qr


def _mm_dd(a3, w6, widx, *, mode, name, out_dtype, n_q=1, n_qr=1, qa=_q0, comm=None):
    d = w6.shape[1]
    f32_in = a3.dtype != BF16
    tm = _tile(a3.shape[1] if mode != "tn" else a3.shape[2], 512 if f32_in else 1024)
    if n_qr > 1:
        qb = lambda q, qr: widx + qr
    elif n_q > 1:
        qb = lambda q, qr: widx + q
    else:
        qb = lambda q, qr: widx
    return _mm(a3, w6, mode=mode, name=name, out_dtype=out_dtype, n_q=n_q, n_qr=n_qr,
               tm=tm, tn=_tile(d, 1024), tk=d, qa=qa, qb=qb, comm=comm)


def _mm_wgrad(a3, b3, *, name, n_q=1, qa=_q0, qb=_q0):
    s, d = a3.shape[1], a3.shape[2]
    f32_in = a3.dtype != BF16 or b3.dtype != BF16
    return _mm(a3, b3, mode="tn", name=name, out_dtype=BF16, n_q=n_q, tm=_tile(d, 1024), tn=b3.shape[2],
               tk=_tile(s, 512 if f32_in else 1024), qa=qa, qb=qb)


def _grp_mm(a, b, *, mode, name, out_dtype, tm, w=None):
    s = a.shape[0]
    if mode == "tn":
        cg = b.shape[1] // len(POOL_WINDOWS)
        g_n = len(POOL_WINDOWS)
        ts = _tile(s, tm)
        nr = s // ts

        def body(a_ref, b_ref, o_ref, acc_ref):
            r = pl.program_id(1)
            part = lax.dot_general(a_ref[...], b_ref[...], _DIMS["tn"], preferred_element_type=F32)

            @pl.when(r == 0)
            def _():
                acc_ref[...] = part

            @pl.when(r > 0)
            def _():
                acc_ref[...] += part

            @pl.when(r == nr - 1)
            def _():
                o_ref[...] = acc_ref[...].astype(o_ref.dtype)

        return pl.pallas_call(
            body, name=name, grid=(g_n, nr),
            in_specs=[pl.BlockSpec((ts, cg), lambda g, r: (r, g)), pl.BlockSpec((ts, cg), lambda g, r: (r, g))],
            out_specs=pl.BlockSpec((None, cg, cg), lambda g, r: (g, 0, 0)),
            out_shape=jax.ShapeDtypeStruct((g_n, cg, cg), out_dtype),
            scratch_shapes=[pltpu.VMEM((cg, cg), F32)],
            compiler_params=_cparams(2),
        )(a, b)
    g_n, cg = w.shape[0], w.shape[1]
    ts = _tile(s, tm)
    dims = _DIMS[mode]

    def body(a_ref, w_ref, o_ref):
        o_ref[...] = lax.dot_general(a_ref[...], w_ref[...], dims, preferred_element_type=F32).astype(o_ref.dtype)

    return pl.pallas_call(
        body, name=name, grid=(g_n, s // ts),
        in_specs=[pl.BlockSpec((ts, cg), lambda g, i: (i, g)), pl.BlockSpec((None, cg, cg), lambda g, i: (g, 0, 0))],
        out_specs=pl.BlockSpec((ts, cg), lambda g, i: (i, g)),
        out_shape=jax.ShapeDtypeStruct((s, g_n * cg), out_dtype),
        compiler_params=_cparams(2),
    )(a, w)


def _causal_ext(load, r0, rows, halo):
    cur = load(r0, rows)
    prev = load(pl.multiple_of(jnp.maximum(r0 - halo, 0), halo), halo)
    prev = jnp.where(r0 > 0, prev, jnp.zeros_like(prev))
    return jnp.concatenate([prev, cur], axis=0)


def _anti_ext(load, r0, rows, halo, s):
    cur = load(r0, rows)
    nxt = load(pl.multiple_of(jnp.minimum(r0 + rows, s - halo), halo), halo)
    nxt = jnp.where(r0 + rows < s, nxt, jnp.zeros_like(nxt))
    return jnp.concatenate([cur, nxt], axis=0)


def _down(ext, k):
    return pltpu.roll(ext, k, axis=0)


def _up(ext, k):
    return pltpu.roll(ext, ext.shape[0] - k, axis=0)


def _fold8(x):
    return jnp.sum(x.reshape(x.shape[0] // 8, 8, x.shape[1]), axis=0)


def _sigmoid(x):
    return 1.0 / (1.0 + jnp.exp(-x))


def _pool(p, *, backward, name, rows=64):
    s, d = p.shape
    strips_per_group = (d // len(POOL_WINDOWS)) // LANES
    assert strips_per_group * LANES * len(POOL_WINDOWS) == d and s % rows == 0

    def body(p_ref, o_ref):
        g = pl.program_id(0) // strips_per_group
        win = jnp.left_shift(2, g).astype(F32)

        def load(r0, n):
            return p_ref[pl.ds(r0, n), :]

        def pick(levels):
            return jnp.where(g == 0, levels[0], jnp.where(g == 1, levels[1], jnp.where(g == 2, levels[2], levels[3])))

        def chunk(c, carry):
            r0 = pl.multiple_of(c * rows, rows)
            if not backward:
                ext = _causal_ext(load, r0, rows, POOL_HALO)
                levels, acc = [], ext
                for k in (1, 2, 4, 8):
                    acc = acc + _down(acc, k)
                    levels.append(acc)
                t = (r0 + lax.broadcasted_iota(jnp.int32, (rows, LANES), 0)).astype(F32)
                cnt = jnp.minimum(t + 1.0, win)
                out = pick(levels)[POOL_HALO:] / cnt - ext[POOL_HALO:]
            else:
                ext = _anti_ext(load, r0, rows, POOL_HALO, s)
                t = (r0 + lax.broadcasted_iota(jnp.int32, (rows + POOL_HALO, LANES), 0)).astype(F32)
                e = ext / jnp.minimum(t + 1.0, win)
                levels, acc = [], e
                for k in (1, 2, 4, 8):
                    acc = acc + _up(acc, k)
                    levels.append(acc)
                out = pick(levels)[:rows] - ext[:rows]
            o_ref[pl.ds(r0, rows), :] = out.astype(o_ref.dtype)
            return carry

        lax.fori_loop(0, s // rows, chunk, 0)

    return pl.pallas_call(
        body, name=name, grid=(d // LANES,),
        in_specs=[pl.BlockSpec((s, LANES), lambda j: (0, j))],
        out_specs=pl.BlockSpec((s, LANES), lambda j: (0, j)),
        out_shape=jax.ShapeDtypeStruct((s, d), BF16),
        compiler_params=_cparams(1),
    )(p)


def _grp_fwd(pooled, w_grp, scale, *, name):
    s, d = pooled.shape
    g_n, cg = w_grp.shape[0], w_grp.shape[1]
    ts = _tile(s, 1024)

    def body(a_ref, w_ref, sc_ref, mg_ref, mx_ref):
        mg = jnp.dot(a_ref[...], w_ref[...], preferred_element_type=F32)
        mg_ref[...] = mg.astype(BF16)
        mx_ref[...] = (mg * sc_ref[...]).astype(BF16)

    blk = pl.BlockSpec((ts, cg), lambda g, i: (i, g))
    return pl.pallas_call(
        body, name=name, grid=(g_n, s // ts),
        in_specs=[blk, pl.BlockSpec((None, cg, cg), lambda g, i: (g, 0, 0)), pl.BlockSpec((1, cg), lambda g, i: (0, g))],
        out_specs=[blk, blk],
        out_shape=[jax.ShapeDtypeStruct((s, d), BF16)] * 2,
        compiler_params=_cparams(2),
    )(pooled, w_grp, scale)


def _grp_bwd_pre(dmixed, mg, scale, *, name):
    s, d = dmixed.shape
    ts = _tile(s, 256, 16)

    def body(dm_ref, mg_ref, sc_ref, dmg_ref, dsc_ref):
        dm = dm_ref[...]
        dmg_ref[...] = (dm * sc_ref[...]).astype(BF16)
        part = jnp.sum(dm * mg_ref[...].astype(F32), axis=0, keepdims=True)

        @pl.when(pl.program_id(0) == 0)
        def _():
            dsc_ref[...] = part

        @pl.when(pl.program_id(0) > 0)
        def _():
            dsc_ref[...] += part

    blk = pl.BlockSpec((ts, d), lambda i: (i, 0))
    vec = pl.BlockSpec((1, d), lambda i: (0, 0))
    return pl.pallas_call(
        body, name=name, grid=(s // ts,),
        in_specs=[blk, blk, vec], out_specs=[blk, vec],
        out_shape=[jax.ShapeDtypeStruct((s, d), BF16), jax.ShapeDtypeStruct((1, d), F32)],
        compiler_params=_cparams(1),
    )(dmixed, mg, scale)


def _ln_fwd(res, mm, g, b, *, name):
    s, d = res.shape
    ts = _tile(s, 128, 16)

    def body(res_ref, mm_ref, g_ref, b_ref, h_ref, hb_ref, xh_ref, rs_ref):
        z = ALPHA * res_ref[...] + mm_ref[...]
        mu = jnp.mean(z, axis=-1, keepdims=True)
        zc = z - mu
        var = jnp.mean(zc * zc, axis=-1, keepdims=True)
        rstd = lax.rsqrt(var + LN_EPS)
        xhat = zc * rstd
        h = xhat * g_ref[...] + b_ref[...]
        h_ref[...] = h
        hb_ref[...] = h.astype(BF16)
        xh_ref[...] = xhat
        rs_ref[...] = rstd

    blk = pl.BlockSpec((ts, d), lambda i: (i, 0))
    vec = pl.BlockSpec((1, d), lambda i: (0, 0))
    return pl.pallas_call(
        body, name=name, grid=(s // ts,),
        in_specs=[blk, blk, vec, vec],
        out_specs=[blk, blk, blk, pl.BlockSpec((ts, 1), lambda i: (i, 0))],
        out_shape=[jax.ShapeDtypeStruct((s, d), F32), jax.ShapeDtypeStruct((s, d), BF16),
                   jax.ShapeDtypeStruct((s, d), F32), jax.ShapeDtypeStruct((s, 1), F32)],
        compiler_params=_cparams(1),
    )(res, mm, g, b)


def _ln_bwd(dres, dmm, xhat, rstd, g, *, name):
    s, d = dmm.shape
    ts = _tile(s, 128, 16)
    has_res = dres is not None

    def body(*refs):
        if has_res:
            dres_ref, dmm_ref, xh_ref, rs_ref, g_ref, dz_ref, dzb_ref, dg_ref, db_ref = refs
            dh = ALPHA * dres_ref[...] + dmm_ref[...]
        else:
            dmm_ref, xh_ref, rs_ref, g_ref, dz_ref, dzb_ref, dg_ref, db_ref = refs
            dh = dmm_ref[...]
        xhat_ = xh_ref[...]
        dxh = dh * g_ref[...]
        c1 = jnp.mean(dxh, axis=-1, keepdims=True)
        c2 = jnp.mean(dxh * xhat_, axis=-1, keepdims=True)
        dz = rs_ref[...] * (dxh - c1 - xhat_ * c2)
        dz_ref[...] = dz
        dzb_ref[...] = dz.astype(BF16)
        dg_part = jnp.sum(dh * xhat_, axis=0, keepdims=True)
        db_part = jnp.sum(dh, axis=0, keepdims=True)

        @pl.when(pl.program_id(0) == 0)
        def _():
            dg_ref[...] = dg_part
            db_ref[...] = db_part

        @pl.when(pl.program_id(0) > 0)
        def _():
            dg_ref[...] += dg_part
            db_ref[...] += db_part

    blk = pl.BlockSpec((ts, d), lambda i: (i, 0))
    vec = pl.BlockSpec((1, d), lambda i: (0, 0))
    col = pl.BlockSpec((ts, 1), lambda i: (i, 0))
    ins = ([dres] if has_res else []) + [dmm, xhat, rstd, g]
    in_specs = ([blk] if has_res else []) + [blk, blk, col, vec]
    return pl.pallas_call(
        body, name=name, grid=(s // ts,),
        in_specs=in_specs, out_specs=[blk, blk, vec, vec],
        out_shape=[jax.ShapeDtypeStruct((s, d), F32), jax.ShapeDtypeStruct((s, d), BF16),
                   jax.ShapeDtypeStruct((1, d), F32), jax.ShapeDtypeStruct((1, d), F32)],
        compiler_params=_cparams(1),
    )(*ins)


def _loss_head(h, tgt, *, name):
    s, d = h.shape
    ts = _tile(s, 256, 16)

    def body(h_ref, t_ref, dh_ref, loss_ref):
        err = h_ref[...] - t_ref[...]
        dh_ref[...] = err * (1.0 / d)
        part = 0.5 * jnp.sum(jnp.mean(err * err, axis=-1, keepdims=True), axis=0, keepdims=True)

        @pl.when(pl.program_id(0) == 0)
        def _():
            loss_ref[...] = part

        @pl.when(pl.program_id(0) > 0)
        def _():
            loss_ref[...] += part

    blk = pl.BlockSpec((ts, d), lambda i: (i, 0))
    return pl.pallas_call(
        body, name=name, grid=(s // ts,),
        in_specs=[blk, blk], out_specs=[blk, pl.BlockSpec((1, 1), lambda i: (0, 0))],
        out_shape=[jax.ShapeDtypeStruct((s, d), F32), jax.ShapeDtypeStruct((1, 1), F32)],
        compiler_params=_cparams(1),
    )(h, tgt)


def _conv(ext, w, bias):
    c = bias + _down(ext, 2) * w[0:1] + _down(ext, 1) * w[1:2] + ext * w[2:3]
    return c[CONV_HALO:]


def _act_specs(s, n2):
    n_strips = pl.cdiv(n2, LANES)
    u_spec = pl.BlockSpec((None, 2, s, LANES), lambda hh, j: (hh, 0, 0, j))
    cwg = pl.BlockSpec((None, 3, LANES), lambda hh, j: (hh, 0, j))
    cwv = pl.BlockSpec((None, 3, LANES), lambda hh, j: (hh + 2, 0, j))
    cbg = pl.BlockSpec((None, 1, LANES), lambda hh, j: (hh, 0, j))
    cbv = pl.BlockSpec((None, 1, LANES), lambda hh, j: (hh + 2, 0, j))
    return n_strips, u_spec, cwg, cwv, cbg, cbv


def _act_fwd(u, cw, cb, *, name, rows=64):
    _, _, s, n2 = u.shape
    n_strips, u_spec, cwg, cwv, cbg, cbv = _act_specs(s, n2)

    def body(u_ref, wg_ref, wv_ref, bg_ref, bv_ref, a_ref):
        wg, wv, bg, bv = wg_ref[...], wv_ref[...], bg_ref[...], bv_ref[...]

        def chunk(c, carry):
            r0 = pl.multiple_of(c * rows, rows)
            cg = _conv(_causal_ext(lambda r, n: u_ref[0, pl.ds(r, n), :].astype(F32), r0, rows, CONV_HALO), wg, bg)
            cv = _conv(_causal_ext(lambda r, n: u_ref[1, pl.ds(r, n), :].astype(F32), r0, rows, CONV_HALO), wv, bv)
            a_ref[pl.ds(r0, rows), :] = (cg * _sigmoid(cg) * cv).astype(BF16)
            return carry

        lax.fori_loop(0, s // rows, chunk, 0)

    return pl.pallas_call(
        body, name=name, grid=(2, n_strips),
        in_specs=[u_spec, cwg, cwv, cbg, cbv],
        out_specs=pl.BlockSpec((None, s, LANES), lambda hh, j: (hh, 0, j)),
        out_shape=jax.ShapeDtypeStruct((2, s, n2), BF16),
        compiler_params=_cparams(2),
    )(u, cw, cw, cb, cb)


def _act_bwd(u, da, cw, cb, *, name, rows=64):
    _, _, s, n2 = u.shape
    n_strips, u_spec, cwg, cwv, cbg, cbv = _act_specs(s, n2)
    n_chunks = s // rows

    def body(u_ref, da_ref, wg_ref, wv_ref, bg_ref, bv_ref, du_ref, dwg_ref, dwv_ref, dbg_ref, dbv_ref, dg_s, dv_s):
        wg, wv, bg, bv = wg_ref[...], wv_ref[...], bg_ref[...], bv_ref[...]

        def first(c, sums):
            r0 = pl.multiple_of(c * rows, rows)
            eg = _causal_ext(lambda r, n: u_ref[0, pl.ds(r, n), :].astype(F32), r0, rows, CONV_HALO)
            ev = _causal_ext(lambda r, n: u_ref[1, pl.ds(r, n), :].astype(F32), r0, rows, CONV_HALO)
            cg, cv = _conv(eg, wg, bg), _conv(ev, wv, bv)
            sg = _sigmoid(cg)
            dact = da_ref[pl.ds(r0, rows), :]
            dval = dact * (cg * sg)
            dgate = dact * cv * (sg * (1.0 + cg * (1.0 - sg)))
            dg_s[pl.ds(r0, rows), :] = dgate
            dv_s[pl.ds(r0, rows), :] = dval
            new = []
            for dc, ext in ((dgate, eg), (dval, ev)):
                new += [_fold8(dc * _down(ext, 2)[CONV_HALO:]), _fold8(dc * _down(ext, 1)[CONV_HALO:]),
                        _fold8(dc * ext[CONV_HALO:]), _fold8(dc)]
            return tuple(acc + x for acc, x in zip(sums, new))

        zero = jnp.zeros((8, LANES), F32)
        sums = lax.fori_loop(0, n_chunks, first, (zero,) * 8)
        red = [jnp.sum(x, axis=0, keepdims=True) for x in sums]
        dwg_ref[...] = jnp.concatenate(red[0:3], axis=0)
        dbg_ref[...] = red[3]
        dwv_ref[...] = jnp.concatenate(red[4:7], axis=0)
        dbv_ref[...] = red[7]

        def second(c, carry):
            r0 = pl.multiple_of(c * rows, rows)
            for gv, (src, w) in enumerate(((dg_s, wg), (dv_s, wv))):
                ext = _anti_ext(lambda r, n: src[pl.ds(r, n), :], r0, rows, CONV_HALO, s)
                du = ext * w[2:3] + _up(ext, 1) * w[1:2] + _up(ext, 2) * w[0:1]
                du_ref[gv, pl.ds(r0, rows), :] = du[:rows].astype(BF16)
            return carry

        lax.fori_loop(0, n_chunks, second, 0)

    w_out = pl.BlockSpec((None, 3, LANES), lambda hh, j: (hh, 0, j))
    b_out = pl.BlockSpec((None, 1, LANES), lambda hh, j: (hh, 0, j))
    return pl.pallas_call(
        body, name=name, grid=(2, n_strips),
        in_specs=[u_spec, pl.BlockSpec((None, s, LANES), lambda hh, j: (hh, 0, j)), cwg, cwv, cbg, cbv],
        out_specs=[u_spec, w_out, w_out, b_out, b_out],
        out_shape=[jax.ShapeDtypeStruct((2, 2, s, n2), BF16),
                   jax.ShapeDtypeStruct((2, 3, n2), F32), jax.ShapeDtypeStruct((2, 3, n2), F32),
                   jax.ShapeDtypeStruct((2, 1, n2), F32), jax.ShapeDtypeStruct((2, 1, n2), F32)],
        scratch_shapes=[pltpu.VMEM((s, LANES), F32), pltpu.VMEM((s, LANES), F32)],
        compiler_params=_cparams(2),
    )(u, da, cw, cw, cb, cb)


def _dot_nt(a, b):
    return lax.dot_general(a, b, _DIMS["nt"], preferred_element_type=F32)


def _dot_tn(a, b):
    return lax.dot_general(a, b, _DIMS["tn"], preferred_element_type=F32)


def _band_masks(b):
    qi = lax.broadcasted_iota(jnp.int32, (ATTN_BLOCK, ATTN_BLOCK), 0)
    kj = lax.broadcasted_iota(jnp.int32, (ATTN_BLOCK, ATTN_BLOCK), 1)
    return kj <= qi, jnp.logical_and(kj >= qi, b > 0)


def _attn_fwd(q, k, v, d, *, name):
    s, dm = q.shape
    heads, seq, scale = dm // HEAD_DIM, s // d, 1.0 / math.sqrt(HEAD_DIM)
    nb = seq // ATTN_BLOCK
    assert nb * ATTN_BLOCK * d == s and heads <= LANES

    def body(q_ref, kc_ref, kp_ref, vc_ref, vp_ref, o_ref, l_ref):
        mask_c, mask_p = _band_masks(pl.program_id(1))
        lane = lax.broadcasted_iota(jnp.int32, (ATTN_BLOCK, LANES), 1)
        lse_all = jnp.zeros((ATTN_BLOCK, LANES), F32)
        for h in range(heads):
            hs = slice(h * HEAD_DIM, (h + 1) * HEAD_DIM)
            qh = q_ref[:, hs]
            sc = jnp.where(mask_c, _dot_nt(qh, kc_ref[:, hs]) * scale, NEG_INF)
            sp = jnp.where(mask_p, _dot_nt(qh, kp_ref[:, hs]) * scale, NEG_INF)
            m = jnp.maximum(jnp.max(sc, axis=-1, keepdims=True), jnp.max(sp, axis=-1, keepdims=True))
            pc, pp = jnp.exp(sc - m), jnp.exp(sp - m)
            den = jnp.sum(pc, axis=-1, keepdims=True) + jnp.sum(pp, axis=-1, keepdims=True)
            acc = (jnp.dot(pc.astype(BF16), vc_ref[:, hs], preferred_element_type=F32)
                   + jnp.dot(pp.astype(BF16), vp_ref[:, hs], preferred_element_type=F32))
            o_ref[:, hs] = acc / den
            lse_all = jnp.where(lane == h, m + jnp.log(den), lse_all)
        l_ref[...] = lse_all

    cur = pl.BlockSpec((ATTN_BLOCK, dm), lambda r, b: (b, r))
    prev = pl.BlockSpec((ATTN_BLOCK, dm), lambda r, b: (jnp.maximum(b - 1, 0), r))
    views = [t.reshape(seq, d * dm) for t in (q, k, k, v, v)]
    o, lse = pl.pallas_call(
        body, name=name, grid=(d, nb),
        in_specs=[cur, cur, prev, cur, prev],
        out_specs=[cur, pl.BlockSpec((ATTN_BLOCK, LANES), lambda r, b: (b, r))],
        out_shape=[jax.ShapeDtypeStruct((seq, d * dm), F32), jax.ShapeDtypeStruct((seq, d * LANES), F32)],
        compiler_params=_cparams(2),
    )(*views)
    return o.reshape(s, dm), lse.reshape(s, LANES)


def _attn_merge(outs, lses, *, name):
    s, dm = outs[0].shape
    heads = dm // HEAD_DIM
    ts = _tile(s, 128, 16)

    def body(o1, o2, o3, l1, l2, l3, o_ref, ob_ref, lt_ref):
        ls = [l1[...], l2[...], l3[...]]
        m = jnp.maximum(jnp.maximum(ls[0], ls[1]), ls[2])
        ws = [jnp.exp(x - m) for x in ls]
        tot = ws[0] + ws[1] + ws[2]
        lt_ref[...] = m + jnp.log(tot)
        ws = [w / tot for w in ws]
        for h in range(heads):
            hs = slice(h * HEAD_DIM, (h + 1) * HEAD_DIM)
            o = (ws[0][:, h:h + 1] * o1[:, hs] + ws[1][:, h:h + 1] * o2[:, hs]) + ws[2][:, h:h + 1] * o3[:, hs]
            o_ref[:, hs] = o
            ob_ref[:, hs] = o.astype(BF16)

    blk = pl.BlockSpec((ts, dm), lambda i: (i, 0))
    st = pl.BlockSpec((ts, LANES), lambda i: (i, 0))
    return pl.pallas_call(
        body, name=name, grid=(s // ts,),
        in_specs=[blk] * 3 + [st] * 3, out_specs=[blk, blk, st],
        out_shape=[jax.ShapeDtypeStruct((s, dm), F32), jax.ShapeDtypeStruct((s, dm), BF16),
                   jax.ShapeDtypeStruct((s, LANES), F32)],
        compiler_params=_cparams(1),
    )(*outs, *lses)


def _attn_delta(do, o, *, name):
    s, dm = o.shape
    heads = dm // HEAD_DIM
    ts = _tile(s, 256, 16)

    def body(do_ref, o_ref, dl_ref):
        lane = lax.broadcasted_iota(jnp.int32, (ts, LANES), 1)
        acc = jnp.zeros((ts, LANES), F32)
        for h in range(heads):
            hs = slice(h * HEAD_DIM, (h + 1) * HEAD_DIM)
            row = jnp.sum(do_ref[:, hs].astype(F32) * o_ref[:, hs], axis=-1, keepdims=True)
            acc = jnp.where(lane == h, row, acc)
        dl_ref[...] = acc

    blk = pl.BlockSpec((ts, dm), lambda i: (i, 0))
    return pl.pallas_call(
        body, name=name, grid=(s // ts,),
        in_specs=[blk, blk], out_specs=pl.BlockSpec((ts, LANES), lambda i: (i, 0)),
        out_shape=jax.ShapeDtypeStruct((s, LANES), F32),
        compiler_params=_cparams(1),
    )(do, o)


def _attn_bwd(q, k, v, do, lse, delta, acc_in, d, *, name):
    s, dm = q.shape
    heads, seq, scale = dm // HEAD_DIM, s // d, 1.0 / math.sqrt(HEAD_DIM)
    nb = seq // ATTN_BLOCK
    has_acc = acc_in is not None

    def body(*refs):
        q_ref, do_ref, l_ref, dl_ref, kc_ref, kp_ref, vc_ref, vp_ref = refs[:8]
        acc_ref = refs[8] if has_acc else None
        out_ref, carry, new, prevc = refs[-4:]
        b = pl.program_id(1)

        @pl.when(b < nb)
        def _():
            mask_c, mask_p = _band_masks(b)
            lse_blk, dl_blk = l_ref[...], dl_ref[...]
            for h in range(heads):
                hs = slice(h * HEAD_DIM, (h + 1) * HEAD_DIM)
                qh, doh = q_ref[:, hs], do_ref[:, hs]
                kc, kp, vc, vp = kc_ref[:, hs], kp_ref[:, hs], vc_ref[:, hs], vp_ref[:, hs]
                lse_h, dl_h = lse_blk[:, h:h + 1], dl_blk[:, h:h + 1]
                sc = jnp.where(mask_c, _dot_nt(qh, kc) * scale, NEG_INF)
                sp = jnp.where(mask_p, _dot_nt(qh, kp) * scale, NEG_INF)
                pc, pp = jnp.exp(sc - lse_h), jnp.exp(sp - lse_h)
                dsc = (pc * (_dot_nt(doh, vc) - dl_h) * scale).astype(BF16)
                dsp = (pp * (_dot_nt(doh, vp) - dl_h) * scale).astype(BF16)
                new[0, :, hs] = (jnp.dot(dsc, kc, preferred_element_type=F32)
                                 + jnp.dot(dsp, kp, preferred_element_type=F32))
                new[1, :, hs] = _dot_tn(dsc, qh)
                new[2, :, hs] = _dot_tn(pc.astype(BF16), doh)
                prevc[0, :, hs] = _dot_tn(dsp, qh)
                prevc[1, :, hs] = _dot_tn(pp.astype(BF16), doh)

        @pl.when(b == nb)
        def _():
            prevc[...] = jnp.zeros_like(prevc)

        @pl.when(b > 0)
        def _():
            for i in range(3):
                val = carry[i]
                if i > 0:
                    val = val + prevc[i - 1]
                if has_acc:
                    val = val + acc_ref[i]
                out_ref[i] = val

        @pl.when(b < nb)
        def _():
            carry[...] = new[...]

    def cb(b):
        return jnp.minimum(b, nb - 1)

    cur = pl.BlockSpec((ATTN_BLOCK, dm), lambda r, b: (cb(b), r))
    prev = pl.BlockSpec((ATTN_BLOCK, dm), lambda r, b: (jnp.maximum(cb(b) - 1, 0), r))
    stat = pl.BlockSpec((ATTN_BLOCK, LANES), lambda r, b: (cb(b), r))
    out = pl.BlockSpec((3, ATTN_BLOCK, dm), lambda r, b: (0, jnp.maximum(b - 1, 0), r))
    wide = lambda t: t.reshape(seq, d * dm)
    ins = [wide(q), wide(do), lse.reshape(seq, d * LANES), delta.reshape(seq, d * LANES),
           wide(k), wide(k), wide(v), wide(v)]
    in_specs = [cur, cur, stat, stat, cur, prev, cur, prev]
    if has_acc:
        ins.append(acc_in.reshape(3, seq, d * dm))
        in_specs.append(out)
    res = pl.pallas_call(
        body, name=name, grid=(d, nb + 1),
        in_specs=in_specs, out_specs=out,
        out_shape=jax.ShapeDtypeStruct((3, seq, d * dm), F32),
        scratch_shapes=[pltpu.VMEM((3, ATTN_BLOCK, dm), F32), pltpu.VMEM((3, ATTN_BLOCK, dm), F32),
                        pltpu.VMEM((2, ATTN_BLOCK, dm), F32)],
        compiler_params=_cparams(2),
    )(*ins)
    return res.reshape(3, s, dm)


def _ew(fn, ins, out_dtypes, *, name, tile_bytes=1 << 20):
    first = ins[0][0] if isinstance(ins[0], tuple) else ins[0]
    rows, cols = first.shape[-2], first.shape[-1]
    tr = _tile(rows, max(16, tile_bytes // (4 * cols)), 16)
    n_in = len(ins)

    def body(*refs):
        outs = fn(*[r[...] for r in refs[:n_in]])
        for o_ref, val in zip(refs[n_in:], outs):
            o_ref[...] = val.astype(o_ref.dtype)

    in_specs, args = [], []
    for item in ins:
        if isinstance(item, tuple):
            arr, lead = item
            in_specs.append(pl.BlockSpec((None, tr, cols), lambda i, lead=lead: (lead, i, 0)))
            args.append(arr)
        else:
            in_specs.append(pl.BlockSpec((tr, cols), lambda i: (i, 0)))
            args.append(item)
    blk = pl.BlockSpec((tr, cols), lambda i: (i, 0))
    return pl.pallas_call(
        body, name=name, grid=(rows // tr,),
        in_specs=in_specs, out_specs=[blk] * len(out_dtypes),
        out_shape=[jax.ShapeDtypeStruct((rows, cols), dt) for dt in out_dtypes],
        compiler_params=_cparams(1),
    )(*args)


def _adamw_math(g, w, m, v):
    m2 = ADAM_B1 * m + (1.0 - ADAM_B1) * g
    v2 = ADAM_B2 * v + (1.0 - ADAM_B2) * (g * g)
    m_hat = m2 / (1.0 - ADAM_B1 ** ADAM_STEP)
    v_hat = v2 / (1.0 - ADAM_B2 ** ADAM_STEP)
    delta = -ADAM_LR * (m_hat / (jnp.sqrt(v_hat) + ADAM_EPS) + ADAM_WD * w)
    return g, delta, m2, v2


def _scalars(*vals):
    return jnp.stack([jnp.asarray(v, jnp.int32) for v in vals])


def _adamw_halves(mine, theirs, core, w, m, v, *, name, lead=0, prev=None):
    shape = w.shape
    a_n, rh, cols = mine.shape
    w3, m3, v3 = (t.reshape(-1, 2 * rh, cols) for t in (w, m, v))
    tr, tc = _tile2(rh, cols, 1 << 17, 8)
    n_i = rh // tr

    def body(c_ref, mine_ref, theirs_ref, w_ref, m_ref, v_ref, *rest):
        g = jnp.where(pl.program_id(1) == c_ref[0], mine_ref[...], theirs_ref[...])
        outs = _adamw_math(g, w_ref[...], m_ref[...], v_ref[...])
        for ref, val in zip(rest[-4:], outs):
            ref[...] = val

    half = pl.BlockSpec((None, tr, tc), lambda a, h, i, j, c_ref: (a, i, j))
    full = pl.BlockSpec((None, tr, tc), lambda a, h, i, j, c_ref: (lead + a, h * n_i + i, j))
    args = [mine, theirs, w3, m3, v3]
    in_specs = [half, half, full, full, full]
    aliases = {}
    if prev is not None:
        args += [p.reshape(w3.shape) for p in prev]
        in_specs += [ANY] * 4
        aliases = {6 + k: k for k in range(4)}
    outs = pl.pallas_call(
        body, name=name,
        grid_spec=pltpu.PrefetchScalarGridSpec(
            num_scalar_prefetch=1, grid=(a_n, 2, n_i, cols // tc), in_specs=in_specs, out_specs=[full] * 4),
        out_shape=[jax.ShapeDtypeStruct(w3.shape, F32)] * 4,
        input_output_aliases=aliases,
        compiler_params=_cparams(4),
    )(_scalars(core), *args)
    return [o.reshape(shape) for o in outs]


def _cast_into(src, buf, lead, slot, *, name, buf_shape=None, dtype=BF16):
    a_n, rows, cols = src.shape
    tr = _tile(rows, max(16, (1 << 21) // (4 * cols)), 16)

    def body(slot_ref, src_ref, *rest):
        rest[-1][...] = src_ref[...].astype(rest[-1].dtype)

    in_specs = [pl.BlockSpec((None, tr, cols), lambda a, i, slot_ref: (a, i, 0))]
    args = [src]
    aliases = {}
    if buf is not None:
        in_specs.append(ANY)
        args.append(buf)
        aliases = {2: 0}
        buf_shape, dtype = buf.shape, buf.dtype
    return pl.pallas_call(
        body, name=name,
        grid_spec=pltpu.PrefetchScalarGridSpec(
            num_scalar_prefetch=1, grid=(a_n, rows // tr), in_specs=in_specs,
            out_specs=pl.BlockSpec((None, None, tr, cols), lambda a, i, slot_ref: (lead + a, slot_ref[0], i, 0))),
        out_shape=jax.ShapeDtypeStruct(buf_shape, dtype),
        input_output_aliases=aliases,
        compiler_params=_cparams(2),
    )(_scalars(slot), *args)


def _place():
    x, y, c = lax.axis_index("x"), lax.axis_index("y"), lax.axis_index("c")
    chips = [(1 - x, y), (x, 1 - y), (1 - x, 1 - y)]
    return x, y, c, chips


def _remote(src, dst, send_sem, recv_sem, dev):
    return pltpu.make_async_remote_copy(src_ref=src, dst_ref=dst, send_sem=send_sem, recv_sem=recv_sem,
                                        device_id=dev, device_id_type=MESH)


def _gather_comm(bufs, pieces):
    def plan(refs, send_sems, recv_sems):
        x, y, c, chips = _place()

        def region(p, q, core):
            t, a0, a1, part, n_parts = pieces[p]
            rh = bufs[t].shape[2] // 2
            sub = rh // n_parts
            return refs[t].at[pl.ds(a0, a1 - a0), q, pl.ds(core * rh + part * sub, sub), :]

        def ici(p, j, q):
            cx, cy = chips[j]
            return _remote(region(p, q, c), region(p, q, c), send_sems.at[6 * p + j], recv_sems.at[6 * p + j], (cx, cy, c))

        def d2d(p, j, core):
            cx, cy = chips[j]
            rows = region(p, 2 * cx + cy, core)
            return _remote(rows, rows, send_sems.at[6 * p + 3 + j], recv_sems.at[6 * p + 3 + j], (x, y, 1 - c))

        return 2 * x + y, c, [2 * cx + cy for cx, cy in chips], ici, d2d

    todo = [(j, p) for j in range(3) for p in range(len(pieces))]

    def start(refs, send_sems, recv_sems):
        q_me, _, _, ici, _ = plan(refs, send_sems, recv_sems)
        for j, p in todo:
            ici(p, j, q_me).start()

    def finish(refs, send_sems, recv_sems):
        q_me, c, q_of, ici, d2d = plan(refs, send_sems, recv_sems)
        for j, p in todo:
            ici(p, j, q_of[j]).wait_recv()
            d2d(p, j, c).start()
        for j, p in todo:
            d2d(p, j, 1 - c).wait_recv()
        for j, p in todo:
            ici(p, j, q_me).wait_send()
            d2d(p, j, c).wait_send()

    return _Comm(bufs, 6 * len(pieces), start, finish)


def _chips_comm(parts, lands):
    n = len(parts)

    def copy(refs, send_sems, recv_sems, t, j, q_src, q_dst):
        x, y, c, chips = _place()
        cx, cy = chips[j]
        return _remote(refs[t].at[:, q_src], refs[n + t].at[q_dst], send_sems.at[3 * t + j], recv_sems.at[3 * t + j],
                       (cx, cy, c))

    todo = [(j, t) for j in range(3) for t in range(n)]

    def qs():
        x, y, _, chips = _place()
        return 2 * x + y, [2 * cx + cy for cx, cy in chips]

    def start(refs, send_sems, recv_sems):
        q_me, q_of = qs()
        for j, t in todo:
            copy(refs, send_sems, recv_sems, t, j, q_of[j], q_me).start()

    def finish(refs, send_sems, recv_sems):
        q_me, q_of = qs()
        for j, t in todo:
            copy(refs, send_sems, recv_sems, t, j, q_me, q_of[j]).wait_recv()
        for j, t in todo:
            copy(refs, send_sems, recv_sems, t, j, q_of[j], q_me).wait_send()

    return _Comm(list(parts) + list(lands), 3 * n, start, finish)


def _comm_call(comm, *, name):
    k = len(comm.operands)

    def body(*refs):
        operands, (send_sems, recv_sems) = refs[k:2 * k], refs[2 * k:]
        comm.start(operands, send_sems, recv_sems)
        comm.finish(operands, send_sems, recv_sems)

    return pl.pallas_call(
        body, name=name, in_specs=[ANY] * k, out_specs=[ANY] * k,
        out_shape=[jax.ShapeDtypeStruct(t.shape, t.dtype) for t in comm.operands],
        input_output_aliases={i: i for i in range(k)},
        scratch_shapes=[pltpu.SemaphoreType.DMA((comm.n_sems,)), pltpu.SemaphoreType.DMA((comm.n_sems,))],
    )(*comm.operands)


def _rs_sibling(grads, *, name):
    n = len(grads)
    halves = [jax.ShapeDtypeStruct(g.shape[:2] + (g.shape[2] // 2, g.shape[3]), g.dtype) for g in grads]

    def body(*refs):
        src, theirs = refs[:n], refs[n:2 * n]
        send_sems, recv_sems = refs[2 * n:]
        x, y, c, _ = _place()
        ops = []
        for t in range(n):
            rh = grads[t].shape[2] // 2
            give = _remote(src[t].at[:, :, pl.ds((1 - c) * rh, rh), :], theirs[t], send_sems.at[t], recv_sems.at[t],
                           (x, y, 1 - c))
            give.start()
            ops.append(give)
        for op in ops:
            op.wait()

    return pl.pallas_call(
        body, name=name, in_specs=[ANY] * n, out_specs=[ANY] * n, out_shape=halves,
        scratch_shapes=[pltpu.SemaphoreType.DMA((n,)), pltpu.SemaphoreType.DMA((n,))],
    )(*grads)


def _rs_add(grad, theirs, core, slot, *, name):
    a_n, _, rh, cols = theirs.shape
    tr, tc = _tile2(rh, cols, 1 << 18, 16)
    n_i = rh // tr

    def body(s_ref, g_ref, t_ref, p_ref, y_ref):
        part = (g_ref[...].astype(F32) + t_ref[...].astype(F32)).astype(BF16)
        p_ref[...] = part

        @pl.when(pl.program_id(3) == s_ref[1])
        def _():
            y_ref[...] = part

    blk = (None, None, tr, tc)
    return pl.pallas_call(
        body, name=name,
        grid_spec=pltpu.PrefetchScalarGridSpec(
            num_scalar_prefetch=1, grid=(a_n, n_i, cols // tc, N_CHIPS),
            in_specs=[pl.BlockSpec(blk, lambda a, i, j, q, s: (a, q, s[0] * n_i + i, j)),
                      pl.BlockSpec(blk, lambda a, i, j, q, s: (a, q, i, j))],
            out_specs=[pl.BlockSpec(blk, lambda a, i, j, q, s: (a, q, i, j)),
                       pl.BlockSpec(blk, lambda a, i, j, q, s: (s[1], a, i, j))]),
        out_shape=[jax.ShapeDtypeStruct(theirs.shape, BF16),
                   jax.ShapeDtypeStruct((N_CHIPS, a_n, rh, cols), BF16)],
        compiler_params=_cparams(4),
    )(_scalars(core, slot), grad, theirs)


def _rs_finish(halves, *, name):
    n = len(halves)

    def body(*refs):
        src, dst = refs[:n], refs[n:2 * n]
        send_sems, recv_sems = refs[2 * n:]
        x, y, c, _ = _place()
        ops = []
        for t in range(n):
            give = _remote(src[t], dst[t], send_sems.at[t], recv_sems.at[t], (x, y, 1 - c))
            give.start()
            ops.append(give)
        for op in ops:
            op.wait()

    return pl.pallas_call(
        body, name=name, in_specs=[ANY] * n, out_specs=[ANY] * n,
        out_shape=[jax.ShapeDtypeStruct(h.shape, h.dtype) for h in halves],
        scratch_shapes=[pltpu.SemaphoreType.DMA((n,)), pltpu.SemaphoreType.DMA((n,))],
    )(*halves)


def _all_reduce_small(vec, *, name):
    rows = vec.shape[0]

    def body(v_ref, o_ref, land, send_sems, recv_sems):
        x, y, c, _ = _place()
        me = 4 * x + 2 * y + c
        land[me] = v_ref[...]
        flips = [(fx, fy, fc) for fx in (0, 1) for fy in (0, 1) for fc in (0, 1)][1:]
        sent = []
        for k, (fx, fy, fc) in enumerate(flips):
            cp = _remote(v_ref, land.at[me], send_sems.at[k], recv_sems.at[k], (x ^ fx, y ^ fy, c ^ fc))
            cp.start()
            sent.append(cp)
        for k, (fx, fy, fc) in enumerate(flips):
            peer = 4 * (x ^ fx) + 2 * (y ^ fy) + (c ^ fc)
            _remote(v_ref, land.at[peer], send_sems.at[k], recv_sems.at[k], (x ^ fx, y ^ fy, c ^ fc)).wait_recv()
        for cp in sent:
            cp.wait_send()
        total = land[0]
        for dev in range(1, 8):
            total = total + land[dev]
        o_ref[...] = total

    whole = pl.BlockSpec(memory_space=pltpu.VMEM)
    return pl.pallas_call(
        body, name=name, in_specs=[whole], out_specs=whole,
        out_shape=jax.ShapeDtypeStruct(vec.shape, F32),
        scratch_shapes=[pltpu.VMEM((8, rows, LANES), F32), pltpu.SemaphoreType.DMA((7,)), pltpu.SemaphoreType.DMA((7,))],
        compiler_params=pltpu.CompilerParams(vmem_limit_bytes=VMEM_LIMIT),
    )(vec)


def _pack(parts, mult=16):
    flat = jnp.concatenate([p.reshape(-1).astype(F32) for p in parts])
    rows = -(-flat.shape[0] // (LANES * mult)) * mult
    return jnp.pad(flat, (0, rows * LANES - flat.shape[0])).reshape(rows, LANES)


def _unpack(vec, shapes):
    flat, out, pos = vec.reshape(-1), [], 0
    for shp in shapes:
        size = math.prod(shp)
        out.append(flat[pos:pos + size].reshape(shp))
        pos += size
    return out


def kernel(x, pool_w_in, pool_w_grp, pool_scale, pool_w_out, attn_w_q, attn_w_o, shared_w_k, shared_w_v, ffn_w_up, ffn_conv_w, ffn_conv_b, ffn_w_down, ln1_g, ln1_b, ln2_g, ln2_b, loss_target, m_pool_w_in, m_pool_w_grp, m_pool_scale, m_pool_w_out, m_attn_w_q, m_attn_w_o, m_shared_w_k, m_shared_w_v, m_ffn_w_up, m_ffn_conv_w, m_ffn_conv_b, m_ffn_w_down, m_ln1_g, m_ln1_b, m_ln2_g, m_ln2_b, v_pool_w_in, v_pool_w_grp, v_pool_scale, v_pool_w_out, v_attn_w_q, v_attn_w_o, v_shared_w_k, v_shared_w_v, v_ffn_w_up, v_ffn_conv_w, v_ffn_conv_b, v_ffn_w_down, v_ln1_g, v_ln1_b, v_ln2_g, v_ln2_b):
    s, d = x.shape[1], x.shape[2]
    n2 = ffn_w_up.shape[2]
    fq = ffn_w_down.shape[1]
    assert 2 * fq == n2 and d % N_CHIPS == 0
    g_n, cg = pool_w_grp.shape[1], pool_w_grp.shape[3]
    xs, tgt = x[0], loss_target[0]
    q_me = 2 * lax.axis_index("x") + lax.axis_index("y")
    core = lax.axis_index("c")
    rq = d // N_CHIPS

    six_g = None
    for i, w_ in enumerate((pool_w_in[0], pool_w_out[0], attn_w_q[0], shared_w_k, shared_w_v, attn_w_o[0])):
        six_g = _cast_into(w_[None], six_g, i, q_me, name=f"cast_w{i}", buf_shape=(6, N_CHIPS, rq, d))
    grp_g = _cast_into(pool_w_grp[0], None, 0, q_me, name="cast_grp", buf_shape=(g_n, N_CHIPS, cg // N_CHIPS, cg))
    up_g = _cast_into(ffn_w_up, None, 0, q_me, name="cast_up", buf_shape=(2, N_CHIPS, d, n2))
    dn_g = _cast_into(ffn_w_down, None, 0, q_me, name="cast_down", buf_shape=(2, N_CHIPS, fq, d))
    small = _pack([ffn_conv_w, pool_scale])
    small_g = _cast_into(small[None], None, 0, q_me, name="cast_small", buf_shape=(1, N_CHIPS) + small.shape, dtype=F32)
    bufs = [six_g, grp_g, up_g, dn_g, small_g]
    SIX, GRP, UP, DN, SMALL = range(5)
    bufs = list(_comm_call(_gather_comm(bufs, [(SIX, IW_IN, IW_OUT + 1, 0, 1), (GRP, 0, g_n, 0, 1), (SMALL, 0, 1, 0, 1),
                                               (UP, 0, 1, 0, 1)]), name="gather_first"))
    small_q = bufs[SMALL].reshape(N_CHIPS, -1)
    n_cw = 2 * 3 * n2
    conv_w = small_q[:, :n_cw].reshape(N_CHIPS, 2, 3, n2).transpose(1, 0, 2, 3)
    scale_full = small_q[:, n_cw:n_cw + d // N_CHIPS].reshape(1, d)
    conv_b = ffn_conv_b.reshape(2, N_CHIPS, 1, n2)

    def w6():
        return bufs[SIX].reshape(6, d, d)

    def up8():
        return bufs[UP].reshape(2 * N_CHIPS, d, n2)

    def dn4():
        return bufs[DN].reshape(4, n2, d)

    def gathered(pieces):
        used = sorted({pc[0] for pc in pieces})
        local = [(used.index(t), a0, a1, part, n_parts) for t, a0, a1, part, n_parts in pieces]
        return _gather_comm([bufs[t] for t in used], local), used

    def store(used, operands):
        for t, arr in zip(used, operands):
            bufs[t] = arr

    def ffn_fwd(l, hb, carry_up, carry_down):
        comm, used = gathered(carry_up)
        u, landed = _mm(hb[None], up8(), mode="nn", name=f"ffn{l}_up", out_dtype=BF16, n_q=4, tm=_tile(s, 512), tn=n2,
                        tk=_tile(d, 1024), qa=_q0, qb=lambda q, qr: 4 * l + _perm(q), comm=comm)
        store(used, landed)
        u = u.reshape(2, 2, s, n2)
        act = _act_fwd(u, conv_w[l], conv_b[l], name=f"ffn{l}_act")
        comm, used = gathered(carry_down) if carry_down else (None, None)
        ff = _mm(act, dn4(), mode="nn", name=f"ffn{l}_down", out_dtype=F32, n_q=1, n_qr=2, tm=_tile(s, 512),
                 tn=_tile(d, 1024), tk=n2, qa=_qr, qb=lambda q, qr: 2 * l + qr, qo=lambda q: 0, comm=comm)
        if comm is not None:
            ff, landed = ff
            store(used, landed)
        return u, act, ff[0]

    p = _mm_dd(xs[None], w6(), IW_IN, mode="nn", name="pool_in", out_dtype=F32)[0]
    pooled = _pool(p, backward=False, name="pool_fwd")
    mg, mixed = _grp_fwd(pooled, bufs[GRP].reshape(g_n, cg, cg), scale_full, name="pool_grp")
    mix0 = _mm_dd(mixed[None], w6(), IW_OUT, mode="nn", name="pool_out", out_dtype=F32)[0]
    h1, h1b, xh1, rs1 = _ln_fwd(xs, mix0, ln1_g[0:1], ln1_b[0:1], name="ln1_0")
    u0, act0, ff0 = ffn_fwd(0, h1b, [(DN, 0, 1, 0, 1), (SIX, IW_Q, IW_O + 1, 0, 1)], [(UP, 1, 2, 0, 2)])
    h2, h2b, xh2, rs2 = _ln_fwd(h1, ff0, ln2_g[0:1], ln2_b[0:1], name="ln2_0")

    comm, used = gathered([(UP, 1, 2, 1, 2)])
    qkv, landed = _mm_dd(h2b[None], w6(), IW_Q, mode="nn", name="attn_qkv", out_dtype=BF16, n_q=3, comm=comm)
    store(used, landed)
    branch = [_attn_fwd(qkv[0], qkv[1], qkv[2], dil, name=f"attn_fwd_d{dil}") for dil in DILATIONS]
    o, ob, lse = _attn_merge([br[0] for br in branch], [br[1] for br in branch], name="attn_merge")
    mix1 = _mm_dd(ob[None], w6(), IW_O, mode="nn", name="attn_out", out_dtype=F32)[0]
    h3, h3b, xh3, rs3 = _ln_fwd(h2, mix1, ln1_g[1:2], ln1_b[1:2], name="ln1_1")
    u1, act1, ff1 = ffn_fwd(1, h3b, [(DN, 1, 2, 0, 1)], None)
    h4, _, xh4, rs4 = _ln_fwd(h3, ff1, ln2_g[1:2], ln2_b[1:2], name="ln2_1")
    dh4, loss_local = _loss_head(h4, tgt, name="loss_head")

    rs_parts, rs_lands = {}, {}

    def rs_prepare(items):
        theirs = _rs_sibling([g_ for _, g_ in items], name="rs_sibling_" + items[0][0])
        for (nm, g_), t_ in zip(items, theirs):
            rs_parts[nm], rs_lands[nm] = _rs_add(g_, t_, core, q_me, name=f"rs_add_{nm}")

    def rs_exchange(names):
        return _chips_comm([rs_parts[nm] for nm in names], [rs_lands[nm] for nm in names])

    def rs_landed(names, operands):
        for nm, land in zip(names, operands[len(names):]):
            rs_lands[nm] = land

    d_conv_w, d_conv_b = [None, None], [None, None]

    def ffn_bwd(l, dzb, u, act, hb, carry_dact):
        da = _mm(dzb[None], dn4(), mode="nt", name=f"ffn{l}_dact", out_dtype=F32, n_q=2, tm=_tile(s, 512), tn=n2,
                 tk=_tile(d, 512), qa=_q0, qb=lambda q, qr: 2 * l + q,
                 comm=rs_exchange(carry_dact) if carry_dact else None)
        if carry_dact:
            da, landed = da
            rs_landed(carry_dact, landed)
        du, dwg, dwv, dbg, dbv = _act_bwd(u, da, conv_w[l], conv_b[l], name=f"ffn{l}_dconv")
        d_conv_w[l] = jnp.concatenate([dwg, dwv], axis=0)
        d_conv_b[l] = jnp.concatenate([dbg, dbv], axis=0)
        du4 = du.reshape(4, s, n2)
        g_dn = _mm(act, dzb[None], mode="tn", name=f"ffn{l}_gdown", out_dtype=BF16, n_q=2, tm=n2, tn=_tile(d, 1024),
                   tk=_tile(s, 512), qa=_qq, qb=_q0)
        rs_prepare([(f"dn{l}", g_dn.reshape(1, N_CHIPS, fq, d))])
        g_up, landed = _mm(hb[None], du4, mode="tn", name=f"ffn{l}_gup", out_dtype=BF16, n_q=4, tm=_tile(d, 512), tn=n2,
                           tk=_tile(s, 1024), qa=_q0, qb=lambda q, qr: _perm(q), comm=rs_exchange([f"dn{l}"]))
        rs_landed([f"dn{l}"], landed)
        rs_prepare([(f"up{l}", g_up.reshape(1, N_CHIPS, d, n2))])
        dh, landed = _mm(du4, up8(), mode="nt", name=f"ffn{l}_dh", out_dtype=F32, n_q=1, n_qr=4, tm=_tile(s, 512),
                         tn=_tile(d, 1024), tk=n2, qa=_qr, qb=lambda q, qr: 4 * l + _perm(qr), qo=lambda q: 0,
                         comm=rs_exchange([f"up{l}"]))
        rs_landed([f"up{l}"], landed)
        return dh[0]

    dz4, dz4b, dg_ln2_1, db_ln2_1 = _ln_bwd(None, dh4, xh4, rs4, ln2_g[1:2], name="dln2_1")
    dh3 = ffn_bwd(1, dz4b, u1, act1, h3b, None)
    dz3, dz3b, dg_ln1_1, db_ln1_1 = _ln_bwd(dz4, dh3, xh3, rs3, ln1_g[1:2], name="dln1_1")
    g_wo = _mm_wgrad(ob[None], dz3b[None], name="attn_gwo")
    do = _mm_dd(dz3b[None], w6(), IW_O, mode="nt", name="attn_do", out_dtype=BF16)[0]
    delta = _attn_delta(do, o, name="attn_delta")
    dqkv = None
    for dil in DILATIONS:
        dqkv = _attn_bwd(qkv[0], qkv[1], qkv[2], do, lse, delta, dqkv, dil, name=f"attn_bwd_d{dil}")
    g_wq = _mm_wgrad(h2b[None], dqkv, name="attn_gwq", qb=lambda q, qr: 0)
    g_wk = _mm_wgrad(h2b[None], dqkv, name="attn_gwk", qb=lambda q, qr: 1)
    g_wv = _mm_wgrad(h2b[None], dqkv, name="attn_gwv", qb=lambda q, qr: 2)
    rs_prepare([(nm, g_.reshape(1, N_CHIPS, rq, d)) for nm, g_ in
                (("wo", g_wo), ("wq", g_wq), ("wk", g_wk), ("wv", g_wv))])
    dh2, landed = _mm_dd(dqkv, w6(), IW_Q, mode="nt", name="attn_dh", out_dtype=F32, n_qr=3, qa=_qr,
                         comm=rs_exchange(["wo", "wq", "wk"]))
    rs_landed(["wo", "wq", "wk"], landed)
    dz2, dz2b, dg_ln2_0, db_ln2_0 = _ln_bwd(dz3, dh2[0], xh2, rs2, ln2_g[0:1], name="dln2_0")
    dh1 = ffn_bwd(0, dz2b, u0, act0, h1b, ["wv"])
    dz1, dz1b, dg_ln1_0, db_ln1_0 = _ln_bwd(dz2, dh1, xh1, rs1, ln1_g[0:1], name="dln1_0")
    g_wout = _mm_wgrad(mixed[None], dz1b[None], name="pool_gwout")
    dmixed = _mm_dd(dz1b[None], w6(), IW_OUT, mode="nt", name="pool_dmixed", out_dtype=F32)[0]
    dmg, d_scale = _grp_bwd_pre(dmixed, mg, scale_full, name="pool_dscale")
    g_wgrp = _grp_mm(pooled, dmg, mode="tn", name="pool_gwgrp", out_dtype=BF16, tm=1024)
    dpooled = _grp_mm(dmg, None, mode="nt", name="pool_dpooled", out_dtype=F32, tm=1024, w=bufs[GRP].reshape(g_n, cg, cg))
    dp = _pool(dpooled, backward=True, name="pool_bwd")
    g_win = _mm_wgrad(xs[None], dp[None], name="pool_gwin")
    dx_mm = _mm_dd(dp[None], w6(), IW_IN, mode="nt", name="pool_dx", out_dtype=F32)[0]
    (grad_x,) = _ew(lambda a, b: (ALPHA * a + b,), [dz1, dx_mm], [F32], name="grad_x")
    last = ["win", "wgrp", "wout"]
    rs_prepare([("win", g_win.reshape(1, N_CHIPS, rq, d)), ("wgrp", g_wgrp.reshape(g_n, N_CHIPS, cg // N_CHIPS, cg)),
                ("wout", g_wout.reshape(1, N_CHIPS, rq, d))])
    rs_landed(last, _comm_call(rs_exchange(last), name="rs_chips_last"))

    rs_names = ["win", "wgrp", "wout", "wq", "wo", "wk", "wv", "up0", "up1", "dn0", "dn1"]
    finished = {}
    for nm in rs_names:
        y4 = rs_lands[nm]
        y3 = y4.reshape(N_CHIPS, -1, y4.shape[-1])
        (tot,) = _ew(lambda a0, a1, a2, a3: (((a0.astype(F32) + a1.astype(F32)) + a2.astype(F32)) + a3.astype(F32),),
                     [(y3, 0), (y3, 1), (y3, 2), (y3, 3)], [F32], name=f"rs_sum_{nm}")
        finished[nm] = tot.reshape(y4.shape[1:])
    others = dict(zip(rs_names, _rs_finish([finished[nm] for nm in rs_names], name="rs_finish")))
    results = {}
    for nm, key, w, m, v in (("pool_w_in", "win", pool_w_in, m_pool_w_in, v_pool_w_in),
                             ("pool_w_grp", "wgrp", pool_w_grp, m_pool_w_grp, v_pool_w_grp),
                             ("pool_w_out", "wout", pool_w_out, m_pool_w_out, v_pool_w_out),
                             ("attn_w_q", "wq", attn_w_q, m_attn_w_q, v_attn_w_q),
                             ("attn_w_o", "wo", attn_w_o, m_attn_w_o, v_attn_w_o),
                             ("shared_w_k", "wk", shared_w_k, m_shared_w_k, v_shared_w_k),
                             ("shared_w_v", "wv", shared_w_v, m_shared_w_v, v_shared_w_v)):
        results[nm] = _adamw_halves(finished[key], others[key], core, w, m, v, name=f"adamw_{nm}")
    for nm, key, w, m, v in (("ffn_w_up", "up", ffn_w_up, m_ffn_w_up, v_ffn_w_up),
                             ("ffn_w_down", "dn", ffn_w_down, m_ffn_w_down, v_ffn_w_down)):
        res = None
        for l in (1, 0):
            res = _adamw_halves(finished[f"{key}{l}"], others[f"{key}{l}"], core, w, m, v, name=f"adamw_{nm}{l}",
                                lead=l, prev=res)
        results[nm] = res

    ln_grads = [jnp.concatenate([a, b], axis=0) for a, b in
                ((dg_ln1_0, dg_ln1_1), (db_ln1_0, db_ln1_1), (dg_ln2_0, dg_ln2_1), (db_ln2_0, db_ln2_1))]
    small_shapes = [(2, N_CHIPS, 3, n2), (2, N_CHIPS, n2)] + [(2, d)] * 4 + [(1, d)]
    vec = _pack([jnp.stack(d_conv_w), jnp.stack(d_conv_b)] + ln_grads + [d_scale], mult=8)
    tot = _unpack(_all_reduce_small(vec, name="allreduce_small"), small_shapes)
    g_cw = lax.dynamic_index_in_dim(tot[0], q_me, axis=1, keepdims=False)
    g_cb = tot[1].reshape(2, N_CHIPS * n2)
    g_scale = lax.dynamic_slice_in_dim(tot[6], q_me * rq, rq, axis=1)
    small_names = ["ffn_conv_w", "ffn_conv_b", "ln1_g", "ln1_b", "ln2_g", "ln2_b", "pool_scale"]
    small_g = [g_cw, g_cb, tot[2], tot[3], tot[4], tot[5], g_scale]
    small_w = [ffn_conv_w, ffn_conv_b, ln1_g, ln1_b, ln2_g, ln2_b, pool_scale]
    small_m = [m_ffn_conv_w, m_ffn_conv_b, m_ln1_g, m_ln1_b, m_ln2_g, m_ln2_b, m_pool_scale]
    small_v = [v_ffn_conv_w, v_ffn_conv_b, v_ln1_g, v_ln1_b, v_ln2_g, v_ln2_b, v_pool_scale]
    packed = _ew(_adamw_math, [_pack(small_g, 8), _pack(small_w, 8), _pack(small_m, 8), _pack(small_v, 8)], [F32] * 4,
                 name="adamw_small")
    shapes = [w.shape for w in small_w]
    unpacked = [_unpack(pk, shapes) for pk in packed]
    for i, nm in enumerate(small_names):
        results[nm] = [unpacked[k][i] for k in range(4)]

    loss = lax.psum(loss_local[0, 0], ("x", "y", "c"))
    order = ["pool_w_in", "pool_w_grp", "pool_scale", "pool_w_out", "attn_w_q", "attn_w_o", "shared_w_k", "shared_w_v",
             "ffn_w_up", "ffn_conv_w", "ffn_conv_b", "ffn_w_down", "ln1_g", "ln1_b", "ln2_g", "ln2_b"]
    outs = [loss, grad_x[None]]
    for k in range(4):
        outs += [results[nm][k] for nm in order]
    return tuple(outs)
```

```python
import functools
import math

import jax
import jax.numpy as jnp
from jax import lax
from jax.experimental import pallas as pl
from jax.experimental.pallas import tpu as pltpu

F32 = jnp.float32
BF16 = jnp.bfloat16

LANES = 128
HEAD_DIM = 128
ATTN_BLOCK = 128
DILATIONS = (1, 4, 16)
ATTN_UNROLL = 4
POOL_WINDOWS = (2, 4, 8, 16)
POOL_HALO = 16
CONV_HALO = 16
DEPTH = 2
ALPHA = (2.0 * DEPTH) ** 0.25
LN_EPS = 1e-5
NEG_INF = -1e30
ADAM_LR = 0.001
ADAM_B1 = 0.9
ADAM_B2 = 0.999
ADAM_EPS = 1e-08
ADAM_WD = 0.01
ADAM_STEP = 10
N_CHIPS = 4
VMEM_LIMIT = 56 * 1024 * 1024
ANY = pl.BlockSpec(memory_space=pl.ANY)
MESH = pl.DeviceIdType.MESH

IW_IN, IW_OUT, IW_Q, IW_K, IW_V, IW_O = range(6)


def _cparams(n_grid):
    return pltpu.CompilerParams(dimension_semantics=("arbitrary",) * n_grid, vmem_limit_bytes=VMEM_LIMIT)


def _tile(dim, pref, align=LANES):
    if dim <= pref:
        return dim
    t = (pref // align) * align
    while t >= align:
        if dim % t == 0:
            return t
        t -= align
    return dim


def _tile2(rows, cols, budget, row_align):
    best = None
    for tc in [cols] + [c for c in range(LANES, cols, LANES) if cols % c == 0]:
        for tr in range(row_align, rows + 1, row_align):
            if rows % tr == 0 and tr * tc <= budget:
                if best is None or (tr * tc, tc) > (best[0] * best[1], best[1]):
                    best = (tr, tc)
    return best if best is not None else (rows, cols)


def _perm(q):
    return (q % 2) * 2 + q // 2


_DIMS = {"nn": (((1,), (0,)), ((), ())), "nt": (((1,), (1,)), ((), ())), "tn": (((0,), (0,)), ((), ()))}


class _Comm:
    def __init__(self, operands, n_sems, start, finish):
        self.operands, self.n_sems, self.start, self.finish = list(operands), n_sems, start, finish


def _mm(a, b, *, mode, name, out_dtype, n_q, tm, tn, tk, qa, qb, qo=lambda q: q, n_qr=1, out_q=None, comm=None):
    if mode == "nn":
        m, kdim, n = a.shape[1], a.shape[2], b.shape[2]
    elif mode == "nt":
        m, kdim, n = a.shape[1], a.shape[2], b.shape[1]
    else:
        kdim, m, n = a.shape[1], a.shape[2], b.shape[2]
    assert m % tm == 0 and n % tn == 0 and kdim % tk == 0, (name, m, n, kdim, tm, tn, tk)
    kr_n = kdim // tk
    nr = n_qr * kr_n
    out_q = n_q if out_q is None else out_q

    def split(r):
        return (r // kr_n, r % kr_n) if n_qr > 1 else (0, r)

    if mode == "tn":
        a_spec = pl.BlockSpec((None, tk, tm), lambda q, i, j, r: (qa(q, split(r)[0]), split(r)[1], i))
    else:
        a_spec = pl.BlockSpec((None, tm, tk), lambda q, i, j, r: (qa(q, split(r)[0]), i, split(r)[1]))
    if mode == "nt":
        b_spec = pl.BlockSpec((None, tn, tk), lambda q, i, j, r: (qb(q, split(r)[0]), j, split(r)[1]))
    else:
        b_spec = pl.BlockSpec((None, tk, tn), lambda q, i, j, r: (qb(q, split(r)[0]), split(r)[1], j))
    o_spec = pl.BlockSpec((None, tm, tn), lambda q, i, j, r: (qo(q), i, j))
    dims = _DIMS[mode]

    n_comm = len(comm.operands) if comm is not None else 0
    grid = (n_q, m // tm, n // tn, nr)

    def body(*refs):
        a_ref, b_ref = refs[0], refs[1]
        o_ref = refs[2 + n_comm]
        if comm is not None:
            comm_refs = refs[3 + n_comm:3 + 2 * n_comm]
            send_sems, recv_sems = refs[-2:]
            ids = [pl.program_id(ax) for ax in range(4)]
            first = functools.reduce(jnp.logical_and, [i == 0 for i in ids])
            last = functools.reduce(jnp.logical_and, [i == g - 1 for i, g in zip(ids, grid)])

            @pl.when(first)
            def _():
                comm.start(comm_refs, send_sems, recv_sems)
        lhs, rhs = a_ref[...], b_ref[...]
        if lhs.dtype != BF16:
            lhs = lhs.astype(BF16)
        if rhs.dtype != BF16:
            rhs = rhs.astype(BF16)
        part = lax.dot_general(lhs, rhs, dims, preferred_element_type=F32)
        if nr == 1:
            o_ref[...] = part.astype(o_ref.dtype)
        else:
            acc_ref = refs[3 + 2 * n_comm]
            r = pl.program_id(3)

            @pl.when(r == 0)
            def _():
                acc_ref[...] = part

            @pl.when(r > 0)
            def _():
                acc_ref[...] += part

            @pl.when(r == nr - 1)
            def _():
                o_ref[...] = acc_ref[...].astype(o_ref.dtype)
        if comm is not None:
            @pl.when(last)
            def _():
                comm.finish(comm_refs, send_sems, recv_sems)

    scratch = [pltpu.VMEM((tm, tn), F32)] if nr > 1 else []
    out_shape = [jax.ShapeDtypeStruct((out_q, m, n), out_dtype)]
    args = [a, b]
    if comm is not None:
        args += comm.operands
        out_shape += [jax.ShapeDtypeStruct(t.shape, t.dtype) for t in comm.operands]
        scratch += [pltpu.SemaphoreType.DMA((comm.n_sems,)), pltpu.SemaphoreType.DMA((comm.n_sems,))]
    outs = pl.pallas_call(
        body, name=name, grid=grid,
        in_specs=[a_spec, b_spec] + [ANY] * n_comm, out_specs=[o_spec] + [ANY] * n_comm,
        out_shape=out_shape, scratch_shapes=scratch,
        input_output_aliases={2 + i: 1 + i for i in range(n_comm)},
        compiler_params=_cparams(4),
    )(*args)
    return (outs[0], outs[1:]) if comm is not None else outs[0]


def _q0(q, qr):
    return 0


def _qq(q, qr):
    return q


def _qr(q, qr):
    return qr


def _mm_dd(a3, w6, widx, *, mode, name, out_dtype, n_q=1, n_qr=1, qa=_q0, comm=None):
    d = w6.shape[1]
    f32_in = a3.dtype != BF16
    tm = _tile(a3.shape[1] if mode != "tn" else a3.shape[2], 512 if f32_in else 1024)
    if n_qr > 1:
        qb = lambda q, qr: widx + qr
    elif n_q > 1:
        qb = lambda q, qr: widx + q
    else:
        qb = lambda q, qr: widx
    return _mm(a3, w6, mode=mode, name=name, out_dtype=out_dtype, n_q=n_q, n_qr=n_qr,
               tm=tm, tn=_tile(d, 1024), tk=d, qa=qa, qb=qb, comm=comm)


def _mm_wgrad(a3, b3, *, name, n_q=1, qa=_q0, qb=_q0):
    s, d = a3.shape[1], a3.shape[2]
    f32_in = a3.dtype != BF16 or b3.dtype != BF16
    return _mm(a3, b3, mode="tn", name=name, out_dtype=BF16, n_q=n_q, tm=_tile(d, 1024), tn=b3.shape[2],
               tk=_tile(s, 512 if f32_in else 1024), qa=qa, qb=qb)


def _grp_mm(a, b, *, mode, name, out_dtype, tm, w=None):
    s = a.shape[0]
    if mode == "tn":
        cg = b.shape[1] // len(POOL_WINDOWS)
        g_n = len(POOL_WINDOWS)
        ts = _tile(s, tm)
        nr = s // ts

        def body(a_ref, b_ref, o_ref, acc_ref):
            r = pl.program_id(1)
            part = lax.dot_general(a_ref[...], b_ref[...], _DIMS["tn"], preferred_element_type=F32)

            @pl.when(r == 0)
            def _():
                acc_ref[...] = part

            @pl.when(r > 0)
            def _():
                acc_ref[...] += part

            @pl.when(r == nr - 1)
            def _():
                o_ref[...] = acc_ref[...].astype(o_ref.dtype)

        return pl.pallas_call(
            body, name=name, grid=(g_n, nr),
            in_specs=[pl.BlockSpec((ts, cg), lambda g, r: (r, g)), pl.BlockSpec((ts, cg), lambda g, r: (r, g))],
            out_specs=pl.BlockSpec((None, cg, cg), lambda g, r: (g, 0, 0)),
            out_shape=jax.ShapeDtypeStruct((g_n, cg, cg), out_dtype),
            scratch_shapes=[pltpu.VMEM((cg, cg), F32)],
            compiler_params=_cparams(2),
        )(a, b)
    g_n, cg = w.shape[0], w.shape[1]
    ts = _tile(s, tm)
    dims = _DIMS[mode]

    def body(a_ref, w_ref, o_ref):
        o_ref[...] = lax.dot_general(a_ref[...], w_ref[...], dims, preferred_element_type=F32).astype(o_ref.dtype)

    return pl.pallas_call(
        body, name=name, grid=(g_n, s // ts),
        in_specs=[pl.BlockSpec((ts, cg), lambda g, i: (i, g)), pl.BlockSpec((None, cg, cg), lambda g, i: (g, 0, 0))],
        out_specs=pl.BlockSpec((ts, cg), lambda g, i: (i, g)),
        out_shape=jax.ShapeDtypeStruct((s, g_n * cg), out_dtype),
        compiler_params=_cparams(2),
    )(a, w)


def _causal_ext(load, r0, rows, halo):
    cur = load(r0, rows)
    prev = load(pl.multiple_of(jnp.maximum(r0 - halo, 0), halo), halo)
    prev = jnp.where(r0 > 0, prev, jnp.zeros_like(prev))
    return jnp.concatenate([prev, cur], axis=0)


def _anti_ext(load, r0, rows, halo, s):
    cur = load(r0, rows)
    nxt = load(pl.multiple_of(jnp.minimum(r0 + rows, s - halo), halo), halo)
    nxt = jnp.where(r0 + rows < s, nxt, jnp.zeros_like(nxt))
    return jnp.concatenate([cur, nxt], axis=0)


def _down(ext, k):
    return pltpu.roll(ext, k, axis=0)


def _up(ext, k):
    return pltpu.roll(ext, ext.shape[0] - k, axis=0)


def _fold8(x):
    return jnp.sum(x.reshape(x.shape[0] // 8, 8, x.shape[1]), axis=0)


def _sigmoid(x):
    return 1.0 / (1.0 + jnp.exp(-x))


def _pool(p, *, backward, name, rows=64):
    s, d = p.shape
    strips_per_group = (d // len(POOL_WINDOWS)) // LANES
    assert strips_per_group * LANES * len(POOL_WINDOWS) == d and s % rows == 0

    def body(p_ref, o_ref):
        g = pl.program_id(0) // strips_per_group
        win = jnp.left_shift(2, g).astype(F32)

        def load(r0, n):
            return p_ref[pl.ds(r0, n), :]

        def pick(levels):
            return jnp.where(g == 0, levels[0], jnp.where(g == 1, levels[1], jnp.where(g == 2, levels[2], levels[3])))

        def chunk(c, carry):
            r0 = pl.multiple_of(c * rows, rows)
            if not backward:
                ext = _causal_ext(load, r0, rows, POOL_HALO)
                levels, acc = [], ext
                for k in (1, 2, 4, 8):
                    acc = acc + _down(acc, k)
                    levels.append(acc)
                t = (r0 + lax.broadcasted_iota(jnp.int32, (rows, LANES), 0)).astype(F32)
                cnt = jnp.minimum(t + 1.0, win)
                out = pick(levels)[POOL_HALO:] / cnt - ext[POOL_HALO:]
            else:
                ext = _anti_ext(load, r0, rows, POOL_HALO, s)
                t = (r0 + lax.broadcasted_iota(jnp.int32, (rows + POOL_HALO, LANES), 0)).astype(F32)
                e = ext / jnp.minimum(t + 1.0, win)
                levels, acc = [], e
                for k in (1, 2, 4, 8):
                    acc = acc + _up(acc, k)
                    levels.append(acc)
                out = pick(levels)[:rows] - ext[:rows]
            o_ref[pl.ds(r0, rows), :] = out.astype(o_ref.dtype)
            return carry

        lax.fori_loop(0, s // rows, chunk, 0)

    return pl.pallas_call(
        body, name=name, grid=(d // LANES,),
        in_specs=[pl.BlockSpec((s, LANES), lambda j: (0, j))],
        out_specs=pl.BlockSpec((s, LANES), lambda j: (0, j)),
        out_shape=jax.ShapeDtypeStruct((s, d), BF16),
        compiler_params=_cparams(1),
    )(p)


def _grp_fwd(pooled, w_grp, scale, *, name):
    s, d = pooled.shape
    g_n, cg = w_grp.shape[0], w_grp.shape[1]
    ts = _tile(s, 1024)

    def body(a_ref, w_ref, sc_ref, mg_ref, mx_ref):
        mg = jnp.dot(a_ref[...], w_ref[...], preferred_element_type=F32)
        mg_ref[...] = mg.astype(BF16)
        mx_ref[...] = (mg * sc_ref[...]).astype(BF16)

    blk = pl.BlockSpec((ts, cg), lambda g, i: (i, g))
    return pl.pallas_call(
        body, name=name, grid=(g_n, s // ts),
        in_specs=[blk, pl.BlockSpec((None, cg, cg), lambda g, i: (g, 0, 0)), pl.BlockSpec((1, cg), lambda g, i: (0, g))],
        out_specs=[blk, blk],
        out_shape=[jax.ShapeDtypeStruct((s, d), BF16)] * 2,
        compiler_params=_cparams(2),
    )(pooled, w_grp, scale)


def _grp_bwd_pre(dmixed, mg, scale, *, name):
    s, d = dmixed.shape
    ts = _tile(s, 256, 16)

    def body(dm_ref, mg_ref, sc_ref, dmg_ref, dsc_ref):
        dm = dm_ref[...]
        dmg_ref[...] = (dm * sc_ref[...]).astype(BF16)
        part = jnp.sum(dm * mg_ref[...].astype(F32), axis=0, keepdims=True)

        @pl.when(pl.program_id(0) == 0)
        def _():
            dsc_ref[...] = part

        @pl.when(pl.program_id(0) > 0)
        def _():
            dsc_ref[...] += part

    blk = pl.BlockSpec((ts, d), lambda i: (i, 0))
    vec = pl.BlockSpec((1, d), lambda i: (0, 0))
    return pl.pallas_call(
        body, name=name, grid=(s // ts,),
        in_specs=[blk, blk, vec], out_specs=[blk, vec],
        out_shape=[jax.ShapeDtypeStruct((s, d), BF16), jax.ShapeDtypeStruct((1, d), F32)],
        compiler_params=_cparams(1),
    )(dmixed, mg, scale)


def _ln_fwd(res, mm, g, b, *, name):
    s, d = res.shape
    ts = _tile(s, 128, 16)

    def body(res_ref, mm_ref, g_ref, b_ref, h_ref, hb_ref, xh_ref, rs_ref):
        z = ALPHA * res_ref[...] + mm_ref[...]
        mu = jnp.mean(z, axis=-1, keepdims=True)
        zc = z - mu
        var = jnp.mean(zc * zc, axis=-1, keepdims=True)
        rstd = lax.rsqrt(var + LN_EPS)
        xhat = zc * rstd
        h = xhat * g_ref[...] + b_ref[...]
        h_ref[...] = h
        hb_ref[...] = h.astype(BF16)
        xh_ref[...] = xhat
        rs_ref[...] = rstd

    blk = pl.BlockSpec((ts, d), lambda i: (i, 0))
    vec = pl.BlockSpec((1, d), lambda i: (0, 0))
    return pl.pallas_call(
        body, name=name, grid=(s // ts,),
        in_specs=[blk, blk, vec, vec],
        out_specs=[blk, blk, blk, pl.BlockSpec((ts, 1), lambda i: (i, 0))],
        out_shape=[jax.ShapeDtypeStruct((s, d), F32), jax.ShapeDtypeStruct((s, d), BF16),
                   jax.ShapeDtypeStruct((s, d), F32), jax.ShapeDtypeStruct((s, 1), F32)],
        compiler_params=_cparams(1),
    )(res, mm, g, b)


def _ln_bwd(dres, dmm, xhat, rstd, g, *, name):
    s, d = dmm.shape
    ts = _tile(s, 128, 16)
    has_res = dres is not None

    def body(*refs):
        if has_res:
            dres_ref, dmm_ref, xh_ref, rs_ref, g_ref, dz_ref, dzb_ref, dg_ref, db_ref = refs
            dh = ALPHA * dres_ref[...] + dmm_ref[...]
        else:
            dmm_ref, xh_ref, rs_ref, g_ref, dz_ref, dzb_ref, dg_ref, db_ref = refs
            dh = dmm_ref[...]
        xhat_ = xh_ref[...]
        dxh = dh * g_ref[...]
        c1 = jnp.mean(dxh, axis=-1, keepdims=True)
        c2 = jnp.mean(dxh * xhat_, axis=-1, keepdims=True)
        dz = rs_ref[...] * (dxh - c1 - xhat_ * c2)
        dz_ref[...] = dz
        dzb_ref[...] = dz.astype(BF16)
        dg_part = jnp.sum(dh * xhat_, axis=0, keepdims=True)
        db_part = jnp.sum(dh, axis=0, keepdims=True)

        @pl.when(pl.program_id(0) == 0)
        def _():
            dg_ref[...] = dg_part
            db_ref[...] = db_part

        @pl.when(pl.program_id(0) > 0)
        def _():
            dg_ref[...] += dg_part
            db_ref[...] += db_part

    blk = pl.BlockSpec((ts, d), lambda i: (i, 0))
    vec = pl.BlockSpec((1, d), lambda i: (0, 0))
    col = pl.BlockSpec((ts, 1), lambda i: (i, 0))
    ins = ([dres] if has_res else []) + [dmm, xhat, rstd, g]
    in_specs = ([blk] if has_res else []) + [blk, blk, col, vec]
    return pl.pallas_call(
        body, name=name, grid=(s // ts,),
        in_specs=in_specs, out_specs=[blk, blk, vec, vec],
        out_shape=[jax.ShapeDtypeStruct((s, d), F32), jax.ShapeDtypeStruct((s, d), BF16),
                   jax.ShapeDtypeStruct((1, d), F32), jax.ShapeDtypeStruct((1, d), F32)],
        compiler_params=_cparams(1),
    )(*ins)


def _loss_head(h, tgt, *, name):
    s, d = h.shape
    ts = _tile(s, 256, 16)

    def body(h_ref, t_ref, dh_ref, loss_ref):
        err = h_ref[...] - t_ref[...]
        dh_ref[...] = err * (1.0 / d)
        part = 0.5 * jnp.sum(jnp.mean(err * err, axis=-1, keepdims=True), axis=0, keepdims=True)

        @pl.when(pl.program_id(0) == 0)
        def _():
            loss_ref[...] = part

        @pl.when(pl.program_id(0) > 0)
        def _():
            loss_ref[...] += part

    blk = pl.BlockSpec((ts, d), lambda i: (i, 0))
    return pl.pallas_call(
        body, name=name, grid=(s // ts,),
        in_specs=[blk, blk], out_specs=[blk, pl.BlockSpec((1, 1), lambda i: (0, 0))],
        out_shape=[jax.ShapeDtypeStruct((s, d), F32), jax.ShapeDtypeStruct((1, 1), F32)],
        compiler_params=_cparams(1),
    )(h, tgt)


def _conv(ext, w, bias):
    c = bias + _down(ext, 2) * w[0:1] + _down(ext, 1) * w[1:2] + ext * w[2:3]
    return c[CONV_HALO:]


def _act_specs(s, n2):
    n_strips = pl.cdiv(n2, LANES)
    u_spec = pl.BlockSpec((None, 2, s, LANES), lambda hh, j: (hh, 0, 0, j))
    cwg = pl.BlockSpec((None, 3, LANES), lambda hh, j: (hh, 0, j))
    cwv = pl.BlockSpec((None, 3, LANES), lambda hh, j: (hh + 2, 0, j))
    cbg = pl.BlockSpec((None, 1, LANES), lambda hh, j: (hh, 0, j))
    cbv = pl.BlockSpec((None, 1, LANES), lambda hh, j: (hh + 2, 0, j))
    return n_strips, u_spec, cwg, cwv, cbg, cbv


def _act_fwd(u, cw, cb, *, name, rows=64):
    _, _, s, n2 = u.shape
    n_strips, u_spec, cwg, cwv, cbg, cbv = _act_specs(s, n2)

    def body(u_ref, wg_ref, wv_ref, bg_ref, bv_ref, a_ref):
        wg, wv, bg, bv = wg_ref[...], wv_ref[...], bg_ref[...], bv_ref[...]

        def chunk(c, carry):
            r0 = pl.multiple_of(c * rows, rows)
            cg = _conv(_causal_ext(lambda r, n: u_ref[0, pl.ds(r, n), :].astype(F32), r0, rows, CONV_HALO), wg, bg)
            cv = _conv(_causal_ext(lambda r, n: u_ref[1, pl.ds(r, n), :].astype(F32), r0, rows, CONV_HALO), wv, bv)
            a_ref[pl.ds(r0, rows), :] = (cg * _sigmoid(cg) * cv).astype(BF16)
            return carry

        lax.fori_loop(0, s // rows, chunk, 0)

    return pl.pallas_call(
        body, name=name, grid=(2, n_strips),
        in_specs=[u_spec, cwg, cwv, cbg, cbv],
        out_specs=pl.BlockSpec((None, s, LANES), lambda hh, j: (hh, 0, j)),
        out_shape=jax.ShapeDtypeStruct((2, s, n2), BF16),
        compiler_params=_cparams(2),
    )(u, cw, cw, cb, cb)


def _act_bwd(u, da, cw, cb, *, name, rows=64):
    _, _, s, n2 = u.shape
    n_strips, u_spec, cwg, cwv, cbg, cbv = _act_specs(s, n2)
    n_chunks = s // rows

    def body(u_ref, da_ref, wg_ref, wv_ref, bg_ref, bv_ref, du_ref, dwg_ref, dwv_ref, dbg_ref, dbv_ref, dg_s, dv_s):
        wg, wv, bg, bv = wg_ref[...], wv_ref[...], bg_ref[...], bv_ref[...]

        def first(c, sums):
            r0 = pl.multiple_of(c * rows, rows)
            eg = _causal_ext(lambda r, n: u_ref[0, pl.ds(r, n), :].astype(F32), r0, rows, CONV_HALO)
            ev = _causal_ext(lambda r, n: u_ref[1, pl.ds(r, n), :].astype(F32), r0, rows, CONV_HALO)
            cg, cv = _conv(eg, wg, bg), _conv(ev, wv, bv)
            sg = _sigmoid(cg)
            dact = da_ref[pl.ds(r0, rows), :]
            dval = dact * (cg * sg)
            dgate = dact * cv * (sg * (1.0 + cg * (1.0 - sg)))
            dg_s[pl.ds(r0, rows), :] = dgate
            dv_s[pl.ds(r0, rows), :] = dval
            new = []
            for dc, ext in ((dgate, eg), (dval, ev)):
                new += [_fold8(dc * _down(ext, 2)[CONV_HALO:]), _fold8(dc * _down(ext, 1)[CONV_HALO:]),
                        _fold8(dc * ext[CONV_HALO:]), _fold8(dc)]
            return tuple(acc + x for acc, x in zip(sums, new))

        zero = jnp.zeros((8, LANES), F32)
        sums = lax.fori_loop(0, n_chunks, first, (zero,) * 8)
        red = [jnp.sum(x, axis=0, keepdims=True) for x in sums]
        dwg_ref[...] = jnp.concatenate(red[0:3], axis=0)
        dbg_ref[...] = red[3]
        dwv_ref[...] = jnp.concatenate(red[4:7], axis=0)
        dbv_ref[...] = red[7]

        def second(c, carry):
            r0 = pl.multiple_of(c * rows, rows)
            for gv, (src, w) in enumerate(((dg_s, wg), (dv_s, wv))):
                ext = _anti_ext(lambda r, n: src[pl.ds(r, n), :], r0, rows, CONV_HALO, s)
                du = ext * w[2:3] + _up(ext, 1) * w[1:2] + _up(ext, 2) * w[0:1]
                du_ref[gv, pl.ds(r0, rows), :] = du[:rows].astype(BF16)
            return carry

        lax.fori_loop(0, n_chunks, second, 0)

    w_out = pl.BlockSpec((None, 3, LANES), lambda hh, j: (hh, 0, j))
    b_out = pl.BlockSpec((None, 1, LANES), lambda hh, j: (hh, 0, j))
    return pl.pallas_call(
        body, name=name, grid=(2, n_strips),
        in_specs=[u_spec, pl.BlockSpec((None, s, LANES), lambda hh, j: (hh, 0, j)), cwg, cwv, cbg, cbv],
        out_specs=[u_spec, w_out, w_out, b_out, b_out],
        out_shape=[jax.ShapeDtypeStruct((2, 2, s, n2), BF16),
                   jax.ShapeDtypeStruct((2, 3, n2), F32), jax.ShapeDtypeStruct((2, 3, n2), F32),
                   jax.ShapeDtypeStruct((2, 1, n2), F32), jax.ShapeDtypeStruct((2, 1, n2), F32)],
        scratch_shapes=[pltpu.VMEM((s, LANES), F32), pltpu.VMEM((s, LANES), F32)],
        compiler_params=_cparams(2),
    )(u, da, cw, cw, cb, cb)


def _dot_nt(a, b):
    return lax.dot_general(a, b, _DIMS["nt"], preferred_element_type=F32)


def _dot_tn(a, b):
    return lax.dot_general(a, b, _DIMS["tn"], preferred_element_type=F32)


def _band_mask(b):
    qi = lax.broadcasted_iota(jnp.int32, (ATTN_BLOCK, 2 * ATTN_BLOCK), 0)
    kj = lax.broadcasted_iota(jnp.int32, (ATTN_BLOCK, 2 * ATTN_BLOCK), 1)
    band = jnp.logical_and(kj >= qi, kj <= qi + ATTN_BLOCK)
    return jnp.logical_and(band, jnp.logical_or(b > 0, kj >= ATTN_BLOCK))


def _to_residues(nat, rm, d, seq, pad):
    seg = seq + pad
    for r in range(d):
        if pad:
            rm[pl.ds(r * seg, pad), :] = jnp.zeros((pad, LANES), rm.dtype)
        rows = nat[pl.ds(r, seq, stride=d), :] if d > 1 else nat[...]
        rm[pl.ds(r * seg + pad, seq), :] = rows.astype(rm.dtype)


def _rows_loop(s, rows, fn):
    def step(c, carry):
        fn(pl.ds(pl.multiple_of(c * rows, rows), rows))
        return carry

    lax.fori_loop(0, s // rows, step, 0)


def _attn_fwd(qkv, *, name):
    _, s, dm = qkv.shape
    heads, scale = dm // HEAD_DIM, 1.0 / math.sqrt(HEAD_DIM)
    pad_rows = s + ATTN_BLOCK * max(DILATIONS)

    def body(q_ref, k_ref, v_ref, o_ref, ob_ref, lse_ref, nat, rq, rk, rv, ro, rl, o_tmp, l_tmp, o_acc, m_acc, s_acc):
        for d in DILATIONS:
            seq = s // d
            nb, seg = seq // ATTN_BLOCK, seq + ATTN_BLOCK
            for src, dst, pad in ((q_ref, rq, 0), (k_ref, rk, ATTN_BLOCK), (v_ref, rv, ATTN_BLOCK)):
                if d == 1:
                    _to_residues(src, dst, d, seq, pad)
                else:
                    nat[...] = src[...].astype(F32)
                    _to_residues(nat, dst, d, seq, pad)

            def block(idx, carry):
                r, b = idx // nb, idx % nb
                qrows = pl.ds(pl.multiple_of(r * seq + b * ATTN_BLOCK, ATTN_BLOCK), ATTN_BLOCK)
                krows = pl.ds(pl.multiple_of(r * seg + b * ATTN_BLOCK, ATTN_BLOCK), 2 * ATTN_BLOCK)
                sc = jnp.where(_band_mask(b), _dot_nt(rq[qrows, :], rk[krows, :]) * scale, NEG_INF)
                m = jnp.max(sc, axis=-1, keepdims=True)
                p = jnp.exp(sc - m)
                den = jnp.sum(p, axis=-1, keepdims=True)
                ro[qrows, :] = jnp.dot(p.astype(BF16), rv[krows, :], preferred_element_type=F32) / den
                rl[qrows, :] = jnp.broadcast_to(m + jnp.log(den), (ATTN_BLOCK, LANES))
                return carry

            lax.fori_loop(0, d * nb, block, 0, unroll=ATTN_UNROLL)
            if d == 1:
                def first(rows):
                    o_acc[rows, :] = ro[rows, :]
                    m_acc[rows, :] = rl[rows, :]
                    s_acc[rows, :] = jnp.ones((rows.size, LANES), F32)

                _rows_loop(s, 64, first)
            else:
                for r in range(d):
                    o_tmp[pl.ds(r, seq, stride=d), :] = ro[pl.ds(r * seq, seq), :]
                    l_tmp[pl.ds(r, seq, stride=d), :] = rl[pl.ds(r * seq, seq), :]

                def merge(rows):
                    m_old, l_new = m_acc[rows, :], l_tmp[rows, :]
                    m_new = jnp.maximum(m_old, l_new)
                    w_old, w_new = jnp.exp(m_old - m_new), jnp.exp(l_new - m_new)
                    o_acc[rows, :] = o_acc[rows, :] * w_old + o_tmp[rows, :] * w_new
                    s_acc[rows, :] = s_acc[rows, :] * w_old + w_new
                    m_acc[rows, :] = m_new

                _rows_loop(s, 64, merge)

        def finish(rows):
            tot = s_acc[rows, :]
            o = o_acc[rows, :] / tot
            o_ref[rows, :] = o
            ob_ref[rows, :] = o.astype(BF16)
            lse_ref[rows, :] = m_acc[rows, :] + jnp.log(tot)

        _rows_loop(s, 64, finish)

    head = lambda i: pl.BlockSpec((None, s, HEAD_DIM), lambda h, i=i: (i, 0, h))
    out = pl.BlockSpec((s, HEAD_DIM), lambda h: (0, h))
    nat_f32 = pltpu.VMEM((s, LANES), F32)
    return pl.pallas_call(
        body, name=name, grid=(heads,),
        in_specs=[head(0), head(1), head(2)], out_specs=[out, out, out],
        out_shape=[jax.ShapeDtypeStruct((s, dm), F32), jax.ShapeDtypeStruct((s, dm), BF16),
                   jax.ShapeDtypeStruct((s, dm), F32)],
        scratch_shapes=[nat_f32, pltpu.VMEM((s, LANES), BF16), pltpu.VMEM((pad_rows, LANES), BF16),
                        pltpu.VMEM((pad_rows, LANES), BF16), nat_f32, nat_f32, nat_f32, nat_f32, nat_f32, nat_f32, nat_f32],
        compiler_params=_cparams(1),
    )(qkv, qkv, qkv)


def _attn_delta(do, o, *, name):
    s, dm = o.shape
    heads = dm // HEAD_DIM
    ts = _tile(s, 256, 16)

    def body(do_ref, o_ref, dl_ref):
        for h in range(heads):
            hs = slice(h * HEAD_DIM, (h + 1) * HEAD_DIM)
            row = jnp.sum(do_ref[:, hs].astype(F32) * o_ref[:, hs], axis=-1, keepdims=True)
            dl_ref[:, hs] = jnp.broadcast_to(row, (ts, HEAD_DIM))

    blk = pl.BlockSpec((ts, dm), lambda i: (i, 0))
    return pl.pallas_call(
        body, name=name, grid=(s // ts,),
        in_specs=[blk, blk], out_specs=blk,
        out_shape=jax.ShapeDtypeStruct((s, dm), F32),
        compiler_params=_cparams(1),
    )(do, o)


def _attn_bwd(qkv, do, lse, delta, *, name):
    _, s, dm = qkv.shape
    heads, scale = dm // HEAD_DIM, 1.0 / math.sqrt(HEAD_DIM)
    pad_rows = s + ATTN_BLOCK * max(DILATIONS)

    def body(q_ref, k_ref, v_ref, do_ref, l_ref, dl_ref, out_ref,
             nat, rq, rdo, rk, rv, rl, rdl, rdq, rdk, rdv, aq, ak, av):
        def zero(rows):
            for acc in (aq, ak, av):
                acc[rows, :] = jnp.zeros((rows.size, LANES), F32)

        _rows_loop(s, 256, zero)
        for d in DILATIONS:
            seq = s // d
            nb, seg = seq // ATTN_BLOCK, seq + ATTN_BLOCK
            for src, dst, pad in ((q_ref, rq, 0), (do_ref, rdo, 0), (k_ref, rk, ATTN_BLOCK), (v_ref, rv, ATTN_BLOCK)):
                if d == 1:
                    _to_residues(src, dst, d, seq, pad)
                else:
                    nat[...] = src[...].astype(F32)
                    _to_residues(nat, dst, d, seq, pad)
            _to_residues(l_ref, rl, d, seq, 0)
            _to_residues(dl_ref, rdl, d, seq, 0)

            def clear(rows):
                rdk[rows, :] = jnp.zeros((rows.size, LANES), F32)
                rdv[rows, :] = jnp.zeros((rows.size, LANES), F32)

            _rows_loop(d * seg, ATTN_BLOCK, clear)

            def block(idx, carry):
                r, b = idx // nb, idx % nb
                qrows = pl.ds(pl.multiple_of(r * seq + b * ATTN_BLOCK, ATTN_BLOCK), ATTN_BLOCK)
                krows = pl.ds(pl.multiple_of(r * seg + b * ATTN_BLOCK, ATTN_BLOCK), 2 * ATTN_BLOCK)
                qb, dob, kw, vw = rq[qrows, :], rdo[qrows, :], rk[krows, :], rv[krows, :]
                lse_b = jnp.concatenate([rl[qrows, :]] * 2, axis=1)
                dl_b = jnp.concatenate([rdl[qrows, :]] * 2, axis=1)
                sc = jnp.where(_band_mask(b), _dot_nt(qb, kw) * scale, NEG_INF)
                p = jnp.exp(sc - lse_b)
                ds = (p * (_dot_nt(dob, vw) - dl_b) * scale).astype(BF16)
                rdq[qrows, :] = jnp.dot(ds, kw, preferred_element_type=F32)
                rdk[krows, :] += _dot_tn(ds, qb)
                rdv[krows, :] += _dot_tn(p.astype(BF16), dob)
                return carry

            lax.fori_loop(0, d * nb, block, 0, unroll=ATTN_UNROLL)
            for r in range(d):
                for acc, src, off in ((aq, rdq, r * seq), (ak, rdk, r * seg + ATTN_BLOCK), (av, rdv, r * seg + ATTN_BLOCK)):
                    rows = pl.ds(r, seq, stride=d) if d > 1 else pl.ds(0, seq)
                    acc[rows, :] += src[pl.ds(off, seq), :]

        def finish(rows):
            for i, acc in enumerate((aq, ak, av)):
                out_ref[i, rows, :] = acc[rows, :].astype(BF16)

        _rows_loop(s, 256, finish)

    head = lambda i: pl.BlockSpec((None, s, HEAD_DIM), lambda h, i=i: (i, 0, h))
    col = pl.BlockSpec((s, HEAD_DIM), lambda h: (0, h))
    f32 = lambda rows: pltpu.VMEM((rows, LANES), F32)
    b16 = lambda rows: pltpu.VMEM((rows, LANES), BF16)
    return pl.pallas_call(
        body, name=name, grid=(heads,),
        in_specs=[head(0), head(1), head(2), col, col, col],
        out_specs=pl.BlockSpec((3, s, HEAD_DIM), lambda h: (0, 0, h)),
        out_shape=jax.ShapeDtypeStruct((3, s, dm), BF16),
        scratch_shapes=[f32(s), b16(s), b16(s), b16(pad_rows), b16(pad_rows), f32(s), f32(s),
                        f32(s), f32(pad_rows), f32(pad_rows), f32(s), f32(s), f32(s)],
        compiler_params=_cparams(1),
    )(qkv, qkv, qkv, do, lse, delta)


def _ew(fn, ins, out_dtypes, *, name, tile_bytes=1 << 20):
    first = ins[0][0] if isinstance(ins[0], tuple) else ins[0]
    rows, cols = first.shape[-2], first.shape[-1]
    tr = _tile(rows, max(16, tile_bytes // (4 * cols)), 16)
    n_in = len(ins)

    def body(*refs):
        outs = fn(*[r[...] for r in refs[:n_in]])
        for o_ref, val in zip(refs[n_in:], outs):
            o_ref[...] = val.astype(o_ref.dtype)

    in_specs, args = [], []
    for item in ins:
        if isinstance(item, tuple):
            arr, lead = item
            in_specs.append(pl.BlockSpec((None, tr, cols), lambda i, lead=lead: (lead, i, 0)))
            args.append(arr)
        else:
            in_specs.append(pl.BlockSpec((tr, cols), lambda i: (i, 0)))
            args.append(item)
    blk = pl.BlockSpec((tr, cols), lambda i: (i, 0))
    return pl.pallas_call(
        body, name=name, grid=(rows // tr,),
        in_specs=in_specs, out_specs=[blk] * len(out_dtypes),
        out_shape=[jax.ShapeDtypeStruct((rows, cols), dt) for dt in out_dtypes],
        compiler_params=_cparams(1),
    )(*args)


def _adamw_math(g, w, m, v):
    m2 = ADAM_B1 * m + (1.0 - ADAM_B1) * g
    v2 = ADAM_B2 * v + (1.0 - ADAM_B2) * (g * g)
    m_hat = m2 / (1.0 - ADAM_B1 ** ADAM_STEP)
    v_hat = v2 / (1.0 - ADAM_B2 ** ADAM_STEP)
    delta = -ADAM_LR * (m_hat / (jnp.sqrt(v_hat) + ADAM_EPS) + ADAM_WD * w)
    return g, delta, m2, v2


def _scalars(*vals):
    return jnp.stack([jnp.asarray(v, jnp.int32) for v in vals])


def _adamw_halves(mine, theirs, core, w, m, v, *, name, lead=0, prev=None):
    shape = w.shape
    a_n, rh, cols = mine.shape
    w3, m3, v3 = (t.reshape(-1, 2 * rh, cols) for t in (w, m, v))
    tr, tc = _tile2(rh, cols, 1 << 17, 8)
    n_i = rh // tr

    def body(c_ref, mine_ref, theirs_ref, w_ref, m_ref, v_ref, *rest):
        g = jnp.where(pl.program_id(1) == c_ref[0], mine_ref[...], theirs_ref[...])
        outs = _adamw_math(g, w_ref[...], m_ref[...], v_ref[...])
        for ref, val in zip(rest[-4:], outs):
            ref[...] = val

    half = pl.BlockSpec((None, tr, tc), lambda a, h, i, j, c_ref: (a, i, j))
    full = pl.BlockSpec((None, tr, tc), lambda a, h, i, j, c_ref: (lead + a, h * n_i + i, j))
    args = [mine, theirs, w3, m3, v3]
    in_specs = [half, half, full, full, full]
    aliases = {}
    if prev is not None:
        args += [p.reshape(w3.shape) for p in prev]
        in_specs += [ANY] * 4
        aliases = {6 + k: k for k in range(4)}
    outs = pl.pallas_call(
        body, name=name,
        grid_spec=pltpu.PrefetchScalarGridSpec(
            num_scalar_prefetch=1, grid=(a_n, 2, n_i, cols // tc), in_specs=in_specs, out_specs=[full] * 4),
        out_shape=[jax.ShapeDtypeStruct(w3.shape, F32)] * 4,
        input_output_aliases=aliases,
        compiler_params=_cparams(4),
    )(_scalars(core), *args)
    return [o.reshape(shape) for o in outs]


def _cast_into(src, buf, lead, slot, *, name, buf_shape=None, dtype=BF16):
    a_n, rows, cols = src.shape
    tr = _tile(rows, max(16, (1 << 21) // (4 * cols)), 16)

    def body(slot_ref, src_ref, *rest):
        rest[-1][...] = src_ref[...].astype(rest[-1].dtype)

    in_specs = [pl.BlockSpec((None, tr, cols), lambda a, i, slot_ref: (a, i, 0))]
    args = [src]
    aliases = {}
    if buf is not None:
        in_specs.append(ANY)
        args.append(buf)
        aliases = {2: 0}
        buf_shape, dtype = buf.shape, buf.dtype
    return pl.pallas_call(
        body, name=name,
        grid_spec=pltpu.PrefetchScalarGridSpec(
            num_scalar_prefetch=1, grid=(a_n, rows // tr), in_specs=in_specs,
            out_specs=pl.BlockSpec((None, None, tr, cols), lambda a, i, slot_ref: (lead + a, slot_ref[0], i, 0))),
        out_shape=jax.ShapeDtypeStruct(buf_shape, dtype),
        input_output_aliases=aliases,
        compiler_params=_cparams(2),
    )(_scalars(slot), *args)


def _place():
    x, y, c = lax.axis_index("x"), lax.axis_index("y"), lax.axis_index("c")
    chips = [(1 - x, y), (x, 1 - y), (1 - x, 1 - y)]
    return x, y, c, chips


def _remote(src, dst, send_sem, recv_sem, dev):
    return pltpu.make_async_remote_copy(src_ref=src, dst_ref=dst, send_sem=send_sem, recv_sem=recv_sem,
                                        device_id=dev, device_id_type=MESH)


def _gather_comm(bufs, pieces):
    def plan(refs, send_sems, recv_sems):
        x, y, c, chips = _place()

        def region(p, q, core):
            t, a0, a1, part, n_parts = pieces[p]
            rh = bufs[t].shape[2] // 2
            sub = rh // n_parts
            return refs[t].at[pl.ds(a0, a1 - a0), q, pl.ds(core * rh + part * sub, sub), :]

        def ici(p, j, q):
            cx, cy = chips[j]
            return _remote(region(p, q, c), region(p, q, c), send_sems.at[6 * p + j], recv_sems.at[6 * p + j], (cx, cy, c))

        def d2d(p, j, core):
            cx, cy = chips[j]
            rows = region(p, 2 * cx + cy, core)
            return _remote(rows, rows, send_sems.at[6 * p + 3 + j], recv_sems.at[6 * p + 3 + j], (x, y, 1 - c))

        return 2 * x + y, c, [2 * cx + cy for cx, cy in chips], ici, d2d

    todo = [(j, p) for j in range(3) for p in range(len(pieces))]

    def start(refs, send_sems, recv_sems):
        q_me, _, _, ici, _ = plan(refs, send_sems, recv_sems)
        for j, p in todo:
            ici(p, j, q_me).start()

    def finish(refs, send_sems, recv_sems):
        q_me, c, q_of, ici, d2d = plan(refs, send_sems, recv_sems)
        for j, p in todo:
            ici(p, j, q_of[j]).wait_recv()
            d2d(p, j, c).start()
        for j, p in todo:
            d2d(p, j, 1 - c).wait_recv()
        for j, p in todo:
            ici(p, j, q_me).wait_send()
            d2d(p, j, c).wait_send()

    return _Comm(bufs, 6 * len(pieces), start, finish)


def _chips_comm(parts, lands):
    n = len(parts)

    def copy(refs, send_sems, recv_sems, t, j, q_src, q_dst):
        x, y, c, chips = _place()
        cx, cy = chips[j]
        return _remote(refs[t].at[:, q_src], refs[n + t].at[q_dst], send_sems.at[3 * t + j], recv_sems.at[3 * t + j],
                       (cx, cy, c))

    todo = [(j, t) for j in range(3) for t in range(n)]

    def qs():
        x, y, _, chips = _place()
        return 2 * x + y, [2 * cx + cy for cx, cy in chips]

    def start(refs, send_sems, recv_sems):
        q_me, q_of = qs()
        for j, t in todo:
            copy(refs, send_sems, recv_sems, t, j, q_of[j], q_me).start()

    def finish(refs, send_sems, recv_sems):
        q_me, q_of = qs()
        for j, t in todo:
            copy(refs, send_sems, recv_sems, t, j, q_me, q_of[j]).wait_recv()
        for j, t in todo:
            copy(refs, send_sems, recv_sems, t, j, q_of[j], q_me).wait_send()

    return _Comm(list(parts) + list(lands), 3 * n, start, finish)


def _comm_call(comm, *, name):
    k = len(comm.operands)

    def body(*refs):
        operands, (send_sems, recv_sems) = refs[k:2 * k], refs[2 * k:]
        comm.start(operands, send_sems, recv_sems)
        comm.finish(operands, send_sems, recv_sems)

    return pl.pallas_call(
        body, name=name, in_specs=[ANY] * k, out_specs=[ANY] * k,
        out_shape=[jax.ShapeDtypeStruct(t.shape, t.dtype) for t in comm.operands],
        input_output_aliases={i: i for i in range(k)},
        scratch_shapes=[pltpu.SemaphoreType.DMA((comm.n_sems,)), pltpu.SemaphoreType.DMA((comm.n_sems,))],
    )(*comm.operands)


def _rs_sibling(grads, *, name):
    n = len(grads)
    halves = [jax.ShapeDtypeStruct(g.shape[:2] + (g.shape[2] // 2, g.shape[3]), g.dtype) for g in grads]

    def body(*refs):
        src, theirs = refs[:n], refs[n:2 * n]
        send_sems, recv_sems = refs[2 * n:]
        x, y, c, _ = _place()
        ops = []
        for t in range(n):
            rh = grads[t].shape[2] // 2
            give = _remote(src[t].at[:, :, pl.ds((1 - c) * rh, rh), :], theirs[t], send_sems.at[t], recv_sems.at[t],
                           (x, y, 1 - c))
            give.start()
            ops.append(give)
        for op in ops:
            op.wait()

    return pl.pallas_call(
        body, name=name, in_specs=[ANY] * n, out_specs=[ANY] * n, out_shape=halves,
        scratch_shapes=[pltpu.SemaphoreType.DMA((n,)), pltpu.SemaphoreType.DMA((n,))],
    )(*grads)


def _rs_add(grad, theirs, core, slot, *, name):
    a_n, _, rh, cols = theirs.shape
    tr, tc = _tile2(rh, cols, 1 << 18, 16)
    n_i = rh // tr

    def body(s_ref, g_ref, t_ref, p_ref, y_ref):
        part = (g_ref[...].astype(F32) + t_ref[...].astype(F32)).astype(BF16)
        p_ref[...] = part

        @pl.when(pl.program_id(3) == s_ref[1])
        def _():
            y_ref[...] = part

    blk = (None, None, tr, tc)
    return pl.pallas_call(
        body, name=name,
        grid_spec=pltpu.PrefetchScalarGridSpec(
            num_scalar_prefetch=1, grid=(a_n, n_i, cols // tc, N_CHIPS),
            in_specs=[pl.BlockSpec(blk, lambda a, i, j, q, s: (a, q, s[0] * n_i + i, j)),
                      pl.BlockSpec(blk, lambda a, i, j, q, s: (a, q, i, j))],
            out_specs=[pl.BlockSpec(blk, lambda a, i, j, q, s: (a, q, i, j)),
                       pl.BlockSpec(blk, lambda a, i, j, q, s: (s[1], a, i, j))]),
        out_shape=[jax.ShapeDtypeStruct(theirs.shape, BF16),
                   jax.ShapeDtypeStruct((N_CHIPS, a_n, rh, cols), BF16)],
        compiler_params=_cparams(4),
    )(_scalars(core, slot), grad, theirs)


def _rs_finish(halves, *, name):
    n = len(halves)

    def body(*refs):
        src, dst = refs[:n], refs[n:2 * n]
        send_sems, recv_sems = refs[2 * n:]
        x, y, c, _ = _place()
        ops = []
        for t in range(n):
            give = _remote(src[t], dst[t], send_sems.at[t], recv_sems.at[t], (x, y, 1 - c))
            give.start()
            ops.append(give)
        for op in ops:
            op.wait()

    return pl.pallas_call(
        body, name=name, in_specs=[ANY] * n, out_specs=[ANY] * n,
        out_shape=[jax.ShapeDtypeStruct(h.shape, h.dtype) for h in halves],
        scratch_shapes=[pltpu.SemaphoreType.DMA((n,)), pltpu.SemaphoreType.DMA((n,))],
    )(*halves)


def _all_reduce_small(vec, *, name):
    rows = vec.shape[0]

    def body(v_ref, o_ref, land, send_sems, recv_sems):
        x, y, c, _ = _place()
        me = 4 * x + 2 * y + c
        land[me] = v_ref[...]
        flips = [(fx, fy, fc) for fx in (0, 1) for fy in (0, 1) for fc in (0, 1)][1:]
        sent = []
        for k, (fx, fy, fc) in enumerate(flips):
            cp = _remote(v_ref, land.at[me], send_sems.at[k], recv_sems.at[k], (x ^ fx, y ^ fy, c ^ fc))
            cp.start()
            sent.append(cp)
        for k, (fx, fy, fc) in enumerate(flips):
            peer = 4 * (x ^ fx) + 2 * (y ^ fy) + (c ^ fc)
            _remote(v_ref, land.at[peer], send_sems.at[k], recv_sems.at[k], (x ^ fx, y ^ fy, c ^ fc)).wait_recv()
        for cp in sent:
            cp.wait_send()
        total = land[0]
        for dev in range(1, 8):
            total = total + land[dev]
        o_ref[...] = total

    whole = pl.BlockSpec(memory_space=pltpu.VMEM)
    return pl.pallas_call(
        body, name=name, in_specs=[whole], out_specs=whole,
        out_shape=jax.ShapeDtypeStruct(vec.shape, F32),
        scratch_shapes=[pltpu.VMEM((8, rows, LANES), F32), pltpu.SemaphoreType.DMA((7,)), pltpu.SemaphoreType.DMA((7,))],
        compiler_params=pltpu.CompilerParams(vmem_limit_bytes=VMEM_LIMIT),
    )(vec)


def _pack(parts, mult=16):
    flat = jnp.concatenate([p.reshape(-1).astype(F32) for p in parts])
    rows = -(-flat.shape[0] // (LANES * mult)) * mult
    return jnp.pad(flat, (0, rows * LANES - flat.shape[0])).reshape(rows, LANES)


def _unpack(vec, shapes):
    flat, out, pos = vec.reshape(-1), [], 0
    for shp in shapes:
        size = math.prod(shp)
        out.append(flat[pos:pos + size].reshape(shp))
        pos += size
    return out


def kernel(x, pool_w_in, pool_w_grp, pool_scale, pool_w_out, attn_w_q, attn_w_o, shared_w_k, shared_w_v, ffn_w_up, ffn_conv_w, ffn_conv_b, ffn_w_down, ln1_g, ln1_b, ln2_g, ln2_b, loss_target, m_pool_w_in, m_pool_w_grp, m_pool_scale, m_pool_w_out, m_attn_w_q, m_attn_w_o, m_shared_w_k, m_shared_w_v, m_ffn_w_up, m_ffn_conv_w, m_ffn_conv_b, m_ffn_w_down, m_ln1_g, m_ln1_b, m_ln2_g, m_ln2_b, v_pool_w_in, v_pool_w_grp, v_pool_scale, v_pool_w_out, v_attn_w_q, v_attn_w_o, v_shared_w_k, v_shared_w_v, v_ffn_w_up, v_ffn_conv_w, v_ffn_conv_b, v_ffn_w_down, v_ln1_g, v_ln1_b, v_ln2_g, v_ln2_b):
    s, d = x.shape[1], x.shape[2]
    n2 = ffn_w_up.shape[2]
    fq = ffn_w_down.shape[1]
    assert 2 * fq == n2 and d % N_CHIPS == 0
    g_n, cg = pool_w_grp.shape[1], pool_w_grp.shape[3]
    xs, tgt = x[0], loss_target[0]
    q_me = 2 * lax.axis_index("x") + lax.axis_index("y")
    core = lax.axis_index("c")
    rq = d // N_CHIPS

    six_g = None
    for i, w_ in enumerate((pool_w_in[0], pool_w_out[0], attn_w_q[0], shared_w_k, shared_w_v, attn_w_o[0])):
        six_g = _cast_into(w_[None], six_g, i, q_me, name=f"cast_w{i}", buf_shape=(6, N_CHIPS, rq, d))
    grp_g = _cast_into(pool_w_grp[0], None, 0, q_me, name="cast_grp", buf_shape=(g_n, N_CHIPS, cg // N_CHIPS, cg))
    up_t, m_up_t, v_up_t = (jnp.swapaxes(t, 1, 2) for t in (ffn_w_up, m_ffn_w_up, v_ffn_w_up))
    up_g = _cast_into(up_t, None, 0, q_me, name="cast_up", buf_shape=(2, N_CHIPS, n2, d))
    dn_g = _cast_into(ffn_w_down, None, 0, q_me, name="cast_down", buf_shape=(2, N_CHIPS, fq, d))
    small = _pack([ffn_conv_w, pool_scale])
    small_g = _cast_into(small[None], None, 0, q_me, name="cast_small", buf_shape=(1, N_CHIPS) + small.shape, dtype=F32)
    bufs = [six_g, grp_g, up_g, dn_g, small_g]
    SIX, GRP, UP, DN, SMALL = range(5)
    bufs = list(_comm_call(_gather_comm(bufs, [(SIX, IW_IN, IW_OUT + 1, 0, 1), (GRP, 0, g_n, 0, 1), (SMALL, 0, 1, 0, 1),
                                               (UP, 0, 1, 0, 1)]), name="gather_first"))
    small_q = bufs[SMALL].reshape(N_CHIPS, -1)
    n_cw = 2 * 3 * n2
    conv_w = small_q[:, :n_cw].reshape(N_CHIPS, 2, 3, n2).transpose(1, 0, 2, 3)
    scale_full = small_q[:, n_cw:n_cw + d // N_CHIPS].reshape(1, d)
    conv_b = ffn_conv_b.reshape(2, N_CHIPS, 1, n2)

    def w6():
        return bufs[SIX].reshape(6, d, d)

    def up8():
        return bufs[UP].reshape(2 * N_CHIPS, n2, d)

    def dn4():
        return bufs[DN].reshape(4, n2, d)

    def gathered(pieces):
        used = sorted({pc[0] for pc in pieces})
        local = [(used.index(t), a0, a1, part, n_parts) for t, a0, a1, part, n_parts in pieces]
        return _gather_comm([bufs[t] for t in used], local), used

    def store(used, operands):
        for t, arr in zip(used, operands):
            bufs[t] = arr

    def ffn_fwd(l, hb, carry_up, carry_down):
        comm, used = gathered(carry_up)
        u, landed = _mm(hb[None], up8(), mode="nt", name=f"ffn{l}_up", out_dtype=BF16, n_q=4, tm=_tile(s, 512), tn=n2,
                        tk=_tile(d, 1024), qa=_q0, qb=lambda q, qr: 4 * l + _perm(q), comm=comm)
        store(used, landed)
        u = u.reshape(2, 2, s, n2)
        act = _act_fwd(u, conv_w[l], conv_b[l], name=f"ffn{l}_act")
        comm, used = gathered(carry_down) if carry_down else (None, None)
        ff = _mm(act, dn4(), mode="nn", name=f"ffn{l}_down", out_dtype=F32, n_q=1, n_qr=2, tm=_tile(s, 512),
                 tn=_tile(d, 1024), tk=n2, qa=_qr, qb=lambda q, qr: 2 * l + qr, qo=lambda q: 0, comm=comm)
        if comm is not None:
            ff, landed = ff
            store(used, landed)
        return u, act, ff[0]

    p = _mm_dd(xs[None], w6(), IW_IN, mode="nn", name="pool_in", out_dtype=F32)[0]
    pooled = _pool(p, backward=False, name="pool_fwd")
    mg, mixed = _grp_fwd(pooled, bufs[GRP].reshape(g_n, cg, cg), scale_full, name="pool_grp")
    mix0 = _mm_dd(mixed[None], w6(), IW_OUT, mode="nn", name="pool_out", out_dtype=F32)[0]
    h1, h1b, xh1, rs1 = _ln_fwd(xs, mix0, ln1_g[0:1], ln1_b[0:1], name="ln1_0")
    u0, act0, ff0 = ffn_fwd(0, h1b, [(DN, 0, 1, 0, 1), (SIX, IW_Q, IW_O + 1, 0, 1)], [(UP, 1, 2, 0, 2)])
    h2, h2b, xh2, rs2 = _ln_fwd(h1, ff0, ln2_g[0:1], ln2_b[0:1], name="ln2_0")

    comm, used = gathered([(UP, 1, 2, 1, 2)])
    qkv, landed = _mm_dd(h2b[None], w6(), IW_Q, mode="nn", name="attn_qkv", out_dtype=BF16, n_q=3, comm=comm)
    store(used, landed)
    o, ob, lse = _attn_fwd(qkv, name="attn_fwd")
    mix1 = _mm_dd(ob[None], w6(), IW_O, mode="nn", name="attn_out", out_dtype=F32)[0]
    h3, h3b, xh3, rs3 = _ln_fwd(h2, mix1, ln1_g[1:2], ln1_b[1:2], name="ln1_1")
    u1, act1, ff1 = ffn_fwd(1, h3b, [(DN, 1, 2, 0, 1)], None)
    h4, _, xh4, rs4 = _ln_fwd(h3, ff1, ln2_g[1:2], ln2_b[1:2], name="ln2_1")
    dh4, loss_local = _loss_head(h4, tgt, name="loss_head")

    rs_parts, rs_lands = {}, {}

    def rs_prepare(items):
        theirs = _rs_sibling([g_ for _, g_ in items], name="rs_sibling_" + items[0][0])
        for (nm, g_), t_ in zip(items, theirs):
            rs_parts[nm], rs_lands[nm] = _rs_add(g_, t_, core, q_me, name=f"rs_add_{nm}")

    def rs_exchange(names):
        return _chips_comm([rs_parts[nm] for nm in names], [rs_lands[nm] for nm in names])

    def rs_landed(names, operands):
        for nm, land in zip(names, operands[len(names):]):
            rs_lands[nm] = land

    d_conv_w, d_conv_b = [None, None], [None, None]

    def ffn_bwd(l, dzb, u, act, hb, carry_dact):
        da = _mm(dzb[None], dn4(), mode="nt", name=f"ffn{l}_dact", out_dtype=F32, n_q=2, tm=_tile(s, 512), tn=n2,
                 tk=_tile(d, 512), qa=_q0, qb=lambda q, qr: 2 * l + q,
                 comm=rs_exchange(carry_dact) if carry_dact else None)
        if carry_dact:
            da, landed = da
            rs_landed(carry_dact, landed)
        du, dwg, dwv, dbg, dbv = _act_bwd(u, da, conv_w[l], conv_b[l], name=f"ffn{l}_dconv")
        d_conv_w[l] = jnp.concatenate([dwg, dwv], axis=0)
        d_conv_b[l] = jnp.concatenate([dbg, dbv], axis=0)
        du4 = du.reshape(4, s, n2)
        g_dn = _mm(act, dzb[None], mode="tn", name=f"ffn{l}_gdown", out_dtype=BF16, n_q=2, tm=n2, tn=_tile(d, 1024),
                   tk=_tile(s, 512), qa=_qq, qb=_q0)
        rs_prepare([(f"dn{l}", g_dn.reshape(1, N_CHIPS, fq, d))])
        g_up, landed = _mm(du4, hb[None], mode="tn", name=f"ffn{l}_gup", out_dtype=BF16, n_q=4, tm=n2, tn=_tile(d, 1024),
                           tk=_tile(s, 512), qa=lambda q, qr: _perm(q), qb=_q0, comm=rs_exchange([f"dn{l}"]))
        rs_landed([f"dn{l}"], landed)
        rs_prepare([(f"up{l}", g_up.reshape(1, N_CHIPS, n2, d))])
        dh, landed = _mm(du4, up8(), mode="nn", name=f"ffn{l}_dh", out_dtype=F32, n_q=1, n_qr=4, tm=_tile(s, 512),
                         tn=_tile(d, 1024), tk=n2, qa=_qr, qb=lambda q, qr: 4 * l + _perm(qr), qo=lambda q: 0,
                         comm=rs_exchange([f"up{l}"]))
        rs_landed([f"up{l}"], landed)
        return dh[0]

    dz4, dz4b, dg_ln2_1, db_ln2_1 = _ln_bwd(None, dh4, xh4, rs4, ln2_g[1:2], name="dln2_1")
    dh3 = ffn_bwd(1, dz4b, u1, act1, h3b, None)
    dz3, dz3b, dg_ln1_1, db_ln1_1 = _ln_bwd(dz4, dh3, xh3, rs3, ln1_g[1:2], name="dln1_1")
    g_wo = _mm_wgrad(ob[None], dz3b[None], name="attn_gwo")
    do = _mm_dd(dz3b[None], w6(), IW_O, mode="nt", name="attn_do", out_dtype=BF16)[0]
    delta = _attn_delta(do, o, name="attn_delta")
    dqkv = _attn_bwd(qkv, do, lse, delta, name="attn_bwd")
    g_wq = _mm_wgrad(h2b[None], dqkv, name="attn_gwq", qb=lambda q, qr: 0)
    g_wk = _mm_wgrad(h2b[None], dqkv, name="attn_gwk", qb=lambda q, qr: 1)
    g_wv = _mm_wgrad(h2b[None], dqkv, name="attn_gwv", qb=lambda q, qr: 2)
    rs_prepare([(nm, g_.reshape(1, N_CHIPS, rq, d)) for nm, g_ in
                (("wo", g_wo), ("wq", g_wq), ("wk", g_wk), ("wv", g_wv))])
    dh2, landed = _mm_dd(dqkv, w6(), IW_Q, mode="nt", name="attn_dh", out_dtype=F32, n_qr=3, qa=_qr,
                         comm=rs_exchange(["wo", "wq", "wk"]))
    rs_landed(["wo", "wq", "wk"], landed)
    dz2, dz2b, dg_ln2_0, db_ln2_0 = _ln_bwd(dz3, dh2[0], xh2, rs2, ln2_g[0:1], name="dln2_0")
    dh1 = ffn_bwd(0, dz2b, u0, act0, h1b, ["wv"])
    dz1, dz1b, dg_ln1_0, db_ln1_0 = _ln_bwd(dz2, dh1, xh1, rs1, ln1_g[0:1], name="dln1_0")
    g_wout = _mm_wgrad(mixed[None], dz1b[None], name="pool_gwout")
    dmixed = _mm_dd(dz1b[None], w6(), IW_OUT, mode="nt", name="pool_dmixed", out_dtype=F32)[0]
    dmg, d_scale = _grp_bwd_pre(dmixed, mg, scale_full, name="pool_dscale")
    g_wgrp = _grp_mm(pooled, dmg, mode="tn", name="pool_gwgrp", out_dtype=BF16, tm=1024)
    dpooled = _grp_mm(dmg, None, mode="nt", name="pool_dpooled", out_dtype=F32, tm=1024, w=bufs[GRP].reshape(g_n, cg, cg))
    dp = _pool(dpooled, backward=True, name="pool_bwd")
    g_win = _mm_wgrad(xs[None], dp[None], name="pool_gwin")
    dx_mm = _mm_dd(dp[None], w6(), IW_IN, mode="nt", name="pool_dx", out_dtype=F32)[0]
    (grad_x,) = _ew(lambda a, b: (ALPHA * a + b,), [dz1, dx_mm], [F32], name="grad_x")
    last = ["win", "wgrp", "wout"]
    rs_prepare([("win", g_win.reshape(1, N_CHIPS, rq, d)), ("wgrp", g_wgrp.reshape(g_n, N_CHIPS, cg // N_CHIPS, cg)),
                ("wout", g_wout.reshape(1, N_CHIPS, rq, d))])
    rs_landed(last, _comm_call(rs_exchange(last), name="rs_chips_last"))

    rs_names = ["win", "wgrp", "wout", "wq", "wo", "wk", "wv", "up0", "up1", "dn0", "dn1"]
    finished = {}
    for nm in rs_names:
        y4 = rs_lands[nm]
        y3 = y4.reshape(N_CHIPS, -1, y4.shape[-1])
        (tot,) = _ew(lambda a0, a1, a2, a3: (((a0.astype(F32) + a1.astype(F32)) + a2.astype(F32)) + a3.astype(F32),),
                     [(y3, 0), (y3, 1), (y3, 2), (y3, 3)], [F32], name=f"rs_sum_{nm}")
        finished[nm] = tot.reshape(y4.shape[1:])
    others = dict(zip(rs_names, _rs_finish([finished[nm] for nm in rs_names], name="rs_finish")))
    results = {}
    for nm, key, w, m, v in (("pool_w_in", "win", pool_w_in, m_pool_w_in, v_pool_w_in),
                             ("pool_w_grp", "wgrp", pool_w_grp, m_pool_w_grp, v_pool_w_grp),
                             ("pool_w_out", "wout", pool_w_out, m_pool_w_out, v_pool_w_out),
                             ("attn_w_q", "wq", attn_w_q, m_attn_w_q, v_attn_w_q),
                             ("attn_w_o", "wo", attn_w_o, m_attn_w_o, v_attn_w_o),
                             ("shared_w_k", "wk", shared_w_k, m_shared_w_k, v_shared_w_k),
                             ("shared_w_v", "wv", shared_w_v, m_shared_w_v, v_shared_w_v)):
        results[nm] = _adamw_halves(finished[key], others[key], core, w, m, v, name=f"adamw_{nm}")
    for nm, key, w, m, v in (("ffn_w_up", "up", up_t, m_up_t, v_up_t),
                             ("ffn_w_down", "dn", ffn_w_down, m_ffn_w_down, v_ffn_w_down)):
        res = None
        for l in (1, 0):
            res = _adamw_halves(finished[f"{key}{l}"], others[f"{key}{l}"], core, w, m, v, name=f"adamw_{nm}{l}",
                                lead=l, prev=res)
        results[nm] = res
    results["ffn_w_up"] = [jnp.swapaxes(t, 1, 2) for t in results["ffn_w_up"]]

    ln_grads = [jnp.concatenate([a, b], axis=0) for a, b in
                ((dg_ln1_0, dg_ln1_1), (db_ln1_0, db_ln1_1), (dg_ln2_0, dg_ln2_1), (db_ln2_0, db_ln2_1))]
    small_shapes = [(2, N_CHIPS, 3, n2), (2, N_CHIPS, n2)] + [(2, d)] * 4 + [(1, d)]
    vec = _pack([jnp.stack(d_conv_w), jnp.stack(d_conv_b)] + ln_grads + [d_scale], mult=8)
    tot = _unpack(_all_reduce_small(vec, name="allreduce_small"), small_shapes)
    g_cw = lax.dynamic_index_in_dim(tot[0], q_me, axis=1, keepdims=False)
    g_cb = tot[1].reshape(2, N_CHIPS * n2)
    g_scale = lax.dynamic_slice_in_dim(tot[6], q_me * rq, rq, axis=1)
    small_names = ["ffn_conv_w", "ffn_conv_b", "ln1_g", "ln1_b", "ln2_g", "ln2_b", "pool_scale"]
    small_g = [g_cw, g_cb, tot[2], tot[3], tot[4], tot[5], g_scale]
    small_w = [ffn_conv_w, ffn_conv_b, ln1_g, ln1_b, ln2_g, ln2_b, pool_scale]
    small_m = [m_ffn_conv_w, m_ffn_conv_b, m_ln1_g, m_ln1_b, m_ln2_g, m_ln2_b, m_pool_scale]
    small_v = [v_ffn_conv_w, v_ffn_conv_b, v_ln1_g, v_ln1_b, v_ln2_g, v_ln2_b, v_pool_scale]
    packed = _ew(_adamw_math, [_pack(small_g, 8), _pack(small_w, 8), _pack(small_m, 8), _pack(small_v, 8)], [F32] * 4,
                 name="adamw_small")
    shapes = [w.shape for w in small_w]
    unpacked = [_unpack(pk, shapes) for pk in packed]
    for i, nm in enumerate(small_names):
        results[nm] = [unpacked[k][i] for k in range(4)]

    loss = lax.psum(loss_local[0, 0], ("x", "y", "c"))
    order = ["pool_w_in", "pool_w_grp", "pool_scale", "pool_w_out", "attn_w_q", "attn_w_o", "shared_w_k", "shared_w_v",
             "ffn_w_up", "ffn_conv_w", "ffn_conv_b", "ffn_w_down", "ln1_g", "ln1_b", "ln2_g", "ln2_b"]
    outs = [loss, grad_x[None]]
    for k in range(4):
        outs += [results[nm][k] for nm in order]
    return tuple(outs)
```

```python
import functools
import math

import jax
import jax.numpy as jnp
from jax import lax
from jax.experimental import pallas as pl
from jax.experimental.pallas import tpu as pltpu

F32 = jnp.float32
BF16 = jnp.bfloat16

LANES = 128
HEAD_DIM = 128
ATTN_BLOCK = 128
DILATIONS = (1, 4, 16)
ATTN_UNROLL = 8
POOL_WINDOWS = (2, 4, 8, 16)
POOL_HALO = 16
CONV_HALO = 16
DEPTH = 2
ALPHA = (2.0 * DEPTH) ** 0.25
LN_EPS = 1e-5
NEG_INF = -1e30
ADAM_LR = 0.001
ADAM_B1 = 0.9
ADAM_B2 = 0.999
ADAM_EPS = 1e-08
ADAM_WD = 0.01
ADAM_STEP = 10
N_CHIPS = 4
VMEM_LIMIT = 56 * 1024 * 1024
ANY = pl.BlockSpec(memory_space=pl.ANY)
MESH = pl.DeviceIdType.MESH

IW_IN, IW_OUT, IW_Q, IW_K, IW_V, IW_O = range(6)


def _cparams(n_grid):
    return pltpu.CompilerParams(dimension_semantics=("arbitrary",) * n_grid, vmem_limit_bytes=VMEM_LIMIT)


def _tile(dim, pref, align=LANES):
    if dim <= pref:
        return dim
    t = (pref // align) * align
    while t >= align:
        if dim % t == 0:
            return t
        t -= align
    return dim


def _tile2(rows, cols, budget, row_align):
    best = None
    for tc in [cols] + [c for c in range(LANES, cols, LANES) if cols % c == 0]:
        for tr in range(row_align, rows + 1, row_align):
            if rows % tr == 0 and tr * tc <= budget:
                if best is None or (tr * tc, tc) > (best[0] * best[1], best[1]):
                    best = (tr, tc)
    return best if best is not None else (rows, cols)


def _perm(q):
    return (q % 2) * 2 + q // 2


_DIMS = {"nn": (((1,), (0,)), ((), ())), "nt": (((1,), (1,)), ((), ())), "tn": (((0,), (0,)), ((), ()))}


class _Comm:
    def __init__(self, operands, n_sems, start, finish):
        self.operands, self.n_sems, self.start, self.finish = list(operands), n_sems, start, finish


def _mm(a, b, *, mode, name, out_dtype, n_q, tm, tn, tk, qa, qb, qo=lambda q: q, n_qr=1, out_q=None, comm=None):
    if mode == "nn":
        m, kdim, n = a.shape[1], a.shape[2], b.shape[2]
    elif mode == "nt":
        m, kdim, n = a.shape[1], a.shape[2], b.shape[1]
    else:
        kdim, m, n = a.shape[1], a.shape[2], b.shape[2]
    assert m % tm == 0 and n % tn == 0 and kdim % tk == 0, (name, m, n, kdim, tm, tn, tk)
    kr_n = kdim // tk
    nr = n_qr * kr_n
    out_q = n_q if out_q is None else out_q

    def split(r):
        return (r // kr_n, r % kr_n) if n_qr > 1 else (0, r)

    if mode == "tn":
        a_spec = pl.BlockSpec((None, tk, tm), lambda q, i, j, r: (qa(q, split(r)[0]), split(r)[1], i))
    else:
        a_spec = pl.BlockSpec((None, tm, tk), lambda q, i, j, r: (qa(q, split(r)[0]), i, split(r)[1]))
    if mode == "nt":
        b_spec = pl.BlockSpec((None, tn, tk), lambda q, i, j, r: (qb(q, split(r)[0]), j, split(r)[1]))
    else:
        b_spec = pl.BlockSpec((None, tk, tn), lambda q, i, j, r: (qb(q, split(r)[0]), split(r)[1], j))
    o_spec = pl.BlockSpec((None, tm, tn), lambda q, i, j, r: (qo(q), i, j))
    dims = _DIMS[mode]

    n_comm = len(comm.operands) if comm is not None else 0
    grid = (n_q, m // tm, n // tn, nr)

    def body(*refs):
        a_ref, b_ref = refs[0], refs[1]
        o_ref = refs[2 + n_comm]
        if comm is not None:
            comm_refs = refs[3 + n_comm:3 + 2 * n_comm]
            send_sems, recv_sems = refs[-2:]
            ids = [pl.program_id(ax) for ax in range(4)]
            first = functools.reduce(jnp.logical_and, [i == 0 for i in ids])
            last = functools.reduce(jnp.logical_and, [i == g - 1 for i, g in zip(ids, grid)])

            @pl.when(first)
            def _():
                comm.start(comm_refs, send_sems, recv_sems)
        lhs, rhs = a_ref[...], b_ref[...]
        if lhs.dtype != BF16:
            lhs = lhs.astype(BF16)
        if rhs.dtype != BF16:
            rhs = rhs.astype(BF16)
        part = lax.dot_general(lhs, rhs, dims, preferred_element_type=F32)
        if nr == 1:
            o_ref[...] = part.astype(o_ref.dtype)
        else:
            acc_ref = refs[3 + 2 * n_comm]
            r = pl.program_id(3)

            @pl.when(r == 0)
            def _():
                acc_ref[...] = part

            @pl.when(r > 0)
            def _():
                acc_ref[...] += part

            @pl.when(r == nr - 1)
            def _():
                o_ref[...] = acc_ref[...].astype(o_ref.dtype)
        if comm is not None:
            @pl.when(last)
            def _():
                comm.finish(comm_refs, send_sems, recv_sems)

    scratch = [pltpu.VMEM((tm, tn), F32)] if nr > 1 else []
    out_shape = [jax.ShapeDtypeStruct((out_q, m, n), out_dtype)]
    args = [a, b]
    if comm is not None:
        args += comm.operands
        out_shape += [jax.ShapeDtypeStruct(t.shape, t.dtype) for t in comm.operands]
        scratch += [pltpu.SemaphoreType.DMA((comm.n_sems,)), pltpu.SemaphoreType.DMA((comm.n_sems,))]
    outs = pl.pallas_call(
        body, name=name, grid=grid,
        in_specs=[a_spec, b_spec] + [ANY] * n_comm, out_specs=[o_spec] + [ANY] * n_comm,
        out_shape=out_shape, scratch_shapes=scratch,
        input_output_aliases={2 + i: 1 + i for i in range(n_comm)},
        compiler_params=_cparams(4),
    )(*args)
    return (outs[0], outs[1:]) if comm is not None else outs[0]


def _q0(q, qr):
    return 0


def _qq(q, qr):
    return q


def _qr(q, qr):
    return qr


def _mm_dd(a3, w6, widx, *, mode, name, out_dtype, n_q=1, n_qr=1, qa=_q0, comm=None):
    d = w6.shape[1]
    f32_in = a3.dtype != BF16
    tm = _tile(a3.shape[1] if mode != "tn" else a3.shape[2], 512 if f32_in else 1024)
    if n_qr > 1:
        qb = lambda q, qr: widx + qr
    elif n_q > 1:
        qb = lambda q, qr: widx + q
    else:
        qb = lambda q, qr: widx
    return _mm(a3, w6, mode=mode, name=name, out_dtype=out_dtype, n_q=n_q, n_qr=n_qr,
               tm=tm, tn=_tile(d, 1024), tk=d, qa=qa, qb=qb, comm=comm)


def _mm_wgrad(a3, b3, *, name, n_q=1, qa=_q0, qb=_q0):
    s, d = a3.shape[1], a3.shape[2]
    f32_in = a3.dtype != BF16 or b3.dtype != BF16
    return _mm(a3, b3, mode="tn", name=name, out_dtype=BF16, n_q=n_q, tm=_tile(d, 1024), tn=b3.shape[2],
               tk=_tile(s, 512 if f32_in else 1024), qa=qa, qb=qb)


def _grp_mm(a, b, *, mode, name, out_dtype, tm, w=None):
    s = a.shape[0]
    if mode == "tn":
        cg = b.shape[1] // len(POOL_WINDOWS)
        g_n = len(POOL_WINDOWS)
        ts = _tile(s, tm)
        nr = s // ts

        def body(a_ref, b_ref, o_ref, acc_ref):
            r = pl.program_id(1)
            part = lax.dot_general(a_ref[...], b_ref[...], _DIMS["tn"], preferred_element_type=F32)

            @pl.when(r == 0)
            def _():
                acc_ref[...] = part

            @pl.when(r > 0)
            def _():
                acc_ref[...] += part

            @pl.when(r == nr - 1)
            def _():
                o_ref[...] = acc_ref[...].astype(o_ref.dtype)

        return pl.pallas_call(
            body, name=name, grid=(g_n, nr),
            in_specs=[pl.BlockSpec((ts, cg), lambda g, r: (r, g)), pl.BlockSpec((ts, cg), lambda g, r: (r, g))],
            out_specs=pl.BlockSpec((None, cg, cg), lambda g, r: (g, 0, 0)),
            out_shape=jax.ShapeDtypeStruct((g_n, cg, cg), out_dtype),
            scratch_shapes=[pltpu.VMEM((cg, cg), F32)],
            compiler_params=_cparams(2),
        )(a, b)
    g_n, cg = w.shape[0], w.shape[1]
    ts = _tile(s, tm)
    dims = _DIMS[mode]

    def body(a_ref, w_ref, o_ref):
        o_ref[...] = lax.dot_general(a_ref[...], w_ref[...], dims, preferred_element_type=F32).astype(o_ref.dtype)

    return pl.pallas_call(
        body, name=name, grid=(g_n, s // ts),
        in_specs=[pl.BlockSpec((ts, cg), lambda g, i: (i, g)), pl.BlockSpec((None, cg, cg), lambda g, i: (g, 0, 0))],
        out_specs=pl.BlockSpec((ts, cg), lambda g, i: (i, g)),
        out_shape=jax.ShapeDtypeStruct((s, g_n * cg), out_dtype),
        compiler_params=_cparams(2),
    )(a, w)


def _causal_ext(load, r0, rows, halo):
    cur = load(r0, rows)
    prev = load(pl.multiple_of(jnp.maximum(r0 - halo, 0), halo), halo)
    prev = jnp.where(r0 > 0, prev, jnp.zeros_like(prev))
    return jnp.concatenate([prev, cur], axis=0)


def _anti_ext(load, r0, rows, halo, s):
    cur = load(r0, rows)
    nxt = load(pl.multiple_of(jnp.minimum(r0 + rows, s - halo), halo), halo)
    nxt = jnp.where(r0 + rows < s, nxt, jnp.zeros_like(nxt))
    return jnp.concatenate([cur, nxt], axis=0)


def _down(ext, k):
    return pltpu.roll(ext, k, axis=0)


def _up(ext, k):
    return pltpu.roll(ext, ext.shape[0] - k, axis=0)


def _fold8(x):
    return jnp.sum(x.reshape(x.shape[0] // 8, 8, x.shape[1]), axis=0)


def _sigmoid(x):
    return 1.0 / (1.0 + jnp.exp(-x))


def _pool(p, *, backward, name, rows=64):
    s, d = p.shape
    strips_per_group = (d // len(POOL_WINDOWS)) // LANES
    assert strips_per_group * LANES * len(POOL_WINDOWS) == d and s % rows == 0

    def body(p_ref, o_ref):
        g = pl.program_id(0) // strips_per_group
        win = jnp.left_shift(2, g).astype(F32)

        def load(r0, n):
            return p_ref[pl.ds(r0, n), :]

        def pick(levels):
            return jnp.where(g == 0, levels[0], jnp.where(g == 1, levels[1], jnp.where(g == 2, levels[2], levels[3])))

        def chunk(c, carry):
            r0 = pl.multiple_of(c * rows, rows)
            if not backward:
                ext = _causal_ext(load, r0, rows, POOL_HALO)
                levels, acc = [], ext
                for k in (1, 2, 4, 8):
                    acc = acc + _down(acc, k)
                    levels.append(acc)
                t = (r0 + lax.broadcasted_iota(jnp.int32, (rows, LANES), 0)).astype(F32)
                cnt = jnp.minimum(t + 1.0, win)
                out = pick(levels)[POOL_HALO:] / cnt - ext[POOL_HALO:]
            else:
                ext = _anti_ext(load, r0, rows, POOL_HALO, s)
                t = (r0 + lax.broadcasted_iota(jnp.int32, (rows + POOL_HALO, LANES), 0)).astype(F32)
                e = ext / jnp.minimum(t + 1.0, win)
                levels, acc = [], e
                for k in (1, 2, 4, 8):
                    acc = acc + _up(acc, k)
                    levels.append(acc)
                out = pick(levels)[:rows] - ext[:rows]
            o_ref[pl.ds(r0, rows), :] = out.astype(o_ref.dtype)
            return carry

        lax.fori_loop(0, s // rows, chunk, 0)

    return pl.pallas_call(
        body, name=name, grid=(d // LANES,),
        in_specs=[pl.BlockSpec((s, LANES), lambda j: (0, j))],
        out_specs=pl.BlockSpec((s, LANES), lambda j: (0, j)),
        out_shape=jax.ShapeDtypeStruct((s, d), BF16),
        compiler_params=_cparams(1),
    )(p)


def _grp_fwd(pooled, w_grp, scale, *, name):
    s, d = pooled.shape
    g_n, cg = w_grp.shape[0], w_grp.shape[1]
    ts = _tile(s, 1024)

    def body(a_ref, w_ref, sc_ref, mg_ref, mx_ref):
        mg = jnp.dot(a_ref[...], w_ref[...], preferred_element_type=F32)
        mg_ref[...] = mg.astype(BF16)
        mx_ref[...] = (mg * sc_ref[...]).astype(BF16)

    blk = pl.BlockSpec((ts, cg), lambda g, i: (i, g))
    return pl.pallas_call(
        body, name=name, grid=(g_n, s // ts),
        in_specs=[blk, pl.BlockSpec((None, cg, cg), lambda g, i: (g, 0, 0)), pl.BlockSpec((1, cg), lambda g, i: (0, g))],
        out_specs=[blk, blk],
        out_shape=[jax.ShapeDtypeStruct((s, d), BF16)] * 2,
        compiler_params=_cparams(2),
    )(pooled, w_grp, scale)


def _grp_bwd_pre(dmixed, mg, scale, *, name):
    s, d = dmixed.shape
    ts = _tile(s, 256, 16)

    def body(dm_ref, mg_ref, sc_ref, dmg_ref, dsc_ref):
        dm = dm_ref[...]
        dmg_ref[...] = (dm * sc_ref[...]).astype(BF16)
        part = jnp.sum(dm * mg_ref[...].astype(F32), axis=0, keepdims=True)

        @pl.when(pl.program_id(0) == 0)
        def _():
            dsc_ref[...] = part

        @pl.when(pl.program_id(0) > 0)
        def _():
            dsc_ref[...] += part

    blk = pl.BlockSpec((ts, d), lambda i: (i, 0))
    vec = pl.BlockSpec((1, d), lambda i: (0, 0))
    return pl.pallas_call(
        body, name=name, grid=(s // ts,),
        in_specs=[blk, blk, vec], out_specs=[blk, vec],
        out_shape=[jax.ShapeDtypeStruct((s, d), BF16), jax.ShapeDtypeStruct((1, d), F32)],
        compiler_params=_cparams(1),
    )(dmixed, mg, scale)


def _ln_fwd(res, mm, g, b, *, name):
    s, d = res.shape
    ts = _tile(s, 128, 16)

    def body(res_ref, mm_ref, g_ref, b_ref, h_ref, hb_ref, xh_ref, rs_ref):
        z = ALPHA * res_ref[...] + mm_ref[...]
        mu = jnp.mean(z, axis=-1, keepdims=True)
        zc = z - mu
        var = jnp.mean(zc * zc, axis=-1, keepdims=True)
        rstd = lax.rsqrt(var + LN_EPS)
        xhat = zc * rstd
        h = xhat * g_ref[...] + b_ref[...]
        h_ref[...] = h
        hb_ref[...] = h.astype(BF16)
        xh_ref[...] = xhat
        rs_ref[...] = rstd

    blk = pl.BlockSpec((ts, d), lambda i: (i, 0))
    vec = pl.BlockSpec((1, d), lambda i: (0, 0))
    return pl.pallas_call(
        body, name=name, grid=(s // ts,),
        in_specs=[blk, blk, vec, vec],
        out_specs=[blk, blk, blk, pl.BlockSpec((ts, 1), lambda i: (i, 0))],
        out_shape=[jax.ShapeDtypeStruct((s, d), F32), jax.ShapeDtypeStruct((s, d), BF16),
                   jax.ShapeDtypeStruct((s, d), F32), jax.ShapeDtypeStruct((s, 1), F32)],
        compiler_params=_cparams(1),
    )(res, mm, g, b)


def _ln_bwd(dres, dmm, xhat, rstd, g, *, name):
    s, d = dmm.shape
    ts = _tile(s, 128, 16)
    has_res = dres is not None

    def body(*refs):
        if has_res:
            dres_ref, dmm_ref, xh_ref, rs_ref, g_ref, dz_ref, dzb_ref, dg_ref, db_ref = refs
            dh = ALPHA * dres_ref[...] + dmm_ref[...]
        else:
            dmm_ref, xh_ref, rs_ref, g_ref, dz_ref, dzb_ref, dg_ref, db_ref = refs
            dh = dmm_ref[...]
        xhat_ = xh_ref[...]
        dxh = dh * g_ref[...]
        c1 = jnp.mean(dxh, axis=-1, keepdims=True)
        c2 = jnp.mean(dxh * xhat_, axis=-1, keepdims=True)
        dz = rs_ref[...] * (dxh - c1 - xhat_ * c2)
        dz_ref[...] = dz
        dzb_ref[...] = dz.astype(BF16)
        dg_part = jnp.sum(dh * xhat_, axis=0, keepdims=True)
        db_part = jnp.sum(dh, axis=0, keepdims=True)

        @pl.when(pl.program_id(0) == 0)
        def _():
            dg_ref[...] = dg_part
            db_ref[...] = db_part

        @pl.when(pl.program_id(0) > 0)
        def _():
            dg_ref[...] += dg_part
            db_ref[...] += db_part

    blk = pl.BlockSpec((ts, d), lambda i: (i, 0))
    vec = pl.BlockSpec((1, d), lambda i: (0, 0))
    col = pl.BlockSpec((ts, 1), lambda i: (i, 0))
    ins = ([dres] if has_res else []) + [dmm, xhat, rstd, g]
    in_specs = ([blk] if has_res else []) + [blk, blk, col, vec]
    return pl.pallas_call(
        body, name=name, grid=(s // ts,),
        in_specs=in_specs, out_specs=[blk, blk, vec, vec],
        out_shape=[jax.ShapeDtypeStruct((s, d), F32), jax.ShapeDtypeStruct((s, d), BF16),
                   jax.ShapeDtypeStruct((1, d), F32), jax.ShapeDtypeStruct((1, d), F32)],
        compiler_params=_cparams(1),
    )(*ins)


def _loss_head(h, tgt, *, name):
    s, d = h.shape
    ts = _tile(s, 256, 16)

    def body(h_ref, t_ref, dh_ref, loss_ref):
        err = h_ref[...] - t_ref[...]
        dh_ref[...] = err * (1.0 / d)
        part = 0.5 * jnp.sum(jnp.mean(err * err, axis=-1, keepdims=True), axis=0, keepdims=True)

        @pl.when(pl.program_id(0) == 0)
        def _():
            loss_ref[...] = part

        @pl.when(pl.program_id(0) > 0)
        def _():
            loss_ref[...] += part

    blk = pl.BlockSpec((ts, d), lambda i: (i, 0))
    return pl.pallas_call(
        body, name=name, grid=(s // ts,),
        in_specs=[blk, blk], out_specs=[blk, pl.BlockSpec((1, 1), lambda i: (0, 0))],
        out_shape=[jax.ShapeDtypeStruct((s, d), F32), jax.ShapeDtypeStruct((1, 1), F32)],
        compiler_params=_cparams(1),
    )(h, tgt)


def _conv(ext, w, bias):
    c = bias + _down(ext, 2) * w[0:1] + _down(ext, 1) * w[1:2] + ext * w[2:3]
    return c[CONV_HALO:]


def _act_specs(s, n2):
    n_strips = pl.cdiv(n2, LANES)
    u_spec = pl.BlockSpec((None, 2, s, LANES), lambda hh, j: (hh, 0, 0, j))
    cwg = pl.BlockSpec((None, 3, LANES), lambda hh, j: (hh, 0, j))
    cwv = pl.BlockSpec((None, 3, LANES), lambda hh, j: (hh + 2, 0, j))
    cbg = pl.BlockSpec((None, 1, LANES), lambda hh, j: (hh, 0, j))
    cbv = pl.BlockSpec((None, 1, LANES), lambda hh, j: (hh + 2, 0, j))
    return n_strips, u_spec, cwg, cwv, cbg, cbv


def _act_fwd(u, cw, cb, *, name, rows=64):
    _, _, s, n2 = u.shape
    n_strips, u_spec, cwg, cwv, cbg, cbv = _act_specs(s, n2)

    def body(u_ref, wg_ref, wv_ref, bg_ref, bv_ref, a_ref):
        wg, wv, bg, bv = wg_ref[...], wv_ref[...], bg_ref[...], bv_ref[...]

        def chunk(c, carry):
            r0 = pl.multiple_of(c * rows, rows)
            cg = _conv(_causal_ext(lambda r, n: u_ref[0, pl.ds(r, n), :].astype(F32), r0, rows, CONV_HALO), wg, bg)
            cv = _conv(_causal_ext(lambda r, n: u_ref[1, pl.ds(r, n), :].astype(F32), r0, rows, CONV_HALO), wv, bv)
            a_ref[pl.ds(r0, rows), :] = (cg * _sigmoid(cg) * cv).astype(BF16)
            return carry

        lax.fori_loop(0, s // rows, chunk, 0)

    return pl.pallas_call(
        body, name=name, grid=(2, n_strips),
        in_specs=[u_spec, cwg, cwv, cbg, cbv],
        out_specs=pl.BlockSpec((None, s, LANES), lambda hh, j: (hh, 0, j)),
        out_shape=jax.ShapeDtypeStruct((2, s, n2), BF16),
        compiler_params=_cparams(2),
    )(u, cw, cw, cb, cb)


def _act_bwd(u, da, cw, cb, *, name, rows=64):
    _, _, s, n2 = u.shape
    n_strips, u_spec, cwg, cwv, cbg, cbv = _act_specs(s, n2)
    n_chunks = s // rows

    def body(u_ref, da_ref, wg_ref, wv_ref, bg_ref, bv_ref, du_ref, dwg_ref, dwv_ref, dbg_ref, dbv_ref, dg_s, dv_s):
        wg, wv, bg, bv = wg_ref[...], wv_ref[...], bg_ref[...], bv_ref[...]

        def first(c, sums):
            r0 = pl.multiple_of(c * rows, rows)
            eg = _causal_ext(lambda r, n: u_ref[0, pl.ds(r, n), :].astype(F32), r0, rows, CONV_HALO)
            ev = _causal_ext(lambda r, n: u_ref[1, pl.ds(r, n), :].astype(F32), r0, rows, CONV_HALO)
            cg, cv = _conv(eg, wg, bg), _conv(ev, wv, bv)
            sg = _sigmoid(cg)
            dact = da_ref[pl.ds(r0, rows), :]
            dval = dact * (cg * sg)
            dgate = dact * cv * (sg * (1.0 + cg * (1.0 - sg)))
            dg_s[pl.ds(r0, rows), :] = dgate
            dv_s[pl.ds(r0, rows), :] = dval
            new = []
            for dc, ext in ((dgate, eg), (dval, ev)):
                new += [_fold8(dc * _down(ext, 2)[CONV_HALO:]), _fold8(dc * _down(ext, 1)[CONV_HALO:]),
                        _fold8(dc * ext[CONV_HALO:]), _fold8(dc)]
            return tuple(acc + x for acc, x in zip(sums, new))

        zero = jnp.zeros((8, LANES), F32)
        sums = lax.fori_loop(0, n_chunks, first, (zero,) * 8)
        red = [jnp.sum(x, axis=0, keepdims=True) for x in sums]
        dwg_ref[...] = jnp.concatenate(red[0:3], axis=0)
        dbg_ref[...] = red[3]
        dwv_ref[...] = jnp.concatenate(red[4:7], axis=0)
        dbv_ref[...] = red[7]

        def second(c, carry):
            r0 = pl.multiple_of(c * rows, rows)
            for gv, (src, w) in enumerate(((dg_s, wg), (dv_s, wv))):
                ext = _anti_ext(lambda r, n: src[pl.ds(r, n), :], r0, rows, CONV_HALO, s)
                du = ext * w[2:3] + _up(ext, 1) * w[1:2] + _up(ext, 2) * w[0:1]
                du_ref[gv, pl.ds(r0, rows), :] = du[:rows].astype(BF16)
            return carry

        lax.fori_loop(0, n_chunks, second, 0)

    w_out = pl.BlockSpec((None, 3, LANES), lambda hh, j: (hh, 0, j))
    b_out = pl.BlockSpec((None, 1, LANES), lambda hh, j: (hh, 0, j))
    return pl.pallas_call(
        body, name=name, grid=(2, n_strips),
        in_specs=[u_spec, pl.BlockSpec((None, s, LANES), lambda hh, j: (hh, 0, j)), cwg, cwv, cbg, cbv],
        out_specs=[u_spec, w_out, w_out, b_out, b_out],
        out_shape=[jax.ShapeDtypeStruct((2, 2, s, n2), BF16),
                   jax.ShapeDtypeStruct((2, 3, n2), F32), jax.ShapeDtypeStruct((2, 3, n2), F32),
                   jax.ShapeDtypeStruct((2, 1, n2), F32), jax.ShapeDtypeStruct((2, 1, n2), F32)],
        scratch_shapes=[pltpu.VMEM((s, LANES), F32), pltpu.VMEM((s, LANES), F32)],
        compiler_params=_cparams(2),
    )(u, da, cw, cw, cb, cb)


def _dot_nt(a, b):
    return lax.dot_general(a, b, _DIMS["nt"], preferred_element_type=F32)


def _dot_tn(a, b):
    return lax.dot_general(a, b, _DIMS["tn"], preferred_element_type=F32)


def _band_mask(b):
    qi = lax.broadcasted_iota(jnp.int32, (ATTN_BLOCK, 2 * ATTN_BLOCK), 0)
    kj = lax.broadcasted_iota(jnp.int32, (ATTN_BLOCK, 2 * ATTN_BLOCK), 1)
    band = jnp.logical_and(kj >= qi, kj <= qi + ATTN_BLOCK)
    return jnp.logical_and(band, jnp.logical_or(b > 0, kj >= ATTN_BLOCK))


def _to_residues(nat, rm, d, seq, pad):
    seg = seq + pad
    for r in range(d):
        if pad:
            rm[pl.ds(r * seg, pad), :] = jnp.zeros((pad, LANES), rm.dtype)
        rows = nat[pl.ds(r, seq, stride=d), :] if d > 1 else nat[...]
        rm[pl.ds(r * seg + pad, seq), :] = rows.astype(rm.dtype)


def _rows_loop(s, rows, fn):
    def step(c, carry):
        fn(pl.ds(pl.multiple_of(c * rows, rows), rows))
        return carry

    lax.fori_loop(0, s // rows, step, 0)


def _attn_fwd(qkv, *, name):
    _, s, dm = qkv.shape
    heads, scale = dm // HEAD_DIM, 1.0 / math.sqrt(HEAD_DIM)
    pad_rows = s + ATTN_BLOCK * max(DILATIONS)

    def body(q_ref, k_ref, v_ref, o_ref, ob_ref, lse_ref, nat, rq, rk, rv, ro, rl, o_tmp, l_tmp, o_acc, m_acc, s_acc):
        for d in DILATIONS:
            seq = s // d
            nb, seg = seq // ATTN_BLOCK, seq + ATTN_BLOCK
            for src, dst, pad in ((q_ref, rq, 0), (k_ref, rk, ATTN_BLOCK), (v_ref, rv, ATTN_BLOCK)):
                if d == 1:
                    _to_residues(src, dst, d, seq, pad)
                else:
                    nat[...] = src[...].astype(F32)
                    _to_residues(nat, dst, d, seq, pad)

            def block(idx, carry):
                r, b = idx // nb, idx % nb
                qrows = pl.ds(pl.multiple_of(r * seq + b * ATTN_BLOCK, ATTN_BLOCK), ATTN_BLOCK)
                krows = pl.ds(pl.multiple_of(r * seg + b * ATTN_BLOCK, ATTN_BLOCK), 2 * ATTN_BLOCK)
                sc = jnp.where(_band_mask(b), _dot_nt(rq[qrows, :], rk[krows, :]) * scale, NEG_INF)
                m = jnp.max(sc, axis=-1, keepdims=True)
                p = jnp.exp(sc - m)
                den = jnp.sum(p, axis=-1, keepdims=True)
                ro[qrows, :] = jnp.dot(p.astype(BF16), rv[krows, :], preferred_element_type=F32) / den
                rl[qrows, :] = jnp.broadcast_to(m + jnp.log(den), (ATTN_BLOCK, LANES))
                return carry

            lax.fori_loop(0, d * nb, block, 0, unroll=ATTN_UNROLL)
            if d == 1:
                def first(rows):
                    o_acc[rows, :] = ro[rows, :]
                    m_acc[rows, :] = rl[rows, :]
                    s_acc[rows, :] = jnp.ones((rows.size, LANES), F32)

                _rows_loop(s, 64, first)
            else:
                for r in range(d):
                    o_tmp[pl.ds(r, seq, stride=d), :] = ro[pl.ds(r * seq, seq), :]
                    l_tmp[pl.ds(r, seq, stride=d), :] = rl[pl.ds(r * seq, seq), :]

                def merge(rows):
                    m_old, l_new = m_acc[rows, :], l_tmp[rows, :]
                    m_new = jnp.maximum(m_old, l_new)
                    w_old, w_new = jnp.exp(m_old - m_new), jnp.exp(l_new - m_new)
                    o_acc[rows, :] = o_acc[rows, :] * w_old + o_tmp[rows, :] * w_new
                    s_acc[rows, :] = s_acc[rows, :] * w_old + w_new
                    m_acc[rows, :] = m_new

                _rows_loop(s, 64, merge)

        def finish(rows):
            tot = s_acc[rows, :]
            o = o_acc[rows, :] / tot
            o_ref[rows, :] = o
            ob_ref[rows, :] = o.astype(BF16)
            lse_ref[rows, :] = m_acc[rows, :] + jnp.log(tot)

        _rows_loop(s, 64, finish)

    head = lambda i: pl.BlockSpec((None, s, HEAD_DIM), lambda h, i=i: (i, 0, h))
    out = pl.BlockSpec((s, HEAD_DIM), lambda h: (0, h))
    nat_f32 = pltpu.VMEM((s, LANES), F32)
    return pl.pallas_call(
        body, name=name, grid=(heads,),
        in_specs=[head(0), head(1), head(2)], out_specs=[out, out, out],
        out_shape=[jax.ShapeDtypeStruct((s, dm), F32), jax.ShapeDtypeStruct((s, dm), BF16),
                   jax.ShapeDtypeStruct((s, dm), F32)],
        scratch_shapes=[nat_f32, pltpu.VMEM((s, LANES), BF16), pltpu.VMEM((pad_rows, LANES), BF16),
                        pltpu.VMEM((pad_rows, LANES), BF16), nat_f32, nat_f32, nat_f32, nat_f32, nat_f32, nat_f32, nat_f32],
        compiler_params=_cparams(1),
    )(qkv, qkv, qkv)


def _attn_delta(do, o, *, name):
    s, dm = o.shape
    heads = dm // HEAD_DIM
    ts = _tile(s, 256, 16)

    def body(do_ref, o_ref, dl_ref):
        for h in range(heads):
            hs = slice(h * HEAD_DIM, (h + 1) * HEAD_DIM)
            row = jnp.sum(do_ref[:, hs].astype(F32) * o_ref[:, hs], axis=-1, keepdims=True)
            dl_ref[:, hs] = jnp.broadcast_to(row, (ts, HEAD_DIM))

    blk = pl.BlockSpec((ts, dm), lambda i: (i, 0))
    return pl.pallas_call(
        body, name=name, grid=(s // ts,),
        in_specs=[blk, blk], out_specs=blk,
        out_shape=jax.ShapeDtypeStruct((s, dm), F32),
        compiler_params=_cparams(1),
    )(do, o)


def _attn_bwd(qkv, do, lse, delta, *, name):
    _, s, dm = qkv.shape
    heads, scale = dm // HEAD_DIM, 1.0 / math.sqrt(HEAD_DIM)
    pad_rows = s + ATTN_BLOCK * max(DILATIONS)

    def body(q_ref, k_ref, v_ref, do_ref, l_ref, dl_ref, out_ref,
             nat, rq, rdo, rk, rv, rl, rdl, rdq, kc, kp, vc, vp, aq, ak, av):
        for d in DILATIONS:
            seq = s // d
            nb, seg = seq // ATTN_BLOCK, seq + ATTN_BLOCK
            for src, dst, pad in ((q_ref, rq, 0), (do_ref, rdo, 0), (k_ref, rk, ATTN_BLOCK), (v_ref, rv, ATTN_BLOCK)):
                if d == 1:
                    _to_residues(src, dst, d, seq, pad)
                else:
                    nat[...] = src[...].astype(F32)
                    _to_residues(nat, dst, d, seq, pad)
            if d == 1:
                lse_rows, dl_rows = l_ref, dl_ref
            else:
                lse_rows, dl_rows = rl, rdl
                _to_residues(l_ref, rl, d, seq, 0)
                _to_residues(dl_ref, rdl, d, seq, 0)
            for r in range(d):
                last = pl.ds(r * seg + nb * ATTN_BLOCK, ATTN_BLOCK)
                kp[last, :] = jnp.zeros((ATTN_BLOCK, LANES), F32)
                vp[last, :] = jnp.zeros((ATTN_BLOCK, LANES), F32)

            def block(idx, carry):
                r, b = idx // nb, idx % nb
                qrows = pl.ds(pl.multiple_of(r * seq + b * ATTN_BLOCK, ATTN_BLOCK), ATTN_BLOCK)
                krow = pl.multiple_of(r * seg + b * ATTN_BLOCK, ATTN_BLOCK)
                krows = pl.ds(krow, 2 * ATTN_BLOCK)
                before, own = pl.ds(krow, ATTN_BLOCK), pl.ds(krow + ATTN_BLOCK, ATTN_BLOCK)
                qb, dob, kw, vw = rq[qrows, :], rdo[qrows, :], rk[krows, :], rv[krows, :]
                lse_b = jnp.concatenate([lse_rows[qrows, :]] * 2, axis=1)
                dl_b = jnp.concatenate([dl_rows[qrows, :]] * 2, axis=1)
                sc = jnp.where(_band_mask(b), _dot_nt(qb, kw) * scale, NEG_INF)
                p = jnp.exp(sc - lse_b)
                ds = (p * (_dot_nt(dob, vw) - dl_b) * scale).astype(BF16)
                rdq[qrows, :] = jnp.dot(ds, kw, preferred_element_type=F32)
                dk, dv = _dot_tn(ds, qb), _dot_tn(p.astype(BF16), dob)
                kp[before, :] = dk[:ATTN_BLOCK]
                kc[own, :] = dk[ATTN_BLOCK:]
                vp[before, :] = dv[:ATTN_BLOCK]
                vc[own, :] = dv[ATTN_BLOCK:]
                return carry

            lax.fori_loop(0, d * nb, block, 0, unroll=ATTN_UNROLL)
            for r in range(d):
                keys = pl.ds(r * seg + ATTN_BLOCK, seq)
                rows = pl.ds(r, seq, stride=d) if d > 1 else pl.ds(0, seq)
                for acc, val in ((aq, rdq[pl.ds(r * seq, seq), :]), (ak, kc[keys, :] + kp[keys, :]),
                                 (av, vc[keys, :] + vp[keys, :])):
                    if d == DILATIONS[0]:
                        acc[rows, :] = val
                    else:
                        acc[rows, :] += val

        def finish(rows):
            for i, acc in enumerate((aq, ak, av)):
                out_ref[i, rows, :] = acc[rows, :].astype(BF16)

        _rows_loop(s, 256, finish)

    head = lambda i: pl.BlockSpec((None, s, HEAD_DIM), lambda h, i=i: (i, 0, h))
    col = pl.BlockSpec((s, HEAD_DIM), lambda h: (0, h))
    f32 = lambda rows: pltpu.VMEM((rows, LANES), F32)
    b16 = lambda rows: pltpu.VMEM((rows, LANES), BF16)
    return pl.pallas_call(
        body, name=name, grid=(heads,),
        in_specs=[head(0), head(1), head(2), col, col, col],
        out_specs=pl.BlockSpec((3, s, HEAD_DIM), lambda h: (0, 0, h)),
        out_shape=jax.ShapeDtypeStruct((3, s, dm), BF16),
        scratch_shapes=[f32(s), b16(s), b16(s), b16(pad_rows), b16(pad_rows), f32(s), f32(s),
                        f32(s), f32(pad_rows), f32(pad_rows), f32(pad_rows), f32(pad_rows), f32(s), f32(s), f32(s)],
        compiler_params=_cparams(1),
    )(qkv, qkv, qkv, do, lse, delta)


def _ew(fn, ins, out_dtypes, *, name, tile_bytes=1 << 20):
    first = ins[0][0] if isinstance(ins[0], tuple) else ins[0]
    rows, cols = first.shape[-2], first.shape[-1]
    tr = _tile(rows, max(16, tile_bytes // (4 * cols)), 16)
    n_in = len(ins)

    def body(*refs):
        outs = fn(*[r[...] for r in refs[:n_in]])
        for o_ref, val in zip(refs[n_in:], outs):
            o_ref[...] = val.astype(o_ref.dtype)

    in_specs, args = [], []
    for item in ins:
        if isinstance(item, tuple):
            arr, lead = item
            in_specs.append(pl.BlockSpec((None, tr, cols), lambda i, lead=lead: (lead, i, 0)))
            args.append(arr)
        else:
            in_specs.append(pl.BlockSpec((tr, cols), lambda i: (i, 0)))
            args.append(item)
    blk = pl.BlockSpec((tr, cols), lambda i: (i, 0))
    return pl.pallas_call(
        body, name=name, grid=(rows // tr,),
        in_specs=in_specs, out_specs=[blk] * len(out_dtypes),
        out_shape=[jax.ShapeDtypeStruct((rows, cols), dt) for dt in out_dtypes],
        compiler_params=_cparams(1),
    )(*args)


def _adamw_math(g, w, m, v):
    m2 = ADAM_B1 * m + (1.0 - ADAM_B1) * g
    v2 = ADAM_B2 * v + (1.0 - ADAM_B2) * (g * g)
    m_hat = m2 / (1.0 - ADAM_B1 ** ADAM_STEP)
    v_hat = v2 / (1.0 - ADAM_B2 ** ADAM_STEP)
    delta = -ADAM_LR * (m_hat / (jnp.sqrt(v_hat) + ADAM_EPS) + ADAM_WD * w)
    return g, delta, m2, v2


def _scalars(*vals):
    return jnp.stack([jnp.asarray(v, jnp.int32) for v in vals])


def _adamw_halves(mine, theirs, core, w, m, v, *, name, lead=0, prev=None):
    shape = w.shape
    a_n, rh, cols = mine.shape
    w3, m3, v3 = (t.reshape(-1, 2 * rh, cols) for t in (w, m, v))
    tr, tc = _tile2(rh, cols, 1 << 17, 8)
    n_i = rh // tr

    def body(c_ref, mine_ref, theirs_ref, w_ref, m_ref, v_ref, *rest):
        g = jnp.where(pl.program_id(1) == c_ref[0], mine_ref[...], theirs_ref[...])
        outs = _adamw_math(g, w_ref[...], m_ref[...], v_ref[...])
        for ref, val in zip(rest[-4:], outs):
            ref[...] = val

    half = pl.BlockSpec((None, tr, tc), lambda a, h, i, j, c_ref: (a, i, j))
    full = pl.BlockSpec((None, tr, tc), lambda a, h, i, j, c_ref: (lead + a, h * n_i + i, j))
    args = [mine, theirs, w3, m3, v3]
    in_specs = [half, half, full, full, full]
    aliases = {}
    if prev is not None:
        args += [p.reshape(w3.shape) for p in prev]
        in_specs += [ANY] * 4
        aliases = {6 + k: k for k in range(4)}
    outs = pl.pallas_call(
        body, name=name,
        grid_spec=pltpu.PrefetchScalarGridSpec(
            num_scalar_prefetch=1, grid=(a_n, 2, n_i, cols // tc), in_specs=in_specs, out_specs=[full] * 4),
        out_shape=[jax.ShapeDtypeStruct(w3.shape, F32)] * 4,
        input_output_aliases=aliases,
        compiler_params=_cparams(4),
    )(_scalars(core), *args)
    return [o.reshape(shape) for o in outs]


def _cast_into(src, buf, lead, slot, *, name, buf_shape=None, dtype=BF16):
    a_n, rows, cols = src.shape
    tr = _tile(rows, max(16, (1 << 21) // (4 * cols)), 16)

    def body(slot_ref, src_ref, *rest):
        rest[-1][...] = src_ref[...].astype(rest[-1].dtype)

    in_specs = [pl.BlockSpec((None, tr, cols), lambda a, i, slot_ref: (a, i, 0))]
    args = [src]
    aliases = {}
    if buf is not None:
        in_specs.append(ANY)
        args.append(buf)
        aliases = {2: 0}
        buf_shape, dtype = buf.shape, buf.dtype
    return pl.pallas_call(
        body, name=name,
        grid_spec=pltpu.PrefetchScalarGridSpec(
            num_scalar_prefetch=1, grid=(a_n, rows // tr), in_specs=in_specs,
            out_specs=pl.BlockSpec((None, None, tr, cols), lambda a, i, slot_ref: (lead + a, slot_ref[0], i, 0))),
        out_shape=jax.ShapeDtypeStruct(buf_shape, dtype),
        input_output_aliases=aliases,
        compiler_params=_cparams(2),
    )(_scalars(slot), *args)


def _place():
    x, y, c = lax.axis_index("x"), lax.axis_index("y"), lax.axis_index("c")
    chips = [(1 - x, y), (x, 1 - y), (1 - x, 1 - y)]
    return x, y, c, chips


def _remote(src, dst, send_sem, recv_sem, dev):
    return pltpu.make_async_remote_copy(src_ref=src, dst_ref=dst, send_sem=send_sem, recv_sem=recv_sem,
                                        device_id=dev, device_id_type=MESH)


def _gather_comm(bufs, pieces):
    def plan(refs, send_sems, recv_sems):
        x, y, c, chips = _place()

        def region(p, q, core):
            t, a0, a1, part, n_parts = pieces[p]
            rh = bufs[t].shape[2] // 2
            sub = rh // n_parts
            return refs[t].at[pl.ds(a0, a1 - a0), q, pl.ds(core * rh + part * sub, sub), :]

        def ici(p, j, q):
            cx, cy = chips[j]
            return _remote(region(p, q, c), region(p, q, c), send_sems.at[6 * p + j], recv_sems.at[6 * p + j], (cx, cy, c))

        def d2d(p, j, core):
            cx, cy = chips[j]
            rows = region(p, 2 * cx + cy, core)
            return _remote(rows, rows, send_sems.at[6 * p + 3 + j], recv_sems.at[6 * p + 3 + j], (x, y, 1 - c))

        return 2 * x + y, c, [2 * cx + cy for cx, cy in chips], ici, d2d

    todo = [(j, p) for j in range(3) for p in range(len(pieces))]

    def start(refs, send_sems, recv_sems):
        q_me, _, _, ici, _ = plan(refs, send_sems, recv_sems)
        for j, p in todo:
            ici(p, j, q_me).start()

    def finish(refs, send_sems, recv_sems):
        q_me, c, q_of, ici, d2d = plan(refs, send_sems, recv_sems)
        for j, p in todo:
            ici(p, j, q_of[j]).wait_recv()
            d2d(p, j, c).start()
        for j, p in todo:
            d2d(p, j, 1 - c).wait_recv()
        for j, p in todo:
            ici(p, j, q_me).wait_send()
            d2d(p, j, c).wait_send()

    return _Comm(bufs, 6 * len(pieces), start, finish)


def _chips_comm(parts, lands):
    n = len(parts)

    def copy(refs, send_sems, recv_sems, t, j, q_src, q_dst):
        x, y, c, chips = _place()
        cx, cy = chips[j]
        return _remote(refs[t].at[:, q_src], refs[n + t].at[q_dst], send_sems.at[3 * t + j], recv_sems.at[3 * t + j],
                       (cx, cy, c))

    todo = [(j, t) for j in range(3) for t in range(n)]

    def qs():
        x, y, _, chips = _place()
        return 2 * x + y, [2 * cx + cy for cx, cy in chips]

    def start(refs, send_sems, recv_sems):
        q_me, q_of = qs()
        for j, t in todo:
            copy(refs, send_sems, recv_sems, t, j, q_of[j], q_me).start()

    def finish(refs, send_sems, recv_sems):
        q_me, q_of = qs()
        for j, t in todo:
            copy(refs, send_sems, recv_sems, t, j, q_me, q_of[j]).wait_recv()
        for j, t in todo:
            copy(refs, send_sems, recv_sems, t, j, q_of[j], q_me).wait_send()

    return _Comm(list(parts) + list(lands), 3 * n, start, finish)


def _comm_call(comm, *, name):
    k = len(comm.operands)

    def body(*refs):
        operands, (send_sems, recv_sems) = refs[k:2 * k], refs[2 * k:]
        comm.start(operands, send_sems, recv_sems)
        comm.finish(operands, send_sems, recv_sems)

    return pl.pallas_call(
        body, name=name, in_specs=[ANY] * k, out_specs=[ANY] * k,
        out_shape=[jax.ShapeDtypeStruct(t.shape, t.dtype) for t in comm.operands],
        input_output_aliases={i: i for i in range(k)},
        scratch_shapes=[pltpu.SemaphoreType.DMA((comm.n_sems,)), pltpu.SemaphoreType.DMA((comm.n_sems,))],
    )(*comm.operands)


def _rs_sibling(grads, *, name):
    n = len(grads)
    halves = [jax.ShapeDtypeStruct(g.shape[:2] + (g.shape[2] // 2, g.shape[3]), g.dtype) for g in grads]

    def body(*refs):
        src, theirs = refs[:n], refs[n:2 * n]
        send_sems, recv_sems = refs[2 * n:]
        x, y, c, _ = _place()
        ops = []
        for t in range(n):
            rh = grads[t].shape[2] // 2
            give = _remote(src[t].at[:, :, pl.ds((1 - c) * rh, rh), :], theirs[t], send_sems.at[t], recv_sems.at[t],
                           (x, y, 1 - c))
            give.start()
            ops.append(give)
        for op in ops:
            op.wait()

    return pl.pallas_call(
        body, name=name, in_specs=[ANY] * n, out_specs=[ANY] * n, out_shape=halves,
        scratch_shapes=[pltpu.SemaphoreType.DMA((n,)), pltpu.SemaphoreType.DMA((n,))],
    )(*grads)


def _rs_add(grad, theirs, core, slot, *, name):
    a_n, _, rh, cols = theirs.shape
    tr, tc = _tile2(rh, cols, 1 << 18, 16)
    n_i = rh // tr

    def body(s_ref, g_ref, t_ref, p_ref, y_ref):
        part = (g_ref[...].astype(F32) + t_ref[...].astype(F32)).astype(BF16)
        p_ref[...] = part

        @pl.when(pl.program_id(3) == s_ref[1])
        def _():
            y_ref[...] = part

    blk = (None, None, tr, tc)
    return pl.pallas_call(
        body, name=name,
        grid_spec=pltpu.PrefetchScalarGridSpec(
            num_scalar_prefetch=1, grid=(a_n, n_i, cols // tc, N_CHIPS),
            in_specs=[pl.BlockSpec(blk, lambda a, i, j, q, s: (a, q, s[0] * n_i + i, j)),
                      pl.BlockSpec(blk, lambda a, i, j, q, s: (a, q, i, j))],
            out_specs=[pl.BlockSpec(blk, lambda a, i, j, q, s: (a, q, i, j)),
                       pl.BlockSpec(blk, lambda a, i, j, q, s: (s[1], a, i, j))]),
        out_shape=[jax.ShapeDtypeStruct(theirs.shape, BF16),
                   jax.ShapeDtypeStruct((N_CHIPS, a_n, rh, cols), BF16)],
        compiler_params=_cparams(4),
    )(_scalars(core, slot), grad, theirs)


def _rs_finish(halves, *, name):
    n = len(halves)

    def body(*refs):
        src, dst = refs[:n], refs[n:2 * n]
        send_sems, recv_sems = refs[2 * n:]
        x, y, c, _ = _place()
        ops = []
        for t in range(n):
            give = _remote(src[t], dst[t], send_sems.at[t], recv_sems.at[t], (x, y, 1 - c))
            give.start()
            ops.append(give)
        for op in ops:
            op.wait()

    return pl.pallas_call(
        body, name=name, in_specs=[ANY] * n, out_specs=[ANY] * n,
        out_shape=[jax.ShapeDtypeStruct(h.shape, h.dtype) for h in halves],
        scratch_shapes=[pltpu.SemaphoreType.DMA((n,)), pltpu.SemaphoreType.DMA((n,))],
    )(*halves)


def _all_reduce_small(vec, *, name):
    rows = vec.shape[0]

    def body(v_ref, o_ref, land, send_sems, recv_sems):
        x, y, c, _ = _place()
        me = 4 * x + 2 * y + c
        land[me] = v_ref[...]
        flips = [(fx, fy, fc) for fx in (0, 1) for fy in (0, 1) for fc in (0, 1)][1:]
        sent = []
        for k, (fx, fy, fc) in enumerate(flips):
            cp = _remote(v_ref, land.at[me], send_sems.at[k], recv_sems.at[k], (x ^ fx, y ^ fy, c ^ fc))
            cp.start()
            sent.append(cp)
        for k, (fx, fy, fc) in enumerate(flips):
            peer = 4 * (x ^ fx) + 2 * (y ^ fy) + (c ^ fc)
            _remote(v_ref, land.at[peer], send_sems.at[k], recv_sems.at[k], (x ^ fx, y ^ fy, c ^ fc)).wait_recv()
        for cp in sent:
            cp.wait_send()
        total = land[0]
        for dev in range(1, 8):
            total = total + land[dev]
        o_ref[...] = total

    whole = pl.BlockSpec(memory_space=pltpu.VMEM)
    return pl.pallas_call(
        body, name=name, in_specs=[whole], out_specs=whole,
        out_shape=jax.ShapeDtypeStruct(vec.shape, F32),
        scratch_shapes=[pltpu.VMEM((8, rows, LANES), F32), pltpu.SemaphoreType.DMA((7,)), pltpu.SemaphoreType.DMA((7,))],
        compiler_params=pltpu.CompilerParams(vmem_limit_bytes=VMEM_LIMIT),
    )(vec)


def _pack(parts, mult=16):
    flat = jnp.concatenate([p.reshape(-1).astype(F32) for p in parts])
    rows = -(-flat.shape[0] // (LANES * mult)) * mult
    return jnp.pad(flat, (0, rows * LANES - flat.shape[0])).reshape(rows, LANES)


def _unpack(vec, shapes):
    flat, out, pos = vec.reshape(-1), [], 0
    for shp in shapes:
        size = math.prod(shp)
        out.append(flat[pos:pos + size].reshape(shp))
        pos += size
    return out


def kernel(x, pool_w_in, pool_w_grp, pool_scale, pool_w_out, attn_w_q, attn_w_o, shared_w_k, shared_w_v, ffn_w_up, ffn_conv_w, ffn_conv_b, ffn_w_down, ln1_g, ln1_b, ln2_g, ln2_b, loss_target, m_pool_w_in, m_pool_w_grp, m_pool_scale, m_pool_w_out, m_attn_w_q, m_attn_w_o, m_shared_w_k, m_shared_w_v, m_ffn_w_up, m_ffn_conv_w, m_ffn_conv_b, m_ffn_w_down, m_ln1_g, m_ln1_b, m_ln2_g, m_ln2_b, v_pool_w_in, v_pool_w_grp, v_pool_scale, v_pool_w_out, v_attn_w_q, v_attn_w_o, v_shared_w_k, v_shared_w_v, v_ffn_w_up, v_ffn_conv_w, v_ffn_conv_b, v_ffn_w_down, v_ln1_g, v_ln1_b, v_ln2_g, v_ln2_b):
    s, d = x.shape[1], x.shape[2]
    n2 = ffn_w_up.shape[2]
    fq = ffn_w_down.shape[1]
    assert 2 * fq == n2 and d % N_CHIPS == 0
    g_n, cg = pool_w_grp.shape[1], pool_w_grp.shape[3]
    xs, tgt = x[0], loss_target[0]
    q_me = 2 * lax.axis_index("x") + lax.axis_index("y")
    core = lax.axis_index("c")
    rq = d // N_CHIPS

    six_g = None
    for i, w_ in enumerate((pool_w_in[0], pool_w_out[0], attn_w_q[0], shared_w_k, shared_w_v, attn_w_o[0])):
        six_g = _cast_into(w_[None], six_g, i, q_me, name=f"cast_w{i}", buf_shape=(6, N_CHIPS, rq, d))
    grp_g = _cast_into(pool_w_grp[0], None, 0, q_me, name="cast_grp", buf_shape=(g_n, N_CHIPS, cg // N_CHIPS, cg))
    up_t, m_up_t, v_up_t = (jnp.swapaxes(t, 1, 2) for t in (ffn_w_up, m_ffn_w_up, v_ffn_w_up))
    up_g = _cast_into(up_t, None, 0, q_me, name="cast_up", buf_shape=(2, N_CHIPS, n2, d))
    dn_g = _cast_into(ffn_w_down, None, 0, q_me, name="cast_down", buf_shape=(2, N_CHIPS, fq, d))
    small = _pack([ffn_conv_w, pool_scale])
    small_g = _cast_into(small[None], None, 0, q_me, name="cast_small", buf_shape=(1, N_CHIPS) + small.shape, dtype=F32)
    bufs = [six_g, grp_g, up_g, dn_g, small_g]
    SIX, GRP, UP, DN, SMALL = range(5)
    first = [SIX, GRP, SMALL]
    landed = _comm_call(_gather_comm([bufs[t] for t in first], [(0, IW_IN, IW_OUT + 1, 0, 1), (1, 0, g_n, 0, 1),
                                                                (2, 0, 1, 0, 1)]), name="gather_first")
    for t, arr in zip(first, landed):
        bufs[t] = arr
    small_q = bufs[SMALL].reshape(N_CHIPS, -1)
    n_cw = 2 * 3 * n2
    conv_w = small_q[:, :n_cw].reshape(N_CHIPS, 2, 3, n2).transpose(1, 0, 2, 3)
    scale_full = small_q[:, n_cw:n_cw + d // N_CHIPS].reshape(1, d)
    conv_b = ffn_conv_b.reshape(2, N_CHIPS, 1, n2)

    def w6():
        return bufs[SIX].reshape(6, d, d)

    def up8():
        return bufs[UP].reshape(2 * N_CHIPS, n2, d)

    def dn4():
        return bufs[DN].reshape(4, n2, d)

    def gathered(pieces):
        used = sorted({pc[0] for pc in pieces})
        local = [(used.index(t), a0, a1, part, n_parts) for t, a0, a1, part, n_parts in pieces]
        return _gather_comm([bufs[t] for t in used], local), used

    def store(used, operands):
        for t, arr in zip(used, operands):
            bufs[t] = arr

    def ffn_fwd(l, hb, carry_up, carry_down):
        comm, used = gathered(carry_up)
        u, landed = _mm(hb[None], up8(), mode="nt", name=f"ffn{l}_up", out_dtype=BF16, n_q=4, tm=_tile(s, 512), tn=n2,
                        tk=d, qa=_q0, qb=lambda q, qr: 4 * l + _perm(q), comm=comm)
        store(used, landed)
        u = u.reshape(2, 2, s, n2)
        act = _act_fwd(u, conv_w[l], conv_b[l], name=f"ffn{l}_act")
        comm, used = gathered(carry_down) if carry_down else (None, None)
        ff = _mm(act, dn4(), mode="nn", name=f"ffn{l}_down", out_dtype=F32, n_q=1, n_qr=2, tm=_tile(s, 1024),
                 tn=_tile(d, 1024), tk=n2, qa=_qr, qb=lambda q, qr: 2 * l + qr, qo=lambda q: 0, comm=comm)
        if comm is not None:
            ff, landed = ff
            store(used, landed)
        return u, act, ff[0]

    comm, used = gathered([(UP, 0, 1, 0, 2)])
    p, landed = _mm_dd(xs[None], w6(), IW_IN, mode="nn", name="pool_in", out_dtype=F32, comm=comm)
    store(used, landed)
    p = p[0]
    pooled = _pool(p, backward=False, name="pool_fwd")
    mg, mixed = _grp_fwd(pooled, bufs[GRP].reshape(g_n, cg, cg), scale_full, name="pool_grp")
    comm, used = gathered([(UP, 0, 1, 1, 2)])
    mix0, landed = _mm_dd(mixed[None], w6(), IW_OUT, mode="nn", name="pool_out", out_dtype=F32, comm=comm)
    store(used, landed)
    mix0 = mix0[0]
    h1, h1b, xh1, rs1 = _ln_fwd(xs, mix0, ln1_g[0:1], ln1_b[0:1], name="ln1_0")
    u0, act0, ff0 = ffn_fwd(0, h1b, [(DN, 0, 1, 0, 1), (SIX, IW_Q, IW_O + 1, 0, 1)], [(UP, 1, 2, 0, 2)])
    h2, h2b, xh2, rs2 = _ln_fwd(h1, ff0, ln2_g[0:1], ln2_b[0:1], name="ln2_0")

    comm, used = gathered([(UP, 1, 2, 1, 2)])
    qkv, landed = _mm_dd(h2b[None], w6(), IW_Q, mode="nn", name="attn_qkv", out_dtype=BF16, n_q=3, comm=comm)
    store(used, landed)
    o, ob, lse = _attn_fwd(qkv, name="attn_fwd")
    mix1 = _mm_dd(ob[None], w6(), IW_O, mode="nn", name="attn_out", out_dtype=F32)[0]
    h3, h3b, xh3, rs3 = _ln_fwd(h2, mix1, ln1_g[1:2], ln1_b[1:2], name="ln1_1")
    u1, act1, ff1 = ffn_fwd(1, h3b, [(DN, 1, 2, 0, 1)], None)
    h4, _, xh4, rs4 = _ln_fwd(h3, ff1, ln2_g[1:2], ln2_b[1:2], name="ln2_1")
    dh4, loss_local = _loss_head(h4, tgt, name="loss_head")

    rs_parts, rs_lands = {}, {}

    def rs_prepare(items):
        theirs = _rs_sibling([g_ for _, g_ in items], name="rs_sibling_" + items[0][0])
        for (nm, g_), t_ in zip(items, theirs):
            rs_parts[nm], rs_lands[nm] = _rs_add(g_, t_, core, q_me, name=f"rs_add_{nm}")

    def rs_exchange(names):
        return _chips_comm([rs_parts[nm] for nm in names], [rs_lands[nm] for nm in names])

    def rs_landed(names, operands):
        for nm, land in zip(names, operands[len(names):]):
            rs_lands[nm] = land

    d_conv_w, d_conv_b = [None, None], [None, None]

    def ffn_bwd(l, dzb, u, act, hb, carry_dact):
        da = _mm(dzb[None], dn4(), mode="nt", name=f"ffn{l}_dact", out_dtype=F32, n_q=2, tm=_tile(s, 512), tn=n2,
                 tk=d, qa=_q0, qb=lambda q, qr: 2 * l + q,
                 comm=rs_exchange(carry_dact) if carry_dact else None)
        if carry_dact:
            da, landed = da
            rs_landed(carry_dact, landed)
        du, dwg, dwv, dbg, dbv = _act_bwd(u, da, conv_w[l], conv_b[l], name=f"ffn{l}_dconv")
        d_conv_w[l] = jnp.concatenate([dwg, dwv], axis=0)
        d_conv_b[l] = jnp.concatenate([dbg, dbv], axis=0)
        du4 = du.reshape(4, s, n2)
        g_dn = _mm(act, dzb[None], mode="tn", name=f"ffn{l}_gdown", out_dtype=BF16, n_q=2, tm=n2, tn=_tile(d, 1024),
                   tk=_tile(s, 512), qa=_qq, qb=_q0)
        rs_prepare([(f"dn{l}", g_dn.reshape(1, N_CHIPS, fq, d))])
        g_up, landed = _mm(du4, hb[None], mode="tn", name=f"ffn{l}_gup", out_dtype=BF16, n_q=4, tm=n2, tn=_tile(d, 1024),
                           tk=_tile(s, 512), qa=lambda q, qr: _perm(q), qb=_q0, comm=rs_exchange([f"dn{l}"]))
        rs_landed([f"dn{l}"], landed)
        rs_prepare([(f"up{l}", g_up.reshape(1, N_CHIPS, n2, d))])
        dh, landed = _mm(du4, up8(), mode="nn", name=f"ffn{l}_dh", out_dtype=F32, n_q=1, n_qr=4, tm=_tile(s, 1024),
                         tn=_tile(d, 1024), tk=n2, qa=_qr, qb=lambda q, qr: 4 * l + _perm(qr), qo=lambda q: 0,
                         comm=rs_exchange([f"up{l}"]))
        rs_landed([f"up{l}"], landed)
        return dh[0]

    dz4, dz4b, dg_ln2_1, db_ln2_1 = _ln_bwd(None, dh4, xh4, rs4, ln2_g[1:2], name="dln2_1")
    dh3 = ffn_bwd(1, dz4b, u1, act1, h3b, None)
    dz3, dz3b, dg_ln1_1, db_ln1_1 = _ln_bwd(dz4, dh3, xh3, rs3, ln1_g[1:2], name="dln1_1")
    g_wo = _mm_wgrad(ob[None], dz3b[None], name="attn_gwo")
    do = _mm_dd(dz3b[None], w6(), IW_O, mode="nt", name="attn_do", out_dtype=BF16)[0]
    delta = _attn_delta(do, o, name="attn_delta")
    dqkv = _attn_bwd(qkv, do, lse, delta, name="attn_bwd")
    g_wq = _mm_wgrad(h2b[None], dqkv, name="attn_gwq", qb=lambda q, qr: 0)
    g_wk = _mm_wgrad(h2b[None], dqkv, name="attn_gwk", qb=lambda q, qr: 1)
    g_wv = _mm_wgrad(h2b[None], dqkv, name="attn_gwv", qb=lambda q, qr: 2)
    rs_prepare([(nm, g_.reshape(1, N_CHIPS, rq, d)) for nm, g_ in
                (("wo", g_wo), ("wq", g_wq), ("wk", g_wk), ("wv", g_wv))])
    dh2, landed = _mm_dd(dqkv, w6(), IW_Q, mode="nt", name="attn_dh", out_dtype=F32, n_qr=3, qa=_qr,
                         comm=rs_exchange(["wo", "wq", "wk"]))
    rs_landed(["wo", "wq", "wk"], landed)
    dz2, dz2b, dg_ln2_0, db_ln2_0 = _ln_bwd(dz3, dh2[0], xh2, rs2, ln2_g[0:1], name="dln2_0")
    dh1 = ffn_bwd(0, dz2b, u0, act0, h1b, ["wv"])
    dz1, dz1b, dg_ln1_0, db_ln1_0 = _ln_bwd(dz2, dh1, xh1, rs1, ln1_g[0:1], name="dln1_0")
    g_wout = _mm_wgrad(mixed[None], dz1b[None], name="pool_gwout")
    rs_prepare([("wout", g_wout.reshape(1, N_CHIPS, rq, d))])
    dmixed, landed = _mm_dd(dz1b[None], w6(), IW_OUT, mode="nt", name="pool_dmixed", out_dtype=F32,
                            comm=rs_exchange(["wout"]))
    rs_landed(["wout"], landed)
    dmg, d_scale = _grp_bwd_pre(dmixed[0], mg, scale_full, name="pool_dscale")
    g_wgrp = _grp_mm(pooled, dmg, mode="tn", name="pool_gwgrp", out_dtype=BF16, tm=1024)
    dpooled = _grp_mm(dmg, None, mode="nt", name="pool_dpooled", out_dtype=F32, tm=1024, w=bufs[GRP].reshape(g_n, cg, cg))
    dp = _pool(dpooled, backward=True, name="pool_bwd")
    g_win = _mm_wgrad(xs[None], dp[None], name="pool_gwin")
    last = ["win", "wgrp"]
    rs_prepare([("win", g_win.reshape(1, N_CHIPS, rq, d)), ("wgrp", g_wgrp.reshape(g_n, N_CHIPS, cg // N_CHIPS, cg))])
    dx_mm, landed = _mm_dd(dp[None], w6(), IW_IN, mode="nt", name="pool_dx", out_dtype=F32, comm=rs_exchange(last))
    rs_landed(last, landed)
    (grad_x,) = _ew(lambda a, b: (ALPHA * a + b,), [dz1, dx_mm[0]], [F32], name="grad_x")

    rs_names = ["win", "wgrp", "wout", "wq", "wo", "wk", "wv", "up0", "up1", "dn0", "dn1"]
    finished = {}
    for nm in rs_names:
        y4 = rs_lands[nm]
        y3 = y4.reshape(N_CHIPS, -1, y4.shape[-1])
        (tot,) = _ew(lambda a0, a1, a2, a3: (((a0.astype(F32) + a1.astype(F32)) + a2.astype(F32)) + a3.astype(F32),),
                     [(y3, 0), (y3, 1), (y3, 2), (y3, 3)], [F32], name=f"rs_sum_{nm}")
        finished[nm] = tot.reshape(y4.shape[1:])
    others = dict(zip(rs_names, _rs_finish([finished[nm] for nm in rs_names], name="rs_finish")))
    results = {}
    for nm, key, w, m, v in (("pool_w_in", "win", pool_w_in, m_pool_w_in, v_pool_w_in),
                             ("pool_w_grp", "wgrp", pool_w_grp, m_pool_w_grp, v_pool_w_grp),
                             ("pool_w_out", "wout", pool_w_out, m_pool_w_out, v_pool_w_out),
                             ("attn_w_q", "wq", attn_w_q, m_attn_w_q, v_attn_w_q),
                             ("attn_w_o", "wo", attn_w_o, m_attn_w_o, v_attn_w_o),
                             ("shared_w_k", "wk", shared_w_k, m_shared_w_k, v_shared_w_k),
                             ("shared_w_v", "wv", shared_w_v, m_shared_w_v, v_shared_w_v)):
        results[nm] = _adamw_halves(finished[key], others[key], core, w, m, v, name=f"adamw_{nm}")
    for nm, key, w, m, v in (("ffn_w_up", "up", up_t, m_up_t, v_up_t),
                             ("ffn_w_down", "dn", ffn_w_down, m_ffn_w_down, v_ffn_w_down)):
        res = None
        for l in (1, 0):
            res = _adamw_halves(finished[f"{key}{l}"], others[f"{key}{l}"], core, w, m, v, name=f"adamw_{nm}{l}",
                                lead=l, prev=res)
        results[nm] = res
    results["ffn_w_up"] = [jnp.swapaxes(t, 1, 2) for t in results["ffn_w_up"]]

    ln_grads = [jnp.concatenate([a, b], axis=0) for a, b in
                ((dg_ln1_0, dg_ln1_1), (db_ln1_0, db_ln1_1), (dg_ln2_0, dg_ln2_1), (db_ln2_0, db_ln2_1))]
    small_shapes = [(2, N_CHIPS, 3, n2), (2, N_CHIPS, n2)] + [(2, d)] * 4 + [(1, d)]
    vec = _pack([jnp.stack(d_conv_w), jnp.stack(d_conv_b)] + ln_grads + [d_scale], mult=8)
    tot = _unpack(_all_reduce_small(vec, name="allreduce_small"), small_shapes)
    g_cw = lax.dynamic_index_in_dim(tot[0], q_me, axis=1, keepdims=False)
    g_cb = tot[1].reshape(2, N_CHIPS * n2)
    g_scale = lax.dynamic_slice_in_dim(tot[6], q_me * rq, rq, axis=1)
    small_names = ["ffn_conv_w", "ffn_conv_b", "ln1_g", "ln1_b", "ln2_g", "ln2_b", "pool_scale"]
    small_g = [g_cw, g_cb, tot[2], tot[3], tot[4], tot[5], g_scale]
    small_w = [ffn_conv_w, ffn_conv_b, ln1_g, ln1_b, ln2_g, ln2_b, pool_scale]
    small_m = [m_ffn_conv_w, m_ffn_conv_b, m_ln1_g, m_ln1_b, m_ln2_g, m_ln2_b, m_pool_scale]
    small_v = [v_ffn_conv_w, v_ffn_conv_b, v_ln1_g, v_ln1_b, v_ln2_g, v_ln2_b, v_pool_scale]
    packed = _ew(_adamw_math, [_pack(small_g, 8), _pack(small_w, 8), _pack(small_m, 8), _pack(small_v, 8)], [F32] * 4,
                 name="adamw_small")
    shapes = [w.shape for w in small_w]
    unpacked = [_unpack(pk, shapes) for pk in packed]
    for i, nm in enumerate(small_names):
        results[nm] = [unpacked[k][i] for k in range(4)]

    loss = lax.psum(loss_local[0, 0], ("x", "y", "c"))
    order = ["pool_w_in", "pool_w_grp", "pool_scale", "pool_w_out", "attn_w_q", "attn_w_o", "shared_w_k", "shared_w_v",
             "ffn_w_up", "ffn_conv_w", "ffn_conv_b", "ffn_w_down", "ln1_g", "ln1_b", "ln2_g", "ln2_b"]
    outs = [loss, grad_x[None]]
    for k in range(4):
        outs += [results[nm][k] for nm in order]
    return tuple(outs)
```

```python
import functools
import math

import jax
import jax.numpy as jnp
from jax import lax
from jax.experimental import pallas as pl
from jax.experimental.pallas import tpu as pltpu

F32 = jnp.float32
BF16 = jnp.bfloat16

LANES = 128
HEAD_DIM = 128
ATTN_BLOCK = 128
DILATIONS = (1, 4, 16)
ATTN_UNROLL = 8
POOL_WINDOWS = (2, 4, 8, 16)
POOL_HALO = 16
CONV_HALO = 16
DEPTH = 2
ALPHA = (2.0 * DEPTH) ** 0.25
LN_EPS = 1e-5
NEG_INF = -1e30
ADAM_LR = 0.001
ADAM_B1 = 0.9
ADAM_B2 = 0.999
ADAM_EPS = 1e-08
ADAM_WD = 0.01
ADAM_STEP = 10
N_CHIPS = 4
VMEM_LIMIT = 56 * 1024 * 1024
ANY = pl.BlockSpec(memory_space=pl.ANY)
MESH = pl.DeviceIdType.MESH

IW_IN, IW_OUT, IW_Q, IW_K, IW_V, IW_O = range(6)


def _cparams(n_grid):
    return pltpu.CompilerParams(dimension_semantics=("arbitrary",) * n_grid, vmem_limit_bytes=VMEM_LIMIT)


def _tile(dim, pref, align=LANES):
    if dim <= pref:
        return dim
    t = (pref // align) * align
    while t >= align:
        if dim % t == 0:
            return t
        t -= align
    return dim


def _tile2(rows, cols, budget, row_align):
    best = None
    for tc in [cols] + [c for c in range(LANES, cols, LANES) if cols % c == 0]:
        for tr in range(row_align, rows + 1, row_align):
            if rows % tr == 0 and tr * tc <= budget:
                if best is None or (tr * tc, tc) > (best[0] * best[1], best[1]):
                    best = (tr, tc)
    return best if best is not None else (rows, cols)


def _perm(q):
    return (q % 2) * 2 + q // 2


_DIMS = {"nn": (((1,), (0,)), ((), ())), "nt": (((1,), (1,)), ((), ())), "tn": (((0,), (0,)), ((), ()))}


class _Comm:
    def __init__(self, operands, n_sems, start, finish):
        self.operands, self.n_sems, self.start, self.finish = list(operands), n_sems, start, finish


def _pallas(body, *, name, grid, in_specs, out_specs, out_shape, args, scratch_shapes=(), comm=None):
    n_in, n_out = len(args), len(out_shape)
    if comm is None:
        return pl.pallas_call(body, name=name, grid=grid, in_specs=list(in_specs), out_specs=list(out_specs),
                              out_shape=list(out_shape), scratch_shapes=list(scratch_shapes),
                              compiler_params=_cparams(len(grid)))(*args)
    k = len(comm.operands)

    def carried(*refs):
        ins, outs = refs[:n_in], refs[n_in + k:n_in + k + n_out]
        operands = refs[n_in + k + n_out:n_in + 2 * k + n_out]
        scratch, (send_sems, recv_sems) = refs[n_in + 2 * k + n_out:-2], refs[-2:]
        ids = [pl.program_id(ax) for ax in range(len(grid))]
        first = functools.reduce(jnp.logical_and, [i == 0 for i in ids])
        last = functools.reduce(jnp.logical_and, [i == g - 1 for i, g in zip(ids, grid)])

        @pl.when(first)
        def _():
            comm.start(operands, send_sems, recv_sems)

        body(*ins, *outs, *scratch)

        @pl.when(last)
        def _():
            comm.finish(operands, send_sems, recv_sems)

    res = pl.pallas_call(
        carried, name=name, grid=grid, in_specs=list(in_specs) + [ANY] * k, out_specs=list(out_specs) + [ANY] * k,
        out_shape=list(out_shape) + [jax.ShapeDtypeStruct(t.shape, t.dtype) for t in comm.operands],
        scratch_shapes=list(scratch_shapes) + [pltpu.SemaphoreType.DMA((comm.n_sems,))] * 2,
        input_output_aliases={n_in + i: n_out + i for i in range(k)},
        compiler_params=_cparams(len(grid)),
    )(*args, *comm.operands)
    return res[:n_out], res[n_out:]


def _mm(a, b, *, mode, name, out_dtype, n_q, tm, tn, tk, qa, qb, qo=lambda q: q, n_qr=1, out_q=None, comm=None):
    if mode == "nn":
        m, kdim, n = a.shape[1], a.shape[2], b.shape[2]
    elif mode == "nt":
        m, kdim, n = a.shape[1], a.shape[2], b.shape[1]
    else:
        kdim, m, n = a.shape[1], a.shape[2], b.shape[2]
    assert m % tm == 0 and n % tn == 0 and kdim % tk == 0, (name, m, n, kdim, tm, tn, tk)
    kr_n = kdim // tk
    nr = n_qr * kr_n
    out_q = n_q if out_q is None else out_q

    def split(r):
        return (r // kr_n, r % kr_n) if n_qr > 1 else (0, r)

    if mode == "tn":
        a_spec = pl.BlockSpec((None, tk, tm), lambda q, i, j, r: (qa(q, split(r)[0]), split(r)[1], i))
    else:
        a_spec = pl.BlockSpec((None, tm, tk), lambda q, i, j, r: (qa(q, split(r)[0]), i, split(r)[1]))
    if mode == "nt":
        b_spec = pl.BlockSpec((None, tn, tk), lambda q, i, j, r: (qb(q, split(r)[0]), j, split(r)[1]))
    else:
        b_spec = pl.BlockSpec((None, tk, tn), lambda q, i, j, r: (qb(q, split(r)[0]), split(r)[1], j))
    o_spec = pl.BlockSpec((None, tm, tn), lambda q, i, j, r: (qo(q), i, j))
    dims = _DIMS[mode]

    n_comm = len(comm.operands) if comm is not None else 0
    grid = (n_q, m // tm, n // tn, nr)

    def body(*refs):
        a_ref, b_ref = refs[0], refs[1]
        o_ref = refs[2 + n_comm]
        if comm is not None:
            comm_refs = refs[3 + n_comm:3 + 2 * n_comm]
            send_sems, recv_sems = refs[-2:]
            ids = [pl.program_id(ax) for ax in range(4)]
            first = functools.reduce(jnp.logical_and, [i == 0 for i in ids])
            last = functools.reduce(jnp.logical_and, [i == g - 1 for i, g in zip(ids, grid)])

            @pl.when(first)
            def _():
                comm.start(comm_refs, send_sems, recv_sems)
        lhs, rhs = a_ref[...], b_ref[...]
        if lhs.dtype != BF16:
            lhs = lhs.astype(BF16)
        if rhs.dtype != BF16:
            rhs = rhs.astype(BF16)
        part = lax.dot_general(lhs, rhs, dims, preferred_element_type=F32)
        if nr == 1:
            o_ref[...] = part.astype(o_ref.dtype)
        else:
            acc_ref = refs[3 + 2 * n_comm]
            r = pl.program_id(3)

            @pl.when(r == 0)
            def _():
                acc_ref[...] = part

            @pl.when(r > 0)
            def _():
                acc_ref[...] += part

            @pl.when(r == nr - 1)
            def _():
                o_ref[...] = acc_ref[...].astype(o_ref.dtype)
        if comm is not None:
            @pl.when(last)
            def _():
                comm.finish(comm_refs, send_sems, recv_sems)

    scratch = [pltpu.VMEM((tm, tn), F32)] if nr > 1 else []
    out_shape = [jax.ShapeDtypeStruct((out_q, m, n), out_dtype)]
    args = [a, b]
    if comm is not None:
        args += comm.operands
        out_shape += [jax.ShapeDtypeStruct(t.shape, t.dtype) for t in comm.operands]
        scratch += [pltpu.SemaphoreType.DMA((comm.n_sems,)), pltpu.SemaphoreType.DMA((comm.n_sems,))]
    outs = pl.pallas_call(
        body, name=name, grid=grid,
        in_specs=[a_spec, b_spec] + [ANY] * n_comm, out_specs=[o_spec] + [ANY] * n_comm,
        out_shape=out_shape, scratch_shapes=scratch,
        input_output_aliases={2 + i: 1 + i for i in range(n_comm)},
        compiler_params=_cparams(4),
    )(*args)
    return (outs[0], outs[1:]) if comm is not None else outs[0]


def _q0(q, qr):
    return 0


def _qq(q, qr):
    return q


def _qr(q, qr):
    return qr


def _mm_dd(a3, w6, widx, *, mode, name, out_dtype, n_q=1, n_qr=1, qa=_q0, comm=None):
    d = w6.shape[1]
    f32_in = a3.dtype != BF16
    tm = _tile(a3.shape[1] if mode != "tn" else a3.shape[2], 512 if f32_in else 1024)
    if n_qr > 1:
        qb = lambda q, qr: widx + qr
    elif n_q > 1:
        qb = lambda q, qr: widx + q
    else:
        qb = lambda q, qr: widx
    return _mm(a3, w6, mode=mode, name=name, out_dtype=out_dtype, n_q=n_q, n_qr=n_qr,
               tm=tm, tn=_tile(d, 1024), tk=d, qa=qa, qb=qb, comm=comm)


def _mm_wgrad(a3, b3, *, name, n_q=1, qa=_q0, qb=_q0):
    s, d = a3.shape[1], a3.shape[2]
    f32_in = a3.dtype != BF16 or b3.dtype != BF16
    return _mm(a3, b3, mode="tn", name=name, out_dtype=BF16, n_q=n_q, tm=_tile(d, 1024), tn=b3.shape[2],
               tk=_tile(s, 512 if f32_in else 1024), qa=qa, qb=qb)


def _grp_mm(a, b, *, mode, name, out_dtype, tm, w=None):
    s = a.shape[0]
    if mode == "tn":
        cg = b.shape[1] // len(POOL_WINDOWS)
        g_n = len(POOL_WINDOWS)
        ts = _tile(s, tm)
        nr = s // ts

        def body(a_ref, b_ref, o_ref, acc_ref):
            r = pl.program_id(1)
            part = lax.dot_general(a_ref[...], b_ref[...], _DIMS["tn"], preferred_element_type=F32)

            @pl.when(r == 0)
            def _():
                acc_ref[...] = part

            @pl.when(r > 0)
            def _():
                acc_ref[...] += part

            @pl.when(r == nr - 1)
            def _():
                o_ref[...] = acc_ref[...].astype(o_ref.dtype)

        return pl.pallas_call(
            body, name=name, grid=(g_n, nr),
            in_specs=[pl.BlockSpec((ts, cg), lambda g, r: (r, g)), pl.BlockSpec((ts, cg), lambda g, r: (r, g))],
            out_specs=pl.BlockSpec((None, cg, cg), lambda g, r: (g, 0, 0)),
            out_shape=jax.ShapeDtypeStruct((g_n, cg, cg), out_dtype),
            scratch_shapes=[pltpu.VMEM((cg, cg), F32)],
            compiler_params=_cparams(2),
        )(a, b)
    g_n, cg = w.shape[0], w.shape[1]
    ts = _tile(s, tm)
    dims = _DIMS[mode]

    def body(a_ref, w_ref, o_ref):
        o_ref[...] = lax.dot_general(a_ref[...], w_ref[...], dims, preferred_element_type=F32).astype(o_ref.dtype)

    return pl.pallas_call(
        body, name=name, grid=(g_n, s // ts),
        in_specs=[pl.BlockSpec((ts, cg), lambda g, i: (i, g)), pl.BlockSpec((None, cg, cg), lambda g, i: (g, 0, 0))],
        out_specs=pl.BlockSpec((ts, cg), lambda g, i: (i, g)),
        out_shape=jax.ShapeDtypeStruct((s, g_n * cg), out_dtype),
        compiler_params=_cparams(2),
    )(a, w)


def _causal_ext(load, r0, rows, halo):
    cur = load(r0, rows)
    prev = load(pl.multiple_of(jnp.maximum(r0 - halo, 0), halo), halo)
    prev = jnp.where(r0 > 0, prev, jnp.zeros_like(prev))
    return jnp.concatenate([prev, cur], axis=0)


def _anti_ext(load, r0, rows, halo, s):
    cur = load(r0, rows)
    nxt = load(pl.multiple_of(jnp.minimum(r0 + rows, s - halo), halo), halo)
    nxt = jnp.where(r0 + rows < s, nxt, jnp.zeros_like(nxt))
    return jnp.concatenate([cur, nxt], axis=0)


def _down(ext, k):
    return pltpu.roll(ext, k, axis=0)


def _up(ext, k):
    return pltpu.roll(ext, ext.shape[0] - k, axis=0)


def _fold8(x):
    return jnp.sum(x.reshape(x.shape[0] // 8, 8, x.shape[1]), axis=0)


def _sigmoid(x):
    return 1.0 / (1.0 + jnp.exp(-x))


def _pool(p, *, backward, name, rows=64, comm=None):
    s, d = p.shape
    strips_per_group = (d // len(POOL_WINDOWS)) // LANES
    assert strips_per_group * LANES * len(POOL_WINDOWS) == d and s % rows == 0

    def body(p_ref, o_ref):
        g = pl.program_id(0) // strips_per_group
        win = jnp.left_shift(2, g).astype(F32)

        def load(r0, n):
            return p_ref[pl.ds(r0, n), :]

        def pick(levels):
            return jnp.where(g == 0, levels[0], jnp.where(g == 1, levels[1], jnp.where(g == 2, levels[2], levels[3])))

        def chunk(c, carry):
            r0 = pl.multiple_of(c * rows, rows)
            if not backward:
                ext = _causal_ext(load, r0, rows, POOL_HALO)
                levels, acc = [], ext
                for k in (1, 2, 4, 8):
                    acc = acc + _down(acc, k)
                    levels.append(acc)
                t = (r0 + lax.broadcasted_iota(jnp.int32, (rows, LANES), 0)).astype(F32)
                cnt = jnp.minimum(t + 1.0, win)
                out = pick(levels)[POOL_HALO:] / cnt - ext[POOL_HALO:]
            else:
                ext = _anti_ext(load, r0, rows, POOL_HALO, s)
                t = (r0 + lax.broadcasted_iota(jnp.int32, (rows + POOL_HALO, LANES), 0)).astype(F32)
                e = ext / jnp.minimum(t + 1.0, win)
                levels, acc = [], e
                for k in (1, 2, 4, 8):
                    acc = acc + _up(acc, k)
                    levels.append(acc)
                out = pick(levels)[:rows] - ext[:rows]
            o_ref[pl.ds(r0, rows), :] = out.astype(o_ref.dtype)
            return carry

        lax.fori_loop(0, s // rows, chunk, 0)

    res = _pallas(
        body, name=name, grid=(d // LANES,),
        in_specs=[pl.BlockSpec((s, LANES), lambda j: (0, j))],
        out_specs=[pl.BlockSpec((s, LANES), lambda j: (0, j))],
        out_shape=[jax.ShapeDtypeStruct((s, d), BF16)], args=[p], comm=comm)
    return res[0] if comm is None else (res[0][0], res[1])


def _grp_fwd(pooled, w_grp, scale, *, name, comm=None):
    s, d = pooled.shape
    g_n, cg = w_grp.shape[0], w_grp.shape[1]
    ts = _tile(s, 1024)

    def body(a_ref, w_ref, sc_ref, mg_ref, mx_ref):
        mg = jnp.dot(a_ref[...], w_ref[...], preferred_element_type=F32)
        mg_ref[...] = mg.astype(BF16)
        mx_ref[...] = (mg * sc_ref[...]).astype(BF16)

    blk = pl.BlockSpec((ts, cg), lambda g, i: (i, g))
    return _pallas(
        body, name=name, grid=(g_n, s // ts),
        in_specs=[blk, pl.BlockSpec((None, cg, cg), lambda g, i: (g, 0, 0)), pl.BlockSpec((1, cg), lambda g, i: (0, g))],
        out_specs=[blk, blk],
        out_shape=[jax.ShapeDtypeStruct((s, d), BF16)] * 2, args=[pooled, w_grp, scale], comm=comm)


def _grp_bwd_pre(dmixed, mg, scale, *, name):
    s, d = dmixed.shape
    ts = _tile(s, 256, 16)

    def body(dm_ref, mg_ref, sc_ref, dmg_ref, dsc_ref):
        dm = dm_ref[...]
        dmg_ref[...] = (dm * sc_ref[...]).astype(BF16)
        part = jnp.sum(dm * mg_ref[...].astype(F32), axis=0, keepdims=True)

        @pl.when(pl.program_id(0) == 0)
        def _():
            dsc_ref[...] = part

        @pl.when(pl.program_id(0) > 0)
        def _():
            dsc_ref[...] += part

    blk = pl.BlockSpec((ts, d), lambda i: (i, 0))
    vec = pl.BlockSpec((1, d), lambda i: (0, 0))
    return pl.pallas_call(
        body, name=name, grid=(s // ts,),
        in_specs=[blk, blk, vec], out_specs=[blk, vec],
        out_shape=[jax.ShapeDtypeStruct((s, d), BF16), jax.ShapeDtypeStruct((1, d), F32)],
        compiler_params=_cparams(1),
    )(dmixed, mg, scale)


def _ln_fwd(res, mm, g, b, *, name, comm=None):
    s, d = res.shape
    ts = _tile(s, 128, 16)

    def body(res_ref, mm_ref, g_ref, b_ref, h_ref, hb_ref, xh_ref, rs_ref):
        z = ALPHA * res_ref[...] + mm_ref[...]
        mu = jnp.mean(z, axis=-1, keepdims=True)
        zc = z - mu
        var = jnp.mean(zc * zc, axis=-1, keepdims=True)
        rstd = lax.rsqrt(var + LN_EPS)
        xhat = zc * rstd
        h = xhat * g_ref[...] + b_ref[...]
        h_ref[...] = h
        hb_ref[...] = h.astype(BF16)
        xh_ref[...] = xhat
        rs_ref[...] = rstd

    blk = pl.BlockSpec((ts, d), lambda i: (i, 0))
    vec = pl.BlockSpec((1, d), lambda i: (0, 0))
    return _pallas(
        body, name=name, grid=(s // ts,),
        in_specs=[blk, blk, vec, vec],
        out_specs=[blk, blk, blk, pl.BlockSpec((ts, 1), lambda i: (i, 0))],
        out_shape=[jax.ShapeDtypeStruct((s, d), F32), jax.ShapeDtypeStruct((s, d), BF16),
                   jax.ShapeDtypeStruct((s, d), F32), jax.ShapeDtypeStruct((s, 1), F32)],
        args=[res, mm, g, b], comm=comm)


def _ln_bwd(dres, dmm, xhat, rstd, g, *, name):
    s, d = dmm.shape
    ts = _tile(s, 128, 16)
    has_res = dres is not None

    def body(*refs):
        if has_res:
            dres_ref, dmm_ref, xh_ref, rs_ref, g_ref, dz_ref, dzb_ref, dg_ref, db_ref = refs
            dh = ALPHA * dres_ref[...] + dmm_ref[...]
        else:
            dmm_ref, xh_ref, rs_ref, g_ref, dz_ref, dzb_ref, dg_ref, db_ref = refs
            dh = dmm_ref[...]
        xhat_ = xh_ref[...]
        dxh = dh * g_ref[...]
        c1 = jnp.mean(dxh, axis=-1, keepdims=True)
        c2 = jnp.mean(dxh * xhat_, axis=-1, keepdims=True)
        dz = rs_ref[...] * (dxh - c1 - xhat_ * c2)
        dz_ref[...] = dz
        dzb_ref[...] = dz.astype(BF16)
        dg_part = jnp.sum(dh * xhat_, axis=0, keepdims=True)
        db_part = jnp.sum(dh, axis=0, keepdims=True)

        @pl.when(pl.program_id(0) == 0)
        def _():
            dg_ref[...] = dg_part
            db_ref[...] = db_part

        @pl.when(pl.program_id(0) > 0)
        def _():
            dg_ref[...] += dg_part
            db_ref[...] += db_part

    blk = pl.BlockSpec((ts, d), lambda i: (i, 0))
    vec = pl.BlockSpec((1, d), lambda i: (0, 0))
    col = pl.BlockSpec((ts, 1), lambda i: (i, 0))
    ins = ([dres] if has_res else []) + [dmm, xhat, rstd, g]
    in_specs = ([blk] if has_res else []) + [blk, blk, col, vec]
    return pl.pallas_call(
        body, name=name, grid=(s // ts,),
        in_specs=in_specs, out_specs=[blk, blk, vec, vec],
        out_shape=[jax.ShapeDtypeStruct((s, d), F32), jax.ShapeDtypeStruct((s, d), BF16),
                   jax.ShapeDtypeStruct((1, d), F32), jax.ShapeDtypeStruct((1, d), F32)],
        compiler_params=_cparams(1),
    )(*ins)


def _loss_head(h, tgt, *, name):
    s, d = h.shape
    ts = _tile(s, 256, 16)

    def body(h_ref, t_ref, dh_ref, loss_ref):
        err = h_ref[...] - t_ref[...]
        dh_ref[...] = err * (1.0 / d)
        part = 0.5 * jnp.sum(jnp.mean(err * err, axis=-1, keepdims=True), axis=0, keepdims=True)

        @pl.when(pl.program_id(0) == 0)
        def _():
            loss_ref[...] = part

        @pl.when(pl.program_id(0) > 0)
        def _():
            loss_ref[...] += part

    blk = pl.BlockSpec((ts, d), lambda i: (i, 0))
    return pl.pallas_call(
        body, name=name, grid=(s // ts,),
        in_specs=[blk, blk], out_specs=[blk, pl.BlockSpec((1, 1), lambda i: (0, 0))],
        out_shape=[jax.ShapeDtypeStruct((s, d), F32), jax.ShapeDtypeStruct((1, 1), F32)],
        compiler_params=_cparams(1),
    )(h, tgt)


def _conv(ext, w, bias):
    c = bias + _down(ext, 2) * w[0:1] + _down(ext, 1) * w[1:2] + ext * w[2:3]
    return c[CONV_HALO:]


def _act_specs(s, n2):
    n_strips = pl.cdiv(n2, LANES)
    u_spec = pl.BlockSpec((None, 2, s, LANES), lambda hh, j: (hh, 0, 0, j))
    cwg = pl.BlockSpec((None, 3, LANES), lambda hh, j: (hh, 0, j))
    cwv = pl.BlockSpec((None, 3, LANES), lambda hh, j: (hh + 2, 0, j))
    cbg = pl.BlockSpec((None, 1, LANES), lambda hh, j: (hh, 0, j))
    cbv = pl.BlockSpec((None, 1, LANES), lambda hh, j: (hh + 2, 0, j))
    return n_strips, u_spec, cwg, cwv, cbg, cbv


def _act_fwd(u, cw, cb, *, name, rows=64):
    _, _, s, n2 = u.shape
    n_strips, u_spec, cwg, cwv, cbg, cbv = _act_specs(s, n2)

    def body(u_ref, wg_ref, wv_ref, bg_ref, bv_ref, a_ref):
        wg, wv, bg, bv = wg_ref[...], wv_ref[...], bg_ref[...], bv_ref[...]

        def chunk(c, carry):
            r0 = pl.multiple_of(c * rows, rows)
            cg = _conv(_causal_ext(lambda r, n: u_ref[0, pl.ds(r, n), :].astype(F32), r0, rows, CONV_HALO), wg, bg)
            cv = _conv(_causal_ext(lambda r, n: u_ref[1, pl.ds(r, n), :].astype(F32), r0, rows, CONV_HALO), wv, bv)
            a_ref[pl.ds(r0, rows), :] = (cg * _sigmoid(cg) * cv).astype(BF16)
            return carry

        lax.fori_loop(0, s // rows, chunk, 0)

    return pl.pallas_call(
        body, name=name, grid=(2, n_strips),
        in_specs=[u_spec, cwg, cwv, cbg, cbv],
        out_specs=pl.BlockSpec((None, s, LANES), lambda hh, j: (hh, 0, j)),
        out_shape=jax.ShapeDtypeStruct((2, s, n2), BF16),
        compiler_params=_cparams(2),
    )(u, cw, cw, cb, cb)


def _act_bwd(u, da, cw, cb, *, name, rows=64):
    _, _, s, n2 = u.shape
    n_strips, u_spec, cwg, cwv, cbg, cbv = _act_specs(s, n2)
    n_chunks = s // rows

    def body(u_ref, da_ref, wg_ref, wv_ref, bg_ref, bv_ref, du_ref, dwg_ref, dwv_ref, dbg_ref, dbv_ref, dg_s, dv_s):
        wg, wv, bg, bv = wg_ref[...], wv_ref[...], bg_ref[...], bv_ref[...]

        def first(c, sums):
            r0 = pl.multiple_of(c * rows, rows)
            eg = _causal_ext(lambda r, n: u_ref[0, pl.ds(r, n), :].astype(F32), r0, rows, CONV_HALO)
            ev = _causal_ext(lambda r, n: u_ref[1, pl.ds(r, n), :].astype(F32), r0, rows, CONV_HALO)
            cg, cv = _conv(eg, wg, bg), _conv(ev, wv, bv)
            sg = _sigmoid(cg)
            dact = da_ref[pl.ds(r0, rows), :]
            dval = dact * (cg * sg)
            dgate = dact * cv * (sg * (1.0 + cg * (1.0 - sg)))
            dg_s[pl.ds(r0, rows), :] = dgate
            dv_s[pl.ds(r0, rows), :] = dval
            new = []
            for dc, ext in ((dgate, eg), (dval, ev)):
                new += [_fold8(dc * _down(ext, 2)[CONV_HALO:]), _fold8(dc * _down(ext, 1)[CONV_HALO:]),
                        _fold8(dc * ext[CONV_HALO:]), _fold8(dc)]
            return tuple(acc + x for acc, x in zip(sums, new))

        zero = jnp.zeros((8, LANES), F32)
        sums = lax.fori_loop(0, n_chunks, first, (zero,) * 8)
        red = [jnp.sum(x, axis=0, keepdims=True) for x in sums]
        dwg_ref[...] = jnp.concatenate(red[0:3], axis=0)
        dbg_ref[...] = red[3]
        dwv_ref[...] = jnp.concatenate(red[4:7], axis=0)
        dbv_ref[...] = red[7]

        def second(c, carry):
            r0 = pl.multiple_of(c * rows, rows)
            for gv, (src, w) in enumerate(((dg_s, wg), (dv_s, wv))):
                ext = _anti_ext(lambda r, n: src[pl.ds(r, n), :], r0, rows, CONV_HALO, s)
                du = ext * w[2:3] + _up(ext, 1) * w[1:2] + _up(ext, 2) * w[0:1]
                du_ref[gv, pl.ds(r0, rows), :] = du[:rows].astype(BF16)
            return carry

        lax.fori_loop(0, n_chunks, second, 0)

    w_out = pl.BlockSpec((None, 3, LANES), lambda hh, j: (hh, 0, j))
    b_out = pl.BlockSpec((None, 1, LANES), lambda hh, j: (hh, 0, j))
    return pl.pallas_call(
        body, name=name, grid=(2, n_strips),
        in_specs=[u_spec, pl.BlockSpec((None, s, LANES), lambda hh, j: (hh, 0, j)), cwg, cwv, cbg, cbv],
        out_specs=[u_spec, w_out, w_out, b_out, b_out],
        out_shape=[jax.ShapeDtypeStruct((2, 2, s, n2), BF16),
                   jax.ShapeDtypeStruct((2, 3, n2), F32), jax.ShapeDtypeStruct((2, 3, n2), F32),
                   jax.ShapeDtypeStruct((2, 1, n2), F32), jax.ShapeDtypeStruct((2, 1, n2), F32)],
        scratch_shapes=[pltpu.VMEM((s, LANES), F32), pltpu.VMEM((s, LANES), F32)],
        compiler_params=_cparams(2),
    )(u, da, cw, cw, cb, cb)


def _dot_nt(a, b):
    return lax.dot_general(a, b, _DIMS["nt"], preferred_element_type=F32)


def _dot_tn(a, b):
    return lax.dot_general(a, b, _DIMS["tn"], preferred_element_type=F32)


def _band_mask(b):
    qi = lax.broadcasted_iota(jnp.int32, (ATTN_BLOCK, 2 * ATTN_BLOCK), 0)
    kj = lax.broadcasted_iota(jnp.int32, (ATTN_BLOCK, 2 * ATTN_BLOCK), 1)
    band = jnp.logical_and(kj >= qi, kj <= qi + ATTN_BLOCK)
    return jnp.logical_and(band, jnp.logical_or(b > 0, kj >= ATTN_BLOCK))


def _to_residues(nat, rm, d, seq, pad):
    seg = seq + pad
    for r in range(d):
        if pad:
            rm[pl.ds(r * seg, pad), :] = jnp.zeros((pad, LANES), rm.dtype)
        rows = nat[pl.ds(r, seq, stride=d), :] if d > 1 else nat[...]
        rm[pl.ds(r * seg + pad, seq), :] = rows.astype(rm.dtype)


def _rows_loop(s, rows, fn):
    def step(c, carry):
        fn(pl.ds(pl.multiple_of(c * rows, rows), rows))
        return carry

    lax.fori_loop(0, s // rows, step, 0)


def _attn_fwd(qkv, *, name, comm=None):
    _, s, dm = qkv.shape
    heads, scale = dm // HEAD_DIM, 1.0 / math.sqrt(HEAD_DIM)
    pad_rows = s + ATTN_BLOCK * max(DILATIONS)

    def body(q_ref, k_ref, v_ref, o_ref, ob_ref, lse_ref, nat, rq, rk, rv, ro, rl, o_tmp, l_tmp, o_acc, m_acc, s_acc):
        for d in DILATIONS:
            seq = s // d
            nb, seg = seq // ATTN_BLOCK, seq + ATTN_BLOCK
            for src, dst, pad in ((q_ref, rq, 0), (k_ref, rk, ATTN_BLOCK), (v_ref, rv, ATTN_BLOCK)):
                if d == 1:
                    _to_residues(src, dst, d, seq, pad)
                else:
                    nat[...] = src[...].astype(F32)
                    _to_residues(nat, dst, d, seq, pad)

            def block(idx, carry):
                r, b = idx // nb, idx % nb
                qrows = pl.ds(pl.multiple_of(r * seq + b * ATTN_BLOCK, ATTN_BLOCK), ATTN_BLOCK)
                krows = pl.ds(pl.multiple_of(r * seg + b * ATTN_BLOCK, ATTN_BLOCK), 2 * ATTN_BLOCK)
                sc = jnp.where(_band_mask(b), _dot_nt(rq[qrows, :], rk[krows, :]) * scale, NEG_INF)
                m = jnp.max(sc, axis=-1, keepdims=True)
                p = jnp.exp(sc - m)
                den = jnp.sum(p, axis=-1, keepdims=True)
                ro[qrows, :] = jnp.dot(p.astype(BF16), rv[krows, :], preferred_element_type=F32) / den
                rl[qrows, :] = jnp.broadcast_to(m + jnp.log(den), (ATTN_BLOCK, LANES))
                return carry

            lax.fori_loop(0, d * nb, block, 0, unroll=ATTN_UNROLL)
            if d == 1:
                def first(rows):
                    o_acc[rows, :] = ro[rows, :]
                    m_acc[rows, :] = rl[rows, :]
                    s_acc[rows, :] = jnp.ones((rows.size, LANES), F32)

                _rows_loop(s, 64, first)
            else:
                for r in range(d):
                    o_tmp[pl.ds(r, seq, stride=d), :] = ro[pl.ds(r * seq, seq), :]
                    l_tmp[pl.ds(r, seq, stride=d), :] = rl[pl.ds(r * seq, seq), :]

                def merge(rows):
                    m_old, l_new = m_acc[rows, :], l_tmp[rows, :]
                    m_new = jnp.maximum(m_old, l_new)
                    w_old, w_new = jnp.exp(m_old - m_new), jnp.exp(l_new - m_new)
                    o_acc[rows, :] = o_acc[rows, :] * w_old + o_tmp[rows, :] * w_new
                    s_acc[rows, :] = s_acc[rows, :] * w_old + w_new
                    m_acc[rows, :] = m_new

                _rows_loop(s, 64, merge)

        def finish(rows):
            tot = s_acc[rows, :]
            o = o_acc[rows, :] / tot
            o_ref[rows, :] = o
            ob_ref[rows, :] = o.astype(BF16)
            lse_ref[rows, :] = m_acc[rows, :] + jnp.log(tot)

        _rows_loop(s, 64, finish)

    head = lambda i: pl.BlockSpec((None, s, HEAD_DIM), lambda h, i=i: (i, 0, h))
    out = pl.BlockSpec((s, HEAD_DIM), lambda h: (0, h))
    nat_f32 = pltpu.VMEM((s, LANES), F32)
    return _pallas(
        body, name=name, grid=(heads,),
        in_specs=[head(0), head(1), head(2)], out_specs=[out, out, out],
        out_shape=[jax.ShapeDtypeStruct((s, dm), F32), jax.ShapeDtypeStruct((s, dm), BF16),
                   jax.ShapeDtypeStruct((s, dm), F32)],
        scratch_shapes=[nat_f32, pltpu.VMEM((s, LANES), BF16), pltpu.VMEM((pad_rows, LANES), BF16),
                        pltpu.VMEM((pad_rows, LANES), BF16), nat_f32, nat_f32, nat_f32, nat_f32, nat_f32, nat_f32, nat_f32],
        args=[qkv, qkv, qkv], comm=comm)


def _attn_delta(do, o, *, name):
    s, dm = o.shape
    heads = dm // HEAD_DIM
    ts = _tile(s, 256, 16)

    def body(do_ref, o_ref, dl_ref):
        for h in range(heads):
            hs = slice(h * HEAD_DIM, (h + 1) * HEAD_DIM)
            row = jnp.sum(do_ref[:, hs].astype(F32) * o_ref[:, hs], axis=-1, keepdims=True)
            dl_ref[:, hs] = jnp.broadcast_to(row, (ts, HEAD_DIM))

    blk = pl.BlockSpec((ts, dm), lambda i: (i, 0))
    return pl.pallas_call(
        body, name=name, grid=(s // ts,),
        in_specs=[blk, blk], out_specs=blk,
        out_shape=jax.ShapeDtypeStruct((s, dm), F32),
        compiler_params=_cparams(1),
    )(do, o)


def _attn_bwd(qkv, do, lse, delta, *, name):
    _, s, dm = qkv.shape
    heads, scale = dm // HEAD_DIM, 1.0 / math.sqrt(HEAD_DIM)
    pad_rows = s + ATTN_BLOCK * max(DILATIONS)

    def body(q_ref, k_ref, v_ref, do_ref, l_ref, dl_ref, out_ref,
             nat, rq, rdo, rk, rv, rl, rdl, rdq, kc, kp, vc, vp, aq, ak, av):
        for d in DILATIONS:
            seq = s // d
            nb, seg = seq // ATTN_BLOCK, seq + ATTN_BLOCK
            for src, dst, pad in ((q_ref, rq, 0), (do_ref, rdo, 0), (k_ref, rk, ATTN_BLOCK), (v_ref, rv, ATTN_BLOCK)):
                if d == 1:
                    _to_residues(src, dst, d, seq, pad)
                else:
                    nat[...] = src[...].astype(F32)
                    _to_residues(nat, dst, d, seq, pad)
            if d == 1:
                lse_rows, dl_rows = l_ref, dl_ref
            else:
                lse_rows, dl_rows = rl, rdl
                _to_residues(l_ref, rl, d, seq, 0)
                _to_residues(dl_ref, rdl, d, seq, 0)
            for r in range(d):
                last = pl.ds(r * seg + nb * ATTN_BLOCK, ATTN_BLOCK)
                kp[last, :] = jnp.zeros((ATTN_BLOCK, LANES), F32)
                vp[last, :] = jnp.zeros((ATTN_BLOCK, LANES), F32)

            def block(idx, carry):
                r, b = idx // nb, idx % nb
                qrows = pl.ds(pl.multiple_of(r * seq + b * ATTN_BLOCK, ATTN_BLOCK), ATTN_BLOCK)
                krow = pl.multiple_of(r * seg + b * ATTN_BLOCK, ATTN_BLOCK)
                krows = pl.ds(krow, 2 * ATTN_BLOCK)
                before, own = pl.ds(krow, ATTN_BLOCK), pl.ds(krow + ATTN_BLOCK, ATTN_BLOCK)
                qb, dob, kw, vw = rq[qrows, :], rdo[qrows, :], rk[krows, :], rv[krows, :]
                lse_b = jnp.concatenate([lse_rows[qrows, :]] * 2, axis=1)
                dl_b = jnp.concatenate([dl_rows[qrows, :]] * 2, axis=1)
                sc = jnp.where(_band_mask(b), _dot_nt(qb, kw) * scale, NEG_INF)
                p = jnp.exp(sc - lse_b)
                ds = (p * (_dot_nt(dob, vw) - dl_b) * scale).astype(BF16)
                rdq[qrows, :] = jnp.dot(ds, kw, preferred_element_type=F32)
                dk, dv = _dot_tn(ds, qb), _dot_tn(p.astype(BF16), dob)
                kp[before, :] = dk[:ATTN_BLOCK]
                kc[own, :] = dk[ATTN_BLOCK:]
                vp[before, :] = dv[:ATTN_BLOCK]
                vc[own, :] = dv[ATTN_BLOCK:]
                return carry

            lax.fori_loop(0, d * nb, block, 0, unroll=ATTN_UNROLL)
            for r in range(d):
                keys = pl.ds(r * seg + ATTN_BLOCK, seq)
                rows = pl.ds(r, seq, stride=d) if d > 1 else pl.ds(0, seq)
                for acc, val in ((aq, rdq[pl.ds(r * seq, seq), :]), (ak, kc[keys, :] + kp[keys, :]),
                                 (av, vc[keys, :] + vp[keys, :])):
                    if d == DILATIONS[0]:
                        acc[rows, :] = val
                    else:
                        acc[rows, :] += val

        def finish(rows):
            for i, acc in enumerate((aq, ak, av)):
                out_ref[i, rows, :] = acc[rows, :].astype(BF16)

        _rows_loop(s, 256, finish)

    head = lambda i: pl.BlockSpec((None, s, HEAD_DIM), lambda h, i=i: (i, 0, h))
    col = pl.BlockSpec((s, HEAD_DIM), lambda h: (0, h))
    f32 = lambda rows: pltpu.VMEM((rows, LANES), F32)
    b16 = lambda rows: pltpu.VMEM((rows, LANES), BF16)
    return pl.pallas_call(
        body, name=name, grid=(heads,),
        in_specs=[head(0), head(1), head(2), col, col, col],
        out_specs=pl.BlockSpec((3, s, HEAD_DIM), lambda h: (0, 0, h)),
        out_shape=jax.ShapeDtypeStruct((3, s, dm), BF16),
        scratch_shapes=[f32(s), b16(s), b16(s), b16(pad_rows), b16(pad_rows), f32(s), f32(s),
                        f32(s), f32(pad_rows), f32(pad_rows), f32(pad_rows), f32(pad_rows), f32(s), f32(s), f32(s)],
        compiler_params=_cparams(1),
    )(qkv, qkv, qkv, do, lse, delta)


def _ew(fn, ins, out_dtypes, *, name, tile_bytes=1 << 20):
    first = ins[0][0] if isinstance(ins[0], tuple) else ins[0]
    rows, cols = first.shape[-2], first.shape[-1]
    tr = _tile(rows, max(16, tile_bytes // (4 * cols)), 16)
    n_in = len(ins)

    def body(*refs):
        outs = fn(*[r[...] for r in refs[:n_in]])
        for o_ref, val in zip(refs[n_in:], outs):
            o_ref[...] = val.astype(o_ref.dtype)

    in_specs, args = [], []
    for item in ins:
        if isinstance(item, tuple):
            arr, lead = item
            in_specs.append(pl.BlockSpec((None, tr, cols), lambda i, lead=lead: (lead, i, 0)))
            args.append(arr)
        else:
            in_specs.append(pl.BlockSpec((tr, cols), lambda i: (i, 0)))
            args.append(item)
    blk = pl.BlockSpec((tr, cols), lambda i: (i, 0))
    return pl.pallas_call(
        body, name=name, grid=(rows // tr,),
        in_specs=in_specs, out_specs=[blk] * len(out_dtypes),
        out_shape=[jax.ShapeDtypeStruct((rows, cols), dt) for dt in out_dtypes],
        compiler_params=_cparams(1),
    )(*args)


def _adamw_math(g, w, m, v):
    m2 = ADAM_B1 * m + (1.0 - ADAM_B1) * g
    v2 = ADAM_B2 * v + (1.0 - ADAM_B2) * (g * g)
    m_hat = m2 / (1.0 - ADAM_B1 ** ADAM_STEP)
    v_hat = v2 / (1.0 - ADAM_B2 ** ADAM_STEP)
    delta = -ADAM_LR * (m_hat / (jnp.sqrt(v_hat) + ADAM_EPS) + ADAM_WD * w)
    return g, delta, m2, v2


def _scalars(*vals):
    return jnp.stack([jnp.asarray(v, jnp.int32) for v in vals])


def _adamw_halves(mine, theirs, core, w, m, v, *, name, lead=0, prev=None):
    shape = w.shape
    a_n, rh, cols = mine.shape
    w3, m3, v3 = (t.reshape(-1, 2 * rh, cols) for t in (w, m, v))
    tr, tc = _tile2(rh, cols, 1 << 17, 8)
    n_i = rh // tr

    def body(c_ref, mine_ref, theirs_ref, w_ref, m_ref, v_ref, *rest):
        g = jnp.where(pl.program_id(1) == c_ref[0], mine_ref[...], theirs_ref[...])
        outs = _adamw_math(g, w_ref[...], m_ref[...], v_ref[...])
        for ref, val in zip(rest[-4:], outs):
            ref[...] = val

    half = pl.BlockSpec((None, tr, tc), lambda a, h, i, j, c_ref: (a, i, j))
    full = pl.BlockSpec((None, tr, tc), lambda a, h, i, j, c_ref: (lead + a, h * n_i + i, j))
    args = [mine, theirs, w3, m3, v3]
    in_specs = [half, half, full, full, full]
    aliases = {}
    if prev is not None:
        args += [p.reshape(w3.shape) for p in prev]
        in_specs += [ANY] * 4
        aliases = {6 + k: k for k in range(4)}
    outs = pl.pallas_call(
        body, name=name,
        grid_spec=pltpu.PrefetchScalarGridSpec(
            num_scalar_prefetch=1, grid=(a_n, 2, n_i, cols // tc), in_specs=in_specs, out_specs=[full] * 4),
        out_shape=[jax.ShapeDtypeStruct(w3.shape, F32)] * 4,
        input_output_aliases=aliases,
        compiler_params=_cparams(4),
    )(_scalars(core), *args)
    return [o.reshape(shape) for o in outs]


def _cast_into(src, buf, lead, slot, *, name, buf_shape=None, dtype=BF16):
    a_n, rows, cols = src.shape
    tr = _tile(rows, max(16, (1 << 21) // (4 * cols)), 16)

    def body(slot_ref, src_ref, *rest):
        rest[-1][...] = src_ref[...].astype(rest[-1].dtype)

    in_specs = [pl.BlockSpec((None, tr, cols), lambda a, i, slot_ref: (a, i, 0))]
    args = [src]
    aliases = {}
    if buf is not None:
        in_specs.append(ANY)
        args.append(buf)
        aliases = {2: 0}
        buf_shape, dtype = buf.shape, buf.dtype
    return pl.pallas_call(
        body, name=name,
        grid_spec=pltpu.PrefetchScalarGridSpec(
            num_scalar_prefetch=1, grid=(a_n, rows // tr), in_specs=in_specs,
            out_specs=pl.BlockSpec((None, None, tr, cols), lambda a, i, slot_ref: (lead + a, slot_ref[0], i, 0))),
        out_shape=jax.ShapeDtypeStruct(buf_shape, dtype),
        input_output_aliases=aliases,
        compiler_params=_cparams(2),
    )(_scalars(slot), *args)


def _place():
    x, y, c = lax.axis_index("x"), lax.axis_index("y"), lax.axis_index("c")
    chips = [(1 - x, y), (x, 1 - y), (1 - x, 1 - y)]
    return x, y, c, chips


def _remote(src, dst, send_sem, recv_sem, dev):
    return pltpu.make_async_remote_copy(src_ref=src, dst_ref=dst, send_sem=send_sem, recv_sem=recv_sem,
                                        device_id=dev, device_id_type=MESH)


def _gather_comm(bufs, pieces):
    def plan(refs, send_sems, recv_sems):
        x, y, c, chips = _place()

        def region(p, q, core):
            t, a0, a1, part, n_parts = pieces[p]
            rh, cw = bufs[t].shape[2] // 2, bufs[t].shape[3] // n_parts
            return refs[t].at[pl.ds(a0, a1 - a0), q, pl.ds(core * rh, rh), pl.ds(part * cw, cw)]

        def ici(p, j, q):
            cx, cy = chips[j]
            return _remote(region(p, q, c), region(p, q, c), send_sems.at[6 * p + j], recv_sems.at[6 * p + j], (cx, cy, c))

        def d2d(p, j, core):
            cx, cy = chips[j]
            rows = region(p, 2 * cx + cy, core)
            return _remote(rows, rows, send_sems.at[6 * p + 3 + j], recv_sems.at[6 * p + 3 + j], (x, y, 1 - c))

        return 2 * x + y, c, [2 * cx + cy for cx, cy in chips], ici, d2d

    todo = [(j, p) for j in range(3) for p in range(len(pieces))]

    def start(refs, send_sems, recv_sems):
        q_me, _, _, ici, _ = plan(refs, send_sems, recv_sems)
        for j, p in todo:
            ici(p, j, q_me).start()

    def finish(refs, send_sems, recv_sems):
        q_me, c, q_of, ici, d2d = plan(refs, send_sems, recv_sems)
        for j, p in todo:
            ici(p, j, q_of[j]).wait_recv()
            d2d(p, j, c).start()
        for j, p in todo:
            d2d(p, j, 1 - c).wait_recv()
        for j, p in todo:
            ici(p, j, q_me).wait_send()
            d2d(p, j, c).wait_send()

    return _Comm(bufs, 6 * len(pieces), start, finish)


def _chips_comm(parts, lands):
    n = len(parts)

    def copy(refs, send_sems, recv_sems, t, j, q_src, q_dst):
        x, y, c, chips = _place()
        cx, cy = chips[j]
        return _remote(refs[t].at[:, q_src], refs[n + t].at[q_dst], send_sems.at[3 * t + j], recv_sems.at[3 * t + j],
                       (cx, cy, c))

    todo = [(j, t) for j in range(3) for t in range(n)]

    def qs():
        x, y, _, chips = _place()
        return 2 * x + y, [2 * cx + cy for cx, cy in chips]

    def start(refs, send_sems, recv_sems):
        q_me, q_of = qs()
        for j, t in todo:
            copy(refs, send_sems, recv_sems, t, j, q_of[j], q_me).start()

    def finish(refs, send_sems, recv_sems):
        q_me, q_of = qs()
        for j, t in todo:
            copy(refs, send_sems, recv_sems, t, j, q_me, q_of[j]).wait_recv()
        for j, t in todo:
            copy(refs, send_sems, recv_sems, t, j, q_of[j], q_me).wait_send()

    return _Comm(list(parts) + list(lands), 3 * n, start, finish)


def _comm_call(comm, *, name):
    k = len(comm.operands)

    def body(*refs):
        operands, (send_sems, recv_sems) = refs[k:2 * k], refs[2 * k:]
        comm.start(operands, send_sems, recv_sems)
        comm.finish(operands, send_sems, recv_sems)

    return pl.pallas_call(
        body, name=name, in_specs=[ANY] * k, out_specs=[ANY] * k,
        out_shape=[jax.ShapeDtypeStruct(t.shape, t.dtype) for t in comm.operands],
        input_output_aliases={i: i for i in range(k)},
        scratch_shapes=[pltpu.SemaphoreType.DMA((comm.n_sems,)), pltpu.SemaphoreType.DMA((comm.n_sems,))],
    )(*comm.operands)


def _rs_sibling(grads, *, name):
    n = len(grads)
    halves = [jax.ShapeDtypeStruct(g.shape[:2] + (g.shape[2] // 2, g.shape[3]), g.dtype) for g in grads]

    def body(*refs):
        src, theirs = refs[:n], refs[n:2 * n]
        send_sems, recv_sems = refs[2 * n:]
        x, y, c, _ = _place()
        ops = []
        for t in range(n):
            rh = grads[t].shape[2] // 2
            give = _remote(src[t].at[:, :, pl.ds((1 - c) * rh, rh), :], theirs[t], send_sems.at[t], recv_sems.at[t],
                           (x, y, 1 - c))
            give.start()
            ops.append(give)
        for op in ops:
            op.wait()

    return pl.pallas_call(
        body, name=name, in_specs=[ANY] * n, out_specs=[ANY] * n, out_shape=halves,
        scratch_shapes=[pltpu.SemaphoreType.DMA((n,)), pltpu.SemaphoreType.DMA((n,))],
    )(*grads)


def _rs_add(grad, theirs, core, slot, *, name):
    a_n, _, rh, cols = theirs.shape
    tr, tc = _tile2(rh, cols, 1 << 18, 16)
    n_i = rh // tr

    def body(s_ref, g_ref, t_ref, p_ref, y_ref):
        part = (g_ref[...].astype(F32) + t_ref[...].astype(F32)).astype(BF16)
        p_ref[...] = part

        @pl.when(pl.program_id(3) == s_ref[1])
        def _():
            y_ref[...] = part

    blk = (None, None, tr, tc)
    return pl.pallas_call(
        body, name=name,
        grid_spec=pltpu.PrefetchScalarGridSpec(
            num_scalar_prefetch=1, grid=(a_n, n_i, cols // tc, N_CHIPS),
            in_specs=[pl.BlockSpec(blk, lambda a, i, j, q, s: (a, q, s[0] * n_i + i, j)),
                      pl.BlockSpec(blk, lambda a, i, j, q, s: (a, q, i, j))],
            out_specs=[pl.BlockSpec(blk, lambda a, i, j, q, s: (a, q, i, j)),
                       pl.BlockSpec(blk, lambda a, i, j, q, s: (s[1], a, i, j))]),
        out_shape=[jax.ShapeDtypeStruct(theirs.shape, BF16),
                   jax.ShapeDtypeStruct((N_CHIPS, a_n, rh, cols), BF16)],
        compiler_params=_cparams(4),
    )(_scalars(core, slot), grad, theirs)


def _rs_finish(halves, *, name):
    n = len(halves)

    def body(*refs):
        src, dst = refs[:n], refs[n:2 * n]
        send_sems, recv_sems = refs[2 * n:]
        x, y, c, _ = _place()
        ops = []
        for t in range(n):
            give = _remote(src[t], dst[t], send_sems.at[t], recv_sems.at[t], (x, y, 1 - c))
            give.start()
            ops.append(give)
        for op in ops:
            op.wait()

    return pl.pallas_call(
        body, name=name, in_specs=[ANY] * n, out_specs=[ANY] * n,
        out_shape=[jax.ShapeDtypeStruct(h.shape, h.dtype) for h in halves],
        scratch_shapes=[pltpu.SemaphoreType.DMA((n,)), pltpu.SemaphoreType.DMA((n,))],
    )(*halves)


def _all_reduce_small(vec, *, name):
    rows = vec.shape[0]

    def body(v_ref, o_ref, land, send_sems, recv_sems):
        x, y, c, _ = _place()
        me = 4 * x + 2 * y + c
        land[me] = v_ref[...]
        flips = [(fx, fy, fc) for fx in (0, 1) for fy in (0, 1) for fc in (0, 1)][1:]
        sent = []
        for k, (fx, fy, fc) in enumerate(flips):
            cp = _remote(v_ref, land.at[me], send_sems.at[k], recv_sems.at[k], (x ^ fx, y ^ fy, c ^ fc))
            cp.start()
            sent.append(cp)
        for k, (fx, fy, fc) in enumerate(flips):
            peer = 4 * (x ^ fx) + 2 * (y ^ fy) + (c ^ fc)
            _remote(v_ref, land.at[peer], send_sems.at[k], recv_sems.at[k], (x ^ fx, y ^ fy, c ^ fc)).wait_recv()
        for cp in sent:
            cp.wait_send()
        total = land[0]
        for dev in range(1, 8):
            total = total + land[dev]
        o_ref[...] = total

    whole = pl.BlockSpec(memory_space=pltpu.VMEM)
    return pl.pallas_call(
        body, name=name, in_specs=[whole], out_specs=whole,
        out_shape=jax.ShapeDtypeStruct(vec.shape, F32),
        scratch_shapes=[pltpu.VMEM((8, rows, LANES), F32), pltpu.SemaphoreType.DMA((7,)), pltpu.SemaphoreType.DMA((7,))],
        compiler_params=pltpu.CompilerParams(vmem_limit_bytes=VMEM_LIMIT),
    )(vec)


def _pack(parts, mult=16):
    flat = jnp.concatenate([p.reshape(-1).astype(F32) for p in parts])
    rows = -(-flat.shape[0] // (LANES * mult)) * mult
    return jnp.pad(flat, (0, rows * LANES - flat.shape[0])).reshape(rows, LANES)


def _unpack(vec, shapes):
    flat, out, pos = vec.reshape(-1), [], 0
    for shp in shapes:
        size = math.prod(shp)
        out.append(flat[pos:pos + size].reshape(shp))
        pos += size
    return out


def kernel(x, pool_w_in, pool_w_grp, pool_scale, pool_w_out, attn_w_q, attn_w_o, shared_w_k, shared_w_v, ffn_w_up, ffn_conv_w, ffn_conv_b, ffn_w_down, ln1_g, ln1_b, ln2_g, ln2_b, loss_target, m_pool_w_in, m_pool_w_grp, m_pool_scale, m_pool_w_out, m_attn_w_q, m_attn_w_o, m_shared_w_k, m_shared_w_v, m_ffn_w_up, m_ffn_conv_w, m_ffn_conv_b, m_ffn_w_down, m_ln1_g, m_ln1_b, m_ln2_g, m_ln2_b, v_pool_w_in, v_pool_w_grp, v_pool_scale, v_pool_w_out, v_attn_w_q, v_attn_w_o, v_shared_w_k, v_shared_w_v, v_ffn_w_up, v_ffn_conv_w, v_ffn_conv_b, v_ffn_w_down, v_ln1_g, v_ln1_b, v_ln2_g, v_ln2_b):
    s, d = x.shape[1], x.shape[2]
    n2 = ffn_w_up.shape[2]
    fq = ffn_w_down.shape[1]
    assert 2 * fq == n2 and d % N_CHIPS == 0
    g_n, cg = pool_w_grp.shape[1], pool_w_grp.shape[3]
    xs, tgt = x[0], loss_target[0]
    q_me = 2 * lax.axis_index("x") + lax.axis_index("y")
    core = lax.axis_index("c")
    rq = d // N_CHIPS

    six_g = None
    for i, w_ in enumerate((pool_w_in[0], pool_w_out[0], attn_w_q[0], shared_w_k, shared_w_v, attn_w_o[0])):
        six_g = _cast_into(w_[None], six_g, i, q_me, name=f"cast_w{i}", buf_shape=(6, N_CHIPS, rq, d))
    grp_g = _cast_into(pool_w_grp[0], None, 0, q_me, name="cast_grp", buf_shape=(g_n, N_CHIPS, cg // N_CHIPS, cg))
    up_t, m_up_t, v_up_t = (jnp.swapaxes(t, 1, 2) for t in (ffn_w_up, m_ffn_w_up, v_ffn_w_up))
    up_g = _cast_into(up_t, None, 0, q_me, name="cast_up", buf_shape=(2, N_CHIPS, n2, d))
    dn_g = _cast_into(ffn_w_down, None, 0, q_me, name="cast_down", buf_shape=(2, N_CHIPS, fq, d))
    small = _pack([ffn_conv_w, pool_scale])
    small_g = _cast_into(small[None], None, 0, q_me, name="cast_small", buf_shape=(1, N_CHIPS) + small.shape, dtype=F32)
    bufs = [six_g, grp_g, up_g, dn_g, small_g]
    SIX, GRP, UP, DN, SMALL = range(5)
    first = [SIX, GRP, SMALL]
    landed = _comm_call(_gather_comm([bufs[t] for t in first], [(0, IW_IN, IW_OUT + 1, 0, 1), (1, 0, g_n, 0, 1),
                                                                (2, 0, 1, 0, 1)]), name="gather_first")
    for t, arr in zip(first, landed):
        bufs[t] = arr
    small_q = bufs[SMALL].reshape(N_CHIPS, -1)
    n_cw = 2 * 3 * n2
    conv_w = small_q[:, :n_cw].reshape(N_CHIPS, 2, 3, n2).transpose(1, 0, 2, 3)
    scale_full = small_q[:, n_cw:n_cw + d // N_CHIPS].reshape(1, d)
    conv_b = ffn_conv_b.reshape(2, N_CHIPS, 1, n2)

    def w6():
        return bufs[SIX].reshape(6, d, d)

    def up8():
        return bufs[UP].reshape(2 * N_CHIPS, n2, d)

    def dn4():
        return bufs[DN].reshape(4, n2, d)

    def gathered(pieces):
        used = sorted({pc[0] for pc in pieces})
        local = [(used.index(t), a0, a1, part, n_parts) for t, a0, a1, part, n_parts in pieces]
        return _gather_comm([bufs[t] for t in used], local), used

    def store(used, operands):
        for t, arr in zip(used, operands):
            bufs[t] = arr

    def ffn_fwd(l, hb, carry_up, carry_down):
        comm, used = gathered(carry_up)
        u, landed = _mm(hb[None], up8(), mode="nt", name=f"ffn{l}_up", out_dtype=BF16, n_q=4, tm=_tile(s, 512), tn=n2,
                        tk=d, qa=_q0, qb=lambda q, qr: 4 * l + _perm(q), comm=comm)
        store(used, landed)
        u = u.reshape(2, 2, s, n2)
        act = _act_fwd(u, conv_w[l], conv_b[l], name=f"ffn{l}_act")
        comm, used = gathered(carry_down) if carry_down else (None, None)
        ff = _mm(act, dn4(), mode="nn", name=f"ffn{l}_down", out_dtype=F32, n_q=1, n_qr=2, tm=_tile(s, 1024),
                 tn=_tile(d, 1024), tk=n2, qa=_qr, qb=lambda q, qr: 2 * l + qr, qo=lambda q: 0, comm=comm)
        if comm is not None:
            ff, landed = ff
            store(used, landed)
        return u, act, ff[0]

    def up0_parts(*parts):
        return gathered([(UP, 0, 1, part, 8) for part in parts])

    comm, used = up0_parts(0, 1)
    p, landed = _mm_dd(xs[None], w6(), IW_IN, mode="nn", name="pool_in", out_dtype=F32, comm=comm)
    store(used, landed)
    comm, used = up0_parts(2)
    pooled, landed = _pool(p[0], backward=False, name="pool_fwd", comm=comm)
    store(used, landed)
    comm, used = up0_parts(3)
    (mg, mixed), landed = _grp_fwd(pooled, bufs[GRP].reshape(g_n, cg, cg), scale_full, name="pool_grp", comm=comm)
    store(used, landed)
    comm, used = up0_parts(4, 5)
    mix0, landed = _mm_dd(mixed[None], w6(), IW_OUT, mode="nn", name="pool_out", out_dtype=F32, comm=comm)
    store(used, landed)
    comm, used = up0_parts(6, 7)
    (h1, h1b, xh1, rs1), landed = _ln_fwd(xs, mix0[0], ln1_g[0:1], ln1_b[0:1], name="ln1_0", comm=comm)
    store(used, landed)
    u0, act0, ff0 = ffn_fwd(0, h1b, [(DN, 0, 1, 0, 1), (SIX, IW_Q, IW_V + 1, 0, 1)], [(SIX, IW_O, IW_O + 1, 0, 1)])
    h2, h2b, xh2, rs2 = _ln_fwd(h1, ff0, ln2_g[0:1], ln2_b[0:1], name="ln2_0")

    qkv = _mm_dd(h2b[None], w6(), IW_Q, mode="nn", name="attn_qkv", out_dtype=BF16, n_q=3)
    comm, used = gathered([(UP, 1, 2, 0, 1)])
    (o, ob, lse), landed = _attn_fwd(qkv, name="attn_fwd", comm=comm)
    store(used, landed)
    mix1 = _mm_dd(ob[None], w6(), IW_O, mode="nn", name="attn_out", out_dtype=F32)[0]
    h3, h3b, xh3, rs3 = _ln_fwd(h2, mix1, ln1_g[1:2], ln1_b[1:2], name="ln1_1")
    u1, act1, ff1 = ffn_fwd(1, h3b, [(DN, 1, 2, 0, 1)], None)
    h4, _, xh4, rs4 = _ln_fwd(h3, ff1, ln2_g[1:2], ln2_b[1:2], name="ln2_1")
    dh4, loss_local = _loss_head(h4, tgt, name="loss_head")

    rs_parts, rs_lands = {}, {}

    def rs_prepare(items):
        theirs = _rs_sibling([g_ for _, g_ in items], name="rs_sibling_" + items[0][0])
        for (nm, g_), t_ in zip(items, theirs):
            rs_parts[nm], rs_lands[nm] = _rs_add(g_, t_, core, q_me, name=f"rs_add_{nm}")

    def rs_exchange(names):
        return _chips_comm([rs_parts[nm] for nm in names], [rs_lands[nm] for nm in names])

    def rs_landed(names, operands):
        for nm, land in zip(names, operands[len(names):]):
            rs_lands[nm] = land

    d_conv_w, d_conv_b = [None, None], [None, None]

    def ffn_bwd(l, dzb, u, act, hb, carry_dact):
        da = _mm(dzb[None], dn4(), mode="nt", name=f"ffn{l}_dact", out_dtype=F32, n_q=2, tm=_tile(s, 512), tn=n2,
                 tk=d, qa=_q0, qb=lambda q, qr: 2 * l + q,
                 comm=rs_exchange(carry_dact) if carry_dact else None)
        if carry_dact:
            da, landed = da
            rs_landed(carry_dact, landed)
        du, dwg, dwv, dbg, dbv = _act_bwd(u, da, conv_w[l], conv_b[l], name=f"ffn{l}_dconv")
        d_conv_w[l] = jnp.concatenate([dwg, dwv], axis=0)
        d_conv_b[l] = jnp.concatenate([dbg, dbv], axis=0)
        du4 = du.reshape(4, s, n2)
        g_dn = _mm(act, dzb[None], mode="tn", name=f"ffn{l}_gdown", out_dtype=BF16, n_q=2, tm=n2, tn=_tile(d, 1024),
                   tk=_tile(s, 512), qa=_qq, qb=_q0)
        rs_prepare([(f"dn{l}", g_dn.reshape(1, N_CHIPS, fq, d))])
        g_up, landed = _mm(du4, hb[None], mode="tn", name=f"ffn{l}_gup", out_dtype=BF16, n_q=4, tm=n2, tn=_tile(d, 1024),
                           tk=_tile(s, 512), qa=lambda q, qr: _perm(q), qb=_q0, comm=rs_exchange([f"dn{l}"]))
        rs_landed([f"dn{l}"], landed)
        rs_prepare([(f"up{l}", g_up.reshape(1, N_CHIPS, n2, d))])
        dh, landed = _mm(du4, up8(), mode="nn", name=f"ffn{l}_dh", out_dtype=F32, n_q=1, n_qr=4, tm=_tile(s, 1024),
                         tn=_tile(d, 1024), tk=n2, qa=_qr, qb=lambda q, qr: 4 * l + _perm(qr), qo=lambda q: 0,
                         comm=rs_exchange([f"up{l}"]))
        rs_landed([f"up{l}"], landed)
        return dh[0]

    dz4, dz4b, dg_ln2_1, db_ln2_1 = _ln_bwd(None, dh4, xh4, rs4, ln2_g[1:2], name="dln2_1")
    dh3 = ffn_bwd(1, dz4b, u1, act1, h3b, None)
    dz3, dz3b, dg_ln1_1, db_ln1_1 = _ln_bwd(dz4, dh3, xh3, rs3, ln1_g[1:2], name="dln1_1")
    g_wo = _mm_wgrad(ob[None], dz3b[None], name="attn_gwo")
    do = _mm_dd(dz3b[None], w6(), IW_O, mode="nt", name="attn_do", out_dtype=BF16)[0]
    delta = _attn_delta(do, o, name="attn_delta")
    dqkv = _attn_bwd(qkv, do, lse, delta, name="attn_bwd")
    g_wq = _mm_wgrad(h2b[None], dqkv, name="attn_gwq", qb=lambda q, qr: 0)
    g_wk = _mm_wgrad(h2b[None], dqkv, name="attn_gwk", qb=lambda q, qr: 1)
    g_wv = _mm_wgrad(h2b[None], dqkv, name="attn_gwv", qb=lambda q, qr: 2)
    rs_prepare([(nm, g_.reshape(1, N_CHIPS, rq, d)) for nm, g_ in
                (("wo", g_wo), ("wq", g_wq), ("wk", g_wk), ("wv", g_wv))])
    dh2, landed = _mm_dd(dqkv, w6(), IW_Q, mode="nt", name="attn_dh", out_dtype=F32, n_qr=3, qa=_qr,
                         comm=rs_exchange(["wo", "wq", "wk"]))
    rs_landed(["wo", "wq", "wk"], landed)
    dz2, dz2b, dg_ln2_0, db_ln2_0 = _ln_bwd(dz3, dh2[0], xh2, rs2, ln2_g[0:1], name="dln2_0")
    dh1 = ffn_bwd(0, dz2b, u0, act0, h1b, ["wv"])
    dz1, dz1b, dg_ln1_0, db_ln1_0 = _ln_bwd(dz2, dh1, xh1, rs1, ln1_g[0:1], name="dln1_0")
    g_wout = _mm_wgrad(mixed[None], dz1b[None], name="pool_gwout")
    rs_prepare([("wout", g_wout.reshape(1, N_CHIPS, rq, d))])
    dmixed, landed = _mm_dd(dz1b[None], w6(), IW_OUT, mode="nt", name="pool_dmixed", out_dtype=F32,
                            comm=rs_exchange(["wout"]))
    rs_landed(["wout"], landed)
    dmg, d_scale = _grp_bwd_pre(dmixed[0], mg, scale_full, name="pool_dscale")
    g_wgrp = _grp_mm(pooled, dmg, mode="tn", name="pool_gwgrp", out_dtype=BF16, tm=1024)
    dpooled = _grp_mm(dmg, None, mode="nt", name="pool_dpooled", out_dtype=F32, tm=1024, w=bufs[GRP].reshape(g_n, cg, cg))
    dp = _pool(dpooled, backward=True, name="pool_bwd")
    g_win = _mm_wgrad(xs[None], dp[None], name="pool_gwin")
    last = ["win", "wgrp"]
    rs_prepare([("win", g_win.reshape(1, N_CHIPS, rq, d)), ("wgrp", g_wgrp.reshape(g_n, N_CHIPS, cg // N_CHIPS, cg))])
    dx_mm, landed = _mm_dd(dp[None], w6(), IW_IN, mode="nt", name="pool_dx", out_dtype=F32, comm=rs_exchange(last))
    rs_landed(last, landed)
    (grad_x,) = _ew(lambda a, b: (ALPHA * a + b,), [dz1, dx_mm[0]], [F32], name="grad_x")

    rs_names = ["win", "wgrp", "wout", "wq", "wo", "wk", "wv", "up0", "up1", "dn0", "dn1"]
    finished = {}
    for nm in rs_names:
        y4 = rs_lands[nm]
        y3 = y4.reshape(N_CHIPS, -1, y4.shape[-1])
        (tot,) = _ew(lambda a0, a1, a2, a3: (((a0.astype(F32) + a1.astype(F32)) + a2.astype(F32)) + a3.astype(F32),),
                     [(y3, 0), (y3, 1), (y3, 2), (y3, 3)], [F32], name=f"rs_sum_{nm}")
        finished[nm] = tot.reshape(y4.shape[1:])
    others = dict(zip(rs_names, _rs_finish([finished[nm] for nm in rs_names], name="rs_finish")))
    results = {}
    for nm, key, w, m, v in (("pool_w_in", "win", pool_w_in, m_pool_w_in, v_pool_w_in),
                             ("pool_w_grp", "wgrp", pool_w_grp, m_pool_w_grp, v_pool_w_grp),
                             ("pool_w_out", "wout", pool_w_out, m_pool_w_out, v_pool_w_out),
                             ("attn_w_q", "wq", attn_w_q, m_attn_w_q, v_attn_w_q),
                             ("attn_w_o", "wo", attn_w_o, m_attn_w_o, v_attn_w_o),
                             ("shared_w_k", "wk", shared_w_k, m_shared_w_k, v_shared_w_k),
                             ("shared_w_v", "wv", shared_w_v, m_shared_w_v, v_shared_w_v)):
        results[nm] = _adamw_halves(finished[key], others[key], core, w, m, v, name=f"adamw_{nm}")
    for nm, key, w, m, v in (("ffn_w_up", "up", up_t, m_up_t, v_up_t),
                             ("ffn_w_down", "dn", ffn_w_down, m_ffn_w_down, v_ffn_w_down)):
        res = None
        for l in (1, 0):
            res = _adamw_halves(finished[f"{key}{l}"], others[f"{key}{l}"], core, w, m, v, name=f"adamw_{nm}{l}",
                                lead=l, prev=res)
        results[nm] = res
    results["ffn_w_up"] = [jnp.swapaxes(t, 1, 2) for t in results["ffn_w_up"]]

    ln_grads = [jnp.concatenate([a, b], axis=0) for a, b in
                ((dg_ln1_0, dg_ln1_1), (db_ln1_0, db_ln1_1), (dg_ln2_0, dg_ln2_1), (db_ln2_0, db_ln2_1))]
    small_shapes = [(2, N_CHIPS, 3, n2), (2, N_CHIPS, n2)] + [(2, d)] * 4 + [(1, d)]
    vec = _pack([jnp.stack(d_conv_w), jnp.stack(d_conv_b)] + ln_grads + [d_scale], mult=8)
    tot = _unpack(_all_reduce_small(vec, name="allreduce_small"), small_shapes)
    g_cw = lax.dynamic_index_in_dim(tot[0], q_me, axis=1, keepdims=False)
    g_cb = tot[1].reshape(2, N_CHIPS * n2)
    g_scale = lax.dynamic_slice_in_dim(tot[6], q_me * rq, rq, axis=1)
    small_names = ["ffn_conv_w", "ffn_conv_b", "ln1_g", "ln1_b", "ln2_g", "ln2_b", "pool_scale"]
    small_g = [g_cw, g_cb, tot[2], tot[3], tot[4], tot[5], g_scale]
    small_w = [ffn_conv_w, ffn_conv_b, ln1_g, ln1_b, ln2_g, ln2_b, pool_scale]
    small_m = [m_ffn_conv_w, m_ffn_conv_b, m_ln1_g, m_ln1_b, m_ln2_g, m_ln2_b, m_pool_scale]
    small_v = [v_ffn_conv_w, v_ffn_conv_b, v_ln1_g, v_ln1_b, v_ln2_g, v_ln2_b, v_pool_scale]
    packed = _ew(_adamw_math, [_pack(small_g, 8), _pack(small_w, 8), _pack(small_m, 8), _pack(small_v, 8)], [F32] * 4,
                 name="adamw_small")
    shapes = [w.shape for w in small_w]
    unpacked = [_unpack(pk, shapes) for pk in packed]
    for i, nm in enumerate(small_names):
        results[nm] = [unpacked[k][i] for k in range(4)]

    loss = lax.psum(loss_local[0, 0], ("x", "y", "c"))
    order = ["pool_w_in", "pool_w_grp", "pool_scale", "pool_w_out", "attn_w_q", "attn_w_o", "shared_w_k", "shared_w_v",
             "ffn_w_up", "ffn_conv_w", "ffn_conv_b", "ffn_w_down", "ln1_g", "ln1_b", "ln2_g", "ln2_b"]
    outs = [loss, grad_x[None]]
    for k in range(4):
        outs += [results[nm][k] for nm in order]
    return tuple(outs)
```

```python
import functools
import math

import jax
import jax.numpy as jnp
from jax import lax
from jax.experimental import pallas as pl
from jax.experimental.pallas import tpu as pltpu

F32 = jnp.float32
BF16 = jnp.bfloat16

LANES = 128
HEAD_DIM = 128
ATTN_BLOCK = 128
DILATIONS = (1, 4, 16)
ATTN_UNROLL = 8
POOL_WINDOWS = (2, 4, 8, 16)
POOL_HALO = 16
CONV_HALO = 16
DEPTH = 2
ALPHA = (2.0 * DEPTH) ** 0.25
LN_EPS = 1e-5
NEG_INF = -1e30
ADAM_LR = 0.001
ADAM_B1 = 0.9
ADAM_B2 = 0.999
ADAM_EPS = 1e-08
ADAM_WD = 0.01
ADAM_STEP = 10
N_CHIPS = 4
VMEM_LIMIT = 56 * 1024 * 1024
ANY = pl.BlockSpec(memory_space=pl.ANY)
MESH = pl.DeviceIdType.MESH

IW_IN, IW_OUT, IW_Q, IW_K, IW_V, IW_O = range(6)


def _cparams(n_grid):
    return pltpu.CompilerParams(dimension_semantics=("arbitrary",) * n_grid, vmem_limit_bytes=VMEM_LIMIT)


def _tile(dim, pref, align=LANES):
    if dim <= pref:
        return dim
    t = (pref // align) * align
    while t >= align:
        if dim % t == 0:
            return t
        t -= align
    return dim


def _tile2(rows, cols, budget, row_align):
    best = None
    for tc in [cols] + [c for c in range(LANES, cols, LANES) if cols % c == 0]:
        for tr in range(row_align, rows + 1, row_align):
            if rows % tr == 0 and tr * tc <= budget:
                if best is None or (tr * tc, tc) > (best[0] * best[1], best[1]):
                    best = (tr, tc)
    return best if best is not None else (rows, cols)


def _perm(q):
    return (q % 2) * 2 + q // 2


_DIMS = {"nn": (((1,), (0,)), ((), ())), "nt": (((1,), (1,)), ((), ())), "tn": (((0,), (0,)), ((), ()))}


class _Comm:
    def __init__(self, operands, n_sems, start, finish):
        self.operands, self.n_sems, self.start, self.finish = list(operands), n_sems, start, finish


def _pallas(body, *, name, grid, in_specs, out_specs, out_shape, args, scratch_shapes=(), comm=None):
    n_in, n_out = len(args), len(out_shape)
    if comm is None:
        return pl.pallas_call(body, name=name, grid=grid, in_specs=list(in_specs), out_specs=list(out_specs),
                              out_shape=list(out_shape), scratch_shapes=list(scratch_shapes),
                              compiler_params=_cparams(len(grid)))(*args)
    k = len(comm.operands)

    def carried(*refs):
        ins, outs = refs[:n_in], refs[n_in + k:n_in + k + n_out]
        operands = refs[n_in + k + n_out:n_in + 2 * k + n_out]
        scratch, (send_sems, recv_sems) = refs[n_in + 2 * k + n_out:-2], refs[-2:]
        ids = [pl.program_id(ax) for ax in range(len(grid))]
        first = functools.reduce(jnp.logical_and, [i == 0 for i in ids])
        last = functools.reduce(jnp.logical_and, [i == g - 1 for i, g in zip(ids, grid)])

        @pl.when(first)
        def _():
            comm.start(operands, send_sems, recv_sems)

        body(*ins, *outs, *scratch)

        @pl.when(last)
        def _():
            comm.finish(operands, send_sems, recv_sems)

    res = pl.pallas_call(
        carried, name=name, grid=grid, in_specs=list(in_specs) + [ANY] * k, out_specs=list(out_specs) + [ANY] * k,
        out_shape=list(out_shape) + [jax.ShapeDtypeStruct(t.shape, t.dtype) for t in comm.operands],
        scratch_shapes=list(scratch_shapes) + [pltpu.SemaphoreType.DMA((comm.n_sems,))] * 2,
        input_output_aliases={n_in + i: n_out + i for i in range(k)},
        compiler_params=_cparams(len(grid)),
    )(*args, *comm.operands)
    return res[:n_out], res[n_out:]


def _mm(a, b, *, mode, name, out_dtype, n_q, tm, tn, tk, qa, qb, qo=lambda q: q, n_qr=1, out_q=None, comm=None):
    if mode == "nn":
        m, kdim, n = a.shape[1], a.shape[2], b.shape[2]
    elif mode == "nt":
        m, kdim, n = a.shape[1], a.shape[2], b.shape[1]
    else:
        kdim, m, n = a.shape[1], a.shape[2], b.shape[2]
    assert m % tm == 0 and n % tn == 0 and kdim % tk == 0, (name, m, n, kdim, tm, tn, tk)
    kr_n = kdim // tk
    nr = n_qr * kr_n
    out_q = n_q if out_q is None else out_q

    def split(r):
        return (r // kr_n, r % kr_n) if n_qr > 1 else (0, r)

    if mode == "tn":
        a_spec = pl.BlockSpec((None, tk, tm), lambda q, i, j, r: (qa(q, split(r)[0]), split(r)[1], i))
    else:
        a_spec = pl.BlockSpec((None, tm, tk), lambda q, i, j, r: (qa(q, split(r)[0]), i, split(r)[1]))
    if mode == "nt":
        b_spec = pl.BlockSpec((None, tn, tk), lambda q, i, j, r: (qb(q, split(r)[0]), j, split(r)[1]))
    else:
        b_spec = pl.BlockSpec((None, tk, tn), lambda q, i, j, r: (qb(q, split(r)[0]), split(r)[1], j))
    o_spec = pl.BlockSpec((None, tm, tn), lambda q, i, j, r: (qo(q), i, j))
    dims = _DIMS[mode]

    n_comm = len(comm.operands) if comm is not None else 0
    grid = (n_q, m // tm, n // tn, nr)

    def body(*refs):
        a_ref, b_ref = refs[0], refs[1]
        o_ref = refs[2 + n_comm]
        if comm is not None:
            comm_refs = refs[3 + n_comm:3 + 2 * n_comm]
            send_sems, recv_sems = refs[-2:]
            ids = [pl.program_id(ax) for ax in range(4)]
            first = functools.reduce(jnp.logical_and, [i == 0 for i in ids])
            last = functools.reduce(jnp.logical_and, [i == g - 1 for i, g in zip(ids, grid)])

            @pl.when(first)
            def _():
                comm.start(comm_refs, send_sems, recv_sems)
        lhs, rhs = a_ref[...], b_ref[...]
        if lhs.dtype != BF16:
            lhs = lhs.astype(BF16)
        if rhs.dtype != BF16:
            rhs = rhs.astype(BF16)
        part = lax.dot_general(lhs, rhs, dims, preferred_element_type=F32)
        if nr == 1:
            o_ref[...] = part.astype(o_ref.dtype)
        else:
            acc_ref = refs[3 + 2 * n_comm]
            r = pl.program_id(3)

            @pl.when(r == 0)
            def _():
                acc_ref[...] = part

            @pl.when(r > 0)
            def _():
                acc_ref[...] += part

            @pl.when(r == nr - 1)
            def _():
                o_ref[...] = acc_ref[...].astype(o_ref.dtype)
        if comm is not None:
            @pl.when(last)
            def _():
                comm.finish(comm_refs, send_sems, recv_sems)

    scratch = [pltpu.VMEM((tm, tn), F32)] if nr > 1 else []
    out_shape = [jax.ShapeDtypeStruct((out_q, m, n), out_dtype)]
    args = [a, b]
    if comm is not None:
        args += comm.operands
        out_shape += [jax.ShapeDtypeStruct(t.shape, t.dtype) for t in comm.operands]
        scratch += [pltpu.SemaphoreType.DMA((comm.n_sems,)), pltpu.SemaphoreType.DMA((comm.n_sems,))]
    outs = pl.pallas_call(
        body, name=name, grid=grid,
        in_specs=[a_spec, b_spec] + [ANY] * n_comm, out_specs=[o_spec] + [ANY] * n_comm,
        out_shape=out_shape, scratch_shapes=scratch,
        input_output_aliases={2 + i: 1 + i for i in range(n_comm)},
        compiler_params=_cparams(4),
    )(*args)
    return (outs[0], outs[1:]) if comm is not None else outs[0]


def _q0(q, qr):
    return 0


def _qq(q, qr):
    return q


def _qr(q, qr):
    return qr


def _mm_dd(a3, w6, widx, *, mode, name, out_dtype, n_q=1, n_qr=1, qa=_q0, comm=None):
    d = w6.shape[1]
    f32_in = a3.dtype != BF16
    tm = _tile(a3.shape[1] if mode != "tn" else a3.shape[2], 512 if f32_in else 1024)
    if n_qr > 1:
        qb = lambda q, qr: widx + qr
    elif n_q > 1:
        qb = lambda q, qr: widx + q
    else:
        qb = lambda q, qr: widx
    return _mm(a3, w6, mode=mode, name=name, out_dtype=out_dtype, n_q=n_q, n_qr=n_qr,
               tm=tm, tn=_tile(d, 1024), tk=d, qa=qa, qb=qb, comm=comm)


def _mm_wgrad(a3, b3, *, name, n_q=1, qa=_q0, qb=_q0):
    s, d = a3.shape[1], a3.shape[2]
    f32_in = a3.dtype != BF16 or b3.dtype != BF16
    return _mm(a3, b3, mode="tn", name=name, out_dtype=BF16, n_q=n_q, tm=_tile(d, 1024), tn=b3.shape[2],
               tk=_tile(s, 512 if f32_in else 1024), qa=qa, qb=qb)


def _grp_mm(a, b, *, mode, name, out_dtype, tm, w=None):
    s = a.shape[0]
    if mode == "tn":
        cg = b.shape[1] // len(POOL_WINDOWS)
        g_n = len(POOL_WINDOWS)
        ts = _tile(s, tm)
        nr = s // ts

        def body(a_ref, b_ref, o_ref, acc_ref):
            r = pl.program_id(1)
            part = lax.dot_general(a_ref[...], b_ref[...], _DIMS["tn"], preferred_element_type=F32)

            @pl.when(r == 0)
            def _():
                acc_ref[...] = part

            @pl.when(r > 0)
            def _():
                acc_ref[...] += part

            @pl.when(r == nr - 1)
            def _():
                o_ref[...] = acc_ref[...].astype(o_ref.dtype)

        return pl.pallas_call(
            body, name=name, grid=(g_n, nr),
            in_specs=[pl.BlockSpec((ts, cg), lambda g, r: (r, g)), pl.BlockSpec((ts, cg), lambda g, r: (r, g))],
            out_specs=pl.BlockSpec((None, cg, cg), lambda g, r: (g, 0, 0)),
            out_shape=jax.ShapeDtypeStruct((g_n, cg, cg), out_dtype),
            scratch_shapes=[pltpu.VMEM((cg, cg), F32)],
            compiler_params=_cparams(2),
        )(a, b)
    g_n, cg = w.shape[0], w.shape[1]
    ts = _tile(s, tm)
    dims = _DIMS[mode]

    def body(a_ref, w_ref, o_ref):
        o_ref[...] = lax.dot_general(a_ref[...], w_ref[...], dims, preferred_element_type=F32).astype(o_ref.dtype)

    return pl.pallas_call(
        body, name=name, grid=(g_n, s // ts),
        in_specs=[pl.BlockSpec((ts, cg), lambda g, i: (i, g)), pl.BlockSpec((None, cg, cg), lambda g, i: (g, 0, 0))],
        out_specs=pl.BlockSpec((ts, cg), lambda g, i: (i, g)),
        out_shape=jax.ShapeDtypeStruct((s, g_n * cg), out_dtype),
        compiler_params=_cparams(2),
    )(a, w)


def _causal_ext(load, r0, rows, halo):
    cur = load(r0, rows)
    prev = load(pl.multiple_of(jnp.maximum(r0 - halo, 0), halo), halo)
    prev = jnp.where(r0 > 0, prev, jnp.zeros_like(prev))
    return jnp.concatenate([prev, cur], axis=0)


def _anti_ext(load, r0, rows, halo, s):
    cur = load(r0, rows)
    nxt = load(pl.multiple_of(jnp.minimum(r0 + rows, s - halo), halo), halo)
    nxt = jnp.where(r0 + rows < s, nxt, jnp.zeros_like(nxt))
    return jnp.concatenate([cur, nxt], axis=0)


def _down(ext, k):
    return pltpu.roll(ext, k, axis=0)


def _up(ext, k):
    return pltpu.roll(ext, ext.shape[0] - k, axis=0)


def _fold8(x):
    return jnp.sum(x.reshape(x.shape[0] // 8, 8, x.shape[1]), axis=0)


def _sigmoid(x):
    return 0.5 * jnp.tanh(0.5 * x) + 0.5


def _pool(p, *, backward, name, rows=64, comm=None):
    s, d = p.shape
    strips_per_group = (d // len(POOL_WINDOWS)) // LANES
    assert strips_per_group * LANES * len(POOL_WINDOWS) == d and s % rows == 0

    def body(p_ref, o_ref):
        g = pl.program_id(0) // strips_per_group
        win = jnp.left_shift(2, g).astype(F32)

        def load(r0, n):
            return p_ref[pl.ds(r0, n), :]

        def pick(levels):
            return jnp.where(g == 0, levels[0], jnp.where(g == 1, levels[1], jnp.where(g == 2, levels[2], levels[3])))

        def chunk(c, carry):
            r0 = pl.multiple_of(c * rows, rows)
            if not backward:
                ext = _causal_ext(load, r0, rows, POOL_HALO)
                levels, acc = [], ext
                for k in (1, 2, 4, 8):
                    acc = acc + _down(acc, k)
                    levels.append(acc)
                t = (r0 + lax.broadcasted_iota(jnp.int32, (rows, LANES), 0)).astype(F32)
                cnt = jnp.minimum(t + 1.0, win)
                out = pick(levels)[POOL_HALO:] / cnt - ext[POOL_HALO:]
            else:
                ext = _anti_ext(load, r0, rows, POOL_HALO, s)
                t = (r0 + lax.broadcasted_iota(jnp.int32, (rows + POOL_HALO, LANES), 0)).astype(F32)
                e = ext / jnp.minimum(t + 1.0, win)
                levels, acc = [], e
                for k in (1, 2, 4, 8):
                    acc = acc + _up(acc, k)
                    levels.append(acc)
                out = pick(levels)[:rows] - ext[:rows]
            o_ref[pl.ds(r0, rows), :] = out.astype(o_ref.dtype)
            return carry

        lax.fori_loop(0, s // rows, chunk, 0)

    res = _pallas(
        body, name=name, grid=(d // LANES,),
        in_specs=[pl.BlockSpec((s, LANES), lambda j: (0, j))],
        out_specs=[pl.BlockSpec((s, LANES), lambda j: (0, j))],
        out_shape=[jax.ShapeDtypeStruct((s, d), BF16)], args=[p], comm=comm)
    return res[0] if comm is None else (res[0][0], res[1])


def _grp_fwd(pooled, w_grp, scale, *, name, comm=None):
    s, d = pooled.shape
    g_n, cg = w_grp.shape[0], w_grp.shape[1]
    ts = _tile(s, 1024)

    def body(a_ref, w_ref, sc_ref, mg_ref, mx_ref):
        mg = jnp.dot(a_ref[...], w_ref[...], preferred_element_type=F32)
        mg_ref[...] = mg.astype(BF16)
        mx_ref[...] = (mg * sc_ref[...]).astype(BF16)

    blk = pl.BlockSpec((ts, cg), lambda g, i: (i, g))
    return _pallas(
        body, name=name, grid=(g_n, s // ts),
        in_specs=[blk, pl.BlockSpec((None, cg, cg), lambda g, i: (g, 0, 0)), pl.BlockSpec((1, cg), lambda g, i: (0, g))],
        out_specs=[blk, blk],
        out_shape=[jax.ShapeDtypeStruct((s, d), BF16)] * 2, args=[pooled, w_grp, scale], comm=comm)


def _grp_bwd_pre(dmixed, mg, scale, *, name):
    s, d = dmixed.shape
    ts = _tile(s, 256, 16)

    def body(dm_ref, mg_ref, sc_ref, dmg_ref, dsc_ref):
        dm = dm_ref[...]
        dmg_ref[...] = (dm * sc_ref[...]).astype(BF16)
        part = jnp.sum(dm * mg_ref[...].astype(F32), axis=0, keepdims=True)

        @pl.when(pl.program_id(0) == 0)
        def _():
            dsc_ref[...] = part

        @pl.when(pl.program_id(0) > 0)
        def _():
            dsc_ref[...] += part

    blk = pl.BlockSpec((ts, d), lambda i: (i, 0))
    vec = pl.BlockSpec((1, d), lambda i: (0, 0))
    return pl.pallas_call(
        body, name=name, grid=(s // ts,),
        in_specs=[blk, blk, vec], out_specs=[blk, vec],
        out_shape=[jax.ShapeDtypeStruct((s, d), BF16), jax.ShapeDtypeStruct((1, d), F32)],
        compiler_params=_cparams(1),
    )(dmixed, mg, scale)


def _ln_fwd(res, mm, g, b, *, name, comm=None):
    s, d = res.shape
    ts = _tile(s, 128, 16)

    def body(res_ref, mm_ref, g_ref, b_ref, h_ref, hb_ref, xh_ref, rs_ref):
        z = ALPHA * res_ref[...] + mm_ref[...]
        mu = jnp.mean(z, axis=-1, keepdims=True)
        zc = z - mu
        var = jnp.mean(zc * zc, axis=-1, keepdims=True)
        rstd = lax.rsqrt(var + LN_EPS)
        xhat = zc * rstd
        h = xhat * g_ref[...] + b_ref[...]
        h_ref[...] = h
        hb_ref[...] = h.astype(BF16)
        xh_ref[...] = xhat
        rs_ref[...] = rstd

    blk = pl.BlockSpec((ts, d), lambda i: (i, 0))
    vec = pl.BlockSpec((1, d), lambda i: (0, 0))
    return _pallas(
        body, name=name, grid=(s // ts,),
        in_specs=[blk, blk, vec, vec],
        out_specs=[blk, blk, blk, pl.BlockSpec((ts, 1), lambda i: (i, 0))],
        out_shape=[jax.ShapeDtypeStruct((s, d), F32), jax.ShapeDtypeStruct((s, d), BF16),
                   jax.ShapeDtypeStruct((s, d), F32), jax.ShapeDtypeStruct((s, 1), F32)],
        args=[res, mm, g, b], comm=comm)


def _ln_bwd(dres, dmm, xhat, rstd, g, *, name):
    s, d = dmm.shape
    ts = _tile(s, 128, 16)
    has_res = dres is not None

    def body(*refs):
        if has_res:
            dres_ref, dmm_ref, xh_ref, rs_ref, g_ref, dz_ref, dzb_ref, dg_ref, db_ref = refs
            dh = ALPHA * dres_ref[...] + dmm_ref[...]
        else:
            dmm_ref, xh_ref, rs_ref, g_ref, dz_ref, dzb_ref, dg_ref, db_ref = refs
            dh = dmm_ref[...]
        xhat_ = xh_ref[...]
        dxh = dh * g_ref[...]
        c1 = jnp.mean(dxh, axis=-1, keepdims=True)
        c2 = jnp.mean(dxh * xhat_, axis=-1, keepdims=True)
        dz = rs_ref[...] * (dxh - c1 - xhat_ * c2)
        dz_ref[...] = dz
        dzb_ref[...] = dz.astype(BF16)
        dg_part = jnp.sum(dh * xhat_, axis=0, keepdims=True)
        db_part = jnp.sum(dh, axis=0, keepdims=True)

        @pl.when(pl.program_id(0) == 0)
        def _():
            dg_ref[...] = dg_part
            db_ref[...] = db_part

        @pl.when(pl.program_id(0) > 0)
        def _():
            dg_ref[...] += dg_part
            db_ref[...] += db_part

    blk = pl.BlockSpec((ts, d), lambda i: (i, 0))
    vec = pl.BlockSpec((1, d), lambda i: (0, 0))
    col = pl.BlockSpec((ts, 1), lambda i: (i, 0))
    ins = ([dres] if has_res else []) + [dmm, xhat, rstd, g]
    in_specs = ([blk] if has_res else []) + [blk, blk, col, vec]
    return pl.pallas_call(
        body, name=name, grid=(s // ts,),
        in_specs=in_specs, out_specs=[blk, blk, vec, vec],
        out_shape=[jax.ShapeDtypeStruct((s, d), F32), jax.ShapeDtypeStruct((s, d), BF16),
                   jax.ShapeDtypeStruct((1, d), F32), jax.ShapeDtypeStruct((1, d), F32)],
        compiler_params=_cparams(1),
    )(*ins)


def _loss_head(h, tgt, *, name):
    s, d = h.shape
    ts = _tile(s, 256, 16)

    def body(h_ref, t_ref, dh_ref, loss_ref):
        err = h_ref[...] - t_ref[...]
        dh_ref[...] = err * (1.0 / d)
        part = 0.5 * jnp.sum(jnp.mean(err * err, axis=-1, keepdims=True), axis=0, keepdims=True)

        @pl.when(pl.program_id(0) == 0)
        def _():
            loss_ref[...] = part

        @pl.when(pl.program_id(0) > 0)
        def _():
            loss_ref[...] += part

    blk = pl.BlockSpec((ts, d), lambda i: (i, 0))
    return pl.pallas_call(
        body, name=name, grid=(s // ts,),
        in_specs=[blk, blk], out_specs=[blk, pl.BlockSpec((1, 1), lambda i: (0, 0))],
        out_shape=[jax.ShapeDtypeStruct((s, d), F32), jax.ShapeDtypeStruct((1, 1), F32)],
        compiler_params=_cparams(1),
    )(h, tgt)


def _conv(ext, w, bias):
    c = bias + _down(ext, 2) * w[0:1] + _down(ext, 1) * w[1:2] + ext * w[2:3]
    return c[CONV_HALO:]


def _act_specs(s, n2):
    n_strips = pl.cdiv(n2, LANES)
    u_spec = pl.BlockSpec((None, 2, s, LANES), lambda hh, j: (hh, 0, 0, j))
    cwg = pl.BlockSpec((None, 3, LANES), lambda hh, j: (hh, 0, j))
    cwv = pl.BlockSpec((None, 3, LANES), lambda hh, j: (hh + 2, 0, j))
    cbg = pl.BlockSpec((None, 1, LANES), lambda hh, j: (hh, 0, j))
    cbv = pl.BlockSpec((None, 1, LANES), lambda hh, j: (hh + 2, 0, j))
    return n_strips, u_spec, cwg, cwv, cbg, cbv


def _act_fwd(u, cw, cb, *, name, rows=64):
    _, _, s, n2 = u.shape
    n_strips, u_spec, cwg, cwv, cbg, cbv = _act_specs(s, n2)

    def body(u_ref, wg_ref, wv_ref, bg_ref, bv_ref, a_ref):
        wg, wv, bg, bv = wg_ref[...], wv_ref[...], bg_ref[...], bv_ref[...]

        def chunk(c, carry):
            r0 = pl.multiple_of(c * rows, rows)
            cg = _conv(_causal_ext(lambda r, n: u_ref[0, pl.ds(r, n), :].astype(F32), r0, rows, CONV_HALO), wg, bg)
            cv = _conv(_causal_ext(lambda r, n: u_ref[1, pl.ds(r, n), :].astype(F32), r0, rows, CONV_HALO), wv, bv)
            a_ref[pl.ds(r0, rows), :] = (cg * _sigmoid(cg) * cv).astype(BF16)
            return carry

        lax.fori_loop(0, s // rows, chunk, 0)

    return pl.pallas_call(
        body, name=name, grid=(2, n_strips),
        in_specs=[u_spec, cwg, cwv, cbg, cbv],
        out_specs=pl.BlockSpec((None, s, LANES), lambda hh, j: (hh, 0, j)),
        out_shape=jax.ShapeDtypeStruct((2, s, n2), BF16),
        compiler_params=_cparams(2),
    )(u, cw, cw, cb, cb)


def _act_bwd(u, da, cw, cb, *, name, rows=64):
    _, _, s, n2 = u.shape
    n_strips, u_spec, cwg, cwv, cbg, cbv = _act_specs(s, n2)
    n_chunks = s // rows

    def body(u_ref, da_ref, wg_ref, wv_ref, bg_ref, bv_ref, du_ref, dwg_ref, dwv_ref, dbg_ref, dbv_ref, dg_s, dv_s):
        wg, wv, bg, bv = wg_ref[...], wv_ref[...], bg_ref[...], bv_ref[...]

        def first(c, sums):
            r0 = pl.multiple_of(c * rows, rows)
            eg = _causal_ext(lambda r, n: u_ref[0, pl.ds(r, n), :].astype(F32), r0, rows, CONV_HALO)
            ev = _causal_ext(lambda r, n: u_ref[1, pl.ds(r, n), :].astype(F32), r0, rows, CONV_HALO)
            cg, cv = _conv(eg, wg, bg), _conv(ev, wv, bv)
            sg = _sigmoid(cg)
            dact = da_ref[pl.ds(r0, rows), :]
            dval = dact * (cg * sg)
            dgate = dact * cv * (sg * (1.0 + cg * (1.0 - sg)))
            dg_s[pl.ds(r0, rows), :] = dgate
            dv_s[pl.ds(r0, rows), :] = dval
            new = []
            for dc, ext in ((dgate, eg), (dval, ev)):
                new += [_fold8(dc * _down(ext, 2)[CONV_HALO:]), _fold8(dc * _down(ext, 1)[CONV_HALO:]),
                        _fold8(dc * ext[CONV_HALO:]), _fold8(dc)]
            return tuple(acc + x for acc, x in zip(sums, new))

        zero = jnp.zeros((8, LANES), F32)
        sums = lax.fori_loop(0, n_chunks, first, (zero,) * 8)
        red = [jnp.sum(x, axis=0, keepdims=True) for x in sums]
        dwg_ref[...] = jnp.concatenate(red[0:3], axis=0)
        dbg_ref[...] = red[3]
        dwv_ref[...] = jnp.concatenate(red[4:7], axis=0)
        dbv_ref[...] = red[7]

        def second(c, carry):
            r0 = pl.multiple_of(c * rows, rows)
            for gv, (src, w) in enumerate(((dg_s, wg), (dv_s, wv))):
                ext = _anti_ext(lambda r, n: src[pl.ds(r, n), :], r0, rows, CONV_HALO, s)
                du = ext * w[2:3] + _up(ext, 1) * w[1:2] + _up(ext, 2) * w[0:1]
                du_ref[gv, pl.ds(r0, rows), :] = du[:rows].astype(BF16)
            return carry

        lax.fori_loop(0, n_chunks, second, 0)

    w_out = pl.BlockSpec((None, 3, LANES), lambda hh, j: (hh, 0, j))
    b_out = pl.BlockSpec((None, 1, LANES), lambda hh, j: (hh, 0, j))
    return pl.pallas_call(
        body, name=name, grid=(2, n_strips),
        in_specs=[u_spec, pl.BlockSpec((None, s, LANES), lambda hh, j: (hh, 0, j)), cwg, cwv, cbg, cbv],
        out_specs=[u_spec, w_out, w_out, b_out, b_out],
        out_shape=[jax.ShapeDtypeStruct((2, 2, s, n2), BF16),
                   jax.ShapeDtypeStruct((2, 3, n2), F32), jax.ShapeDtypeStruct((2, 3, n2), F32),
                   jax.ShapeDtypeStruct((2, 1, n2), F32), jax.ShapeDtypeStruct((2, 1, n2), F32)],
        scratch_shapes=[pltpu.VMEM((s, LANES), F32), pltpu.VMEM((s, LANES), F32)],
        compiler_params=_cparams(2),
    )(u, da, cw, cw, cb, cb)


def _dot_nt(a, b):
    return lax.dot_general(a, b, _DIMS["nt"], preferred_element_type=F32)


def _dot_tn(a, b):
    return lax.dot_general(a, b, _DIMS["tn"], preferred_element_type=F32)


def _band_mask(b):
    qi = lax.broadcasted_iota(jnp.int32, (ATTN_BLOCK, 2 * ATTN_BLOCK), 0)
    kj = lax.broadcasted_iota(jnp.int32, (ATTN_BLOCK, 2 * ATTN_BLOCK), 1)
    band = jnp.logical_and(kj >= qi, kj <= qi + ATTN_BLOCK)
    return jnp.logical_and(band, jnp.logical_or(b > 0, kj >= ATTN_BLOCK))


def _to_residues(nat, rm, d, seq, pad):
    seg = seq + pad
    for r in range(d):
        if pad:
            rm[pl.ds(r * seg, pad), :] = jnp.zeros((pad, LANES), rm.dtype)
        rows = nat[pl.ds(r, seq, stride=d), :] if d > 1 else nat[...]
        rm[pl.ds(r * seg + pad, seq), :] = rows.astype(rm.dtype)


def _rows_loop(s, rows, fn):
    def step(c, carry):
        fn(pl.ds(pl.multiple_of(c * rows, rows), rows))
        return carry

    lax.fori_loop(0, s // rows, step, 0)


def _attn_fwd(qkv, *, name, comm=None):
    _, s, dm = qkv.shape
    heads, scale = dm // HEAD_DIM, 1.0 / math.sqrt(HEAD_DIM)
    pad_rows = s + ATTN_BLOCK * max(DILATIONS)

    def body(q_ref, k_ref, v_ref, o_ref, ob_ref, lse_ref, nat, rq, rk, rv, ro, rl, o_tmp, l_tmp, o_acc, m_acc, s_acc):
        for d in DILATIONS:
            seq = s // d
            nb, seg = seq // ATTN_BLOCK, seq + ATTN_BLOCK
            for src, dst, pad in ((q_ref, rq, 0), (k_ref, rk, ATTN_BLOCK), (v_ref, rv, ATTN_BLOCK)):
                if d == 1:
                    _to_residues(src, dst, d, seq, pad)
                else:
                    nat[...] = src[...].astype(F32)
                    _to_residues(nat, dst, d, seq, pad)

            def block(idx, carry):
                r, b = idx // nb, idx % nb
                qrows = pl.ds(pl.multiple_of(r * seq + b * ATTN_BLOCK, ATTN_BLOCK), ATTN_BLOCK)
                krows = pl.ds(pl.multiple_of(r * seg + b * ATTN_BLOCK, ATTN_BLOCK), 2 * ATTN_BLOCK)
                sc = jnp.where(_band_mask(b), _dot_nt(rq[qrows, :], rk[krows, :]) * scale, NEG_INF)
                m = jnp.max(sc, axis=-1, keepdims=True)
                p = jnp.exp(sc - m)
                den = jnp.sum(p, axis=-1, keepdims=True)
                ro[qrows, :] = jnp.dot(p.astype(BF16), rv[krows, :], preferred_element_type=F32) / den
                rl[qrows, :] = jnp.broadcast_to(m + jnp.log(den), (ATTN_BLOCK, LANES))
                return carry

            lax.fori_loop(0, d * nb, block, 0, unroll=ATTN_UNROLL)
            if d == 1:
                def first(rows):
                    o_acc[rows, :] = ro[rows, :]
                    m_acc[rows, :] = rl[rows, :]
                    s_acc[rows, :] = jnp.ones((rows.size, LANES), F32)

                _rows_loop(s, 64, first)
            else:
                for r in range(d):
                    o_tmp[pl.ds(r, seq, stride=d), :] = ro[pl.ds(r * seq, seq), :]
                    l_tmp[pl.ds(r, seq, stride=d), :] = rl[pl.ds(r * seq, seq), :]

                def merge(rows):
                    m_old, l_new = m_acc[rows, :], l_tmp[rows, :]
                    m_new = jnp.maximum(m_old, l_new)
                    w_old, w_new = jnp.exp(m_old - m_new), jnp.exp(l_new - m_new)
                    o_acc[rows, :] = o_acc[rows, :] * w_old + o_tmp[rows, :] * w_new
                    s_acc[rows, :] = s_acc[rows, :] * w_old + w_new
                    m_acc[rows, :] = m_new

                _rows_loop(s, 64, merge)

        def finish(rows):
            tot = s_acc[rows, :]
            o = o_acc[rows, :] / tot
            o_ref[rows, :] = o
            ob_ref[rows, :] = o.astype(BF16)
            lse_ref[rows, :] = m_acc[rows, :] + jnp.log(tot)

        _rows_loop(s, 64, finish)

    head = lambda i: pl.BlockSpec((None, s, HEAD_DIM), lambda h, i=i: (i, 0, h))
    out = pl.BlockSpec((s, HEAD_DIM), lambda h: (0, h))
    nat_f32 = pltpu.VMEM((s, LANES), F32)
    return _pallas(
        body, name=name, grid=(heads,),
        in_specs=[head(0), head(1), head(2)], out_specs=[out, out, out],
        out_shape=[jax.ShapeDtypeStruct((s, dm), F32), jax.ShapeDtypeStruct((s, dm), BF16),
                   jax.ShapeDtypeStruct((s, dm), F32)],
        scratch_shapes=[nat_f32, pltpu.VMEM((s, LANES), BF16), pltpu.VMEM((pad_rows, LANES), BF16),
                        pltpu.VMEM((pad_rows, LANES), BF16), nat_f32, nat_f32, nat_f32, nat_f32, nat_f32, nat_f32, nat_f32],
        args=[qkv, qkv, qkv], comm=comm)


def _attn_delta(do, o, *, name):
    s, dm = o.shape
    heads = dm // HEAD_DIM
    ts = _tile(s, 256, 16)

    def body(do_ref, o_ref, dl_ref):
        for h in range(heads):
            hs = slice(h * HEAD_DIM, (h + 1) * HEAD_DIM)
            row = jnp.sum(do_ref[:, hs].astype(F32) * o_ref[:, hs], axis=-1, keepdims=True)
            dl_ref[:, hs] = jnp.broadcast_to(row, (ts, HEAD_DIM))

    blk = pl.BlockSpec((ts, dm), lambda i: (i, 0))
    return pl.pallas_call(
        body, name=name, grid=(s // ts,),
        in_specs=[blk, blk], out_specs=blk,
        out_shape=jax.ShapeDtypeStruct((s, dm), F32),
        compiler_params=_cparams(1),
    )(do, o)


def _attn_bwd(qkv, do, lse, delta, *, name):
    _, s, dm = qkv.shape
    heads, scale = dm // HEAD_DIM, 1.0 / math.sqrt(HEAD_DIM)
    pad_rows = s + ATTN_BLOCK * max(DILATIONS)

    def body(q_ref, k_ref, v_ref, do_ref, l_ref, dl_ref, out_ref,
             nat, rq, rdo, rk, rv, rl, rdl, rdq, kc, kp, vc, vp, aq, ak, av):
        for d in DILATIONS:
            seq = s // d
            nb, seg = seq // ATTN_BLOCK, seq + ATTN_BLOCK
            for src, dst, pad in ((q_ref, rq, 0), (do_ref, rdo, 0), (k_ref, rk, ATTN_BLOCK), (v_ref, rv, ATTN_BLOCK)):
                if d == 1:
                    _to_residues(src, dst, d, seq, pad)
                else:
                    nat[...] = src[...].astype(F32)
                    _to_residues(nat, dst, d, seq, pad)
            if d == 1:
                lse_rows, dl_rows = l_ref, dl_ref
            else:
                lse_rows, dl_rows = rl, rdl
                _to_residues(l_ref, rl, d, seq, 0)
                _to_residues(dl_ref, rdl, d, seq, 0)
            for r in range(d):
                last = pl.ds(r * seg + nb * ATTN_BLOCK, ATTN_BLOCK)
                kp[last, :] = jnp.zeros((ATTN_BLOCK, LANES), F32)
                vp[last, :] = jnp.zeros((ATTN_BLOCK, LANES), F32)

            def block(idx, carry):
                r, b = idx // nb, idx % nb
                qrows = pl.ds(pl.multiple_of(r * seq + b * ATTN_BLOCK, ATTN_BLOCK), ATTN_BLOCK)
                krow = pl.multiple_of(r * seg + b * ATTN_BLOCK, ATTN_BLOCK)
                krows = pl.ds(krow, 2 * ATTN_BLOCK)
                before, own = pl.ds(krow, ATTN_BLOCK), pl.ds(krow + ATTN_BLOCK, ATTN_BLOCK)
                qb, dob, kw, vw = rq[qrows, :], rdo[qrows, :], rk[krows, :], rv[krows, :]
                lse_b = jnp.concatenate([lse_rows[qrows, :]] * 2, axis=1)
                dl_b = jnp.concatenate([dl_rows[qrows, :]] * 2, axis=1)
                sc = jnp.where(_band_mask(b), _dot_nt(qb, kw) * scale, NEG_INF)
                p = jnp.exp(sc - lse_b)
                ds = (p * (_dot_nt(dob, vw) - dl_b) * scale).astype(BF16)
                rdq[qrows, :] = jnp.dot(ds, kw, preferred_element_type=F32)
                dk, dv = _dot_tn(ds, qb), _dot_tn(p.astype(BF16), dob)
                kp[before, :] = dk[:ATTN_BLOCK]
                kc[own, :] = dk[ATTN_BLOCK:]
                vp[before, :] = dv[:ATTN_BLOCK]
                vc[own, :] = dv[ATTN_BLOCK:]
                return carry

            lax.fori_loop(0, d * nb, block, 0, unroll=ATTN_UNROLL)
            for r in range(d):
                keys = pl.ds(r * seg + ATTN_BLOCK, seq)
                rows = pl.ds(r, seq, stride=d) if d > 1 else pl.ds(0, seq)
                for acc, val in ((aq, rdq[pl.ds(r * seq, seq), :]), (ak, kc[keys, :] + kp[keys, :]),
                                 (av, vc[keys, :] + vp[keys, :])):
                    if d == DILATIONS[0]:
                        acc[rows, :] = val
                    else:
                        acc[rows, :] += val

        def finish(rows):
            for i, acc in enumerate((aq, ak, av)):
                out_ref[i, rows, :] = acc[rows, :].astype(BF16)

        _rows_loop(s, 256, finish)

    head = lambda i: pl.BlockSpec((None, s, HEAD_DIM), lambda h, i=i: (i, 0, h))
    col = pl.BlockSpec((s, HEAD_DIM), lambda h: (0, h))
    f32 = lambda rows: pltpu.VMEM((rows, LANES), F32)
    b16 = lambda rows: pltpu.VMEM((rows, LANES), BF16)
    return pl.pallas_call(
        body, name=name, grid=(heads,),
        in_specs=[head(0), head(1), head(2), col, col, col],
        out_specs=pl.BlockSpec((3, s, HEAD_DIM), lambda h: (0, 0, h)),
        out_shape=jax.ShapeDtypeStruct((3, s, dm), BF16),
        scratch_shapes=[f32(s), b16(s), b16(s), b16(pad_rows), b16(pad_rows), f32(s), f32(s),
                        f32(s), f32(pad_rows), f32(pad_rows), f32(pad_rows), f32(pad_rows), f32(s), f32(s), f32(s)],
        compiler_params=_cparams(1),
    )(qkv, qkv, qkv, do, lse, delta)


def _ew(fn, ins, out_dtypes, *, name, tile_bytes=1 << 20):
    first = ins[0][0] if isinstance(ins[0], tuple) else ins[0]
    rows, cols = first.shape[-2], first.shape[-1]
    tr = _tile(rows, max(16, tile_bytes // (4 * cols)), 16)
    n_in = len(ins)

    def body(*refs):
        outs = fn(*[r[...] for r in refs[:n_in]])
        for o_ref, val in zip(refs[n_in:], outs):
            o_ref[...] = val.astype(o_ref.dtype)

    in_specs, args = [], []
    for item in ins:
        if isinstance(item, tuple):
            arr, lead = item
            in_specs.append(pl.BlockSpec((None, tr, cols), lambda i, lead=lead: (lead, i, 0)))
            args.append(arr)
        else:
            in_specs.append(pl.BlockSpec((tr, cols), lambda i: (i, 0)))
            args.append(item)
    blk = pl.BlockSpec((tr, cols), lambda i: (i, 0))
    return pl.pallas_call(
        body, name=name, grid=(rows // tr,),
        in_specs=in_specs, out_specs=[blk] * len(out_dtypes),
        out_shape=[jax.ShapeDtypeStruct((rows, cols), dt) for dt in out_dtypes],
        compiler_params=_cparams(1),
    )(*args)


def _adamw_math(g, w, m, v):
    m2 = ADAM_B1 * m + (1.0 - ADAM_B1) * g
    v2 = ADAM_B2 * v + (1.0 - ADAM_B2) * (g * g)
    m_hat = m2 / (1.0 - ADAM_B1 ** ADAM_STEP)
    v_hat = v2 / (1.0 - ADAM_B2 ** ADAM_STEP)
    delta = -ADAM_LR * (m_hat / (jnp.sqrt(v_hat) + ADAM_EPS) + ADAM_WD * w)
    return g, delta, m2, v2


def _scalars(*vals):
    return jnp.stack([jnp.asarray(v, jnp.int32) for v in vals])


def _adamw_halves(mine, theirs, core, w, m, v, *, name, lead=0, prev=None):
    shape = w.shape
    a_n, rh, cols = mine.shape
    w3, m3, v3 = (t.reshape(-1, 2 * rh, cols) for t in (w, m, v))
    tr, tc = _tile2(rh, cols, 1 << 17, 8)
    n_i = rh // tr

    def body(c_ref, mine_ref, theirs_ref, w_ref, m_ref, v_ref, *rest):
        g = jnp.where(pl.program_id(1) == c_ref[0], mine_ref[...], theirs_ref[...])
        outs = _adamw_math(g, w_ref[...], m_ref[...], v_ref[...])
        for ref, val in zip(rest[-4:], outs):
            ref[...] = val

    def half(mine_rows):
        def index(a, h, i, j, c_ref):
            use = (h == c_ref[0]) if mine_rows else (h != c_ref[0])
            return (a, jnp.where(use, i, 0), jnp.where(use, j, 0))
        return pl.BlockSpec((None, tr, tc), index)

    full = pl.BlockSpec((None, tr, tc), lambda a, h, i, j, c_ref: (lead + a, h * n_i + i, j))
    args = [mine, theirs, w3, m3, v3]
    in_specs = [half(True), half(False), full, full, full]
    aliases = {}
    if prev is not None:
        args += [p.reshape(w3.shape) for p in prev]
        in_specs += [ANY] * 4
        aliases = {6 + k: k for k in range(4)}
    outs = pl.pallas_call(
        body, name=name,
        grid_spec=pltpu.PrefetchScalarGridSpec(
            num_scalar_prefetch=1, grid=(a_n, 2, n_i, cols // tc), in_specs=in_specs, out_specs=[full] * 4),
        out_shape=[jax.ShapeDtypeStruct(w3.shape, F32)] * 4,
        input_output_aliases=aliases,
        compiler_params=_cparams(4),
    )(_scalars(core), *args)
    return [o.reshape(shape) for o in outs]


def _cast_into(src, buf, lead, slot, *, name, buf_shape=None, dtype=BF16):
    a_n, rows, cols = src.shape
    tr = _tile(rows, max(16, (1 << 21) // (4 * cols)), 16)

    def body(slot_ref, src_ref, *rest):
        rest[-1][...] = src_ref[...].astype(rest[-1].dtype)

    in_specs = [pl.BlockSpec((None, tr, cols), lambda a, i, slot_ref: (a, i, 0))]
    args = [src]
    aliases = {}
    if buf is not None:
        in_specs.append(ANY)
        args.append(buf)
        aliases = {2: 0}
        buf_shape, dtype = buf.shape, buf.dtype
    return pl.pallas_call(
        body, name=name,
        grid_spec=pltpu.PrefetchScalarGridSpec(
            num_scalar_prefetch=1, grid=(a_n, rows // tr), in_specs=in_specs,
            out_specs=pl.BlockSpec((None, None, tr, cols), lambda a, i, slot_ref: (lead + a, slot_ref[0], i, 0))),
        out_shape=jax.ShapeDtypeStruct(buf_shape, dtype),
        input_output_aliases=aliases,
        compiler_params=_cparams(2),
    )(_scalars(slot), *args)


def _place():
    x, y, c = lax.axis_index("x"), lax.axis_index("y"), lax.axis_index("c")
    chips = [(1 - x, y), (x, 1 - y), (1 - x, 1 - y)]
    return x, y, c, chips


def _remote(src, dst, send_sem, recv_sem, dev):
    return pltpu.make_async_remote_copy(src_ref=src, dst_ref=dst, send_sem=send_sem, recv_sem=recv_sem,
                                        device_id=dev, device_id_type=MESH)


def _gather_comm(bufs, pieces):
    def plan(refs, send_sems, recv_sems):
        x, y, c, chips = _place()

        def region(p, q, core):
            t, a0, a1, part, n_parts = pieces[p]
            rh, cw = bufs[t].shape[2] // 2, bufs[t].shape[3] // n_parts
            return refs[t].at[pl.ds(a0, a1 - a0), q, pl.ds(core * rh, rh), pl.ds(part * cw, cw)]

        def ici(p, j, q):
            cx, cy = chips[j]
            return _remote(region(p, q, c), region(p, q, c), send_sems.at[6 * p + j], recv_sems.at[6 * p + j], (cx, cy, c))

        def d2d(p, j, core):
            cx, cy = chips[j]
            rows = region(p, 2 * cx + cy, core)
            return _remote(rows, rows, send_sems.at[6 * p + 3 + j], recv_sems.at[6 * p + 3 + j], (x, y, 1 - c))

        return 2 * x + y, c, [2 * cx + cy for cx, cy in chips], ici, d2d

    todo = [(j, p) for j in range(3) for p in range(len(pieces))]

    def start(refs, send_sems, recv_sems):
        q_me, _, _, ici, _ = plan(refs, send_sems, recv_sems)
        for j, p in todo:
            ici(p, j, q_me).start()

    def finish(refs, send_sems, recv_sems):
        q_me, c, q_of, ici, d2d = plan(refs, send_sems, recv_sems)
        for j, p in todo:
            ici(p, j, q_of[j]).wait_recv()
            d2d(p, j, c).start()
        for j, p in todo:
            d2d(p, j, 1 - c).wait_recv()
        for j, p in todo:
            ici(p, j, q_me).wait_send()
            d2d(p, j, c).wait_send()

    return _Comm(bufs, 6 * len(pieces), start, finish)


def _chips_comm(parts, lands):
    n = len(parts)

    def copy(refs, send_sems, recv_sems, t, j, q_src, q_dst):
        x, y, c, chips = _place()
        cx, cy = chips[j]
        return _remote(refs[t].at[:, q_src], refs[n + t].at[q_dst], send_sems.at[3 * t + j], recv_sems.at[3 * t + j],
                       (cx, cy, c))

    todo = [(j, t) for j in range(3) for t in range(n)]

    def qs():
        x, y, _, chips = _place()
        return 2 * x + y, [2 * cx + cy for cx, cy in chips]

    def start(refs, send_sems, recv_sems):
        q_me, q_of = qs()
        for j, t in todo:
            copy(refs, send_sems, recv_sems, t, j, q_of[j], q_me).start()

    def finish(refs, send_sems, recv_sems):
        q_me, q_of = qs()
        for j, t in todo:
            copy(refs, send_sems, recv_sems, t, j, q_me, q_of[j]).wait_recv()
        for j, t in todo:
            copy(refs, send_sems, recv_sems, t, j, q_of[j], q_me).wait_send()

    return _Comm(list(parts) + list(lands), 3 * n, start, finish)


def _comm_call(comm, *, name):
    k = len(comm.operands)

    def body(*refs):
        operands, (send_sems, recv_sems) = refs[k:2 * k], refs[2 * k:]
        comm.start(operands, send_sems, recv_sems)
        comm.finish(operands, send_sems, recv_sems)

    return pl.pallas_call(
        body, name=name, in_specs=[ANY] * k, out_specs=[ANY] * k,
        out_shape=[jax.ShapeDtypeStruct(t.shape, t.dtype) for t in comm.operands],
        input_output_aliases={i: i for i in range(k)},
        scratch_shapes=[pltpu.SemaphoreType.DMA((comm.n_sems,)), pltpu.SemaphoreType.DMA((comm.n_sems,))],
    )(*comm.operands)


def _rs_sibling(grads, *, name):
    n = len(grads)
    halves = [jax.ShapeDtypeStruct(g.shape[:2] + (g.shape[2] // 2, g.shape[3]), g.dtype) for g in grads]

    def body(*refs):
        src, theirs = refs[:n], refs[n:2 * n]
        send_sems, recv_sems = refs[2 * n:]
        x, y, c, _ = _place()
        ops = []
        for t in range(n):
            rh = grads[t].shape[2] // 2
            give = _remote(src[t].at[:, :, pl.ds((1 - c) * rh, rh), :], theirs[t], send_sems.at[t], recv_sems.at[t],
                           (x, y, 1 - c))
            give.start()
            ops.append(give)
        for op in ops:
            op.wait()

    return pl.pallas_call(
        body, name=name, in_specs=[ANY] * n, out_specs=[ANY] * n, out_shape=halves,
        scratch_shapes=[pltpu.SemaphoreType.DMA((n,)), pltpu.SemaphoreType.DMA((n,))],
    )(*grads)


def _rs_add(grad, theirs, core, slot, *, name):
    a_n, _, rh, cols = theirs.shape
    tr, tc = _tile2(rh, cols, 1 << 18, 16)
    n_i = rh // tr

    def body(s_ref, g_ref, t_ref, p_ref, y_ref):
        part = (g_ref[...].astype(F32) + t_ref[...].astype(F32)).astype(BF16)
        p_ref[...] = part

        @pl.when(pl.program_id(3) == s_ref[1])
        def _():
            y_ref[...] = part

    blk = (None, None, tr, tc)
    return pl.pallas_call(
        body, name=name,
        grid_spec=pltpu.PrefetchScalarGridSpec(
            num_scalar_prefetch=1, grid=(a_n, n_i, cols // tc, N_CHIPS),
            in_specs=[pl.BlockSpec(blk, lambda a, i, j, q, s: (a, q, s[0] * n_i + i, j)),
                      pl.BlockSpec(blk, lambda a, i, j, q, s: (a, q, i, j))],
            out_specs=[pl.BlockSpec(blk, lambda a, i, j, q, s: (a, q, i, j)),
                       pl.BlockSpec(blk, lambda a, i, j, q, s: (s[1], a, i, j))]),
        out_shape=[jax.ShapeDtypeStruct(theirs.shape, BF16),
                   jax.ShapeDtypeStruct((N_CHIPS, a_n, rh, cols), BF16)],
        compiler_params=_cparams(4),
    )(_scalars(core, slot), grad, theirs)


def _rs_finish(halves, *, name):
    n = len(halves)

    def body(*refs):
        src, dst = refs[:n], refs[n:2 * n]
        send_sems, recv_sems = refs[2 * n:]
        x, y, c, _ = _place()
        ops = []
        for t in range(n):
            give = _remote(src[t], dst[t], send_sems.at[t], recv_sems.at[t], (x, y, 1 - c))
            give.start()
            ops.append(give)
        for op in ops:
            op.wait()

    return pl.pallas_call(
        body, name=name, in_specs=[ANY] * n, out_specs=[ANY] * n,
        out_shape=[jax.ShapeDtypeStruct(h.shape, h.dtype) for h in halves],
        scratch_shapes=[pltpu.SemaphoreType.DMA((n,)), pltpu.SemaphoreType.DMA((n,))],
    )(*halves)


def _all_reduce_small(vec, *, name):
    rows = vec.shape[0]

    def body(v_ref, o_ref, land, send_sems, recv_sems):
        x, y, c, _ = _place()
        me = 4 * x + 2 * y + c
        land[me] = v_ref[...]
        flips = [(fx, fy, fc) for fx in (0, 1) for fy in (0, 1) for fc in (0, 1)][1:]
        sent = []
        for k, (fx, fy, fc) in enumerate(flips):
            cp = _remote(v_ref, land.at[me], send_sems.at[k], recv_sems.at[k], (x ^ fx, y ^ fy, c ^ fc))
            cp.start()
            sent.append(cp)
        for k, (fx, fy, fc) in enumerate(flips):
            peer = 4 * (x ^ fx) + 2 * (y ^ fy) + (c ^ fc)
            _remote(v_ref, land.at[peer], send_sems.at[k], recv_sems.at[k], (x ^ fx, y ^ fy, c ^ fc)).wait_recv()
        for cp in sent:
            cp.wait_send()
        total = land[0]
        for dev in range(1, 8):
            total = total + land[dev]
        o_ref[...] = total

    whole = pl.BlockSpec(memory_space=pltpu.VMEM)
    return pl.pallas_call(
        body, name=name, in_specs=[whole], out_specs=whole,
        out_shape=jax.ShapeDtypeStruct(vec.shape, F32),
        scratch_shapes=[pltpu.VMEM((8, rows, LANES), F32), pltpu.SemaphoreType.DMA((7,)), pltpu.SemaphoreType.DMA((7,))],
        compiler_params=pltpu.CompilerParams(vmem_limit_bytes=VMEM_LIMIT),
    )(vec)


def _pack(parts, mult=16):
    flat = jnp.concatenate([p.reshape(-1).astype(F32) for p in parts])
    rows = -(-flat.shape[0] // (LANES * mult)) * mult
    return jnp.pad(flat, (0, rows * LANES - flat.shape[0])).reshape(rows, LANES)


def _unpack(vec, shapes):
    flat, out, pos = vec.reshape(-1), [], 0
    for shp in shapes:
        size = math.prod(shp)
        out.append(flat[pos:pos + size].reshape(shp))
        pos += size
    return out


def kernel(x, pool_w_in, pool_w_grp, pool_scale, pool_w_out, attn_w_q, attn_w_o, shared_w_k, shared_w_v, ffn_w_up, ffn_conv_w, ffn_conv_b, ffn_w_down, ln1_g, ln1_b, ln2_g, ln2_b, loss_target, m_pool_w_in, m_pool_w_grp, m_pool_scale, m_pool_w_out, m_attn_w_q, m_attn_w_o, m_shared_w_k, m_shared_w_v, m_ffn_w_up, m_ffn_conv_w, m_ffn_conv_b, m_ffn_w_down, m_ln1_g, m_ln1_b, m_ln2_g, m_ln2_b, v_pool_w_in, v_pool_w_grp, v_pool_scale, v_pool_w_out, v_attn_w_q, v_attn_w_o, v_shared_w_k, v_shared_w_v, v_ffn_w_up, v_ffn_conv_w, v_ffn_conv_b, v_ffn_w_down, v_ln1_g, v_ln1_b, v_ln2_g, v_ln2_b):
    s, d = x.shape[1], x.shape[2]
    n2 = ffn_w_up.shape[2]
    fq = ffn_w_down.shape[1]
    assert 2 * fq == n2 and d % N_CHIPS == 0
    g_n, cg = pool_w_grp.shape[1], pool_w_grp.shape[3]
    xs, tgt = x[0], loss_target[0]
    q_me = 2 * lax.axis_index("x") + lax.axis_index("y")
    core = lax.axis_index("c")
    rq = d // N_CHIPS

    six_g = None
    for i, w_ in enumerate((pool_w_in[0], pool_w_out[0], attn_w_q[0], shared_w_k, shared_w_v, attn_w_o[0])):
        six_g = _cast_into(w_[None], six_g, i, q_me, name=f"cast_w{i}", buf_shape=(6, N_CHIPS, rq, d))
    grp_g = _cast_into(pool_w_grp[0], None, 0, q_me, name="cast_grp", buf_shape=(g_n, N_CHIPS, cg // N_CHIPS, cg))
    up_t, m_up_t, v_up_t = (jnp.swapaxes(t, 1, 2) for t in (ffn_w_up, m_ffn_w_up, v_ffn_w_up))
    up_g = _cast_into(up_t, None, 0, q_me, name="cast_up", buf_shape=(2, N_CHIPS, n2, d))
    dn_g = _cast_into(ffn_w_down, None, 0, q_me, name="cast_down", buf_shape=(2, N_CHIPS, fq, d))
    small = _pack([ffn_conv_w, pool_scale])
    small_g = _cast_into(small[None], None, 0, q_me, name="cast_small", buf_shape=(1, N_CHIPS) + small.shape, dtype=F32)
    bufs = [six_g, grp_g, up_g, dn_g, small_g]
    SIX, GRP, UP, DN, SMALL = range(5)
    first = [SIX, GRP, SMALL]
    landed = _comm_call(_gather_comm([bufs[t] for t in first], [(0, IW_IN, IW_OUT + 1, 0, 1), (1, 0, g_n, 0, 1),
                                                                (2, 0, 1, 0, 1)]), name="gather_first")
    for t, arr in zip(first, landed):
        bufs[t] = arr
    small_q = bufs[SMALL].reshape(N_CHIPS, -1)
    n_cw = 2 * 3 * n2
    conv_w = small_q[:, :n_cw].reshape(N_CHIPS, 2, 3, n2).transpose(1, 0, 2, 3)
    scale_full = small_q[:, n_cw:n_cw + d // N_CHIPS].reshape(1, d)
    conv_b = ffn_conv_b.reshape(2, N_CHIPS, 1, n2)

    def w6():
        return bufs[SIX].reshape(6, d, d)

    def up8():
        return bufs[UP].reshape(2 * N_CHIPS, n2, d)

    def dn4():
        return bufs[DN].reshape(4, n2, d)

    def gathered(pieces):
        used = sorted({pc[0] for pc in pieces})
        local = [(used.index(t), a0, a1, part, n_parts) for t, a0, a1, part, n_parts in pieces]
        return _gather_comm([bufs[t] for t in used], local), used

    def store(used, operands):
        for t, arr in zip(used, operands):
            bufs[t] = arr

    def ffn_fwd(l, hb, carry_up, carry_down):
        comm, used = gathered(carry_up)
        u, landed = _mm(hb[None], up8(), mode="nt", name=f"ffn{l}_up", out_dtype=BF16, n_q=4, tm=_tile(s, 512), tn=n2,
                        tk=d, qa=_q0, qb=lambda q, qr: 4 * l + _perm(q), comm=comm)
        store(used, landed)
        u = u.reshape(2, 2, s, n2)
        act = _act_fwd(u, conv_w[l], conv_b[l], name=f"ffn{l}_act")
        comm, used = gathered(carry_down) if carry_down else (None, None)
        ff = _mm(act, dn4(), mode="nn", name=f"ffn{l}_down", out_dtype=F32, n_q=1, n_qr=2, tm=_tile(s, 1024),
                 tn=_tile(d, 1024), tk=n2, qa=_qr, qb=lambda q, qr: 2 * l + qr, qo=lambda q: 0, comm=comm)
        if comm is not None:
            ff, landed = ff
            store(used, landed)
        return u, act, ff[0]

    def up0_parts(*parts):
        return gathered([(UP, 0, 1, part, 8) for part in parts])

    comm, used = up0_parts(0, 1)
    p, landed = _mm_dd(xs[None], w6(), IW_IN, mode="nn", name="pool_in", out_dtype=F32, comm=comm)
    store(used, landed)
    comm, used = up0_parts(2)
    pooled, landed = _pool(p[0], backward=False, name="pool_fwd", comm=comm)
    store(used, landed)
    comm, used = up0_parts(3)
    (mg, mixed), landed = _grp_fwd(pooled, bufs[GRP].reshape(g_n, cg, cg), scale_full, name="pool_grp", comm=comm)
    store(used, landed)
    comm, used = up0_parts(4, 5)
    mix0, landed = _mm_dd(mixed[None], w6(), IW_OUT, mode="nn", name="pool_out", out_dtype=F32, comm=comm)
    store(used, landed)
    comm, used = up0_parts(6, 7)
    (h1, h1b, xh1, rs1), landed = _ln_fwd(xs, mix0[0], ln1_g[0:1], ln1_b[0:1], name="ln1_0", comm=comm)
    store(used, landed)
    u0, act0, ff0 = ffn_fwd(0, h1b, [(DN, 0, 1, 0, 1), (SIX, IW_Q, IW_V + 1, 0, 1)], [(SIX, IW_O, IW_O + 1, 0, 1)])
    h2, h2b, xh2, rs2 = _ln_fwd(h1, ff0, ln2_g[0:1], ln2_b[0:1], name="ln2_0")

    qkv = _mm_dd(h2b[None], w6(), IW_Q, mode="nn", name="attn_qkv", out_dtype=BF16, n_q=3)
    comm, used = gathered([(UP, 1, 2, 0, 1)])
    (o, ob, lse), landed = _attn_fwd(qkv, name="attn_fwd", comm=comm)
    store(used, landed)
    mix1 = _mm_dd(ob[None], w6(), IW_O, mode="nn", name="attn_out", out_dtype=F32)[0]
    h3, h3b, xh3, rs3 = _ln_fwd(h2, mix1, ln1_g[1:2], ln1_b[1:2], name="ln1_1")
    u1, act1, ff1 = ffn_fwd(1, h3b, [(DN, 1, 2, 0, 1)], None)
    h4, _, xh4, rs4 = _ln_fwd(h3, ff1, ln2_g[1:2], ln2_b[1:2], name="ln2_1")
    dh4, loss_local = _loss_head(h4, tgt, name="loss_head")

    rs_parts, rs_lands = {}, {}

    def rs_prepare(items):
        theirs = _rs_sibling([g_ for _, g_ in items], name="rs_sibling_" + items[0][0])
        for (nm, g_), t_ in zip(items, theirs):
            rs_parts[nm], rs_lands[nm] = _rs_add(g_, t_, core, q_me, name=f"rs_add_{nm}")

    def rs_exchange(names):
        return _chips_comm([rs_parts[nm] for nm in names], [rs_lands[nm] for nm in names])

    def rs_landed(names, operands):
        for nm, land in zip(names, operands[len(names):]):
            rs_lands[nm] = land

    d_conv_w, d_conv_b = [None, None], [None, None]

    def ffn_bwd(l, dzb, u, act, hb, carry_dact):
        da = _mm(dzb[None], dn4(), mode="nt", name=f"ffn{l}_dact", out_dtype=F32, n_q=2, tm=_tile(s, 512), tn=n2,
                 tk=d, qa=_q0, qb=lambda q, qr: 2 * l + q,
                 comm=rs_exchange(carry_dact) if carry_dact else None)
        if carry_dact:
            da, landed = da
            rs_landed(carry_dact, landed)
        du, dwg, dwv, dbg, dbv = _act_bwd(u, da, conv_w[l], conv_b[l], name=f"ffn{l}_dconv")
        d_conv_w[l] = jnp.concatenate([dwg, dwv], axis=0)
        d_conv_b[l] = jnp.concatenate([dbg, dbv], axis=0)
        du4 = du.reshape(4, s, n2)
        g_dn = _mm(act, dzb[None], mode="tn", name=f"ffn{l}_gdown", out_dtype=BF16, n_q=2, tm=n2, tn=_tile(d, 512),
                   tk=_tile(s, 2048), qa=_qq, qb=_q0)
        rs_prepare([(f"dn{l}", g_dn.reshape(1, N_CHIPS, fq, d))])
        g_up, landed = _mm(du4, hb[None], mode="tn", name=f"ffn{l}_gup", out_dtype=BF16, n_q=4, tm=n2, tn=_tile(d, 512),
                           tk=_tile(s, 2048), qa=lambda q, qr: _perm(q), qb=_q0, comm=rs_exchange([f"dn{l}"]))
        rs_landed([f"dn{l}"], landed)
        rs_prepare([(f"up{l}", g_up.reshape(1, N_CHIPS, n2, d))])
        dh, landed = _mm(du4, up8(), mode="nn", name=f"ffn{l}_dh", out_dtype=F32, n_q=1, n_qr=4, tm=_tile(s, 1024),
                         tn=_tile(d, 1024), tk=n2, qa=_qr, qb=lambda q, qr: 4 * l + _perm(qr), qo=lambda q: 0,
                         comm=rs_exchange([f"up{l}"]))
        rs_landed([f"up{l}"], landed)
        return dh[0]

    dz4, dz4b, dg_ln2_1, db_ln2_1 = _ln_bwd(None, dh4, xh4, rs4, ln2_g[1:2], name="dln2_1")
    dh3 = ffn_bwd(1, dz4b, u1, act1, h3b, None)
    dz3, dz3b, dg_ln1_1, db_ln1_1 = _ln_bwd(dz4, dh3, xh3, rs3, ln1_g[1:2], name="dln1_1")
    g_wo = _mm_wgrad(ob[None], dz3b[None], name="attn_gwo")
    do = _mm_dd(dz3b[None], w6(), IW_O, mode="nt", name="attn_do", out_dtype=BF16)[0]
    delta = _attn_delta(do, o, name="attn_delta")
    dqkv = _attn_bwd(qkv, do, lse, delta, name="attn_bwd")
    g_wq = _mm_wgrad(h2b[None], dqkv, name="attn_gwq", qb=lambda q, qr: 0)
    g_wk = _mm_wgrad(h2b[None], dqkv, name="attn_gwk", qb=lambda q, qr: 1)
    g_wv = _mm_wgrad(h2b[None], dqkv, name="attn_gwv", qb=lambda q, qr: 2)
    rs_prepare([(nm, g_.reshape(1, N_CHIPS, rq, d)) for nm, g_ in
                (("wo", g_wo), ("wq", g_wq), ("wk", g_wk), ("wv", g_wv))])
    dh2, landed = _mm_dd(dqkv, w6(), IW_Q, mode="nt", name="attn_dh", out_dtype=F32, n_qr=3, qa=_qr,
                         comm=rs_exchange(["wo", "wq", "wk"]))
    rs_landed(["wo", "wq", "wk"], landed)
    dz2, dz2b, dg_ln2_0, db_ln2_0 = _ln_bwd(dz3, dh2[0], xh2, rs2, ln2_g[0:1], name="dln2_0")
    dh1 = ffn_bwd(0, dz2b, u0, act0, h1b, ["wv"])
    dz1, dz1b, dg_ln1_0, db_ln1_0 = _ln_bwd(dz2, dh1, xh1, rs1, ln1_g[0:1], name="dln1_0")
    g_wout = _mm_wgrad(mixed[None], dz1b[None], name="pool_gwout")
    rs_prepare([("wout", g_wout.reshape(1, N_CHIPS, rq, d))])
    dmixed, landed = _mm_dd(dz1b[None], w6(), IW_OUT, mode="nt", name="pool_dmixed", out_dtype=F32,
                            comm=rs_exchange(["wout"]))
    rs_landed(["wout"], landed)
    dmg, d_scale = _grp_bwd_pre(dmixed[0], mg, scale_full, name="pool_dscale")
    g_wgrp = _grp_mm(pooled, dmg, mode="tn", name="pool_gwgrp", out_dtype=BF16, tm=1024)
    dpooled = _grp_mm(dmg, None, mode="nt", name="pool_dpooled", out_dtype=F32, tm=1024, w=bufs[GRP].reshape(g_n, cg, cg))
    dp = _pool(dpooled, backward=True, name="pool_bwd")
    g_win = _mm_wgrad(xs[None], dp[None], name="pool_gwin")
    last = ["win", "wgrp"]
    rs_prepare([("win", g_win.reshape(1, N_CHIPS, rq, d)), ("wgrp", g_wgrp.reshape(g_n, N_CHIPS, cg // N_CHIPS, cg))])
    dx_mm, landed = _mm_dd(dp[None], w6(), IW_IN, mode="nt", name="pool_dx", out_dtype=F32, comm=rs_exchange(last))
    rs_landed(last, landed)
    (grad_x,) = _ew(lambda a, b: (ALPHA * a + b,), [dz1, dx_mm[0]], [F32], name="grad_x")

    rs_names = ["win", "wgrp", "wout", "wq", "wo", "wk", "wv", "up0", "up1", "dn0", "dn1"]
    finished = {}
    for nm in rs_names:
        y4 = rs_lands[nm]
        y3 = y4.reshape(N_CHIPS, -1, y4.shape[-1])
        (tot,) = _ew(lambda a0, a1, a2, a3: (((a0.astype(F32) + a1.astype(F32)) + a2.astype(F32)) + a3.astype(F32),),
                     [(y3, 0), (y3, 1), (y3, 2), (y3, 3)], [F32], name=f"rs_sum_{nm}")
        finished[nm] = tot.reshape(y4.shape[1:])
    others = dict(zip(rs_names, _rs_finish([finished[nm] for nm in rs_names], name="rs_finish")))
    results = {}
    for nm, key, w, m, v in (("pool_w_in", "win", pool_w_in, m_pool_w_in, v_pool_w_in),
                             ("pool_w_grp", "wgrp", pool_w_grp, m_pool_w_grp, v_pool_w_grp),
                             ("pool_w_out", "wout", pool_w_out, m_pool_w_out, v_pool_w_out),
                             ("attn_w_q", "wq", attn_w_q, m_attn_w_q, v_attn_w_q),
                             ("attn_w_o", "wo", attn_w_o, m_attn_w_o, v_attn_w_o),
                             ("shared_w_k", "wk", shared_w_k, m_shared_w_k, v_shared_w_k),
                             ("shared_w_v", "wv", shared_w_v, m_shared_w_v, v_shared_w_v)):
        results[nm] = _adamw_halves(finished[key], others[key], core, w, m, v, name=f"adamw_{nm}")
    for nm, key, w, m, v in (("ffn_w_up", "up", up_t, m_up_t, v_up_t),
                             ("ffn_w_down", "dn", ffn_w_down, m_ffn_w_down, v_ffn_w_down)):
        res = None
        for l in (1, 0):
            res = _adamw_halves(finished[f"{key}{l}"], others[f"{key}{l}"], core, w, m, v, name=f"adamw_{nm}{l}",
                                lead=l, prev=res)
        results[nm] = res
    results["ffn_w_up"] = [jnp.swapaxes(t, 1, 2) for t in results["ffn_w_up"]]

    ln_grads = [jnp.concatenate([a, b], axis=0) for a, b in
                ((dg_ln1_0, dg_ln1_1), (db_ln1_0, db_ln1_1), (dg_ln2_0, dg_ln2_1), (db_ln2_0, db_ln2_1))]
    small_shapes = [(2, N_CHIPS, 3, n2), (2, N_CHIPS, n2)] + [(2, d)] * 4 + [(1, d)]
    vec = _pack([jnp.stack(d_conv_w), jnp.stack(d_conv_b)] + ln_grads + [d_scale], mult=8)
    tot = _unpack(_all_reduce_small(vec, name="allreduce_small"), small_shapes)
    g_cw = lax.dynamic_index_in_dim(tot[0], q_me, axis=1, keepdims=False)
    g_cb = tot[1].reshape(2, N_CHIPS * n2)
    g_scale = lax.dynamic_slice_in_dim(tot[6], q_me * rq, rq, axis=1)
    small_names = ["ffn_conv_w", "ffn_conv_b", "ln1_g", "ln1_b", "ln2_g", "ln2_b", "pool_scale"]
    small_g = [g_cw, g_cb, tot[2], tot[3], tot[4], tot[5], g_scale]
    small_w = [ffn_conv_w, ffn_conv_b, ln1_g, ln1_b, ln2_g, ln2_b, pool_scale]
    small_m = [m_ffn_conv_w, m_ffn_conv_b, m_ln1_g, m_ln1_b, m_ln2_g, m_ln2_b, m_pool_scale]
    small_v = [v_ffn_conv_w, v_ffn_conv_b, v_ln1_g, v_ln1_b, v_ln2_g, v_ln2_b, v_pool_scale]
    packed = _ew(_adamw_math, [_pack(small_g, 8), _pack(small_w, 8), _pack(small_m, 8), _pack(small_v, 8)], [F32] * 4,
                 name="adamw_small")
    shapes = [w.shape for w in small_w]
    unpacked = [_unpack(pk, shapes) for pk in packed]
    for i, nm in enumerate(small_names):
        results[nm] = [unpacked[k][i] for k in range(4)]

    loss = lax.psum(loss_local[0, 0], ("x", "y", "c"))
    order = ["pool_w_in", "pool_w_grp", "pool_scale", "pool_w_out", "attn_w_q", "attn_w_o", "shared_w_k", "shared_w_v",
             "ffn_w_up", "ffn_conv_w", "ffn_conv_b", "ffn_w_down", "ln1_g", "ln1_b", "ln2_g", "ln2_b"]
    outs = [loss, grad_x[None]]
    for k in range(4):
        outs += [results[nm][k] for nm in order]
    return tuple(outs)
```

```python
import functools
import math

import jax
import jax.numpy as jnp
from jax import lax
from jax.experimental import pallas as pl
from jax.experimental.pallas import tpu as pltpu

F32 = jnp.float32
BF16 = jnp.bfloat16

LANES = 128
HEAD_DIM = 128
ATTN_BLOCK = 128
DILATIONS = (1, 4, 16)
ATTN_FWD_TILE = (128, 8)
ATTN_BWD_TILE = (256, 4)
POOL_WINDOWS = (2, 4, 8, 16)
POOL_HALO = 16
CONV_HALO = 16
DEPTH = 2
ALPHA = (2.0 * DEPTH) ** 0.25
LN_EPS = 1e-5
NEG_INF = -1e30
ADAM_LR = 0.001
ADAM_B1 = 0.9
ADAM_B2 = 0.999
ADAM_EPS = 1e-08
ADAM_WD = 0.01
ADAM_STEP = 10
N_CHIPS = 4
VMEM_LIMIT = 56 * 1024 * 1024
ANY = pl.BlockSpec(memory_space=pl.ANY)
MESH = pl.DeviceIdType.MESH

IW_IN, IW_OUT, IW_Q, IW_K, IW_V, IW_O = range(6)


def _cparams(n_grid):
    return pltpu.CompilerParams(dimension_semantics=("arbitrary",) * n_grid, vmem_limit_bytes=VMEM_LIMIT)


def _tile(dim, pref, align=LANES):
    if dim <= pref:
        return dim
    t = (pref // align) * align
    while t >= align:
        if dim % t == 0:
            return t
        t -= align
    return dim


def _tile2(rows, cols, budget, row_align):
    best = None
    for tc in [cols] + [c for c in range(LANES, cols, LANES) if cols % c == 0]:
        for tr in range(row_align, rows + 1, row_align):
            if rows % tr == 0 and tr * tc <= budget:
                if best is None or (tr * tc, tc) > (best[0] * best[1], best[1]):
                    best = (tr, tc)
    return best if best is not None else (rows, cols)


def _perm(q):
    return (q % 2) * 2 + q // 2


_DIMS = {"nn": (((1,), (0,)), ((), ())), "nt": (((1,), (1,)), ((), ())), "tn": (((0,), (0,)), ((), ()))}


class _Comm:
    def __init__(self, operands, n_sems, start, finish):
        self.operands, self.n_sems, self.start, self.finish = list(operands), n_sems, start, finish


def _pallas(body, *, name, grid, in_specs, out_specs, out_shape, args, scratch_shapes=(), comm=None):
    n_in, n_out = len(args), len(out_shape)
    if comm is None:
        return pl.pallas_call(body, name=name, grid=grid, in_specs=list(in_specs), out_specs=list(out_specs),
                              out_shape=list(out_shape), scratch_shapes=list(scratch_shapes),
                              compiler_params=_cparams(len(grid)))(*args)
    k = len(comm.operands)

    def carried(*refs):
        ins, outs = refs[:n_in], refs[n_in + k:n_in + k + n_out]
        operands = refs[n_in + k + n_out:n_in + 2 * k + n_out]
        scratch, (send_sems, recv_sems) = refs[n_in + 2 * k + n_out:-2], refs[-2:]
        ids = [pl.program_id(ax) for ax in range(len(grid))]
        first = functools.reduce(jnp.logical_and, [i == 0 for i in ids])
        last = functools.reduce(jnp.logical_and, [i == g - 1 for i, g in zip(ids, grid)])

        @pl.when(first)
        def _():
            comm.start(operands, send_sems, recv_sems)

        body(*ins, *outs, *scratch)

        @pl.when(last)
        def _():
            comm.finish(operands, send_sems, recv_sems)

    res = pl.pallas_call(
        carried, name=name, grid=grid, in_specs=list(in_specs) + [ANY] * k, out_specs=list(out_specs) + [ANY] * k,
        out_shape=list(out_shape) + [jax.ShapeDtypeStruct(t.shape, t.dtype) for t in comm.operands],
        scratch_shapes=list(scratch_shapes) + [pltpu.SemaphoreType.DMA((comm.n_sems,))] * 2,
        input_output_aliases={n_in + i: n_out + i for i in range(k)},
        compiler_params=_cparams(len(grid)),
    )(*args, *comm.operands)
    return res[:n_out], res[n_out:]


def _mm(a, b, *, mode, name, out_dtype, n_q, tm, tn, tk, qa, qb, qo=lambda q: q, n_qr=1, out_q=None, comm=None):
    if mode == "nn":
        m, kdim, n = a.shape[1], a.shape[2], b.shape[2]
    elif mode == "nt":
        m, kdim, n = a.shape[1], a.shape[2], b.shape[1]
    else:
        kdim, m, n = a.shape[1], a.shape[2], b.shape[2]
    assert m % tm == 0 and n % tn == 0 and kdim % tk == 0, (name, m, n, kdim, tm, tn, tk)
    kr_n = kdim // tk
    nr = n_qr * kr_n
    out_q = n_q if out_q is None else out_q

    def split(r):
        return (r // kr_n, r % kr_n) if n_qr > 1 else (0, r)

    if mode == "tn":
        a_spec = pl.BlockSpec((None, tk, tm), lambda q, i, j, r: (qa(q, split(r)[0]), split(r)[1], i))
    else:
        a_spec = pl.BlockSpec((None, tm, tk), lambda q, i, j, r: (qa(q, split(r)[0]), i, split(r)[1]))
    if mode == "nt":
        b_spec = pl.BlockSpec((None, tn, tk), lambda q, i, j, r: (qb(q, split(r)[0]), j, split(r)[1]))
    else:
        b_spec = pl.BlockSpec((None, tk, tn), lambda q, i, j, r: (qb(q, split(r)[0]), split(r)[1], j))
    o_spec = pl.BlockSpec((None, tm, tn), lambda q, i, j, r: (qo(q), i, j))
    dims = _DIMS[mode]

    n_comm = len(comm.operands) if comm is not None else 0
    grid = (n_q, m // tm, n // tn, nr)

    def body(*refs):
        a_ref, b_ref = refs[0], refs[1]
        o_ref = refs[2 + n_comm]
        if comm is not None:
            comm_refs = refs[3 + n_comm:3 + 2 * n_comm]
            send_sems, recv_sems = refs[-2:]
            ids = [pl.program_id(ax) for ax in range(4)]
            first = functools.reduce(jnp.logical_and, [i == 0 for i in ids])
            last = functools.reduce(jnp.logical_and, [i == g - 1 for i, g in zip(ids, grid)])

            @pl.when(first)
            def _():
                comm.start(comm_refs, send_sems, recv_sems)
        lhs, rhs = a_ref[...], b_ref[...]
        if lhs.dtype != BF16:
            lhs = lhs.astype(BF16)
        if rhs.dtype != BF16:
            rhs = rhs.astype(BF16)
        part = lax.dot_general(lhs, rhs, dims, preferred_element_type=F32)
        if nr == 1:
            o_ref[...] = part.astype(o_ref.dtype)
        else:
            acc_ref = refs[3 + 2 * n_comm]
            r = pl.program_id(3)

            @pl.when(r == 0)
            def _():
                acc_ref[...] = part

            @pl.when(r > 0)
            def _():
                acc_ref[...] += part

            @pl.when(r == nr - 1)
            def _():
                o_ref[...] = acc_ref[...].astype(o_ref.dtype)
        if comm is not None:
            @pl.when(last)
            def _():
                comm.finish(comm_refs, send_sems, recv_sems)

    scratch = [pltpu.VMEM((tm, tn), F32)] if nr > 1 else []
    out_shape = [jax.ShapeDtypeStruct((out_q, m, n), out_dtype)]
    args = [a, b]
    if comm is not None:
        args += comm.operands
        out_shape += [jax.ShapeDtypeStruct(t.shape, t.dtype) for t in comm.operands]
        scratch += [pltpu.SemaphoreType.DMA((comm.n_sems,)), pltpu.SemaphoreType.DMA((comm.n_sems,))]
    outs = pl.pallas_call(
        body, name=name, grid=grid,
        in_specs=[a_spec, b_spec] + [ANY] * n_comm, out_specs=[o_spec] + [ANY] * n_comm,
        out_shape=out_shape, scratch_shapes=scratch,
        input_output_aliases={2 + i: 1 + i for i in range(n_comm)},
        compiler_params=_cparams(4),
    )(*args)
    return (outs[0], outs[1:]) if comm is not None else outs[0]


def _q0(q, qr):
    return 0


def _qq(q, qr):
    return q


def _qr(q, qr):
    return qr


def _mm_dd(a3, w6, widx, *, mode, name, out_dtype, n_q=1, n_qr=1, qa=_q0, comm=None):
    d = w6.shape[1]
    f32_in = a3.dtype != BF16
    tm = _tile(a3.shape[1] if mode != "tn" else a3.shape[2], 512 if f32_in else 1024)
    if n_qr > 1:
        qb = lambda q, qr: widx + qr
    elif n_q > 1:
        qb = lambda q, qr: widx + q
    else:
        qb = lambda q, qr: widx
    return _mm(a3, w6, mode=mode, name=name, out_dtype=out_dtype, n_q=n_q, n_qr=n_qr,
               tm=tm, tn=_tile(d, 1024), tk=d, qa=qa, qb=qb, comm=comm)


def _mm_wgrad(a3, b3, *, name, n_q=1, qa=_q0, qb=_q0):
    s, d = a3.shape[1], a3.shape[2]
    f32_in = a3.dtype != BF16 or b3.dtype != BF16
    return _mm(a3, b3, mode="tn", name=name, out_dtype=BF16, n_q=n_q, tm=_tile(d, 1024), tn=_tile(b3.shape[2], 512),
               tk=_tile(s, 1024 if f32_in else 2048), qa=qa, qb=qb)


def _grp_mm(a, b, *, mode, name, out_dtype, tm, w=None):
    s = a.shape[0]
    if mode == "tn":
        cg = b.shape[1] // len(POOL_WINDOWS)
        g_n = len(POOL_WINDOWS)
        ts = _tile(s, tm)
        nr = s // ts

        def body(a_ref, b_ref, o_ref, acc_ref):
            r = pl.program_id(1)
            part = lax.dot_general(a_ref[...], b_ref[...], _DIMS["tn"], preferred_element_type=F32)

            @pl.when(r == 0)
            def _():
                acc_ref[...] = part

            @pl.when(r > 0)
            def _():
                acc_ref[...] += part

            @pl.when(r == nr - 1)
            def _():
                o_ref[...] = acc_ref[...].astype(o_ref.dtype)

        return pl.pallas_call(
            body, name=name, grid=(g_n, nr),
            in_specs=[pl.BlockSpec((ts, cg), lambda g, r: (r, g)), pl.BlockSpec((ts, cg), lambda g, r: (r, g))],
            out_specs=pl.BlockSpec((None, cg, cg), lambda g, r: (g, 0, 0)),
            out_shape=jax.ShapeDtypeStruct((g_n, cg, cg), out_dtype),
            scratch_shapes=[pltpu.VMEM((cg, cg), F32)],
            compiler_params=_cparams(2),
        )(a, b)
    g_n, cg = w.shape[0], w.shape[1]
    ts = _tile(s, tm)
    dims = _DIMS[mode]

    def body(a_ref, w_ref, o_ref):
        o_ref[...] = lax.dot_general(a_ref[...], w_ref[...], dims, preferred_element_type=F32).astype(o_ref.dtype)

    return pl.pallas_call(
        body, name=name, grid=(g_n, s // ts),
        in_specs=[pl.BlockSpec((ts, cg), lambda g, i: (i, g)), pl.BlockSpec((None, cg, cg), lambda g, i: (g, 0, 0))],
        out_specs=pl.BlockSpec((ts, cg), lambda g, i: (i, g)),
        out_shape=jax.ShapeDtypeStruct((s, g_n * cg), out_dtype),
        compiler_params=_cparams(2),
    )(a, w)


def _causal_ext(load, r0, rows, halo):
    cur = load(r0, rows)
    prev = load(pl.multiple_of(jnp.maximum(r0 - halo, 0), halo), halo)
    prev = jnp.where(r0 > 0, prev, jnp.zeros_like(prev))
    return jnp.concatenate([prev, cur], axis=0)


def _anti_ext(load, r0, rows, halo, s):
    cur = load(r0, rows)
    nxt = load(pl.multiple_of(jnp.minimum(r0 + rows, s - halo), halo), halo)
    nxt = jnp.where(r0 + rows < s, nxt, jnp.zeros_like(nxt))
    return jnp.concatenate([cur, nxt], axis=0)


def _down(ext, k):
    return pltpu.roll(ext, k, axis=0)


def _up(ext, k):
    return pltpu.roll(ext, ext.shape[0] - k, axis=0)


def _fold8(x):
    return jnp.sum(x.reshape(x.shape[0] // 8, 8, x.shape[1]), axis=0)


def _sigmoid(x):
    return 0.5 * jnp.tanh(0.5 * x) + 0.5


def _pool(p, *, backward, name, rows=64, comm=None):
    s, d = p.shape
    strips_per_group = (d // len(POOL_WINDOWS)) // LANES
    assert strips_per_group * LANES * len(POOL_WINDOWS) == d and s % rows == 0

    def body(p_ref, o_ref):
        g = pl.program_id(0) // strips_per_group
        win = jnp.left_shift(2, g).astype(F32)

        def load(r0, n):
            return p_ref[pl.ds(r0, n), :]

        def pick(levels):
            return jnp.where(g == 0, levels[0], jnp.where(g == 1, levels[1], jnp.where(g == 2, levels[2], levels[3])))

        def chunk(c, carry):
            r0 = pl.multiple_of(c * rows, rows)
            if not backward:
                ext = _causal_ext(load, r0, rows, POOL_HALO)
                levels, acc = [], ext
                for k in (1, 2, 4, 8):
                    acc = acc + _down(acc, k)
                    levels.append(acc)
                t = (r0 + lax.broadcasted_iota(jnp.int32, (rows, LANES), 0)).astype(F32)
                cnt = jnp.minimum(t + 1.0, win)
                out = pick(levels)[POOL_HALO:] / cnt - ext[POOL_HALO:]
            else:
                ext = _anti_ext(load, r0, rows, POOL_HALO, s)
                t = (r0 + lax.broadcasted_iota(jnp.int32, (rows + POOL_HALO, LANES), 0)).astype(F32)
                e = ext / jnp.minimum(t + 1.0, win)
                levels, acc = [], e
                for k in (1, 2, 4, 8):
                    acc = acc + _up(acc, k)
                    levels.append(acc)
                out = pick(levels)[:rows] - ext[:rows]
            o_ref[pl.ds(r0, rows), :] = out.astype(o_ref.dtype)
            return carry

        lax.fori_loop(0, s // rows, chunk, 0)

    res = _pallas(
        body, name=name, grid=(d // LANES,),
        in_specs=[pl.BlockSpec((s, LANES), lambda j: (0, j))],
        out_specs=[pl.BlockSpec((s, LANES), lambda j: (0, j))],
        out_shape=[jax.ShapeDtypeStruct((s, d), BF16)], args=[p], comm=comm)
    return res[0] if comm is None else (res[0][0], res[1])


def _grp_fwd(pooled, w_grp, scale, *, name, comm=None):
    s, d = pooled.shape
    g_n, cg = w_grp.shape[0], w_grp.shape[1]
    ts = _tile(s, 1024)

    def body(a_ref, w_ref, sc_ref, mg_ref, mx_ref):
        mg = jnp.dot(a_ref[...], w_ref[...], preferred_element_type=F32)
        mg_ref[...] = mg.astype(BF16)
        mx_ref[...] = (mg * sc_ref[...]).astype(BF16)

    blk = pl.BlockSpec((ts, cg), lambda g, i: (i, g))
    return _pallas(
        body, name=name, grid=(g_n, s // ts),
        in_specs=[blk, pl.BlockSpec((None, cg, cg), lambda g, i: (g, 0, 0)), pl.BlockSpec((1, cg), lambda g, i: (0, g))],
        out_specs=[blk, blk],
        out_shape=[jax.ShapeDtypeStruct((s, d), BF16)] * 2, args=[pooled, w_grp, scale], comm=comm)


def _grp_bwd_pre(dmixed, mg, scale, *, name):
    s, d = dmixed.shape
    ts = _tile(s, 256, 16)

    def body(dm_ref, mg_ref, sc_ref, dmg_ref, dsc_ref):
        dm = dm_ref[...]
        dmg_ref[...] = (dm * sc_ref[...]).astype(BF16)
        part = jnp.sum(dm * mg_ref[...].astype(F32), axis=0, keepdims=True)

        @pl.when(pl.program_id(0) == 0)
        def _():
            dsc_ref[...] = part

        @pl.when(pl.program_id(0) > 0)
        def _():
            dsc_ref[...] += part

    blk = pl.BlockSpec((ts, d), lambda i: (i, 0))
    vec = pl.BlockSpec((1, d), lambda i: (0, 0))
    return pl.pallas_call(
        body, name=name, grid=(s // ts,),
        in_specs=[blk, blk, vec], out_specs=[blk, vec],
        out_shape=[jax.ShapeDtypeStruct((s, d), BF16), jax.ShapeDtypeStruct((1, d), F32)],
        compiler_params=_cparams(1),
    )(dmixed, mg, scale)


def _ln_fwd(res, mm, g, b, *, name, comm=None):
    s, d = res.shape
    ts = _tile(s, 128, 16)

    def body(res_ref, mm_ref, g_ref, b_ref, h_ref, hb_ref, xh_ref, rs_ref):
        z = ALPHA * res_ref[...] + mm_ref[...]
        mu = jnp.mean(z, axis=-1, keepdims=True)
        zc = z - mu
        var = jnp.mean(zc * zc, axis=-1, keepdims=True)
        rstd = lax.rsqrt(var + LN_EPS)
        xhat = zc * rstd
        h = xhat * g_ref[...] + b_ref[...]
        h_ref[...] = h
        hb_ref[...] = h.astype(BF16)
        xh_ref[...] = xhat
        rs_ref[...] = rstd

    blk = pl.BlockSpec((ts, d), lambda i: (i, 0))
    vec = pl.BlockSpec((1, d), lambda i: (0, 0))
    return _pallas(
        body, name=name, grid=(s // ts,),
        in_specs=[blk, blk, vec, vec],
        out_specs=[blk, blk, blk, pl.BlockSpec((ts, 1), lambda i: (i, 0))],
        out_shape=[jax.ShapeDtypeStruct((s, d), F32), jax.ShapeDtypeStruct((s, d), BF16),
                   jax.ShapeDtypeStruct((s, d), F32), jax.ShapeDtypeStruct((s, 1), F32)],
        args=[res, mm, g, b], comm=comm)


def _ln_bwd(dres, dmm, xhat, rstd, g, *, name):
    s, d = dmm.shape
    ts = _tile(s, 128, 16)
    has_res = dres is not None

    def body(*refs):
        if has_res:
            dres_ref, dmm_ref, xh_ref, rs_ref, g_ref, dz_ref, dzb_ref, dg_ref, db_ref = refs
            dh = ALPHA * dres_ref[...] + dmm_ref[...]
        else:
            dmm_ref, xh_ref, rs_ref, g_ref, dz_ref, dzb_ref, dg_ref, db_ref = refs
            dh = dmm_ref[...]
        xhat_ = xh_ref[...]
        dxh = dh * g_ref[...]
        c1 = jnp.mean(dxh, axis=-1, keepdims=True)
        c2 = jnp.mean(dxh * xhat_, axis=-1, keepdims=True)
        dz = rs_ref[...] * (dxh - c1 - xhat_ * c2)
        dz_ref[...] = dz
        dzb_ref[...] = dz.astype(BF16)
        dg_part = jnp.sum(dh * xhat_, axis=0, keepdims=True)
        db_part = jnp.sum(dh, axis=0, keepdims=True)

        @pl.when(pl.program_id(0) == 0)
        def _():
            dg_ref[...] = dg_part
            db_ref[...] = db_part

        @pl.when(pl.program_id(0) > 0)
        def _():
            dg_ref[...] += dg_part
            db_ref[...] += db_part

    blk = pl.BlockSpec((ts, d), lambda i: (i, 0))
    vec = pl.BlockSpec((1, d), lambda i: (0, 0))
    col = pl.BlockSpec((ts, 1), lambda i: (i, 0))
    ins = ([dres] if has_res else []) + [dmm, xhat, rstd, g]
    in_specs = ([blk] if has_res else []) + [blk, blk, col, vec]
    return pl.pallas_call(
        body, name=name, grid=(s // ts,),
        in_specs=in_specs, out_specs=[blk, blk, vec, vec],
        out_shape=[jax.ShapeDtypeStruct((s, d), F32), jax.ShapeDtypeStruct((s, d), BF16),
                   jax.ShapeDtypeStruct((1, d), F32), jax.ShapeDtypeStruct((1, d), F32)],
        compiler_params=_cparams(1),
    )(*ins)


def _loss_head(h, tgt, *, name):
    s, d = h.shape
    ts = _tile(s, 256, 16)

    def body(h_ref, t_ref, dh_ref, loss_ref):
        err = h_ref[...] - t_ref[...]
        dh_ref[...] = err * (1.0 / d)
        part = 0.5 * jnp.sum(jnp.mean(err * err, axis=-1, keepdims=True), axis=0, keepdims=True)

        @pl.when(pl.program_id(0) == 0)
        def _():
            loss_ref[...] = part

        @pl.when(pl.program_id(0) > 0)
        def _():
            loss_ref[...] += part

    blk = pl.BlockSpec((ts, d), lambda i: (i, 0))
    return pl.pallas_call(
        body, name=name, grid=(s // ts,),
        in_specs=[blk, blk], out_specs=[blk, pl.BlockSpec((1, 1), lambda i: (0, 0))],
        out_shape=[jax.ShapeDtypeStruct((s, d), F32), jax.ShapeDtypeStruct((1, 1), F32)],
        compiler_params=_cparams(1),
    )(h, tgt)


def _conv(ext, w, bias):
    c = bias + _down(ext, 2) * w[0:1] + _down(ext, 1) * w[1:2] + ext * w[2:3]
    return c[CONV_HALO:]


def _act_specs(s, n2):
    n_strips = pl.cdiv(n2, LANES)
    u_spec = pl.BlockSpec((None, 2, s, LANES), lambda hh, j: (hh, 0, 0, j))
    cwg = pl.BlockSpec((None, 3, LANES), lambda hh, j: (hh, 0, j))
    cwv = pl.BlockSpec((None, 3, LANES), lambda hh, j: (hh + 2, 0, j))
    cbg = pl.BlockSpec((None, 1, LANES), lambda hh, j: (hh, 0, j))
    cbv = pl.BlockSpec((None, 1, LANES), lambda hh, j: (hh + 2, 0, j))
    return n_strips, u_spec, cwg, cwv, cbg, cbv


def _act_fwd(u, cw, cb, *, name, rows=64):
    _, _, s, n2 = u.shape
    n_strips, u_spec, cwg, cwv, cbg, cbv = _act_specs(s, n2)

    def body(u_ref, wg_ref, wv_ref, bg_ref, bv_ref, a_ref):
        wg, wv, bg, bv = wg_ref[...], wv_ref[...], bg_ref[...], bv_ref[...]

        def chunk(c, carry):
            r0 = pl.multiple_of(c * rows, rows)
            cg = _conv(_causal_ext(lambda r, n: u_ref[0, pl.ds(r, n), :].astype(F32), r0, rows, CONV_HALO), wg, bg)
            cv = _conv(_causal_ext(lambda r, n: u_ref[1, pl.ds(r, n), :].astype(F32), r0, rows, CONV_HALO), wv, bv)
            a_ref[pl.ds(r0, rows), :] = (cg * _sigmoid(cg) * cv).astype(BF16)
            return carry

        lax.fori_loop(0, s // rows, chunk, 0)

    return pl.pallas_call(
        body, name=name, grid=(2, n_strips),
        in_specs=[u_spec, cwg, cwv, cbg, cbv],
        out_specs=pl.BlockSpec((None, s, LANES), lambda hh, j: (hh, 0, j)),
        out_shape=jax.ShapeDtypeStruct((2, s, n2), BF16),
        compiler_params=_cparams(2),
    )(u, cw, cw, cb, cb)


def _act_bwd(u, da, cw, cb, *, name, rows=64):
    _, _, s, n2 = u.shape
    n_strips, u_spec, cwg, cwv, cbg, cbv = _act_specs(s, n2)
    n_chunks = s // rows

    def body(u_ref, da_ref, wg_ref, wv_ref, bg_ref, bv_ref, du_ref, dwg_ref, dwv_ref, dbg_ref, dbv_ref, dg_s, dv_s):
        wg, wv, bg, bv = wg_ref[...], wv_ref[...], bg_ref[...], bv_ref[...]

        def first(c, sums):
            r0 = pl.multiple_of(c * rows, rows)
            eg = _causal_ext(lambda r, n: u_ref[0, pl.ds(r, n), :].astype(F32), r0, rows, CONV_HALO)
            ev = _causal_ext(lambda r, n: u_ref[1, pl.ds(r, n), :].astype(F32), r0, rows, CONV_HALO)
            cg, cv = _conv(eg, wg, bg), _conv(ev, wv, bv)
            sg = _sigmoid(cg)
            dact = da_ref[pl.ds(r0, rows), :]
            dval = dact * (cg * sg)
            dgate = dact * cv * (sg * (1.0 + cg * (1.0 - sg)))
            dg_s[pl.ds(r0, rows), :] = dgate
            dv_s[pl.ds(r0, rows), :] = dval
            new = []
            for dc, ext in ((dgate, eg), (dval, ev)):
                new += [_fold8(dc * _down(ext, 2)[CONV_HALO:]), _fold8(dc * _down(ext, 1)[CONV_HALO:]),
                        _fold8(dc * ext[CONV_HALO:]), _fold8(dc)]
            return tuple(acc + x for acc, x in zip(sums, new))

        zero = jnp.zeros((8, LANES), F32)
        sums = lax.fori_loop(0, n_chunks, first, (zero,) * 8)
        red = [jnp.sum(x, axis=0, keepdims=True) for x in sums]
        dwg_ref[...] = jnp.concatenate(red[0:3], axis=0)
        dbg_ref[...] = red[3]
        dwv_ref[...] = jnp.concatenate(red[4:7], axis=0)
        dbv_ref[...] = red[7]

        def second(c, carry):
            r0 = pl.multiple_of(c * rows, rows)
            for gv, (src, w) in enumerate(((dg_s, wg), (dv_s, wv))):
                ext = _anti_ext(lambda r, n: src[pl.ds(r, n), :], r0, rows, CONV_HALO, s)
                du = ext * w[2:3] + _up(ext, 1) * w[1:2] + _up(ext, 2) * w[0:1]
                du_ref[gv, pl.ds(r0, rows), :] = du[:rows].astype(BF16)
            return carry

        lax.fori_loop(0, n_chunks, second, 0)

    w_out = pl.BlockSpec((None, 3, LANES), lambda hh, j: (hh, 0, j))
    b_out = pl.BlockSpec((None, 1, LANES), lambda hh, j: (hh, 0, j))
    return pl.pallas_call(
        body, name=name, grid=(2, n_strips),
        in_specs=[u_spec, pl.BlockSpec((None, s, LANES), lambda hh, j: (hh, 0, j)), cwg, cwv, cbg, cbv],
        out_specs=[u_spec, w_out, w_out, b_out, b_out],
        out_shape=[jax.ShapeDtypeStruct((2, 2, s, n2), BF16),
                   jax.ShapeDtypeStruct((2, 3, n2), F32), jax.ShapeDtypeStruct((2, 3, n2), F32),
                   jax.ShapeDtypeStruct((2, 1, n2), F32), jax.ShapeDtypeStruct((2, 1, n2), F32)],
        scratch_shapes=[pltpu.VMEM((s, LANES), F32), pltpu.VMEM((s, LANES), F32)],
        compiler_params=_cparams(2),
    )(u, da, cw, cw, cb, cb)


def _dot_nt(a, b):
    return lax.dot_general(a, b, _DIMS["nt"], preferred_element_type=F32)


def _dot_tn(a, b):
    return lax.dot_general(a, b, _DIMS["tn"], preferred_element_type=F32)


def _band_mask(b, ATTN_QROWS):
    qi = lax.broadcasted_iota(jnp.int32, (ATTN_QROWS, ATTN_QROWS + ATTN_BLOCK), 0)
    kj = lax.broadcasted_iota(jnp.int32, (ATTN_QROWS, ATTN_QROWS + ATTN_BLOCK), 1)
    band = jnp.logical_and(kj >= qi, kj <= qi + ATTN_BLOCK)
    return jnp.logical_and(band, jnp.logical_or(b > 0, kj >= ATTN_BLOCK))


def _to_residues(nat, rm, d, seq, pad):
    seg = seq + pad
    for r in range(d):
        if pad:
            rm[pl.ds(r * seg, pad), :] = jnp.zeros((pad, LANES), rm.dtype)
        rows = nat[pl.ds(r, seq, stride=d), :] if d > 1 else nat[...]
        rm[pl.ds(r * seg + pad, seq), :] = rows.astype(rm.dtype)


def _rows_loop(s, rows, fn):
    def step(c, carry):
        fn(pl.ds(pl.multiple_of(c * rows, rows), rows))
        return carry

    lax.fori_loop(0, s // rows, step, 0)


def _attn_fwd(qkv, *, name, comm=None):
    _, s, dm = qkv.shape
    heads, scale = dm // HEAD_DIM, 1.0 / math.sqrt(HEAD_DIM)
    pad_rows = s + ATTN_BLOCK * max(DILATIONS)
    ATTN_QROWS, ATTN_UNROLL = ATTN_FWD_TILE

    def body(q_ref, k_ref, v_ref, o_ref, ob_ref, lse_ref, nat, rq, rk, rv, ro, rl, o_tmp, l_tmp, o_acc, m_acc, s_acc):
        for d in DILATIONS:
            seq = s // d
            nb, seg = seq // ATTN_QROWS, seq + ATTN_BLOCK
            for src, dst, pad in ((q_ref, rq, 0), (k_ref, rk, ATTN_BLOCK), (v_ref, rv, ATTN_BLOCK)):
                if d == 1:
                    _to_residues(src, dst, d, seq, pad)
                else:
                    nat[...] = src[...].astype(F32)
                    _to_residues(nat, dst, d, seq, pad)

            def block(idx, carry):
                r, b = idx // nb, idx % nb
                qrows = pl.ds(pl.multiple_of(r * seq + b * ATTN_QROWS, ATTN_BLOCK), ATTN_QROWS)
                krows = pl.ds(pl.multiple_of(r * seg + b * ATTN_QROWS, ATTN_BLOCK), ATTN_QROWS + ATTN_BLOCK)
                sc = jnp.where(_band_mask(b, ATTN_QROWS),_dot_nt(rq[qrows, :], rk[krows, :]) * scale, NEG_INF)
                m = jnp.max(sc, axis=-1, keepdims=True)
                p = jnp.exp(sc - m)
                den = jnp.sum(p, axis=-1, keepdims=True)
                ro[qrows, :] = jnp.dot(p.astype(BF16), rv[krows, :], preferred_element_type=F32) / den
                rl[qrows, :] = jnp.broadcast_to(m + jnp.log(den), (ATTN_QROWS, LANES))
                return carry

            lax.fori_loop(0, d * nb, block, 0, unroll=ATTN_UNROLL)
            if d == 1:
                def first(rows):
                    o_acc[rows, :] = ro[rows, :]
                    m_acc[rows, :] = rl[rows, :]
                    s_acc[rows, :] = jnp.ones((rows.size, LANES), F32)

                _rows_loop(s, 64, first)
            else:
                for r in range(d):
                    o_tmp[pl.ds(r, seq, stride=d), :] = ro[pl.ds(r * seq, seq), :]
                    l_tmp[pl.ds(r, seq, stride=d), :] = rl[pl.ds(r * seq, seq), :]

                def merge(rows):
                    m_old, l_new = m_acc[rows, :], l_tmp[rows, :]
                    m_new = jnp.maximum(m_old, l_new)
                    w_old, w_new = jnp.exp(m_old - m_new), jnp.exp(l_new - m_new)
                    o_acc[rows, :] = o_acc[rows, :] * w_old + o_tmp[rows, :] * w_new
                    s_acc[rows, :] = s_acc[rows, :] * w_old + w_new
                    m_acc[rows, :] = m_new

                _rows_loop(s, 64, merge)

        def finish(rows):
            tot = s_acc[rows, :]
            o = o_acc[rows, :] / tot
            o_ref[rows, :] = o
            ob_ref[rows, :] = o.astype(BF16)
            lse_ref[rows, :] = m_acc[rows, :] + jnp.log(tot)

        _rows_loop(s, 64, finish)

    head = lambda i: pl.BlockSpec((None, s, HEAD_DIM), lambda h, i=i: (i, 0, h))
    out = pl.BlockSpec((s, HEAD_DIM), lambda h: (0, h))
    nat_f32 = pltpu.VMEM((s, LANES), F32)
    return _pallas(
        body, name=name, grid=(heads,),
        in_specs=[head(0), head(1), head(2)], out_specs=[out, out, out],
        out_shape=[jax.ShapeDtypeStruct((s, dm), F32), jax.ShapeDtypeStruct((s, dm), BF16),
                   jax.ShapeDtypeStruct((s, dm), F32)],
        scratch_shapes=[nat_f32, pltpu.VMEM((s, LANES), BF16), pltpu.VMEM((pad_rows, LANES), BF16),
                        pltpu.VMEM((pad_rows, LANES), BF16), nat_f32, nat_f32, nat_f32, nat_f32, nat_f32, nat_f32, nat_f32],
        args=[qkv, qkv, qkv], comm=comm)


def _attn_delta(do, o, *, name):
    s, dm = o.shape
    heads = dm // HEAD_DIM
    ts = _tile(s, 256, 16)

    def body(do_ref, o_ref, dl_ref):
        for h in range(heads):
            hs = slice(h * HEAD_DIM, (h + 1) * HEAD_DIM)
            row = jnp.sum(do_ref[:, hs].astype(F32) * o_ref[:, hs], axis=-1, keepdims=True)
            dl_ref[:, hs] = jnp.broadcast_to(row, (ts, HEAD_DIM))

    blk = pl.BlockSpec((ts, dm), lambda i: (i, 0))
    return pl.pallas_call(
        body, name=name, grid=(s // ts,),
        in_specs=[blk, blk], out_specs=blk,
        out_shape=jax.ShapeDtypeStruct((s, dm), F32),
        compiler_params=_cparams(1),
    )(do, o)


def _attn_bwd(qkv, do, lse, delta, *, name):
    _, s, dm = qkv.shape
    heads, scale = dm // HEAD_DIM, 1.0 / math.sqrt(HEAD_DIM)
    pad_rows = s + ATTN_BLOCK * max(DILATIONS)
    ATTN_QROWS, ATTN_UNROLL = ATTN_BWD_TILE

    def body(q_ref, k_ref, v_ref, do_ref, l_ref, dl_ref, out_ref,
             nat, rq, rdo, rk, rv, rl, rdl, rdq, kc, kp, vc, vp, aq, ak, av):
        for d in DILATIONS:
            seq = s // d
            nb, seg = seq // ATTN_QROWS, seq + ATTN_BLOCK
            for src, dst, pad in ((q_ref, rq, 0), (do_ref, rdo, 0), (k_ref, rk, ATTN_BLOCK), (v_ref, rv, ATTN_BLOCK)):
                if d == 1:
                    _to_residues(src, dst, d, seq, pad)
                else:
                    nat[...] = src[...].astype(F32)
                    _to_residues(nat, dst, d, seq, pad)
            if d == 1:
                lse_rows, dl_rows = l_ref, dl_ref
            else:
                lse_rows, dl_rows = rl, rdl
                _to_residues(l_ref, rl, d, seq, 0)
                _to_residues(dl_ref, rdl, d, seq, 0)
            def clear(rows):
                kp[rows, :] = jnp.zeros((rows.size, LANES), F32)
                vp[rows, :] = jnp.zeros((rows.size, LANES), F32)

            _rows_loop(d * seg, ATTN_BLOCK, clear)

            def block(idx, carry):
                r, b = idx // nb, idx % nb
                qrows = pl.ds(pl.multiple_of(r * seq + b * ATTN_QROWS, ATTN_BLOCK), ATTN_QROWS)
                krow = pl.multiple_of(r * seg + b * ATTN_QROWS, ATTN_BLOCK)
                krows = pl.ds(krow, ATTN_QROWS + ATTN_BLOCK)
                before, own = pl.ds(krow, ATTN_BLOCK), pl.ds(krow + ATTN_BLOCK, ATTN_QROWS)
                qb, dob, kw, vw = rq[qrows, :], rdo[qrows, :], rk[krows, :], rv[krows, :]
                lse_b = jnp.concatenate([lse_rows[qrows, :]] * (ATTN_QROWS // LANES + 1), axis=1)
                dl_b = jnp.concatenate([dl_rows[qrows, :]] * (ATTN_QROWS // LANES + 1), axis=1)
                sc = jnp.where(_band_mask(b, ATTN_QROWS),_dot_nt(qb, kw) * scale, NEG_INF)
                p = jnp.exp(sc - lse_b)
                ds = (p * (_dot_nt(dob, vw) - dl_b) * scale).astype(BF16)
                rdq[qrows, :] = jnp.dot(ds, kw, preferred_element_type=F32)
                dk, dv = _dot_tn(ds, qb), _dot_tn(p.astype(BF16), dob)
                kp[before, :] = dk[:ATTN_BLOCK]
                kc[own, :] = dk[ATTN_BLOCK:]
                vp[before, :] = dv[:ATTN_BLOCK]
                vc[own, :] = dv[ATTN_BLOCK:]
                return carry

            lax.fori_loop(0, d * nb, block, 0, unroll=ATTN_UNROLL)
            for r in range(d):
                keys = pl.ds(r * seg + ATTN_BLOCK, seq)
                rows = pl.ds(r, seq, stride=d) if d > 1 else pl.ds(0, seq)
                for acc, val in ((aq, rdq[pl.ds(r * seq, seq), :]), (ak, kc[keys, :] + kp[keys, :]),
                                 (av, vc[keys, :] + vp[keys, :])):
                    if d == DILATIONS[0]:
                        acc[rows, :] = val
                    else:
                        acc[rows, :] += val

        def finish(rows):
            for i, acc in enumerate((aq, ak, av)):
                out_ref[i, rows, :] = acc[rows, :].astype(BF16)

        _rows_loop(s, 256, finish)

    head = lambda i: pl.BlockSpec((None, s, HEAD_DIM), lambda h, i=i: (i, 0, h))
    col = pl.BlockSpec((s, HEAD_DIM), lambda h: (0, h))
    f32 = lambda rows: pltpu.VMEM((rows, LANES), F32)
    b16 = lambda rows: pltpu.VMEM((rows, LANES), BF16)
    return pl.pallas_call(
        body, name=name, grid=(heads,),
        in_specs=[head(0), head(1), head(2), col, col, col],
        out_specs=pl.BlockSpec((3, s, HEAD_DIM), lambda h: (0, 0, h)),
        out_shape=jax.ShapeDtypeStruct((3, s, dm), BF16),
        scratch_shapes=[f32(s), b16(s), b16(s), b16(pad_rows), b16(pad_rows), f32(s), f32(s),
                        f32(s), f32(pad_rows), f32(pad_rows), f32(pad_rows), f32(pad_rows), f32(s), f32(s), f32(s)],
        compiler_params=_cparams(1),
    )(qkv, qkv, qkv, do, lse, delta)


def _ew(fn, ins, out_dtypes, *, name, tile_bytes=1 << 20):
    first = ins[0][0] if isinstance(ins[0], tuple) else ins[0]
    rows, cols = first.shape[-2], first.shape[-1]
    tr = _tile(rows, max(16, tile_bytes // (4 * cols)), 16)
    n_in = len(ins)

    def body(*refs):
        outs = fn(*[r[...] for r in refs[:n_in]])
        for o_ref, val in zip(refs[n_in:], outs):
            o_ref[...] = val.astype(o_ref.dtype)

    in_specs, args = [], []
    for item in ins:
        if isinstance(item, tuple):
            arr, lead = item
            in_specs.append(pl.BlockSpec((None, tr, cols), lambda i, lead=lead: (lead, i, 0)))
            args.append(arr)
        else:
            in_specs.append(pl.BlockSpec((tr, cols), lambda i: (i, 0)))
            args.append(item)
    blk = pl.BlockSpec((tr, cols), lambda i: (i, 0))
    return pl.pallas_call(
        body, name=name, grid=(rows // tr,),
        in_specs=in_specs, out_specs=[blk] * len(out_dtypes),
        out_shape=[jax.ShapeDtypeStruct((rows, cols), dt) for dt in out_dtypes],
        compiler_params=_cparams(1),
    )(*args)


def _adamw_math(g, w, m, v):
    m2 = ADAM_B1 * m + (1.0 - ADAM_B1) * g
    v2 = ADAM_B2 * v + (1.0 - ADAM_B2) * (g * g)
    m_hat = m2 / (1.0 - ADAM_B1 ** ADAM_STEP)
    v_hat = v2 / (1.0 - ADAM_B2 ** ADAM_STEP)
    delta = -ADAM_LR * (m_hat / (jnp.sqrt(v_hat) + ADAM_EPS) + ADAM_WD * w)
    return g, delta, m2, v2


def _scalars(*vals):
    return jnp.stack([jnp.asarray(v, jnp.int32) for v in vals])


def _adamw_halves(mine, theirs, core, w, m, v, *, name, lead=0, prev=None):
    shape = w.shape
    a_n, rh, cols = mine.shape
    w3, m3, v3 = (t.reshape(-1, 2 * rh, cols) for t in (w, m, v))
    tr, tc = _tile2(rh, cols, 1 << 17, 8)
    n_i = rh // tr

    def body(c_ref, mine_ref, theirs_ref, w_ref, m_ref, v_ref, *rest):
        g = jnp.where(pl.program_id(1) == c_ref[0], mine_ref[...], theirs_ref[...])
        outs = _adamw_math(g, w_ref[...], m_ref[...], v_ref[...])
        for ref, val in zip(rest[-4:], outs):
            ref[...] = val

    def half(mine_rows):
        def index(a, h, i, j, c_ref):
            use = (h == c_ref[0]) if mine_rows else (h != c_ref[0])
            return (a, jnp.where(use, i, 0), jnp.where(use, j, 0))
        return pl.BlockSpec((None, tr, tc), index)

    full = pl.BlockSpec((None, tr, tc), lambda a, h, i, j, c_ref: (lead + a, h * n_i + i, j))
    args = [mine, theirs, w3, m3, v3]
    in_specs = [half(True), half(False), full, full, full]
    aliases = {}
    if prev is not None:
        args += [p.reshape(w3.shape) for p in prev]
        in_specs += [ANY] * 4
        aliases = {6 + k: k for k in range(4)}
    outs = pl.pallas_call(
        body, name=name,
        grid_spec=pltpu.PrefetchScalarGridSpec(
            num_scalar_prefetch=1, grid=(a_n, 2, n_i, cols // tc), in_specs=in_specs, out_specs=[full] * 4),
        out_shape=[jax.ShapeDtypeStruct(w3.shape, F32)] * 4,
        input_output_aliases=aliases,
        compiler_params=_cparams(4),
    )(_scalars(core), *args)
    return [o.reshape(shape) for o in outs]


def _cast_into(src, buf, lead, slot, *, name, buf_shape=None, dtype=BF16):
    a_n, rows, cols = src.shape
    tr = _tile(rows, max(16, (1 << 21) // (4 * cols)), 16)

    def body(slot_ref, src_ref, *rest):
        rest[-1][...] = src_ref[...].astype(rest[-1].dtype)

    in_specs = [pl.BlockSpec((None, tr, cols), lambda a, i, slot_ref: (a, i, 0))]
    args = [src]
    aliases = {}
    if buf is not None:
        in_specs.append(ANY)
        args.append(buf)
        aliases = {2: 0}
        buf_shape, dtype = buf.shape, buf.dtype
    return pl.pallas_call(
        body, name=name,
        grid_spec=pltpu.PrefetchScalarGridSpec(
            num_scalar_prefetch=1, grid=(a_n, rows // tr), in_specs=in_specs,
            out_specs=pl.BlockSpec((None, None, tr, cols), lambda a, i, slot_ref: (lead + a, slot_ref[0], i, 0))),
        out_shape=jax.ShapeDtypeStruct(buf_shape, dtype),
        input_output_aliases=aliases,
        compiler_params=_cparams(2),
    )(_scalars(slot), *args)


def _place():
    x, y, c = lax.axis_index("x"), lax.axis_index("y"), lax.axis_index("c")
    chips = [(1 - x, y), (x, 1 - y), (1 - x, 1 - y)]
    return x, y, c, chips


def _remote(src, dst, send_sem, recv_sem, dev):
    return pltpu.make_async_remote_copy(src_ref=src, dst_ref=dst, send_sem=send_sem, recv_sem=recv_sem,
                                        device_id=dev, device_id_type=MESH)


def _gather_comm(bufs, pieces):
    def plan(refs, send_sems, recv_sems):
        x, y, c, chips = _place()

        def region(p, q, core):
            t, a0, a1, part, n_parts = pieces[p]
            rh, cw = bufs[t].shape[2] // 2, bufs[t].shape[3] // n_parts
            return refs[t].at[pl.ds(a0, a1 - a0), q, pl.ds(core * rh, rh), pl.ds(part * cw, cw)]

        def ici(p, j, q):
            cx, cy = chips[j]
            return _remote(region(p, q, c), region(p, q, c), send_sems.at[6 * p + j], recv_sems.at[6 * p + j], (cx, cy, c))

        def d2d(p, j, core):
            cx, cy = chips[j]
            rows = region(p, 2 * cx + cy, core)
            return _remote(rows, rows, send_sems.at[6 * p + 3 + j], recv_sems.at[6 * p + 3 + j], (x, y, 1 - c))

        return 2 * x + y, c, [2 * cx + cy for cx, cy in chips], ici, d2d

    todo = [(j, p) for j in range(3) for p in range(len(pieces))]

    def start(refs, send_sems, recv_sems):
        q_me, _, _, ici, _ = plan(refs, send_sems, recv_sems)
        for j, p in todo:
            ici(p, j, q_me).start()

    def finish(refs, send_sems, recv_sems):
        q_me, c, q_of, ici, d2d = plan(refs, send_sems, recv_sems)
        for j, p in todo:
            ici(p, j, q_of[j]).wait_recv()
            d2d(p, j, c).start()
        for j, p in todo:
            d2d(p, j, 1 - c).wait_recv()
        for j, p in todo:
            ici(p, j, q_me).wait_send()
            d2d(p, j, c).wait_send()

    return _Comm(bufs, 6 * len(pieces), start, finish)


def _chips_comm(parts, lands):
    n = len(parts)

    def copy(refs, send_sems, recv_sems, t, j, q_src, q_dst):
        x, y, c, chips = _place()
        cx, cy = chips[j]
        return _remote(refs[t].at[:, q_src], refs[n + t].at[q_dst], send_sems.at[3 * t + j], recv_sems.at[3 * t + j],
                       (cx, cy, c))

    todo = [(j, t) for j in range(3) for t in range(n)]

    def qs():
        x, y, _, chips = _place()
        return 2 * x + y, [2 * cx + cy for cx, cy in chips]

    def start(refs, send_sems, recv_sems):
        q_me, q_of = qs()
        for j, t in todo:
            copy(refs, send_sems, recv_sems, t, j, q_of[j], q_me).start()

    def finish(refs, send_sems, recv_sems):
        q_me, q_of = qs()
        for j, t in todo:
            copy(refs, send_sems, recv_sems, t, j, q_me, q_of[j]).wait_recv()
        for j, t in todo:
            copy(refs, send_sems, recv_sems, t, j, q_of[j], q_me).wait_send()

    return _Comm(list(parts) + list(lands), 3 * n, start, finish)


def _comm_call(comm, *, name):
    k = len(comm.operands)

    def body(*refs):
        operands, (send_sems, recv_sems) = refs[k:2 * k], refs[2 * k:]
        comm.start(operands, send_sems, recv_sems)
        comm.finish(operands, send_sems, recv_sems)

    return pl.pallas_call(
        body, name=name, in_specs=[ANY] * k, out_specs=[ANY] * k,
        out_shape=[jax.ShapeDtypeStruct(t.shape, t.dtype) for t in comm.operands],
        input_output_aliases={i: i for i in range(k)},
        scratch_shapes=[pltpu.SemaphoreType.DMA((comm.n_sems,)), pltpu.SemaphoreType.DMA((comm.n_sems,))],
    )(*comm.operands)


def _rs_sibling(grads, *, name):
    n = len(grads)
    halves = [jax.ShapeDtypeStruct(g.shape[:2] + (g.shape[2] // 2, g.shape[3]), g.dtype) for g in grads]

    def body(*refs):
        src, theirs = refs[:n], refs[n:2 * n]
        send_sems, recv_sems = refs[2 * n:]
        x, y, c, _ = _place()
        ops = []
        for t in range(n):
            rh = grads[t].shape[2] // 2
            give = _remote(src[t].at[:, :, pl.ds((1 - c) * rh, rh), :], theirs[t], send_sems.at[t], recv_sems.at[t],
                           (x, y, 1 - c))
            give.start()
            ops.append(give)
        for op in ops:
            op.wait()

    return pl.pallas_call(
        body, name=name, in_specs=[ANY] * n, out_specs=[ANY] * n, out_shape=halves,
        scratch_shapes=[pltpu.SemaphoreType.DMA((n,)), pltpu.SemaphoreType.DMA((n,))],
    )(*grads)


def _rs_add(grad, theirs, core, slot, *, name):
    a_n, _, rh, cols = theirs.shape
    tr, tc = _tile2(rh, cols, 1 << 18, 16)
    n_i = rh // tr

    def body(s_ref, g_ref, t_ref, p_ref, y_ref):
        part = (g_ref[...].astype(F32) + t_ref[...].astype(F32)).astype(BF16)
        p_ref[...] = part

        @pl.when(pl.program_id(3) == s_ref[1])
        def _():
            y_ref[...] = part

    blk = (None, None, tr, tc)
    return pl.pallas_call(
        body, name=name,
        grid_spec=pltpu.PrefetchScalarGridSpec(
            num_scalar_prefetch=1, grid=(a_n, n_i, cols // tc, N_CHIPS),
            in_specs=[pl.BlockSpec(blk, lambda a, i, j, q, s: (a, q, s[0] * n_i + i, j)),
                      pl.BlockSpec(blk, lambda a, i, j, q, s: (a, q, i, j))],
            out_specs=[pl.BlockSpec(blk, lambda a, i, j, q, s: (a, q, i, j)),
                       pl.BlockSpec(blk, lambda a, i, j, q, s: (s[1], a, i, j))]),
        out_shape=[jax.ShapeDtypeStruct(theirs.shape, BF16),
                   jax.ShapeDtypeStruct((N_CHIPS, a_n, rh, cols), BF16)],
        compiler_params=_cparams(4),
    )(_scalars(core, slot), grad, theirs)


def _rs_finish(halves, *, name):
    n = len(halves)

    def body(*refs):
        src, dst = refs[:n], refs[n:2 * n]
        send_sems, recv_sems = refs[2 * n:]
        x, y, c, _ = _place()
        ops = []
        for t in range(n):
            give = _remote(src[t], dst[t], send_sems.at[t], recv_sems.at[t], (x, y, 1 - c))
            give.start()
            ops.append(give)
        for op in ops:
            op.wait()

    return pl.pallas_call(
        body, name=name, in_specs=[ANY] * n, out_specs=[ANY] * n,
        out_shape=[jax.ShapeDtypeStruct(h.shape, h.dtype) for h in halves],
        scratch_shapes=[pltpu.SemaphoreType.DMA((n,)), pltpu.SemaphoreType.DMA((n,))],
    )(*halves)


def _all_reduce_small(vec, *, name):
    rows = vec.shape[0]

    def body(v_ref, o_ref, land, send_sems, recv_sems):
        x, y, c, _ = _place()
        me = 4 * x + 2 * y + c
        land[me] = v_ref[...]
        flips = [(fx, fy, fc) for fx in (0, 1) for fy in (0, 1) for fc in (0, 1)][1:]
        sent = []
        for k, (fx, fy, fc) in enumerate(flips):
            cp = _remote(v_ref, land.at[me], send_sems.at[k], recv_sems.at[k], (x ^ fx, y ^ fy, c ^ fc))
            cp.start()
            sent.append(cp)
        for k, (fx, fy, fc) in enumerate(flips):
            peer = 4 * (x ^ fx) + 2 * (y ^ fy) + (c ^ fc)
            _remote(v_ref, land.at[peer], send_sems.at[k], recv_sems.at[k], (x ^ fx, y ^ fy, c ^ fc)).wait_recv()
        for cp in sent:
            cp.wait_send()
        total = land[0]
        for dev in range(1, 8):
            total = total + land[dev]
        o_ref[...] = total

    whole = pl.BlockSpec(memory_space=pltpu.VMEM)
    return pl.pallas_call(
        body, name=name, in_specs=[whole], out_specs=whole,
        out_shape=jax.ShapeDtypeStruct(vec.shape, F32),
        scratch_shapes=[pltpu.VMEM((8, rows, LANES), F32), pltpu.SemaphoreType.DMA((7,)), pltpu.SemaphoreType.DMA((7,))],
        compiler_params=pltpu.CompilerParams(vmem_limit_bytes=VMEM_LIMIT),
    )(vec)


def _pack(parts, mult=16):
    flat = jnp.concatenate([p.reshape(-1).astype(F32) for p in parts])
    rows = -(-flat.shape[0] // (LANES * mult)) * mult
    return jnp.pad(flat, (0, rows * LANES - flat.shape[0])).reshape(rows, LANES)


def _unpack(vec, shapes):
    flat, out, pos = vec.reshape(-1), [], 0
    for shp in shapes:
        size = math.prod(shp)
        out.append(flat[pos:pos + size].reshape(shp))
        pos += size
    return out


def kernel(x, pool_w_in, pool_w_grp, pool_scale, pool_w_out, attn_w_q, attn_w_o, shared_w_k, shared_w_v, ffn_w_up, ffn_conv_w, ffn_conv_b, ffn_w_down, ln1_g, ln1_b, ln2_g, ln2_b, loss_target, m_pool_w_in, m_pool_w_grp, m_pool_scale, m_pool_w_out, m_attn_w_q, m_attn_w_o, m_shared_w_k, m_shared_w_v, m_ffn_w_up, m_ffn_conv_w, m_ffn_conv_b, m_ffn_w_down, m_ln1_g, m_ln1_b, m_ln2_g, m_ln2_b, v_pool_w_in, v_pool_w_grp, v_pool_scale, v_pool_w_out, v_attn_w_q, v_attn_w_o, v_shared_w_k, v_shared_w_v, v_ffn_w_up, v_ffn_conv_w, v_ffn_conv_b, v_ffn_w_down, v_ln1_g, v_ln1_b, v_ln2_g, v_ln2_b):
    s, d = x.shape[1], x.shape[2]
    n2 = ffn_w_up.shape[2]
    fq = ffn_w_down.shape[1]
    assert 2 * fq == n2 and d % N_CHIPS == 0
    g_n, cg = pool_w_grp.shape[1], pool_w_grp.shape[3]
    xs, tgt = x[0], loss_target[0]
    q_me = 2 * lax.axis_index("x") + lax.axis_index("y")
    core = lax.axis_index("c")
    rq = d // N_CHIPS

    six_g = None
    for i, w_ in enumerate((pool_w_in[0], pool_w_out[0], attn_w_q[0], shared_w_k, shared_w_v, attn_w_o[0])):
        six_g = _cast_into(w_[None], six_g, i, q_me, name=f"cast_w{i}", buf_shape=(6, N_CHIPS, rq, d))
    grp_g = _cast_into(pool_w_grp[0], None, 0, q_me, name="cast_grp", buf_shape=(g_n, N_CHIPS, cg // N_CHIPS, cg))
    up_t, m_up_t, v_up_t = (jnp.swapaxes(t, 1, 2) for t in (ffn_w_up, m_ffn_w_up, v_ffn_w_up))
    up_g = _cast_into(up_t, None, 0, q_me, name="cast_up", buf_shape=(2, N_CHIPS, n2, d))
    dn_g = _cast_into(ffn_w_down, None, 0, q_me, name="cast_down", buf_shape=(2, N_CHIPS, fq, d))
    small = _pack([ffn_conv_w, pool_scale])
    small_g = _cast_into(small[None], None, 0, q_me, name="cast_small", buf_shape=(1, N_CHIPS) + small.shape, dtype=F32)
    bufs = [six_g, grp_g, up_g, dn_g, small_g]
    SIX, GRP, UP, DN, SMALL = range(5)
    first = [SIX, GRP, SMALL]
    landed = _comm_call(_gather_comm([bufs[t] for t in first], [(0, IW_IN, IW_OUT + 1, 0, 1), (1, 0, g_n, 0, 1),
                                                                (2, 0, 1, 0, 1)]), name="gather_first")
    for t, arr in zip(first, landed):
        bufs[t] = arr
    small_q = bufs[SMALL].reshape(N_CHIPS, -1)
    n_cw = 2 * 3 * n2
    conv_w = small_q[:, :n_cw].reshape(N_CHIPS, 2, 3, n2).transpose(1, 0, 2, 3)
    scale_full = small_q[:, n_cw:n_cw + d // N_CHIPS].reshape(1, d)
    conv_b = ffn_conv_b.reshape(2, N_CHIPS, 1, n2)

    def w6():
        return bufs[SIX].reshape(6, d, d)

    def up8():
        return bufs[UP].reshape(2 * N_CHIPS, n2, d)

    def dn4():
        return bufs[DN].reshape(4, n2, d)

    def gathered(pieces):
        used = sorted({pc[0] for pc in pieces})
        local = [(used.index(t), a0, a1, part, n_parts) for t, a0, a1, part, n_parts in pieces]
        return _gather_comm([bufs[t] for t in used], local), used

    def store(used, operands):
        for t, arr in zip(used, operands):
            bufs[t] = arr

    def ffn_fwd(l, hb, carry_up, carry_down):
        comm, used = gathered(carry_up)
        u, landed = _mm(hb[None], up8(), mode="nt", name=f"ffn{l}_up", out_dtype=BF16, n_q=4, tm=_tile(s, 512), tn=n2,
                        tk=d, qa=_q0, qb=lambda q, qr: 4 * l + _perm(q), comm=comm)
        store(used, landed)
        u = u.reshape(2, 2, s, n2)
        act = _act_fwd(u, conv_w[l], conv_b[l], name=f"ffn{l}_act")
        comm, used = gathered(carry_down) if carry_down else (None, None)
        ff = _mm(act, dn4(), mode="nn", name=f"ffn{l}_down", out_dtype=F32, n_q=1, n_qr=2, tm=_tile(s, 1024),
                 tn=_tile(d, 1024), tk=n2, qa=_qr, qb=lambda q, qr: 2 * l + qr, qo=lambda q: 0, comm=comm)
        if comm is not None:
            ff, landed = ff
            store(used, landed)
        return u, act, ff[0]

    def up0_parts(*parts):
        return gathered([(UP, 0, 1, part, 8) for part in parts])

    comm, used = up0_parts(0, 1)
    p, landed = _mm_dd(xs[None], w6(), IW_IN, mode="nn", name="pool_in", out_dtype=F32, comm=comm)
    store(used, landed)
    comm, used = up0_parts(2)
    pooled, landed = _pool(p[0], backward=False, name="pool_fwd", comm=comm)
    store(used, landed)
    comm, used = up0_parts(3)
    (mg, mixed), landed = _grp_fwd(pooled, bufs[GRP].reshape(g_n, cg, cg), scale_full, name="pool_grp", comm=comm)
    store(used, landed)
    comm, used = up0_parts(4, 5)
    mix0, landed = _mm_dd(mixed[None], w6(), IW_OUT, mode="nn", name="pool_out", out_dtype=F32, comm=comm)
    store(used, landed)
    comm, used = up0_parts(6, 7)
    (h1, h1b, xh1, rs1), landed = _ln_fwd(xs, mix0[0], ln1_g[0:1], ln1_b[0:1], name="ln1_0", comm=comm)
    store(used, landed)
    u0, act0, ff0 = ffn_fwd(0, h1b, [(DN, 0, 1, 0, 1), (SIX, IW_Q, IW_V + 1, 0, 1)], [(SIX, IW_O, IW_O + 1, 0, 1)])
    h2, h2b, xh2, rs2 = _ln_fwd(h1, ff0, ln2_g[0:1], ln2_b[0:1], name="ln2_0")

    qkv = _mm_dd(h2b[None], w6(), IW_Q, mode="nn", name="attn_qkv", out_dtype=BF16, n_q=3)
    comm, used = gathered([(UP, 1, 2, 0, 1)])
    (o, ob, lse), landed = _attn_fwd(qkv, name="attn_fwd", comm=comm)
    store(used, landed)
    mix1 = _mm_dd(ob[None], w6(), IW_O, mode="nn", name="attn_out", out_dtype=F32)[0]
    h3, h3b, xh3, rs3 = _ln_fwd(h2, mix1, ln1_g[1:2], ln1_b[1:2], name="ln1_1")
    u1, act1, ff1 = ffn_fwd(1, h3b, [(DN, 1, 2, 0, 1)], None)
    h4, _, xh4, rs4 = _ln_fwd(h3, ff1, ln2_g[1:2], ln2_b[1:2], name="ln2_1")
    dh4, loss_local = _loss_head(h4, tgt, name="loss_head")

    rs_parts, rs_lands = {}, {}

    def rs_prepare(items):
        theirs = _rs_sibling([g_ for _, g_ in items], name="rs_sibling_" + items[0][0])
        for (nm, g_), t_ in zip(items, theirs):
            rs_parts[nm], rs_lands[nm] = _rs_add(g_, t_, core, q_me, name=f"rs_add_{nm}")

    def rs_exchange(names):
        return _chips_comm([rs_parts[nm] for nm in names], [rs_lands[nm] for nm in names])

    def rs_landed(names, operands):
        for nm, land in zip(names, operands[len(names):]):
            rs_lands[nm] = land

    d_conv_w, d_conv_b = [None, None], [None, None]

    def ffn_bwd(l, dzb, u, act, hb, carry_dact):
        da = _mm(dzb[None], dn4(), mode="nt", name=f"ffn{l}_dact", out_dtype=F32, n_q=2, tm=_tile(s, 512), tn=n2,
                 tk=d, qa=_q0, qb=lambda q, qr: 2 * l + q,
                 comm=rs_exchange(carry_dact) if carry_dact else None)
        if carry_dact:
            da, landed = da
            rs_landed(carry_dact, landed)
        du, dwg, dwv, dbg, dbv = _act_bwd(u, da, conv_w[l], conv_b[l], name=f"ffn{l}_dconv")
        d_conv_w[l] = jnp.concatenate([dwg, dwv], axis=0)
        d_conv_b[l] = jnp.concatenate([dbg, dbv], axis=0)
        du4 = du.reshape(4, s, n2)
        g_dn = _mm(act, dzb[None], mode="tn", name=f"ffn{l}_gdown", out_dtype=BF16, n_q=2, tm=n2, tn=_tile(d, 512),
                   tk=_tile(s, 2048), qa=_qq, qb=_q0)
        rs_prepare([(f"dn{l}", g_dn.reshape(1, N_CHIPS, fq, d))])
        g_up, landed = _mm(du4, hb[None], mode="tn", name=f"ffn{l}_gup", out_dtype=BF16, n_q=4, tm=n2, tn=_tile(d, 512),
                           tk=_tile(s, 2048), qa=lambda q, qr: _perm(q), qb=_q0, comm=rs_exchange([f"dn{l}"]))
        rs_landed([f"dn{l}"], landed)
        rs_prepare([(f"up{l}", g_up.reshape(1, N_CHIPS, n2, d))])
        dh, landed = _mm(du4, up8(), mode="nn", name=f"ffn{l}_dh", out_dtype=F32, n_q=1, n_qr=4, tm=_tile(s, 1024),
                         tn=_tile(d, 1024), tk=n2, qa=_qr, qb=lambda q, qr: 4 * l + _perm(qr), qo=lambda q: 0,
                         comm=rs_exchange([f"up{l}"]))
        rs_landed([f"up{l}"], landed)
        return dh[0]

    dz4, dz4b, dg_ln2_1, db_ln2_1 = _ln_bwd(None, dh4, xh4, rs4, ln2_g[1:2], name="dln2_1")
    dh3 = ffn_bwd(1, dz4b, u1, act1, h3b, None)
    dz3, dz3b, dg_ln1_1, db_ln1_1 = _ln_bwd(dz4, dh3, xh3, rs3, ln1_g[1:2], name="dln1_1")
    g_wo = _mm_wgrad(ob[None], dz3b[None], name="attn_gwo")
    do = _mm_dd(dz3b[None], w6(), IW_O, mode="nt", name="attn_do", out_dtype=BF16)[0]
    delta = _attn_delta(do, o, name="attn_delta")
    dqkv = _attn_bwd(qkv, do, lse, delta, name="attn_bwd")
    g_wq = _mm_wgrad(h2b[None], dqkv, name="attn_gwq", qb=lambda q, qr: 0)
    g_wk = _mm_wgrad(h2b[None], dqkv, name="attn_gwk", qb=lambda q, qr: 1)
    g_wv = _mm_wgrad(h2b[None], dqkv, name="attn_gwv", qb=lambda q, qr: 2)
    rs_prepare([(nm, g_.reshape(1, N_CHIPS, rq, d)) for nm, g_ in
                (("wo", g_wo), ("wq", g_wq), ("wk", g_wk), ("wv", g_wv))])
    dh2, landed = _mm_dd(dqkv, w6(), IW_Q, mode="nt", name="attn_dh", out_dtype=F32, n_qr=3, qa=_qr,
                         comm=rs_exchange(["wo", "wq", "wk"]))
    rs_landed(["wo", "wq", "wk"], landed)
    dz2, dz2b, dg_ln2_0, db_ln2_0 = _ln_bwd(dz3, dh2[0], xh2, rs2, ln2_g[0:1], name="dln2_0")
    dh1 = ffn_bwd(0, dz2b, u0, act0, h1b, ["wv"])
    dz1, dz1b, dg_ln1_0, db_ln1_0 = _ln_bwd(dz2, dh1, xh1, rs1, ln1_g[0:1], name="dln1_0")
    g_wout = _mm_wgrad(mixed[None], dz1b[None], name="pool_gwout")
    rs_prepare([("wout", g_wout.reshape(1, N_CHIPS, rq, d))])
    dmixed, landed = _mm_dd(dz1b[None], w6(), IW_OUT, mode="nt", name="pool_dmixed", out_dtype=F32,
                            comm=rs_exchange(["wout"]))
    rs_landed(["wout"], landed)
    dmg, d_scale = _grp_bwd_pre(dmixed[0], mg, scale_full, name="pool_dscale")
    g_wgrp = _grp_mm(pooled, dmg, mode="tn", name="pool_gwgrp", out_dtype=BF16, tm=1024)
    dpooled = _grp_mm(dmg, None, mode="nt", name="pool_dpooled", out_dtype=F32, tm=1024, w=bufs[GRP].reshape(g_n, cg, cg))
    dp = _pool(dpooled, backward=True, name="pool_bwd")
    g_win = _mm_wgrad(xs[None], dp[None], name="pool_gwin")
    last = ["win", "wgrp"]
    rs_prepare([("win", g_win.reshape(1, N_CHIPS, rq, d)), ("wgrp", g_wgrp.reshape(g_n, N_CHIPS, cg // N_CHIPS, cg))])
    dx_mm, landed = _mm_dd(dp[None], w6(), IW_IN, mode="nt", name="pool_dx", out_dtype=F32, comm=rs_exchange(last))
    rs_landed(last, landed)
    (grad_x,) = _ew(lambda a, b: (ALPHA * a + b,), [dz1, dx_mm[0]], [F32], name="grad_x")

    rs_names = ["win", "wgrp", "wout", "wq", "wo", "wk", "wv", "up0", "up1", "dn0", "dn1"]
    finished = {}
    for nm in rs_names:
        y4 = rs_lands[nm]
        y3 = y4.reshape(N_CHIPS, -1, y4.shape[-1])
        (tot,) = _ew(lambda a0, a1, a2, a3: (((a0.astype(F32) + a1.astype(F32)) + a2.astype(F32)) + a3.astype(F32),),
                     [(y3, 0), (y3, 1), (y3, 2), (y3, 3)], [F32], name=f"rs_sum_{nm}")
        finished[nm] = tot.reshape(y4.shape[1:])
    others = dict(zip(rs_names, _rs_finish([finished[nm] for nm in rs_names], name="rs_finish")))
    results = {}
    for nm, key, w, m, v in (("pool_w_in", "win", pool_w_in, m_pool_w_in, v_pool_w_in),
                             ("pool_w_grp", "wgrp", pool_w_grp, m_pool_w_grp, v_pool_w_grp),
                             ("pool_w_out", "wout", pool_w_out, m_pool_w_out, v_pool_w_out),
                             ("attn_w_q", "wq", attn_w_q, m_attn_w_q, v_attn_w_q),
                             ("attn_w_o", "wo", attn_w_o, m_attn_w_o, v_attn_w_o),
                             ("shared_w_k", "wk", shared_w_k, m_shared_w_k, v_shared_w_k),
                             ("shared_w_v", "wv", shared_w_v, m_shared_w_v, v_shared_w_v)):
        results[nm] = _adamw_halves(finished[key], others[key], core, w, m, v, name=f"adamw_{nm}")
    for nm, key, w, m, v in (("ffn_w_up", "up", up_t, m_up_t, v_up_t),
                             ("ffn_w_down", "dn", ffn_w_down, m_ffn_w_down, v_ffn_w_down)):
        res = None
        for l in (1, 0):
            res = _adamw_halves(finished[f"{key}{l}"], others[f"{key}{l}"], core, w, m, v, name=f"adamw_{nm}{l}",
                                lead=l, prev=res)
        results[nm] = res
    results["ffn_w_up"] = [jnp.swapaxes(t, 1, 2) for t in results["ffn_w_up"]]

    ln_grads = [jnp.concatenate([a, b], axis=0) for a, b in
                ((dg_ln1_0, dg_ln1_1), (db_ln1_0, db_ln1_1), (dg_ln2_0, dg_ln2_1), (db_ln2_0, db_ln2_1))]
    small_shapes = [(2, N_CHIPS, 3, n2), (2, N_CHIPS, n2)] + [(2, d)] * 4 + [(1, d)]
    vec = _pack([jnp.stack(d_conv_w), jnp.stack(d_conv_b)] + ln_grads + [d_scale], mult=8)
    tot = _unpack(_all_reduce_small(vec, name="allreduce_small"), small_shapes)
    g_cw = lax.dynamic_index_in_dim(tot[0], q_me, axis=1, keepdims=False)
    g_cb = tot[1].reshape(2, N_CHIPS * n2)
    g_scale = lax.dynamic_slice_in_dim(tot[6], q_me * rq, rq, axis=1)
    small_names = ["ffn_conv_w", "ffn_conv_b", "ln1_g", "ln1_b", "ln2_g", "ln2_b", "pool_scale"]
    small_g = [g_cw, g_cb, tot[2], tot[3], tot[4], tot[5], g_scale]
    small_w = [ffn_conv_w, ffn_conv_b, ln1_g, ln1_b, ln2_g, ln2_b, pool_scale]
    small_m = [m_ffn_conv_w, m_ffn_conv_b, m_ln1_g, m_ln1_b, m_ln2_g, m_ln2_b, m_pool_scale]
    small_v = [v_ffn_conv_w, v_ffn_conv_b, v_ln1_g, v_ln1_b, v_ln2_g, v_ln2_b, v_pool_scale]
    packed = _ew(_adamw_math, [_pack(small_g, 8), _pack(small_w, 8), _pack(small_m, 8), _pack(small_v, 8)], [F32] * 4,
                 name="adamw_small")
    shapes = [w.shape for w in small_w]
    unpacked = [_unpack(pk, shapes) for pk in packed]
    for i, nm in enumerate(small_names):
        results[nm] = [unpacked[k][i] for k in range(4)]

    loss = lax.psum(loss_local[0, 0], ("x", "y", "c"))
    order = ["pool_w_in", "pool_w_grp", "pool_scale", "pool_w_out", "attn_w_q", "attn_w_o", "shared_w_k", "shared_w_v",
             "ffn_w_up", "ffn_conv_w", "ffn_conv_b", "ffn_w_down", "ln1_g", "ln1_b", "ln2_g", "ln2_b"]
    outs = [loss, grad_x[None]]
    for k in range(4):
        outs += [results[nm][k] for nm in order]
    return tuple(outs)
```

```python
import functools
import math

import jax
import jax.numpy as jnp
from jax import lax
from jax.experimental import pallas as pl
from jax.experimental.pallas import tpu as pltpu

F32 = jnp.float32
BF16 = jnp.bfloat16

LANES = 128
HEAD_DIM = 128
ATTN_BLOCK = 128
DILATIONS = (1, 4, 16)
ATTN_FWD_TILE = (128, 8)
ATTN_BWD_TILE = (256, 4)
POOL_WINDOWS = (2, 4, 8, 16)
POOL_HALO = 16
CONV_HALO = 16
DEPTH = 2
ALPHA = (2.0 * DEPTH) ** 0.25
LN_EPS = 1e-5
NEG_INF = -1e30
ADAM_LR = 0.001
ADAM_B1 = 0.9
ADAM_B2 = 0.999
ADAM_EPS = 1e-08
ADAM_WD = 0.01
ADAM_STEP = 10
N_CHIPS = 4
VMEM_LIMIT = 56 * 1024 * 1024
ANY = pl.BlockSpec(memory_space=pl.ANY)
MESH = pl.DeviceIdType.MESH

IW_IN, IW_OUT, IW_Q, IW_K, IW_V, IW_O = range(6)


def _cparams(n_grid):
    return pltpu.CompilerParams(dimension_semantics=("arbitrary",) * n_grid, vmem_limit_bytes=VMEM_LIMIT)


def _tile(dim, pref, align=LANES):
    if dim <= pref:
        return dim
    t = (pref // align) * align
    while t >= align:
        if dim % t == 0:
            return t
        t -= align
    return dim


def _tile2(rows, cols, budget, row_align):
    best = None
    for tc in [cols] + [c for c in range(LANES, cols, LANES) if cols % c == 0]:
        for tr in range(row_align, rows + 1, row_align):
            if rows % tr == 0 and tr * tc <= budget:
                if best is None or (tr * tc, tc) > (best[0] * best[1], best[1]):
                    best = (tr, tc)
    return best if best is not None else (rows, cols)


def _perm(q):
    return (q % 2) * 2 + q // 2


_DIMS = {"nn": (((1,), (0,)), ((), ())), "nt": (((1,), (1,)), ((), ())), "tn": (((0,), (0,)), ((), ()))}


class _Comm:
    def __init__(self, operands, n_sems, start, finish):
        self.operands, self.n_sems, self.start, self.finish = list(operands), n_sems, start, finish


def _pallas(body, *, name, grid, in_specs, out_specs, out_shape, args, scratch_shapes=(), comm=None):
    n_in, n_out = len(args), len(out_shape)
    if comm is None:
        return pl.pallas_call(body, name=name, grid=grid, in_specs=list(in_specs), out_specs=list(out_specs),
                              out_shape=list(out_shape), scratch_shapes=list(scratch_shapes),
                              compiler_params=_cparams(len(grid)))(*args)
    k = len(comm.operands)

    def carried(*refs):
        ins, outs = refs[:n_in], refs[n_in + k:n_in + k + n_out]
        operands = refs[n_in + k + n_out:n_in + 2 * k + n_out]
        scratch, (send_sems, recv_sems) = refs[n_in + 2 * k + n_out:-2], refs[-2:]
        ids = [pl.program_id(ax) for ax in range(len(grid))]
        first = functools.reduce(jnp.logical_and, [i == 0 for i in ids])
        last = functools.reduce(jnp.logical_and, [i == g - 1 for i, g in zip(ids, grid)])

        @pl.when(first)
        def _():
            comm.start(operands, send_sems, recv_sems)

        body(*ins, *outs, *scratch)

        @pl.when(last)
        def _():
            comm.finish(operands, send_sems, recv_sems)

    res = pl.pallas_call(
        carried, name=name, grid=grid, in_specs=list(in_specs) + [ANY] * k, out_specs=list(out_specs) + [ANY] * k,
        out_shape=list(out_shape) + [jax.ShapeDtypeStruct(t.shape, t.dtype) for t in comm.operands],
        scratch_shapes=list(scratch_shapes) + [pltpu.SemaphoreType.DMA((comm.n_sems,))] * 2,
        input_output_aliases={n_in + i: n_out + i for i in range(k)},
        compiler_params=_cparams(len(grid)),
    )(*args, *comm.operands)
    return res[:n_out], res[n_out:]


def _mm(a, b, *, mode, name, out_dtype, n_q, tm, tn, tk, qa, qb, qo=lambda q: q, n_qr=1, out_q=None, comm=None):
    if mode == "nn":
        m, kdim, n = a.shape[1], a.shape[2], b.shape[2]
    elif mode == "nt":
        m, kdim, n = a.shape[1], a.shape[2], b.shape[1]
    else:
        kdim, m, n = a.shape[1], a.shape[2], b.shape[2]
    assert m % tm == 0 and n % tn == 0 and kdim % tk == 0, (name, m, n, kdim, tm, tn, tk)
    kr_n = kdim // tk
    nr = n_qr * kr_n
    out_q = n_q if out_q is None else out_q

    def split(r):
        return (r // kr_n, r % kr_n) if n_qr > 1 else (0, r)

    if mode == "tn":
        a_spec = pl.BlockSpec((None, tk, tm), lambda q, i, j, r: (qa(q, split(r)[0]), split(r)[1], i))
    else:
        a_spec = pl.BlockSpec((None, tm, tk), lambda q, i, j, r: (qa(q, split(r)[0]), i, split(r)[1]))
    if mode == "nt":
        b_spec = pl.BlockSpec((None, tn, tk), lambda q, i, j, r: (qb(q, split(r)[0]), j, split(r)[1]))
    else:
        b_spec = pl.BlockSpec((None, tk, tn), lambda q, i, j, r: (qb(q, split(r)[0]), split(r)[1], j))
    o_spec = pl.BlockSpec((None, tm, tn), lambda q, i, j, r: (qo(q), i, j))
    dims = _DIMS[mode]

    n_comm = len(comm.operands) if comm is not None else 0
    grid = (n_q, m // tm, n // tn, nr)

    def body(*refs):
        a_ref, b_ref = refs[0], refs[1]
        o_ref = refs[2 + n_comm]
        if comm is not None:
            comm_refs = refs[3 + n_comm:3 + 2 * n_comm]
            send_sems, recv_sems = refs[-2:]
            ids = [pl.program_id(ax) for ax in range(4)]
            first = functools.reduce(jnp.logical_and, [i == 0 for i in ids])
            last = functools.reduce(jnp.logical_and, [i == g - 1 for i, g in zip(ids, grid)])

            @pl.when(first)
            def _():
                comm.start(comm_refs, send_sems, recv_sems)
        lhs, rhs = a_ref[...], b_ref[...]
        if lhs.dtype != BF16:
            lhs = lhs.astype(BF16)
        if rhs.dtype != BF16:
            rhs = rhs.astype(BF16)
        part = lax.dot_general(lhs, rhs, dims, preferred_element_type=F32)
        if nr == 1:
            o_ref[...] = part.astype(o_ref.dtype)
        else:
            acc_ref = refs[3 + 2 * n_comm]
            r = pl.program_id(3)

            @pl.when(r == 0)
            def _():
                acc_ref[...] = part

            @pl.when(r > 0)
            def _():
                acc_ref[...] += part

            @pl.when(r == nr - 1)
            def _():
                o_ref[...] = acc_ref[...].astype(o_ref.dtype)
        if comm is not None:
            @pl.when(last)
            def _():
                comm.finish(comm_refs, send_sems, recv_sems)

    scratch = [pltpu.VMEM((tm, tn), F32)] if nr > 1 else []
    out_shape = [jax.ShapeDtypeStruct((out_q, m, n), out_dtype)]
    args = [a, b]
    if comm is not None:
        args += comm.operands
        out_shape += [jax.ShapeDtypeStruct(t.shape, t.dtype) for t in comm.operands]
        scratch += [pltpu.SemaphoreType.DMA((comm.n_sems,)), pltpu.SemaphoreType.DMA((comm.n_sems,))]
    outs = pl.pallas_call(
        body, name=name, grid=grid,
        in_specs=[a_spec, b_spec] + [ANY] * n_comm, out_specs=[o_spec] + [ANY] * n_comm,
        out_shape=out_shape, scratch_shapes=scratch,
        input_output_aliases={2 + i: 1 + i for i in range(n_comm)},
        compiler_params=_cparams(4),
    )(*args)
    return (outs[0], outs[1:]) if comm is not None else outs[0]


def _q0(q, qr):
    return 0


def _qq(q, qr):
    return q


def _qr(q, qr):
    return qr


def _mm_dd(a3, w6, widx, *, mode, name, out_dtype, n_q=1, n_qr=1, qa=_q0, comm=None):
    d = w6.shape[1]
    f32_in = a3.dtype != BF16
    tm = _tile(a3.shape[1] if mode != "tn" else a3.shape[2], 512 if f32_in else 1024)
    if n_qr > 1:
        qb = lambda q, qr: widx + qr
    elif n_q > 1:
        qb = lambda q, qr: widx + q
    else:
        qb = lambda q, qr: widx
    return _mm(a3, w6, mode=mode, name=name, out_dtype=out_dtype, n_q=n_q, n_qr=n_qr,
               tm=tm, tn=_tile(d, 1024), tk=d, qa=qa, qb=qb, comm=comm)


def _mm_wgrad(a3, b3, *, name, n_q=1, qa=_q0, qb=_q0):
    s, d = a3.shape[1], a3.shape[2]
    f32_in = a3.dtype != BF16 or b3.dtype != BF16
    return _mm(a3, b3, mode="tn", name=name, out_dtype=BF16, n_q=n_q, tm=_tile(d, 1024), tn=b3.shape[2],
               tk=_tile(s, 512 if f32_in else 1024), qa=qa, qb=qb)


def _grp_mm(a, b, *, mode, name, out_dtype, tm, w=None):
    s = a.shape[0]
    if mode == "tn":
        cg = b.shape[1] // len(POOL_WINDOWS)
        g_n = len(POOL_WINDOWS)
        ts = _tile(s, tm)
        nr = s // ts

        def body(a_ref, b_ref, o_ref, acc_ref):
            r = pl.program_id(1)
            part = lax.dot_general(a_ref[...], b_ref[...], _DIMS["tn"], preferred_element_type=F32)

            @pl.when(r == 0)
            def _():
                acc_ref[...] = part

            @pl.when(r > 0)
            def _():
                acc_ref[...] += part

            @pl.when(r == nr - 1)
            def _():
                o_ref[...] = acc_ref[...].astype(o_ref.dtype)

        return pl.pallas_call(
            body, name=name, grid=(g_n, nr),
            in_specs=[pl.BlockSpec((ts, cg), lambda g, r: (r, g)), pl.BlockSpec((ts, cg), lambda g, r: (r, g))],
            out_specs=pl.BlockSpec((None, cg, cg), lambda g, r: (g, 0, 0)),
            out_shape=jax.ShapeDtypeStruct((g_n, cg, cg), out_dtype),
            scratch_shapes=[pltpu.VMEM((cg, cg), F32)],
            compiler_params=_cparams(2),
        )(a, b)
    g_n, cg = w.shape[0], w.shape[1]
    ts = _tile(s, tm)
    dims = _DIMS[mode]

    def body(a_ref, w_ref, o_ref):
        o_ref[...] = lax.dot_general(a_ref[...], w_ref[...], dims, preferred_element_type=F32).astype(o_ref.dtype)

    return pl.pallas_call(
        body, name=name, grid=(g_n, s // ts),
        in_specs=[pl.BlockSpec((ts, cg), lambda g, i: (i, g)), pl.BlockSpec((None, cg, cg), lambda g, i: (g, 0, 0))],
        out_specs=pl.BlockSpec((ts, cg), lambda g, i: (i, g)),
        out_shape=jax.ShapeDtypeStruct((s, g_n * cg), out_dtype),
        compiler_params=_cparams(2),
    )(a, w)


def _causal_ext(load, r0, rows, halo):
    cur = load(r0, rows)
    prev = load(pl.multiple_of(jnp.maximum(r0 - halo, 0), halo), halo)
    prev = jnp.where(r0 > 0, prev, jnp.zeros_like(prev))
    return jnp.concatenate([prev, cur], axis=0)


def _anti_ext(load, r0, rows, halo, s):
    cur = load(r0, rows)
    nxt = load(pl.multiple_of(jnp.minimum(r0 + rows, s - halo), halo), halo)
    nxt = jnp.where(r0 + rows < s, nxt, jnp.zeros_like(nxt))
    return jnp.concatenate([cur, nxt], axis=0)


def _down(ext, k):
    return pltpu.roll(ext, k, axis=0)


def _up(ext, k):
    return pltpu.roll(ext, ext.shape[0] - k, axis=0)


def _fold8(x):
    return jnp.sum(x.reshape(x.shape[0] // 8, 8, x.shape[1]), axis=0)


def _sigmoid(x):
    return 0.5 * jnp.tanh(0.5 * x) + 0.5


def _pool(p, *, backward, name, rows=64, comm=None):
    s, d = p.shape
    strips_per_group = (d // len(POOL_WINDOWS)) // LANES
    assert strips_per_group * LANES * len(POOL_WINDOWS) == d and s % rows == 0

    def body(p_ref, o_ref):
        g = pl.program_id(0) // strips_per_group
        win = jnp.left_shift(2, g).astype(F32)

        def load(r0, n):
            return p_ref[pl.ds(r0, n), :]

        def pick(levels):
            return jnp.where(g == 0, levels[0], jnp.where(g == 1, levels[1], jnp.where(g == 2, levels[2], levels[3])))

        def chunk(c, carry):
            r0 = pl.multiple_of(c * rows, rows)
            if not backward:
                ext = _causal_ext(load, r0, rows, POOL_HALO)
                levels, acc = [], ext
                for k in (1, 2, 4, 8):
                    acc = acc + _down(acc, k)
                    levels.append(acc)
                t = (r0 + lax.broadcasted_iota(jnp.int32, (rows, LANES), 0)).astype(F32)
                cnt = jnp.minimum(t + 1.0, win)
                out = pick(levels)[POOL_HALO:] / cnt - ext[POOL_HALO:]
            else:
                ext = _anti_ext(load, r0, rows, POOL_HALO, s)
                t = (r0 + lax.broadcasted_iota(jnp.int32, (rows + POOL_HALO, LANES), 0)).astype(F32)
                e = ext / jnp.minimum(t + 1.0, win)
                levels, acc = [], e
                for k in (1, 2, 4, 8):
                    acc = acc + _up(acc, k)
                    levels.append(acc)
                out = pick(levels)[:rows] - ext[:rows]
            o_ref[pl.ds(r0, rows), :] = out.astype(o_ref.dtype)
            return carry

        lax.fori_loop(0, s // rows, chunk, 0)

    res = _pallas(
        body, name=name, grid=(d // LANES,),
        in_specs=[pl.BlockSpec((s, LANES), lambda j: (0, j))],
        out_specs=[pl.BlockSpec((s, LANES), lambda j: (0, j))],
        out_shape=[jax.ShapeDtypeStruct((s, d), BF16)], args=[p], comm=comm)
    return res[0] if comm is None else (res[0][0], res[1])


def _grp_fwd(pooled, w_grp, scale, *, name, comm=None):
    s, d = pooled.shape
    g_n, cg = w_grp.shape[0], w_grp.shape[1]
    ts = _tile(s, 1024)

    def body(a_ref, w_ref, sc_ref, mg_ref, mx_ref):
        mg = jnp.dot(a_ref[...], w_ref[...], preferred_element_type=F32)
        mg_ref[...] = mg.astype(BF16)
        mx_ref[...] = (mg * sc_ref[...]).astype(BF16)

    blk = pl.BlockSpec((ts, cg), lambda g, i: (i, g))
    return _pallas(
        body, name=name, grid=(g_n, s // ts),
        in_specs=[blk, pl.BlockSpec((None, cg, cg), lambda g, i: (g, 0, 0)), pl.BlockSpec((1, cg), lambda g, i: (0, g))],
        out_specs=[blk, blk],
        out_shape=[jax.ShapeDtypeStruct((s, d), BF16)] * 2, args=[pooled, w_grp, scale], comm=comm)


def _grp_bwd_pre(dmixed, mg, scale, *, name):
    s, d = dmixed.shape
    ts = _tile(s, 256, 16)

    def body(dm_ref, mg_ref, sc_ref, dmg_ref, dsc_ref):
        dm = dm_ref[...]
        dmg_ref[...] = (dm * sc_ref[...]).astype(BF16)
        part = jnp.sum(dm * mg_ref[...].astype(F32), axis=0, keepdims=True)

        @pl.when(pl.program_id(0) == 0)
        def _():
            dsc_ref[...] = part

        @pl.when(pl.program_id(0) > 0)
        def _():
            dsc_ref[...] += part

    blk = pl.BlockSpec((ts, d), lambda i: (i, 0))
    vec = pl.BlockSpec((1, d), lambda i: (0, 0))
    return pl.pallas_call(
        body, name=name, grid=(s // ts,),
        in_specs=[blk, blk, vec], out_specs=[blk, vec],
        out_shape=[jax.ShapeDtypeStruct((s, d), BF16), jax.ShapeDtypeStruct((1, d), F32)],
        compiler_params=_cparams(1),
    )(dmixed, mg, scale)


def _ln_fwd(res, mm, g, b, *, name, comm=None):
    s, d = res.shape
    ts = _tile(s, 256, 16)

    def body(res_ref, mm_ref, g_ref, b_ref, h_ref, hb_ref, xh_ref, rs_ref):
        z = ALPHA * res_ref[...] + mm_ref[...]
        mu = jnp.mean(z, axis=-1, keepdims=True)
        zc = z - mu
        var = jnp.mean(zc * zc, axis=-1, keepdims=True)
        rstd = lax.rsqrt(var + LN_EPS)
        xhat = zc * rstd
        h = xhat * g_ref[...] + b_ref[...]
        h_ref[...] = h
        hb_ref[...] = h.astype(BF16)
        xh_ref[...] = xhat
        rs_ref[...] = rstd

    blk = pl.BlockSpec((ts, d), lambda i: (i, 0))
    vec = pl.BlockSpec((1, d), lambda i: (0, 0))
    return _pallas(
        body, name=name, grid=(s // ts,),
        in_specs=[blk, blk, vec, vec],
        out_specs=[blk, blk, blk, pl.BlockSpec((ts, 1), lambda i: (i, 0))],
        out_shape=[jax.ShapeDtypeStruct((s, d), F32), jax.ShapeDtypeStruct((s, d), BF16),
                   jax.ShapeDtypeStruct((s, d), F32), jax.ShapeDtypeStruct((s, 1), F32)],
        args=[res, mm, g, b], comm=comm)


def _ln_bwd(dres, dmm, xhat, rstd, g, *, name):
    s, d = dmm.shape
    ts = _tile(s, 256, 16)
    has_res = dres is not None

    def body(*refs):
        if has_res:
            dres_ref, dmm_ref, xh_ref, rs_ref, g_ref, dz_ref, dzb_ref, dg_ref, db_ref = refs
            dh = ALPHA * dres_ref[...] + dmm_ref[...]
        else:
            dmm_ref, xh_ref, rs_ref, g_ref, dz_ref, dzb_ref, dg_ref, db_ref = refs
            dh = dmm_ref[...]
        xhat_ = xh_ref[...]
        dxh = dh * g_ref[...]
        c1 = jnp.mean(dxh, axis=-1, keepdims=True)
        c2 = jnp.mean(dxh * xhat_, axis=-1, keepdims=True)
        dz = rs_ref[...] * (dxh - c1 - xhat_ * c2)
        dz_ref[...] = dz
        dzb_ref[...] = dz.astype(BF16)
        dg_part = jnp.sum(dh * xhat_, axis=0, keepdims=True)
        db_part = jnp.sum(dh, axis=0, keepdims=True)

        @pl.when(pl.program_id(0) == 0)
        def _():
            dg_ref[...] = dg_part
            db_ref[...] = db_part

        @pl.when(pl.program_id(0) > 0)
        def _():
            dg_ref[...] += dg_part
            db_ref[...] += db_part

    blk = pl.BlockSpec((ts, d), lambda i: (i, 0))
    vec = pl.BlockSpec((1, d), lambda i: (0, 0))
    col = pl.BlockSpec((ts, 1), lambda i: (i, 0))
    ins = ([dres] if has_res else []) + [dmm, xhat, rstd, g]
    in_specs = ([blk] if has_res else []) + [blk, blk, col, vec]
    return pl.pallas_call(
        body, name=name, grid=(s // ts,),
        in_specs=in_specs, out_specs=[blk, blk, vec, vec],
        out_shape=[jax.ShapeDtypeStruct((s, d), F32), jax.ShapeDtypeStruct((s, d), BF16),
                   jax.ShapeDtypeStruct((1, d), F32), jax.ShapeDtypeStruct((1, d), F32)],
        compiler_params=_cparams(1),
    )(*ins)


def _loss_head(h, tgt, *, name):
    s, d = h.shape
    ts = _tile(s, 256, 16)

    def body(h_ref, t_ref, dh_ref, loss_ref):
        err = h_ref[...] - t_ref[...]
        dh_ref[...] = err * (1.0 / d)
        part = 0.5 * jnp.sum(jnp.mean(err * err, axis=-1, keepdims=True), axis=0, keepdims=True)

        @pl.when(pl.program_id(0) == 0)
        def _():
            loss_ref[...] = part

        @pl.when(pl.program_id(0) > 0)
        def _():
            loss_ref[...] += part

    blk = pl.BlockSpec((ts, d), lambda i: (i, 0))
    return pl.pallas_call(
        body, name=name, grid=(s // ts,),
        in_specs=[blk, blk], out_specs=[blk, pl.BlockSpec((1, 1), lambda i: (0, 0))],
        out_shape=[jax.ShapeDtypeStruct((s, d), F32), jax.ShapeDtypeStruct((1, 1), F32)],
        compiler_params=_cparams(1),
    )(h, tgt)


def _conv(ext, w, bias):
    c = bias + _down(ext, 2) * w[0:1] + _down(ext, 1) * w[1:2] + ext * w[2:3]
    return c[CONV_HALO:]


def _act_specs(s, n2):
    n_strips = pl.cdiv(n2, LANES)
    u_spec = pl.BlockSpec((None, 2, s, LANES), lambda hh, j: (hh, 0, 0, j))
    cwg = pl.BlockSpec((None, 3, LANES), lambda hh, j: (hh, 0, j))
    cwv = pl.BlockSpec((None, 3, LANES), lambda hh, j: (hh + 2, 0, j))
    cbg = pl.BlockSpec((None, 1, LANES), lambda hh, j: (hh, 0, j))
    cbv = pl.BlockSpec((None, 1, LANES), lambda hh, j: (hh + 2, 0, j))
    return n_strips, u_spec, cwg, cwv, cbg, cbv


def _act_fwd(u, cw, cb, *, name, rows=64):
    _, _, s, n2 = u.shape
    n_strips, u_spec, cwg, cwv, cbg, cbv = _act_specs(s, n2)

    def body(u_ref, wg_ref, wv_ref, bg_ref, bv_ref, a_ref):
        wg, wv, bg, bv = wg_ref[...], wv_ref[...], bg_ref[...], bv_ref[...]

        def chunk(c, carry):
            r0 = pl.multiple_of(c * rows, rows)
            cg = _conv(_causal_ext(lambda r, n: u_ref[0, pl.ds(r, n), :].astype(F32), r0, rows, CONV_HALO), wg, bg)
            cv = _conv(_causal_ext(lambda r, n: u_ref[1, pl.ds(r, n), :].astype(F32), r0, rows, CONV_HALO), wv, bv)
            a_ref[pl.ds(r0, rows), :] = (cg * _sigmoid(cg) * cv).astype(BF16)
            return carry

        lax.fori_loop(0, s // rows, chunk, 0)

    return pl.pallas_call(
        body, name=name, grid=(2, n_strips),
        in_specs=[u_spec, cwg, cwv, cbg, cbv],
        out_specs=pl.BlockSpec((None, s, LANES), lambda hh, j: (hh, 0, j)),
        out_shape=jax.ShapeDtypeStruct((2, s, n2), BF16),
        compiler_params=_cparams(2),
    )(u, cw, cw, cb, cb)


def _act_bwd(u, da, cw, cb, *, name, rows=64):
    _, _, s, n2 = u.shape
    n_strips, u_spec, cwg, cwv, cbg, cbv = _act_specs(s, n2)
    n_chunks = s // rows

    def body(u_ref, da_ref, wg_ref, wv_ref, bg_ref, bv_ref, du_ref, dwg_ref, dwv_ref, dbg_ref, dbv_ref, dg_s, dv_s):
        wg, wv, bg, bv = wg_ref[...], wv_ref[...], bg_ref[...], bv_ref[...]

        def first(c, sums):
            r0 = pl.multiple_of(c * rows, rows)
            eg = _causal_ext(lambda r, n: u_ref[0, pl.ds(r, n), :].astype(F32), r0, rows, CONV_HALO)
            ev = _causal_ext(lambda r, n: u_ref[1, pl.ds(r, n), :].astype(F32), r0, rows, CONV_HALO)
            cg, cv = _conv(eg, wg, bg), _conv(ev, wv, bv)
            sg = _sigmoid(cg)
            dact = da_ref[pl.ds(r0, rows), :]
            dval = dact * (cg * sg)
            dgate = dact * cv * (sg * (1.0 + cg * (1.0 - sg)))
            dg_s[pl.ds(r0, rows), :] = dgate
            dv_s[pl.ds(r0, rows), :] = dval
            new = []
            for dc, ext in ((dgate, eg), (dval, ev)):
                new += [_fold8(dc * _down(ext, 2)[CONV_HALO:]), _fold8(dc * _down(ext, 1)[CONV_HALO:]),
                        _fold8(dc * ext[CONV_HALO:]), _fold8(dc)]
            return tuple(acc + x for acc, x in zip(sums, new))

        zero = jnp.zeros((8, LANES), F32)
        sums = lax.fori_loop(0, n_chunks, first, (zero,) * 8)
        red = [jnp.sum(x, axis=0, keepdims=True) for x in sums]
        dwg_ref[...] = jnp.concatenate(red[0:3], axis=0)
        dbg_ref[...] = red[3]
        dwv_ref[...] = jnp.concatenate(red[4:7], axis=0)
        dbv_ref[...] = red[7]

        def second(c, carry):
            r0 = pl.multiple_of(c * rows, rows)
            for gv, (src, w) in enumerate(((dg_s, wg), (dv_s, wv))):
                ext = _anti_ext(lambda r, n: src[pl.ds(r, n), :], r0, rows, CONV_HALO, s)
                du = ext * w[2:3] + _up(ext, 1) * w[1:2] + _up(ext, 2) * w[0:1]
                du_ref[gv, pl.ds(r0, rows), :] = du[:rows].astype(BF16)
            return carry

        lax.fori_loop(0, n_chunks, second, 0)

    w_out = pl.BlockSpec((None, 3, LANES), lambda hh, j: (hh, 0, j))
    b_out = pl.BlockSpec((None, 1, LANES), lambda hh, j: (hh, 0, j))
    return pl.pallas_call(
        body, name=name, grid=(2, n_strips),
        in_specs=[u_spec, pl.BlockSpec((None, s, LANES), lambda hh, j: (hh, 0, j)), cwg, cwv, cbg, cbv],
        out_specs=[u_spec, w_out, w_out, b_out, b_out],
        out_shape=[jax.ShapeDtypeStruct((2, 2, s, n2), BF16),
                   jax.ShapeDtypeStruct((2, 3, n2), F32), jax.ShapeDtypeStruct((2, 3, n2), F32),
                   jax.ShapeDtypeStruct((2, 1, n2), F32), jax.ShapeDtypeStruct((2, 1, n2), F32)],
        scratch_shapes=[pltpu.VMEM((s, LANES), F32), pltpu.VMEM((s, LANES), F32)],
        compiler_params=_cparams(2),
    )(u, da, cw, cw, cb, cb)


def _dot_nt(a, b):
    return lax.dot_general(a, b, _DIMS["nt"], preferred_element_type=F32)


def _dot_tn(a, b):
    return lax.dot_general(a, b, _DIMS["tn"], preferred_element_type=F32)


def _band_mask(b, ATTN_QROWS):
    qi = lax.broadcasted_iota(jnp.int32, (ATTN_QROWS, ATTN_QROWS + ATTN_BLOCK), 0)
    kj = lax.broadcasted_iota(jnp.int32, (ATTN_QROWS, ATTN_QROWS + ATTN_BLOCK), 1)
    band = jnp.logical_and(kj >= qi, kj <= qi + ATTN_BLOCK)
    return jnp.logical_and(band, jnp.logical_or(b > 0, kj >= ATTN_BLOCK))


def _to_residues(nat, rm, d, seq, pad):
    seg = seq + pad
    for r in range(d):
        if pad:
            rm[pl.ds(r * seg, pad), :] = jnp.zeros((pad, LANES), rm.dtype)
        rows = nat[pl.ds(r, seq, stride=d), :] if d > 1 else nat[...]
        rm[pl.ds(r * seg + pad, seq), :] = rows.astype(rm.dtype)


def _rows_loop(s, rows, fn):
    def step(c, carry):
        fn(pl.ds(pl.multiple_of(c * rows, rows), rows))
        return carry

    lax.fori_loop(0, s // rows, step, 0)


def _attn_fwd(qkv, *, name, comm=None):
    _, s, dm = qkv.shape
    heads, scale = dm // HEAD_DIM, 1.0 / math.sqrt(HEAD_DIM)
    pad_rows = s + ATTN_BLOCK * max(DILATIONS)
    ATTN_QROWS, ATTN_UNROLL = ATTN_FWD_TILE

    def body(q_ref, k_ref, v_ref, o_ref, ob_ref, lse_ref, nat, rq, rk, rv, ro, rl, o_tmp, l_tmp, o_acc, m_acc, s_acc):
        for d in DILATIONS:
            seq = s // d
            nb, seg = seq // ATTN_QROWS, seq + ATTN_BLOCK
            for src, dst, pad in ((q_ref, rq, 0), (k_ref, rk, ATTN_BLOCK), (v_ref, rv, ATTN_BLOCK)):
                if d == 1:
                    _to_residues(src, dst, d, seq, pad)
                else:
                    nat[...] = src[...].astype(F32)
                    _to_residues(nat, dst, d, seq, pad)

            def block(idx, carry):
                r, b = idx // nb, idx % nb
                qrows = pl.ds(pl.multiple_of(r * seq + b * ATTN_QROWS, ATTN_BLOCK), ATTN_QROWS)
                krows = pl.ds(pl.multiple_of(r * seg + b * ATTN_QROWS, ATTN_BLOCK), ATTN_QROWS + ATTN_BLOCK)
                sc = jnp.where(_band_mask(b, ATTN_QROWS),_dot_nt(rq[qrows, :], rk[krows, :]) * scale, NEG_INF)
                m = jnp.max(sc, axis=-1, keepdims=True)
                p = jnp.exp(sc - m)
                den = jnp.sum(p, axis=-1, keepdims=True)
                ro[qrows, :] = jnp.dot(p.astype(BF16), rv[krows, :], preferred_element_type=F32) / den
                rl[qrows, :] = jnp.broadcast_to(m + jnp.log(den), (ATTN_QROWS, LANES))
                return carry

            lax.fori_loop(0, d * nb, block, 0, unroll=ATTN_UNROLL)
            if d == 1:
                def first(rows):
                    o_acc[rows, :] = ro[rows, :]
                    m_acc[rows, :] = rl[rows, :]
                    s_acc[rows, :] = jnp.ones((rows.size, LANES), F32)

                _rows_loop(s, 64, first)
            else:
                for r in range(d):
                    o_tmp[pl.ds(r, seq, stride=d), :] = ro[pl.ds(r * seq, seq), :]
                    l_tmp[pl.ds(r, seq, stride=d), :] = rl[pl.ds(r * seq, seq), :]

                def merge(rows):
                    m_old, l_new = m_acc[rows, :], l_tmp[rows, :]
                    m_new = jnp.maximum(m_old, l_new)
                    w_old, w_new = jnp.exp(m_old - m_new), jnp.exp(l_new - m_new)
                    o_acc[rows, :] = o_acc[rows, :] * w_old + o_tmp[rows, :] * w_new
                    s_acc[rows, :] = s_acc[rows, :] * w_old + w_new
                    m_acc[rows, :] = m_new

                _rows_loop(s, 64, merge)

        def finish(rows):
            tot = s_acc[rows, :]
            o = o_acc[rows, :] / tot
            o_ref[rows, :] = o
            ob_ref[rows, :] = o.astype(BF16)
            lse_ref[rows, :] = m_acc[rows, :] + jnp.log(tot)

        _rows_loop(s, 64, finish)

    head = lambda i: pl.BlockSpec((None, s, HEAD_DIM), lambda h, i=i: (i, 0, h))
    out = pl.BlockSpec((s, HEAD_DIM), lambda h: (0, h))
    nat_f32 = pltpu.VMEM((s, LANES), F32)
    return _pallas(
        body, name=name, grid=(heads,),
        in_specs=[head(0), head(1), head(2)], out_specs=[out, out, out],
        out_shape=[jax.ShapeDtypeStruct((s, dm), F32), jax.ShapeDtypeStruct((s, dm), BF16),
                   jax.ShapeDtypeStruct((s, dm), F32)],
        scratch_shapes=[nat_f32, pltpu.VMEM((s, LANES), BF16), pltpu.VMEM((pad_rows, LANES), BF16),
                        pltpu.VMEM((pad_rows, LANES), BF16), nat_f32, nat_f32, nat_f32, nat_f32, nat_f32, nat_f32, nat_f32],
        args=[qkv, qkv, qkv], comm=comm)


def _attn_delta(do, o, *, name):
    s, dm = o.shape
    heads = dm // HEAD_DIM
    ts = _tile(s, 256, 16)

    def body(do_ref, o_ref, dl_ref):
        for h in range(heads):
            hs = slice(h * HEAD_DIM, (h + 1) * HEAD_DIM)
            row = jnp.sum(do_ref[:, hs].astype(F32) * o_ref[:, hs], axis=-1, keepdims=True)
            dl_ref[:, hs] = jnp.broadcast_to(row, (ts, HEAD_DIM))

    blk = pl.BlockSpec((ts, dm), lambda i: (i, 0))
    return pl.pallas_call(
        body, name=name, grid=(s // ts,),
        in_specs=[blk, blk], out_specs=blk,
        out_shape=jax.ShapeDtypeStruct((s, dm), F32),
        compiler_params=_cparams(1),
    )(do, o)


def _attn_bwd(qkv, do, lse, delta, *, name):
    _, s, dm = qkv.shape
    heads, scale = dm // HEAD_DIM, 1.0 / math.sqrt(HEAD_DIM)
    pad_rows = s + ATTN_BLOCK * max(DILATIONS)
    ATTN_QROWS, ATTN_UNROLL = ATTN_BWD_TILE

    def body(q_ref, k_ref, v_ref, do_ref, l_ref, dl_ref, out_ref,
             nat, rq, rdo, rk, rv, rl, rdl, rdq, kc, kp, vc, vp, aq, ak, av):
        for d in DILATIONS:
            seq = s // d
            nb, seg = seq // ATTN_QROWS, seq + ATTN_BLOCK
            for src, dst, pad in ((q_ref, rq, 0), (do_ref, rdo, 0), (k_ref, rk, ATTN_BLOCK), (v_ref, rv, ATTN_BLOCK)):
                if d == 1:
                    _to_residues(src, dst, d, seq, pad)
                else:
                    nat[...] = src[...].astype(F32)
                    _to_residues(nat, dst, d, seq, pad)
            if d == 1:
                lse_rows, dl_rows = l_ref, dl_ref
            else:
                lse_rows, dl_rows = rl, rdl
                _to_residues(l_ref, rl, d, seq, 0)
                _to_residues(dl_ref, rdl, d, seq, 0)
            def clear(rows):
                kp[rows, :] = jnp.zeros((rows.size, LANES), F32)
                vp[rows, :] = jnp.zeros((rows.size, LANES), F32)

            _rows_loop(d * seg, ATTN_BLOCK, clear)

            def block(idx, carry):
                r, b = idx // nb, idx % nb
                qrows = pl.ds(pl.multiple_of(r * seq + b * ATTN_QROWS, ATTN_BLOCK), ATTN_QROWS)
                krow = pl.multiple_of(r * seg + b * ATTN_QROWS, ATTN_BLOCK)
                krows = pl.ds(krow, ATTN_QROWS + ATTN_BLOCK)
                before, own = pl.ds(krow, ATTN_BLOCK), pl.ds(krow + ATTN_BLOCK, ATTN_QROWS)
                qb, dob, kw, vw = rq[qrows, :], rdo[qrows, :], rk[krows, :], rv[krows, :]
                lse_b = jnp.concatenate([lse_rows[qrows, :]] * (ATTN_QROWS // LANES + 1), axis=1)
                dl_b = jnp.concatenate([dl_rows[qrows, :]] * (ATTN_QROWS // LANES + 1), axis=1)
                sc = jnp.where(_band_mask(b, ATTN_QROWS),_dot_nt(qb, kw) * scale, NEG_INF)
                p = jnp.exp(sc - lse_b)
                ds = (p * (_dot_nt(dob, vw) - dl_b) * scale).astype(BF16)
                rdq[qrows, :] = jnp.dot(ds, kw, preferred_element_type=F32)
                dk, dv = _dot_tn(ds, qb), _dot_tn(p.astype(BF16), dob)
                kp[before, :] = dk[:ATTN_BLOCK]
                kc[own, :] = dk[ATTN_BLOCK:]
                vp[before, :] = dv[:ATTN_BLOCK]
                vc[own, :] = dv[ATTN_BLOCK:]
                return carry

            lax.fori_loop(0, d * nb, block, 0, unroll=ATTN_UNROLL)
            for r in range(d):
                keys = pl.ds(r * seg + ATTN_BLOCK, seq)
                rows = pl.ds(r, seq, stride=d) if d > 1 else pl.ds(0, seq)
                for acc, val in ((aq, rdq[pl.ds(r * seq, seq), :]), (ak, kc[keys, :] + kp[keys, :]),
                                 (av, vc[keys, :] + vp[keys, :])):
                    if d == DILATIONS[0]:
                        acc[rows, :] = val
                    else:
                        acc[rows, :] += val

        def finish(rows):
            for i, acc in enumerate((aq, ak, av)):
                out_ref[i, rows, :] = acc[rows, :].astype(BF16)

        _rows_loop(s, 256, finish)

    head = lambda i: pl.BlockSpec((None, s, HEAD_DIM), lambda h, i=i: (i, 0, h))
    col = pl.BlockSpec((s, HEAD_DIM), lambda h: (0, h))
    f32 = lambda rows: pltpu.VMEM((rows, LANES), F32)
    b16 = lambda rows: pltpu.VMEM((rows, LANES), BF16)
    return pl.pallas_call(
        body, name=name, grid=(heads,),
        in_specs=[head(0), head(1), head(2), col, col, col],
        out_specs=pl.BlockSpec((3, s, HEAD_DIM), lambda h: (0, 0, h)),
        out_shape=jax.ShapeDtypeStruct((3, s, dm), BF16),
        scratch_shapes=[f32(s), b16(s), b16(s), b16(pad_rows), b16(pad_rows), f32(s), f32(s),
                        f32(s), f32(pad_rows), f32(pad_rows), f32(pad_rows), f32(pad_rows), f32(s), f32(s), f32(s)],
        compiler_params=_cparams(1),
    )(qkv, qkv, qkv, do, lse, delta)


def _ew(fn, ins, out_dtypes, *, name, tile_bytes=1 << 20):
    first = ins[0][0] if isinstance(ins[0], tuple) else ins[0]
    rows, cols = first.shape[-2], first.shape[-1]
    tr = _tile(rows, max(16, tile_bytes // (4 * cols)), 16)
    n_in = len(ins)

    def body(*refs):
        outs = fn(*[r[...] for r in refs[:n_in]])
        for o_ref, val in zip(refs[n_in:], outs):
            o_ref[...] = val.astype(o_ref.dtype)

    in_specs, args = [], []
    for item in ins:
        if isinstance(item, tuple):
            arr, lead = item
            in_specs.append(pl.BlockSpec((None, tr, cols), lambda i, lead=lead: (lead, i, 0)))
            args.append(arr)
        else:
            in_specs.append(pl.BlockSpec((tr, cols), lambda i: (i, 0)))
            args.append(item)
    blk = pl.BlockSpec((tr, cols), lambda i: (i, 0))
    return pl.pallas_call(
        body, name=name, grid=(rows // tr,),
        in_specs=in_specs, out_specs=[blk] * len(out_dtypes),
        out_shape=[jax.ShapeDtypeStruct((rows, cols), dt) for dt in out_dtypes],
        compiler_params=_cparams(1),
    )(*args)


def _adamw_math(g, w, m, v):
    m2 = ADAM_B1 * m + (1.0 - ADAM_B1) * g
    v2 = ADAM_B2 * v + (1.0 - ADAM_B2) * (g * g)
    m_hat = m2 / (1.0 - ADAM_B1 ** ADAM_STEP)
    v_hat = v2 / (1.0 - ADAM_B2 ** ADAM_STEP)
    delta = -ADAM_LR * (m_hat / (jnp.sqrt(v_hat) + ADAM_EPS) + ADAM_WD * w)
    return g, delta, m2, v2


def _scalars(*vals):
    return jnp.stack([jnp.asarray(v, jnp.int32) for v in vals])


def _adamw_halves(mine, theirs, core, w, m, v, *, name, lead=0, prev=None):
    shape = w.shape
    a_n, rh, cols = mine.shape
    w3, m3, v3 = (t.reshape(-1, 2 * rh, cols) for t in (w, m, v))
    tr, tc = _tile2(rh, cols, 1 << 17, 8)
    n_i = rh // tr

    def body(c_ref, mine_ref, theirs_ref, w_ref, m_ref, v_ref, *rest):
        g = jnp.where(pl.program_id(1) == c_ref[0], mine_ref[...], theirs_ref[...])
        outs = _adamw_math(g, w_ref[...], m_ref[...], v_ref[...])
        for ref, val in zip(rest[-4:], outs):
            ref[...] = val

    def half(mine_rows):
        def index(a, h, i, j, c_ref):
            use = (h == c_ref[0]) if mine_rows else (h != c_ref[0])
            return (a, jnp.where(use, i, 0), jnp.where(use, j, 0))
        return pl.BlockSpec((None, tr, tc), index)

    full = pl.BlockSpec((None, tr, tc), lambda a, h, i, j, c_ref: (lead + a, h * n_i + i, j))
    args = [mine, theirs, w3, m3, v3]
    in_specs = [half(True), half(False), full, full, full]
    aliases = {}
    if prev is not None:
        args += [p.reshape(w3.shape) for p in prev]
        in_specs += [ANY] * 4
        aliases = {6 + k: k for k in range(4)}
    outs = pl.pallas_call(
        body, name=name,
        grid_spec=pltpu.PrefetchScalarGridSpec(
            num_scalar_prefetch=1, grid=(a_n, 2, n_i, cols // tc), in_specs=in_specs, out_specs=[full] * 4),
        out_shape=[jax.ShapeDtypeStruct(w3.shape, F32)] * 4,
        input_output_aliases=aliases,
        compiler_params=_cparams(4),
    )(_scalars(core), *args)
    return [o.reshape(shape) for o in outs]


def _cast_into(src, buf, lead, slot, *, name, buf_shape=None, dtype=BF16):
    a_n, rows, cols = src.shape
    tr = _tile(rows, max(16, (1 << 21) // (4 * cols)), 16)

    def body(slot_ref, src_ref, *rest):
        rest[-1][...] = src_ref[...].astype(rest[-1].dtype)

    in_specs = [pl.BlockSpec((None, tr, cols), lambda a, i, slot_ref: (a, i, 0))]
    args = [src]
    aliases = {}
    if buf is not None:
        in_specs.append(ANY)
        args.append(buf)
        aliases = {2: 0}
        buf_shape, dtype = buf.shape, buf.dtype
    return pl.pallas_call(
        body, name=name,
        grid_spec=pltpu.PrefetchScalarGridSpec(
            num_scalar_prefetch=1, grid=(a_n, rows // tr), in_specs=in_specs,
            out_specs=pl.BlockSpec((None, None, tr, cols), lambda a, i, slot_ref: (lead + a, slot_ref[0], i, 0))),
        out_shape=jax.ShapeDtypeStruct(buf_shape, dtype),
        input_output_aliases=aliases,
        compiler_params=_cparams(2),
    )(_scalars(slot), *args)


def _place():
    x, y, c = lax.axis_index("x"), lax.axis_index("y"), lax.axis_index("c")
    chips = [(1 - x, y), (x, 1 - y), (1 - x, 1 - y)]
    return x, y, c, chips


def _remote(src, dst, send_sem, recv_sem, dev):
    return pltpu.make_async_remote_copy(src_ref=src, dst_ref=dst, send_sem=send_sem, recv_sem=recv_sem,
                                        device_id=dev, device_id_type=MESH)


def _gather_comm(bufs, pieces):
    def plan(refs, send_sems, recv_sems):
        x, y, c, chips = _place()

        def region(p, q, core):
            t, a0, a1, part, n_parts = pieces[p]
            rh, cw = bufs[t].shape[2] // 2, bufs[t].shape[3] // n_parts
            return refs[t].at[pl.ds(a0, a1 - a0), q, pl.ds(core * rh, rh), pl.ds(part * cw, cw)]

        def ici(p, j, q):
            cx, cy = chips[j]
            return _remote(region(p, q, c), region(p, q, c), send_sems.at[6 * p + j], recv_sems.at[6 * p + j], (cx, cy, c))

        def d2d(p, j, core):
            cx, cy = chips[j]
            rows = region(p, 2 * cx + cy, core)
            return _remote(rows, rows, send_sems.at[6 * p + 3 + j], recv_sems.at[6 * p + 3 + j], (x, y, 1 - c))

        return 2 * x + y, c, [2 * cx + cy for cx, cy in chips], ici, d2d

    todo = [(j, p) for j in range(3) for p in range(len(pieces))]

    def start(refs, send_sems, recv_sems):
        q_me, _, _, ici, _ = plan(refs, send_sems, recv_sems)
        for j, p in todo:
            ici(p, j, q_me).start()

    def finish(refs, send_sems, recv_sems):
        q_me, c, q_of, ici, d2d = plan(refs, send_sems, recv_sems)
        for j, p in todo:
            ici(p, j, q_of[j]).wait_recv()
            d2d(p, j, c).start()
        for j, p in todo:
            d2d(p, j, 1 - c).wait_recv()
        for j, p in todo:
            ici(p, j, q_me).wait_send()
            d2d(p, j, c).wait_send()

    return _Comm(bufs, 6 * len(pieces), start, finish)


def _chips_comm(parts, lands):
    n = len(parts)

    def copy(refs, send_sems, recv_sems, t, j, q_src, q_dst):
        x, y, c, chips = _place()
        cx, cy = chips[j]
        return _remote(refs[t].at[:, q_src], refs[n + t].at[q_dst], send_sems.at[3 * t + j], recv_sems.at[3 * t + j],
                       (cx, cy, c))

    todo = [(j, t) for j in range(3) for t in range(n)]

    def qs():
        x, y, _, chips = _place()
        return 2 * x + y, [2 * cx + cy for cx, cy in chips]

    def start(refs, send_sems, recv_sems):
        q_me, q_of = qs()
        for j, t in todo:
            copy(refs, send_sems, recv_sems, t, j, q_of[j], q_me).start()

    def finish(refs, send_sems, recv_sems):
        q_me, q_of = qs()
        for j, t in todo:
            copy(refs, send_sems, recv_sems, t, j, q_me, q_of[j]).wait_recv()
        for j, t in todo:
            copy(refs, send_sems, recv_sems, t, j, q_of[j], q_me).wait_send()

    return _Comm(list(parts) + list(lands), 3 * n, start, finish)


def _comm_call(comm, *, name):
    k = len(comm.operands)

    def body(*refs):
        operands, (send_sems, recv_sems) = refs[k:2 * k], refs[2 * k:]
        comm.start(operands, send_sems, recv_sems)
        comm.finish(operands, send_sems, recv_sems)

    return pl.pallas_call(
        body, name=name, in_specs=[ANY] * k, out_specs=[ANY] * k,
        out_shape=[jax.ShapeDtypeStruct(t.shape, t.dtype) for t in comm.operands],
        input_output_aliases={i: i for i in range(k)},
        scratch_shapes=[pltpu.SemaphoreType.DMA((comm.n_sems,)), pltpu.SemaphoreType.DMA((comm.n_sems,))],
    )(*comm.operands)


def _rs_sibling(grads, *, name):
    n = len(grads)
    halves = [jax.ShapeDtypeStruct(g.shape[:2] + (g.shape[2] // 2, g.shape[3]), g.dtype) for g in grads]

    def body(*refs):
        src, theirs = refs[:n], refs[n:2 * n]
        send_sems, recv_sems = refs[2 * n:]
        x, y, c, _ = _place()
        ops = []
        for t in range(n):
            rh = grads[t].shape[2] // 2
            give = _remote(src[t].at[:, :, pl.ds((1 - c) * rh, rh), :], theirs[t], send_sems.at[t], recv_sems.at[t],
                           (x, y, 1 - c))
            give.start()
            ops.append(give)
        for op in ops:
            op.wait()

    return pl.pallas_call(
        body, name=name, in_specs=[ANY] * n, out_specs=[ANY] * n, out_shape=halves,
        scratch_shapes=[pltpu.SemaphoreType.DMA((n,)), pltpu.SemaphoreType.DMA((n,))],
    )(*grads)


def _rs_add(grad, theirs, core, slot, *, name):
    a_n, _, rh, cols = theirs.shape
    tr, tc = _tile2(rh, cols, 1 << 18, 16)
    n_i = rh // tr

    def body(s_ref, g_ref, t_ref, p_ref, y_ref):
        part = (g_ref[...].astype(F32) + t_ref[...].astype(F32)).astype(BF16)
        p_ref[...] = part

        @pl.when(pl.program_id(3) == s_ref[1])
        def _():
            y_ref[...] = part

    blk = (None, None, tr, tc)
    return pl.pallas_call(
        body, name=name,
        grid_spec=pltpu.PrefetchScalarGridSpec(
            num_scalar_prefetch=1, grid=(a_n, n_i, cols // tc, N_CHIPS),
            in_specs=[pl.BlockSpec(blk, lambda a, i, j, q, s: (a, q, s[0] * n_i + i, j)),
                      pl.BlockSpec(blk, lambda a, i, j, q, s: (a, q, i, j))],
            out_specs=[pl.BlockSpec(blk, lambda a, i, j, q, s: (a, q, i, j)),
                       pl.BlockSpec(blk, lambda a, i, j, q, s: (s[1], a, i, j))]),
        out_shape=[jax.ShapeDtypeStruct(theirs.shape, BF16),
                   jax.ShapeDtypeStruct((N_CHIPS, a_n, rh, cols), BF16)],
        compiler_params=_cparams(4),
    )(_scalars(core, slot), grad, theirs)


def _rs_finish(halves, *, name):
    n = len(halves)

    def body(*refs):
        src, dst = refs[:n], refs[n:2 * n]
        send_sems, recv_sems = refs[2 * n:]
        x, y, c, _ = _place()
        ops = []
        for t in range(n):
            give = _remote(src[t], dst[t], send_sems.at[t], recv_sems.at[t], (x, y, 1 - c))
            give.start()
            ops.append(give)
        for op in ops:
            op.wait()

    return pl.pallas_call(
        body, name=name, in_specs=[ANY] * n, out_specs=[ANY] * n,
        out_shape=[jax.ShapeDtypeStruct(h.shape, h.dtype) for h in halves],
        scratch_shapes=[pltpu.SemaphoreType.DMA((n,)), pltpu.SemaphoreType.DMA((n,))],
    )(*halves)


def _all_reduce_small(vec, *, name):
    rows = vec.shape[0]

    def body(v_ref, o_ref, land, send_sems, recv_sems):
        x, y, c, _ = _place()
        me = 4 * x + 2 * y + c
        land[me] = v_ref[...]
        flips = [(fx, fy, fc) for fx in (0, 1) for fy in (0, 1) for fc in (0, 1)][1:]
        sent = []
        for k, (fx, fy, fc) in enumerate(flips):
            cp = _remote(v_ref, land.at[me], send_sems.at[k], recv_sems.at[k], (x ^ fx, y ^ fy, c ^ fc))
            cp.start()
            sent.append(cp)
        for k, (fx, fy, fc) in enumerate(flips):
            peer = 4 * (x ^ fx) + 2 * (y ^ fy) + (c ^ fc)
            _remote(v_ref, land.at[peer], send_sems.at[k], recv_sems.at[k], (x ^ fx, y ^ fy, c ^ fc)).wait_recv()
        for cp in sent:
            cp.wait_send()
        total = land[0]
        for dev in range(1, 8):
            total = total + land[dev]
        o_ref[...] = total

    whole = pl.BlockSpec(memory_space=pltpu.VMEM)
    return pl.pallas_call(
        body, name=name, in_specs=[whole], out_specs=whole,
        out_shape=jax.ShapeDtypeStruct(vec.shape, F32),
        scratch_shapes=[pltpu.VMEM((8, rows, LANES), F32), pltpu.SemaphoreType.DMA((7,)), pltpu.SemaphoreType.DMA((7,))],
        compiler_params=pltpu.CompilerParams(vmem_limit_bytes=VMEM_LIMIT),
    )(vec)


def _pack(parts, mult=16):
    flat = jnp.concatenate([p.reshape(-1).astype(F32) for p in parts])
    rows = -(-flat.shape[0] // (LANES * mult)) * mult
    return jnp.pad(flat, (0, rows * LANES - flat.shape[0])).reshape(rows, LANES)


def _unpack(vec, shapes):
    flat, out, pos = vec.reshape(-1), [], 0
    for shp in shapes:
        size = math.prod(shp)
        out.append(flat[pos:pos + size].reshape(shp))
        pos += size
    return out


def kernel(x, pool_w_in, pool_w_grp, pool_scale, pool_w_out, attn_w_q, attn_w_o, shared_w_k, shared_w_v, ffn_w_up, ffn_conv_w, ffn_conv_b, ffn_w_down, ln1_g, ln1_b, ln2_g, ln2_b, loss_target, m_pool_w_in, m_pool_w_grp, m_pool_scale, m_pool_w_out, m_attn_w_q, m_attn_w_o, m_shared_w_k, m_shared_w_v, m_ffn_w_up, m_ffn_conv_w, m_ffn_conv_b, m_ffn_w_down, m_ln1_g, m_ln1_b, m_ln2_g, m_ln2_b, v_pool_w_in, v_pool_w_grp, v_pool_scale, v_pool_w_out, v_attn_w_q, v_attn_w_o, v_shared_w_k, v_shared_w_v, v_ffn_w_up, v_ffn_conv_w, v_ffn_conv_b, v_ffn_w_down, v_ln1_g, v_ln1_b, v_ln2_g, v_ln2_b):
    s, d = x.shape[1], x.shape[2]
    n2 = ffn_w_up.shape[2]
    fq = ffn_w_down.shape[1]
    assert 2 * fq == n2 and d % N_CHIPS == 0
    g_n, cg = pool_w_grp.shape[1], pool_w_grp.shape[3]
    xs, tgt = x[0], loss_target[0]
    q_me = 2 * lax.axis_index("x") + lax.axis_index("y")
    core = lax.axis_index("c")
    rq = d // N_CHIPS

    six_g = None
    for i, w_ in enumerate((pool_w_in[0], pool_w_out[0], attn_w_q[0], shared_w_k, shared_w_v, attn_w_o[0])):
        six_g = _cast_into(w_[None], six_g, i, q_me, name=f"cast_w{i}", buf_shape=(6, N_CHIPS, rq, d))
    grp_g = _cast_into(pool_w_grp[0], None, 0, q_me, name="cast_grp", buf_shape=(g_n, N_CHIPS, cg // N_CHIPS, cg))
    up_t, m_up_t, v_up_t = (jnp.swapaxes(t, 1, 2) for t in (ffn_w_up, m_ffn_w_up, v_ffn_w_up))
    up_g = _cast_into(up_t, None, 0, q_me, name="cast_up", buf_shape=(2, N_CHIPS, n2, d))
    dn_g = _cast_into(ffn_w_down, None, 0, q_me, name="cast_down", buf_shape=(2, N_CHIPS, fq, d))
    small = _pack([ffn_conv_w, pool_scale])
    small_g = _cast_into(small[None], None, 0, q_me, name="cast_small", buf_shape=(1, N_CHIPS) + small.shape, dtype=F32)
    bufs = [six_g, grp_g, up_g, dn_g, small_g]
    SIX, GRP, UP, DN, SMALL = range(5)
    first = [SIX, GRP, SMALL]
    landed = _comm_call(_gather_comm([bufs[t] for t in first], [(0, IW_IN, IW_OUT + 1, 0, 1), (1, 0, g_n, 0, 1),
                                                                (2, 0, 1, 0, 1)]), name="gather_first")
    for t, arr in zip(first, landed):
        bufs[t] = arr
    small_q = bufs[SMALL].reshape(N_CHIPS, -1)
    n_cw = 2 * 3 * n2
    conv_w = small_q[:, :n_cw].reshape(N_CHIPS, 2, 3, n2).transpose(1, 0, 2, 3)
    scale_full = small_q[:, n_cw:n_cw + d // N_CHIPS].reshape(1, d)
    conv_b = ffn_conv_b.reshape(2, N_CHIPS, 1, n2)

    def w6():
        return bufs[SIX].reshape(6, d, d)

    def up8():
        return bufs[UP].reshape(2 * N_CHIPS, n2, d)

    def dn4():
        return bufs[DN].reshape(4, n2, d)

    def gathered(pieces):
        used = sorted({pc[0] for pc in pieces})
        local = [(used.index(t), a0, a1, part, n_parts) for t, a0, a1, part, n_parts in pieces]
        return _gather_comm([bufs[t] for t in used], local), used

    def store(used, operands):
        for t, arr in zip(used, operands):
            bufs[t] = arr

    def ffn_fwd(l, hb, carry_up, carry_down):
        comm, used = gathered(carry_up)
        u, landed = _mm(hb[None], up8(), mode="nt", name=f"ffn{l}_up", out_dtype=BF16, n_q=4, tm=_tile(s, 512), tn=n2,
                        tk=d, qa=_q0, qb=lambda q, qr: 4 * l + _perm(q), comm=comm)
        store(used, landed)
        u = u.reshape(2, 2, s, n2)
        act = _act_fwd(u, conv_w[l], conv_b[l], name=f"ffn{l}_act")
        comm, used = gathered(carry_down) if carry_down else (None, None)
        ff = _mm(act, dn4(), mode="nn", name=f"ffn{l}_down", out_dtype=F32, n_q=1, n_qr=2, tm=_tile(s, 1024),
                 tn=_tile(d, 1024), tk=n2, qa=_qr, qb=lambda q, qr: 2 * l + qr, qo=lambda q: 0, comm=comm)
        if comm is not None:
            ff, landed = ff
            store(used, landed)
        return u, act, ff[0]

    def up0_parts(*parts):
        return gathered([(UP, 0, 1, part, 8) for part in parts])

    comm, used = up0_parts(0, 1)
    p, landed = _mm_dd(xs[None], w6(), IW_IN, mode="nn", name="pool_in", out_dtype=F32, comm=comm)
    store(used, landed)
    comm, used = up0_parts(2)
    pooled, landed = _pool(p[0], backward=False, name="pool_fwd", comm=comm)
    store(used, landed)
    comm, used = up0_parts(3)
    (mg, mixed), landed = _grp_fwd(pooled, bufs[GRP].reshape(g_n, cg, cg), scale_full, name="pool_grp", comm=comm)
    store(used, landed)
    comm, used = up0_parts(4, 5)
    mix0, landed = _mm_dd(mixed[None], w6(), IW_OUT, mode="nn", name="pool_out", out_dtype=F32, comm=comm)
    store(used, landed)
    comm, used = up0_parts(6, 7)
    (h1, h1b, xh1, rs1), landed = _ln_fwd(xs, mix0[0], ln1_g[0:1], ln1_b[0:1], name="ln1_0", comm=comm)
    store(used, landed)
    u0, act0, ff0 = ffn_fwd(0, h1b, [(DN, 0, 1, 0, 1), (SIX, IW_Q, IW_V + 1, 0, 1)], [(SIX, IW_O, IW_O + 1, 0, 1)])
    h2, h2b, xh2, rs2 = _ln_fwd(h1, ff0, ln2_g[0:1], ln2_b[0:1], name="ln2_0")

    qkv = _mm_dd(h2b[None], w6(), IW_Q, mode="nn", name="attn_qkv", out_dtype=BF16, n_q=3)
    comm, used = gathered([(UP, 1, 2, 0, 1)])
    (o, ob, lse), landed = _attn_fwd(qkv, name="attn_fwd", comm=comm)
    store(used, landed)
    mix1 = _mm_dd(ob[None], w6(), IW_O, mode="nn", name="attn_out", out_dtype=F32)[0]
    h3, h3b, xh3, rs3 = _ln_fwd(h2, mix1, ln1_g[1:2], ln1_b[1:2], name="ln1_1")
    u1, act1, ff1 = ffn_fwd(1, h3b, [(DN, 1, 2, 0, 1)], None)
    h4, _, xh4, rs4 = _ln_fwd(h3, ff1, ln2_g[1:2], ln2_b[1:2], name="ln2_1")
    dh4, loss_local = _loss_head(h4, tgt, name="loss_head")

    rs_parts, rs_lands = {}, {}

    def rs_prepare(items):
        theirs = _rs_sibling([g_ for _, g_ in items], name="rs_sibling_" + items[0][0])
        for (nm, g_), t_ in zip(items, theirs):
            rs_parts[nm], rs_lands[nm] = _rs_add(g_, t_, core, q_me, name=f"rs_add_{nm}")

    def rs_exchange(names):
        return _chips_comm([rs_parts[nm] for nm in names], [rs_lands[nm] for nm in names])

    def rs_landed(names, operands):
        for nm, land in zip(names, operands[len(names):]):
            rs_lands[nm] = land

    d_conv_w, d_conv_b = [None, None], [None, None]

    def ffn_bwd(l, dzb, u, act, hb, carry_dact):
        da = _mm(dzb[None], dn4(), mode="nt", name=f"ffn{l}_dact", out_dtype=F32, n_q=2, tm=_tile(s, 512), tn=n2,
                 tk=d, qa=_q0, qb=lambda q, qr: 2 * l + q,
                 comm=rs_exchange(carry_dact) if carry_dact else None)
        if carry_dact:
            da, landed = da
            rs_landed(carry_dact, landed)
        du, dwg, dwv, dbg, dbv = _act_bwd(u, da, conv_w[l], conv_b[l], name=f"ffn{l}_dconv")
        d_conv_w[l] = jnp.concatenate([dwg, dwv], axis=0)
        d_conv_b[l] = jnp.concatenate([dbg, dbv], axis=0)
        du4 = du.reshape(4, s, n2)
        g_dn = _mm(act, dzb[None], mode="tn", name=f"ffn{l}_gdown", out_dtype=BF16, n_q=2, tm=n2, tn=_tile(d, 512),
                   tk=_tile(s, 2048), qa=_qq, qb=_q0)
        rs_prepare([(f"dn{l}", g_dn.reshape(1, N_CHIPS, fq, d))])
        g_up, landed = _mm(du4, hb[None], mode="tn", name=f"ffn{l}_gup", out_dtype=BF16, n_q=4, tm=n2, tn=_tile(d, 512),
                           tk=_tile(s, 2048), qa=lambda q, qr: _perm(q), qb=_q0, comm=rs_exchange([f"dn{l}"]))
        rs_landed([f"dn{l}"], landed)
        rs_prepare([(f"up{l}", g_up.reshape(1, N_CHIPS, n2, d))])
        dh, landed = _mm(du4, up8(), mode="nn", name=f"ffn{l}_dh", out_dtype=F32, n_q=1, n_qr=4, tm=_tile(s, 1024),
                         tn=_tile(d, 1024), tk=n2, qa=_qr, qb=lambda q, qr: 4 * l + _perm(qr), qo=lambda q: 0,
                         comm=rs_exchange([f"up{l}"]))
        rs_landed([f"up{l}"], landed)
        return dh[0]

    dz4, dz4b, dg_ln2_1, db_ln2_1 = _ln_bwd(None, dh4, xh4, rs4, ln2_g[1:2], name="dln2_1")
    dh3 = ffn_bwd(1, dz4b, u1, act1, h3b, None)
    dz3, dz3b, dg_ln1_1, db_ln1_1 = _ln_bwd(dz4, dh3, xh3, rs3, ln1_g[1:2], name="dln1_1")
    g_wo = _mm_wgrad(ob[None], dz3b[None], name="attn_gwo")
    do = _mm_dd(dz3b[None], w6(), IW_O, mode="nt", name="attn_do", out_dtype=BF16)[0]
    delta = _attn_delta(do, o, name="attn_delta")
    dqkv = _attn_bwd(qkv, do, lse, delta, name="attn_bwd")
    g_wq = _mm_wgrad(h2b[None], dqkv, name="attn_gwq", qb=lambda q, qr: 0)
    g_wk = _mm_wgrad(h2b[None], dqkv, name="attn_gwk", qb=lambda q, qr: 1)
    g_wv = _mm_wgrad(h2b[None], dqkv, name="attn_gwv", qb=lambda q, qr: 2)
    rs_prepare([(nm, g_.reshape(1, N_CHIPS, rq, d)) for nm, g_ in
                (("wo", g_wo), ("wq", g_wq), ("wk", g_wk), ("wv", g_wv))])
    dh2, landed = _mm_dd(dqkv, w6(), IW_Q, mode="nt", name="attn_dh", out_dtype=F32, n_qr=3, qa=_qr,
                         comm=rs_exchange(["wo", "wq", "wk"]))
    rs_landed(["wo", "wq", "wk"], landed)
    dz2, dz2b, dg_ln2_0, db_ln2_0 = _ln_bwd(dz3, dh2[0], xh2, rs2, ln2_g[0:1], name="dln2_0")
    dh1 = ffn_bwd(0, dz2b, u0, act0, h1b, ["wv"])
    dz1, dz1b, dg_ln1_0, db_ln1_0 = _ln_bwd(dz2, dh1, xh1, rs1, ln1_g[0:1], name="dln1_0")
    g_wout = _mm_wgrad(mixed[None], dz1b[None], name="pool_gwout")
    rs_prepare([("wout", g_wout.reshape(1, N_CHIPS, rq, d))])
    dmixed, landed = _mm_dd(dz1b[None], w6(), IW_OUT, mode="nt", name="pool_dmixed", out_dtype=F32,
                            comm=rs_exchange(["wout"]))
    rs_landed(["wout"], landed)
    dmg, d_scale = _grp_bwd_pre(dmixed[0], mg, scale_full, name="pool_dscale")
    g_wgrp = _grp_mm(pooled, dmg, mode="tn", name="pool_gwgrp", out_dtype=BF16, tm=1024)
    dpooled = _grp_mm(dmg, None, mode="nt", name="pool_dpooled", out_dtype=F32, tm=1024, w=bufs[GRP].reshape(g_n, cg, cg))
    dp = _pool(dpooled, backward=True, name="pool_bwd")
    g_win = _mm_wgrad(xs[None], dp[None], name="pool_gwin")
    last = ["win", "wgrp"]
    rs_prepare([("win", g_win.reshape(1, N_CHIPS, rq, d)), ("wgrp", g_wgrp.reshape(g_n, N_CHIPS, cg // N_CHIPS, cg))])
    dx_mm, landed = _mm_dd(dp[None], w6(), IW_IN, mode="nt", name="pool_dx", out_dtype=F32, comm=rs_exchange(last))
    rs_landed(last, landed)
    (grad_x,) = _ew(lambda a, b: (ALPHA * a + b,), [dz1, dx_mm[0]], [F32], name="grad_x")

    rs_names = ["win", "wgrp", "wout", "wq", "wo", "wk", "wv", "up0", "up1", "dn0", "dn1"]
    finished = {}
    for nm in rs_names:
        y4 = rs_lands[nm]
        y3 = y4.reshape(N_CHIPS, -1, y4.shape[-1])
        (tot,) = _ew(lambda a0, a1, a2, a3: (((a0.astype(F32) + a1.astype(F32)) + a2.astype(F32)) + a3.astype(F32),),
                     [(y3, 0), (y3, 1), (y3, 2), (y3, 3)], [F32], name=f"rs_sum_{nm}")
        finished[nm] = tot.reshape(y4.shape[1:])
    others = dict(zip(rs_names, _rs_finish([finished[nm] for nm in rs_names], name="rs_finish")))
    results = {}
    for nm, key, w, m, v in (("pool_w_in", "win", pool_w_in, m_pool_w_in, v_pool_w_in),
                             ("pool_w_grp", "wgrp", pool_w_grp, m_pool_w_grp, v_pool_w_grp),
                             ("pool_w_out", "wout", pool_w_out, m_pool_w_out, v_pool_w_out),
                             ("attn_w_q", "wq", attn_w_q, m_attn_w_q, v_attn_w_q),
                             ("attn_w_o", "wo", attn_w_o, m_attn_w_o, v_attn_w_o),
                             ("shared_w_k", "wk", shared_w_k, m_shared_w_k, v_shared_w_k),
                             ("shared_w_v", "wv", shared_w_v, m_shared_w_v, v_shared_w_v)):
        results[nm] = _adamw_halves(finished[key], others[key], core, w, m, v, name=f"adamw_{nm}")
    for nm, key, w, m, v in (("ffn_w_up", "up", up_t, m_up_t, v_up_t),
                             ("ffn_w_down", "dn", ffn_w_down, m_ffn_w_down, v_ffn_w_down)):
        res = None
        for l in (1, 0):
            res = _adamw_halves(finished[f"{key}{l}"], others[f"{key}{l}"], core, w, m, v, name=f"adamw_{nm}{l}",
                                lead=l, prev=res)
        results[nm] = res
    results["ffn_w_up"] = [jnp.swapaxes(t, 1, 2) for t in results["ffn_w_up"]]

    ln_grads = [jnp.concatenate([a, b], axis=0) for a, b in
                ((dg_ln1_0, dg_ln1_1), (db_ln1_0, db_ln1_1), (dg_ln2_0, dg_ln2_1), (db_ln2_0, db_ln2_1))]
    small_shapes = [(2, N_CHIPS, 3, n2), (2, N_CHIPS, n2)] + [(2, d)] * 4 + [(1, d)]
    vec = _pack([jnp.stack(d_conv_w), jnp.stack(d_conv_b)] + ln_grads + [d_scale], mult=8)
    tot = _unpack(_all_reduce_small(vec, name="allreduce_small"), small_shapes)
    g_cw = lax.dynamic_index_in_dim(tot[0], q_me, axis=1, keepdims=False)
    g_cb = tot[1].reshape(2, N_CHIPS * n2)
    g_scale = lax.dynamic_slice_in_dim(tot[6], q_me * rq, rq, axis=1)
    small_names = ["ffn_conv_w", "ffn_conv_b", "ln1_g", "ln1_b", "ln2_g", "ln2_b", "pool_scale"]
    small_g = [g_cw, g_cb, tot[2], tot[3], tot[4], tot[5], g_scale]
    small_w = [ffn_conv_w, ffn_conv_b, ln1_g, ln1_b, ln2_g, ln2_b, pool_scale]
    small_m = [m_ffn_conv_w, m_ffn_conv_b, m_ln1_g, m_ln1_b, m_ln2_g, m_ln2_b, m_pool_scale]
    small_v = [v_ffn_conv_w, v_ffn_conv_b, v_ln1_g, v_ln1_b, v_ln2_g, v_ln2_b, v_pool_scale]
    packed = _ew(_adamw_math, [_pack(small_g, 8), _pack(small_w, 8), _pack(small_m, 8), _pack(small_v, 8)], [F32] * 4,
                 name="adamw_small")
    shapes = [w.shape for w in small_w]
    unpacked = [_unpack(pk, shapes) for pk in packed]
    for i, nm in enumerate(small_names):
        results[nm] = [unpacked[k][i] for k in range(4)]

    loss = lax.psum(loss_local[0, 0], ("x", "y", "c"))
    order = ["pool_w_in", "pool_w_grp", "pool_scale", "pool_w_out", "attn_w_q", "attn_w_o", "shared_w_k", "shared_w_v",
             "ffn_w_up", "ffn_conv_w", "ffn_conv_b", "ffn_w_down", "ln1_g", "ln1_b", "ln2_g", "ln2_b"]
    outs = [loss, grad_x[None]]
    for k in range(4):
        outs += [results[nm][k] for nm in order]
    return tuple(outs)
```

```python
import functools
import math

import jax
import jax.numpy as jnp
from jax import lax
from jax.experimental import pallas as pl
from jax.experimental.pallas import tpu as pltpu

F32 = jnp.float32
BF16 = jnp.bfloat16

LANES = 128
HEAD_DIM = 128
ATTN_BLOCK = 128
DILATIONS = (1, 4, 16)
ATTN_FWD_TILE = (128, 8)
ATTN_BWD_TILE = (256, 4)
POOL_WINDOWS = (2, 4, 8, 16)
POOL_HALO = 16
CONV_HALO = 16
DEPTH = 2
ALPHA = (2.0 * DEPTH) ** 0.25
LN_EPS = 1e-5
NEG_INF = -1e30
ADAM_LR = 0.001
ADAM_B1 = 0.9
ADAM_B2 = 0.999
ADAM_EPS = 1e-08
ADAM_WD = 0.01
ADAM_STEP = 10
N_CHIPS = 4
VMEM_LIMIT = 56 * 1024 * 1024
ANY = pl.BlockSpec(memory_space=pl.ANY)
MESH = pl.DeviceIdType.MESH

IW_IN, IW_OUT, IW_Q, IW_K, IW_V, IW_O = range(6)


def _cparams(n_grid):
    return pltpu.CompilerParams(dimension_semantics=("arbitrary",) * n_grid, vmem_limit_bytes=VMEM_LIMIT)


def _tile(dim, pref, align=LANES):
    if dim <= pref:
        return dim
    t = (pref // align) * align
    while t >= align:
        if dim % t == 0:
            return t
        t -= align
    return dim


def _tile2(rows, cols, budget, row_align):
    best = None
    for tc in [cols] + [c for c in range(LANES, cols, LANES) if cols % c == 0]:
        for tr in range(row_align, rows + 1, row_align):
            if rows % tr == 0 and tr * tc <= budget:
                if best is None or (tr * tc, tc) > (best[0] * best[1], best[1]):
                    best = (tr, tc)
    return best if best is not None else (rows, cols)


def _perm(q):
    return (q % 2) * 2 + q // 2


_DIMS = {"nn": (((1,), (0,)), ((), ())), "nt": (((1,), (1,)), ((), ())), "tn": (((0,), (0,)), ((), ()))}


class _Comm:
    def __init__(self, operands, n_sems, start, finish):
        self.operands, self.n_sems, self.start, self.finish = list(operands), n_sems, start, finish


def _pallas(body, *, name, grid, in_specs, out_specs, out_shape, args, scratch_shapes=(), comm=None):
    n_in, n_out = len(args), len(out_shape)
    if comm is None:
        return pl.pallas_call(body, name=name, grid=grid, in_specs=list(in_specs), out_specs=list(out_specs),
                              out_shape=list(out_shape), scratch_shapes=list(scratch_shapes),
                              compiler_params=_cparams(len(grid)))(*args)
    k = len(comm.operands)

    def carried(*refs):
        ins, outs = refs[:n_in], refs[n_in + k:n_in + k + n_out]
        operands = refs[n_in + k + n_out:n_in + 2 * k + n_out]
        scratch, (send_sems, recv_sems) = refs[n_in + 2 * k + n_out:-2], refs[-2:]
        ids = [pl.program_id(ax) for ax in range(len(grid))]
        first = functools.reduce(jnp.logical_and, [i == 0 for i in ids])
        last = functools.reduce(jnp.logical_and, [i == g - 1 for i, g in zip(ids, grid)])

        @pl.when(first)
        def _():
            comm.start(operands, send_sems, recv_sems)

        body(*ins, *outs, *scratch)

        @pl.when(last)
        def _():
            comm.finish(operands, send_sems, recv_sems)

    res = pl.pallas_call(
        carried, name=name, grid=grid, in_specs=list(in_specs) + [ANY] * k, out_specs=list(out_specs) + [ANY] * k,
        out_shape=list(out_shape) + [jax.ShapeDtypeStruct(t.shape, t.dtype) for t in comm.operands],
        scratch_shapes=list(scratch_shapes) + [pltpu.SemaphoreType.DMA((comm.n_sems,))] * 2,
        input_output_aliases={n_in + i: n_out + i for i in range(k)},
        compiler_params=_cparams(len(grid)),
    )(*args, *comm.operands)
    return res[:n_out], res[n_out:]


def _mm(a, b, *, mode, name, out_dtype, n_q, tm, tn, tk, qa, qb, qo=lambda q: q, n_qr=1, out_q=None, comm=None):
    if mode == "nn":
        m, kdim, n = a.shape[1], a.shape[2], b.shape[2]
    elif mode == "nt":
        m, kdim, n = a.shape[1], a.shape[2], b.shape[1]
    else:
        kdim, m, n = a.shape[1], a.shape[2], b.shape[2]
    assert m % tm == 0 and n % tn == 0 and kdim % tk == 0, (name, m, n, kdim, tm, tn, tk)
    kr_n = kdim // tk
    nr = n_qr * kr_n
    out_q = n_q if out_q is None else out_q

    def split(r):
        return (r // kr_n, r % kr_n) if n_qr > 1 else (0, r)

    if mode == "tn":
        a_spec = pl.BlockSpec((None, tk, tm), lambda q, i, j, r: (qa(q, split(r)[0]), split(r)[1], i))
    else:
        a_spec = pl.BlockSpec((None, tm, tk), lambda q, i, j, r: (qa(q, split(r)[0]), i, split(r)[1]))
    if mode == "nt":
        b_spec = pl.BlockSpec((None, tn, tk), lambda q, i, j, r: (qb(q, split(r)[0]), j, split(r)[1]))
    else:
        b_spec = pl.BlockSpec((None, tk, tn), lambda q, i, j, r: (qb(q, split(r)[0]), split(r)[1], j))
    o_spec = pl.BlockSpec((None, tm, tn), lambda q, i, j, r: (qo(q), i, j))
    dims = _DIMS[mode]

    n_comm = len(comm.operands) if comm is not None else 0
    grid = (n_q, m // tm, n // tn, nr)

    def body(*refs):
        a_ref, b_ref = refs[0], refs[1]
        o_ref = refs[2 + n_comm]
        if comm is not None:
            comm_refs = refs[3 + n_comm:3 + 2 * n_comm]
            send_sems, recv_sems = refs[-2:]
            ids = [pl.program_id(ax) for ax in range(4)]
            first = functools.reduce(jnp.logical_and, [i == 0 for i in ids])
            last = functools.reduce(jnp.logical_and, [i == g - 1 for i, g in zip(ids, grid)])

            @pl.when(first)
            def _():
                comm.start(comm_refs, send_sems, recv_sems)
        lhs, rhs = a_ref[...], b_ref[...]
        if lhs.dtype != BF16:
            lhs = lhs.astype(BF16)
        if rhs.dtype != BF16:
            rhs = rhs.astype(BF16)
        part = lax.dot_general(lhs, rhs, dims, preferred_element_type=F32)
        if nr == 1:
            o_ref[...] = part.astype(o_ref.dtype)
        else:
            acc_ref = refs[3 + 2 * n_comm]
            r = pl.program_id(3)

            @pl.when(r == 0)
            def _():
                acc_ref[...] = part

            @pl.when(r > 0)
            def _():
                acc_ref[...] += part

            @pl.when(r == nr - 1)
            def _():
                o_ref[...] = acc_ref[...].astype(o_ref.dtype)
        if comm is not None:
            @pl.when(last)
            def _():
                comm.finish(comm_refs, send_sems, recv_sems)

    scratch = [pltpu.VMEM((tm, tn), F32)] if nr > 1 else []
    out_shape = [jax.ShapeDtypeStruct((out_q, m, n), out_dtype)]
    args = [a, b]
    if comm is not None:
        args += comm.operands
        out_shape += [jax.ShapeDtypeStruct(t.shape, t.dtype) for t in comm.operands]
        scratch += [pltpu.SemaphoreType.DMA((comm.n_sems,)), pltpu.SemaphoreType.DMA((comm.n_sems,))]
    outs = pl.pallas_call(
        body, name=name, grid=grid,
        in_specs=[a_spec, b_spec] + [ANY] * n_comm, out_specs=[o_spec] + [ANY] * n_comm,
        out_shape=out_shape, scratch_shapes=scratch,
        input_output_aliases={2 + i: 1 + i for i in range(n_comm)},
        compiler_params=_cparams(4),
    )(*args)
    return (outs[0], outs[1:]) if comm is not None else outs[0]


def _q0(q, qr):
    return 0


def _qq(q, qr):
    return q


def _qr(q, qr):
    return qr


def _mm_dd(a3, w6, widx, *, mode, name, out_dtype, n_q=1, n_qr=1, qa=_q0, comm=None):
    d = w6.shape[1]
    f32_in = a3.dtype != BF16
    tm = _tile(a3.shape[1] if mode != "tn" else a3.shape[2], 512 if f32_in else 1024)
    if n_qr > 1:
        qb = lambda q, qr: widx + qr
    elif n_q > 1:
        qb = lambda q, qr: widx + q
    else:
        qb = lambda q, qr: widx
    return _mm(a3, w6, mode=mode, name=name, out_dtype=out_dtype, n_q=n_q, n_qr=n_qr,
               tm=tm, tn=_tile(d, 1024), tk=d, qa=qa, qb=qb, comm=comm)


def _mm_wgrad(a3, b3, *, name, n_q=1, qa=_q0, qb=_q0):
    s, d = a3.shape[1], a3.shape[2]
    f32_in = a3.dtype != BF16 or b3.dtype != BF16
    return _mm(a3, b3, mode="tn", name=name, out_dtype=BF16, n_q=n_q, tm=_tile(d, 1024), tn=b3.shape[2],
               tk=_tile(s, 512 if f32_in else 1024), qa=qa, qb=qb)


def _grp_mm(a, b, *, mode, name, out_dtype, tm, w=None):
    s = a.shape[0]
    if mode == "tn":
        cg = b.shape[1] // len(POOL_WINDOWS)
        g_n = len(POOL_WINDOWS)
        ts = _tile(s, tm)
        nr = s // ts

        def body(a_ref, b_ref, o_ref, acc_ref):
            r = pl.program_id(1)
            part = lax.dot_general(a_ref[...], b_ref[...], _DIMS["tn"], preferred_element_type=F32)

            @pl.when(r == 0)
            def _():
                acc_ref[...] = part

            @pl.when(r > 0)
            def _():
                acc_ref[...] += part

            @pl.when(r == nr - 1)
            def _():
                o_ref[...] = acc_ref[...].astype(o_ref.dtype)

        return pl.pallas_call(
            body, name=name, grid=(g_n, nr),
            in_specs=[pl.BlockSpec((ts, cg), lambda g, r: (r, g)), pl.BlockSpec((ts, cg), lambda g, r: (r, g))],
            out_specs=pl.BlockSpec((None, cg, cg), lambda g, r: (g, 0, 0)),
            out_shape=jax.ShapeDtypeStruct((g_n, cg, cg), out_dtype),
            scratch_shapes=[pltpu.VMEM((cg, cg), F32)],
            compiler_params=_cparams(2),
        )(a, b)
    g_n, cg = w.shape[0], w.shape[1]
    ts = _tile(s, tm)
    dims = _DIMS[mode]

    def body(a_ref, w_ref, o_ref):
        o_ref[...] = lax.dot_general(a_ref[...], w_ref[...], dims, preferred_element_type=F32).astype(o_ref.dtype)

    return pl.pallas_call(
        body, name=name, grid=(g_n, s // ts),
        in_specs=[pl.BlockSpec((ts, cg), lambda g, i: (i, g)), pl.BlockSpec((None, cg, cg), lambda g, i: (g, 0, 0))],
        out_specs=pl.BlockSpec((ts, cg), lambda g, i: (i, g)),
        out_shape=jax.ShapeDtypeStruct((s, g_n * cg), out_dtype),
        compiler_params=_cparams(2),
    )(a, w)


def _causal_ext(load, r0, rows, halo):
    cur = load(r0, rows)
    prev = load(pl.multiple_of(jnp.maximum(r0 - halo, 0), halo), halo)
    prev = jnp.where(r0 > 0, prev, jnp.zeros_like(prev))
    return jnp.concatenate([prev, cur], axis=0)


def _anti_ext(load, r0, rows, halo, s):
    cur = load(r0, rows)
    nxt = load(pl.multiple_of(jnp.minimum(r0 + rows, s - halo), halo), halo)
    nxt = jnp.where(r0 + rows < s, nxt, jnp.zeros_like(nxt))
    return jnp.concatenate([cur, nxt], axis=0)


def _down(ext, k):
    return pltpu.roll(ext, k, axis=0)


def _up(ext, k):
    return pltpu.roll(ext, ext.shape[0] - k, axis=0)


def _fold8(x):
    return jnp.sum(x.reshape(x.shape[0] // 8, 8, x.shape[1]), axis=0)


def _sigmoid(x):
    return 0.5 * jnp.tanh(0.5 * x) + 0.5


def _pool(p, *, backward, name, rows=64, comm=None):
    s, d = p.shape
    strips_per_group = (d // len(POOL_WINDOWS)) // LANES
    assert strips_per_group * LANES * len(POOL_WINDOWS) == d and s % rows == 0

    def body(p_ref, o_ref):
        g = pl.program_id(0) // strips_per_group
        win = jnp.left_shift(2, g).astype(F32)

        def load(r0, n):
            return p_ref[pl.ds(r0, n), :]

        def pick(levels):
            return jnp.where(g == 0, levels[0], jnp.where(g == 1, levels[1], jnp.where(g == 2, levels[2], levels[3])))

        def chunk(c, carry):
            r0 = pl.multiple_of(c * rows, rows)
            if not backward:
                ext = _causal_ext(load, r0, rows, POOL_HALO)
                levels, acc = [], ext
                for k in (1, 2, 4, 8):
                    acc = acc + _down(acc, k)
                    levels.append(acc)
                t = (r0 + lax.broadcasted_iota(jnp.int32, (rows, LANES), 0)).astype(F32)
                cnt = jnp.minimum(t + 1.0, win)
                out = pick(levels)[POOL_HALO:] / cnt - ext[POOL_HALO:]
            else:
                ext = _anti_ext(load, r0, rows, POOL_HALO, s)
                t = (r0 + lax.broadcasted_iota(jnp.int32, (rows + POOL_HALO, LANES), 0)).astype(F32)
                e = ext / jnp.minimum(t + 1.0, win)
                levels, acc = [], e
                for k in (1, 2, 4, 8):
                    acc = acc + _up(acc, k)
                    levels.append(acc)
                out = pick(levels)[:rows] - ext[:rows]
            o_ref[pl.ds(r0, rows), :] = out.astype(o_ref.dtype)
            return carry

        lax.fori_loop(0, s // rows, chunk, 0)

    res = _pallas(
        body, name=name, grid=(d // LANES,),
        in_specs=[pl.BlockSpec((s, LANES), lambda j: (0, j))],
        out_specs=[pl.BlockSpec((s, LANES), lambda j: (0, j))],
        out_shape=[jax.ShapeDtypeStruct((s, d), BF16)], args=[p], comm=comm)
    return res[0] if comm is None else (res[0][0], res[1])


def _grp_fwd(pooled, w_grp, scale, *, name, comm=None):
    s, d = pooled.shape
    g_n, cg = w_grp.shape[0], w_grp.shape[1]
    ts = _tile(s, 1024)

    def body(a_ref, w_ref, sc_ref, mg_ref, mx_ref):
        mg = jnp.dot(a_ref[...], w_ref[...], preferred_element_type=F32)
        mg_ref[...] = mg.astype(BF16)
        mx_ref[...] = (mg * sc_ref[...]).astype(BF16)

    blk = pl.BlockSpec((ts, cg), lambda g, i: (i, g))
    return _pallas(
        body, name=name, grid=(g_n, s // ts),
        in_specs=[blk, pl.BlockSpec((None, cg, cg), lambda g, i: (g, 0, 0)), pl.BlockSpec((1, cg), lambda g, i: (0, g))],
        out_specs=[blk, blk],
        out_shape=[jax.ShapeDtypeStruct((s, d), BF16)] * 2, args=[pooled, w_grp, scale], comm=comm)


def _grp_bwd_pre(dmixed, mg, scale, *, name):
    s, d = dmixed.shape
    ts = _tile(s, 256, 16)

    def body(dm_ref, mg_ref, sc_ref, dmg_ref, dsc_ref):
        dm = dm_ref[...]
        dmg_ref[...] = (dm * sc_ref[...]).astype(BF16)
        part = jnp.sum(dm * mg_ref[...].astype(F32), axis=0, keepdims=True)

        @pl.when(pl.program_id(0) == 0)
        def _():
            dsc_ref[...] = part

        @pl.when(pl.program_id(0) > 0)
        def _():
            dsc_ref[...] += part

    blk = pl.BlockSpec((ts, d), lambda i: (i, 0))
    vec = pl.BlockSpec((1, d), lambda i: (0, 0))
    return pl.pallas_call(
        body, name=name, grid=(s // ts,),
        in_specs=[blk, blk, vec], out_specs=[blk, vec],
        out_shape=[jax.ShapeDtypeStruct((s, d), BF16), jax.ShapeDtypeStruct((1, d), F32)],
        compiler_params=_cparams(1),
    )(dmixed, mg, scale)


def _ln_fwd(res, mm, g, b, *, name, comm=None):
    s, d = res.shape
    ts = _tile(s, 256, 16)

    def body(res_ref, mm_ref, g_ref, b_ref, h_ref, hb_ref, xh_ref, rs_ref):
        z = ALPHA * res_ref[...] + mm_ref[...]
        mu = jnp.mean(z, axis=-1, keepdims=True)
        zc = z - mu
        var = jnp.mean(zc * zc, axis=-1, keepdims=True)
        rstd = lax.rsqrt(var + LN_EPS)
        xhat = zc * rstd
        h = xhat * g_ref[...] + b_ref[...]
        h_ref[...] = h
        hb_ref[...] = h.astype(BF16)
        xh_ref[...] = xhat
        rs_ref[...] = rstd

    blk = pl.BlockSpec((ts, d), lambda i: (i, 0))
    vec = pl.BlockSpec((1, d), lambda i: (0, 0))
    return _pallas(
        body, name=name, grid=(s // ts,),
        in_specs=[blk, blk, vec, vec],
        out_specs=[blk, blk, blk, pl.BlockSpec((ts, 1), lambda i: (i, 0))],
        out_shape=[jax.ShapeDtypeStruct((s, d), F32), jax.ShapeDtypeStruct((s, d), BF16),
                   jax.ShapeDtypeStruct((s, d), F32), jax.ShapeDtypeStruct((s, 1), F32)],
        args=[res, mm, g, b], comm=comm)


def _ln_bwd(dres, dmm, xhat, rstd, g, *, name):
    s, d = dmm.shape
    ts = _tile(s, 256, 16)
    has_res = dres is not None

    def body(*refs):
        if has_res:
            dres_ref, dmm_ref, xh_ref, rs_ref, g_ref, dz_ref, dzb_ref, dg_ref, db_ref = refs
            dh = ALPHA * dres_ref[...] + dmm_ref[...]
        else:
            dmm_ref, xh_ref, rs_ref, g_ref, dz_ref, dzb_ref, dg_ref, db_ref = refs
            dh = dmm_ref[...]
        xhat_ = xh_ref[...]
        dxh = dh * g_ref[...]
        c1 = jnp.mean(dxh, axis=-1, keepdims=True)
        c2 = jnp.mean(dxh * xhat_, axis=-1, keepdims=True)
        dz = rs_ref[...] * (dxh - c1 - xhat_ * c2)
        dz_ref[...] = dz
        dzb_ref[...] = dz.astype(BF16)
        dg_part = jnp.sum(dh * xhat_, axis=0, keepdims=True)
        db_part = jnp.sum(dh, axis=0, keepdims=True)

        @pl.when(pl.program_id(0) == 0)
        def _():
            dg_ref[...] = dg_part
            db_ref[...] = db_part

        @pl.when(pl.program_id(0) > 0)
        def _():
            dg_ref[...] += dg_part
            db_ref[...] += db_part

    blk = pl.BlockSpec((ts, d), lambda i: (i, 0))
    vec = pl.BlockSpec((1, d), lambda i: (0, 0))
    col = pl.BlockSpec((ts, 1), lambda i: (i, 0))
    ins = ([dres] if has_res else []) + [dmm, xhat, rstd, g]
    in_specs = ([blk] if has_res else []) + [blk, blk, col, vec]
    return pl.pallas_call(
        body, name=name, grid=(s // ts,),
        in_specs=in_specs, out_specs=[blk, blk, vec, vec],
        out_shape=[jax.ShapeDtypeStruct((s, d), F32), jax.ShapeDtypeStruct((s, d), BF16),
                   jax.ShapeDtypeStruct((1, d), F32), jax.ShapeDtypeStruct((1, d), F32)],
        compiler_params=_cparams(1),
    )(*ins)


def _loss_head(h, tgt, *, name):
    s, d = h.shape
    ts = _tile(s, 256, 16)

    def body(h_ref, t_ref, dh_ref, loss_ref):
        err = h_ref[...] - t_ref[...]
        dh_ref[...] = err * (1.0 / d)
        part = 0.5 * jnp.sum(jnp.mean(err * err, axis=-1, keepdims=True), axis=0, keepdims=True)

        @pl.when(pl.program_id(0) == 0)
        def _():
            loss_ref[...] = part

        @pl.when(pl.program_id(0) > 0)
        def _():
            loss_ref[...] += part

    blk = pl.BlockSpec((ts, d), lambda i: (i, 0))
    return pl.pallas_call(
        body, name=name, grid=(s // ts,),
        in_specs=[blk, blk], out_specs=[blk, pl.BlockSpec((1, 1), lambda i: (0, 0))],
        out_shape=[jax.ShapeDtypeStruct((s, d), F32), jax.ShapeDtypeStruct((1, 1), F32)],
        compiler_params=_cparams(1),
    )(h, tgt)


def _conv(ext, w, bias):
    c = bias + _down(ext, 2) * w[0:1] + _down(ext, 1) * w[1:2] + ext * w[2:3]
    return c[CONV_HALO:]


def _act_specs(s, n2):
    n_strips = pl.cdiv(n2, LANES)
    u_spec = pl.BlockSpec((None, 2, s, LANES), lambda hh, j: (hh, 0, 0, j))
    cwg = pl.BlockSpec((None, 3, LANES), lambda hh, j: (hh, 0, j))
    cwv = pl.BlockSpec((None, 3, LANES), lambda hh, j: (hh + 2, 0, j))
    cbg = pl.BlockSpec((None, 1, LANES), lambda hh, j: (hh, 0, j))
    cbv = pl.BlockSpec((None, 1, LANES), lambda hh, j: (hh + 2, 0, j))
    return n_strips, u_spec, cwg, cwv, cbg, cbv


def _act_fwd(u, cw, cb, *, name, rows=64, comm=None):
    _, _, s, n2 = u.shape
    n_strips, u_spec, cwg, cwv, cbg, cbv = _act_specs(s, n2)

    def body(u_ref, wg_ref, wv_ref, bg_ref, bv_ref, a_ref):
        wg, wv, bg, bv = wg_ref[...], wv_ref[...], bg_ref[...], bv_ref[...]

        def chunk(c, carry):
            r0 = pl.multiple_of(c * rows, rows)
            cg = _conv(_causal_ext(lambda r, n: u_ref[0, pl.ds(r, n), :].astype(F32), r0, rows, CONV_HALO), wg, bg)
            cv = _conv(_causal_ext(lambda r, n: u_ref[1, pl.ds(r, n), :].astype(F32), r0, rows, CONV_HALO), wv, bv)
            a_ref[pl.ds(r0, rows), :] = (cg * _sigmoid(cg) * cv).astype(BF16)
            return carry

        lax.fori_loop(0, s // rows, chunk, 0)

    res = _pallas(
        body, name=name, grid=(2, n_strips),
        in_specs=[u_spec, cwg, cwv, cbg, cbv],
        out_specs=[pl.BlockSpec((None, s, LANES), lambda hh, j: (hh, 0, j))],
        out_shape=[jax.ShapeDtypeStruct((2, s, n2), BF16)], args=[u, cw, cw, cb, cb], comm=comm)
    return res[0] if comm is None else (res[0][0], res[1])


def _act_bwd(u, da, cw, cb, *, name, rows=64):
    _, _, s, n2 = u.shape
    n_strips, u_spec, cwg, cwv, cbg, cbv = _act_specs(s, n2)
    n_chunks = s // rows

    def body(u_ref, da_ref, wg_ref, wv_ref, bg_ref, bv_ref, du_ref, dwg_ref, dwv_ref, dbg_ref, dbv_ref, dg_s, dv_s):
        wg, wv, bg, bv = wg_ref[...], wv_ref[...], bg_ref[...], bv_ref[...]

        def first(c, sums):
            r0 = pl.multiple_of(c * rows, rows)
            eg = _causal_ext(lambda r, n: u_ref[0, pl.ds(r, n), :].astype(F32), r0, rows, CONV_HALO)
            ev = _causal_ext(lambda r, n: u_ref[1, pl.ds(r, n), :].astype(F32), r0, rows, CONV_HALO)
            cg, cv = _conv(eg, wg, bg), _conv(ev, wv, bv)
            sg = _sigmoid(cg)
            dact = da_ref[pl.ds(r0, rows), :]
            dval = dact * (cg * sg)
            dgate = dact * cv * (sg * (1.0 + cg * (1.0 - sg)))
            dg_s[pl.ds(r0, rows), :] = dgate
            dv_s[pl.ds(r0, rows), :] = dval
            new = []
            for dc, ext in ((dgate, eg), (dval, ev)):
                new += [_fold8(dc * _down(ext, 2)[CONV_HALO:]), _fold8(dc * _down(ext, 1)[CONV_HALO:]),
                        _fold8(dc * ext[CONV_HALO:]), _fold8(dc)]
            return tuple(acc + x for acc, x in zip(sums, new))

        zero = jnp.zeros((8, LANES), F32)
        sums = lax.fori_loop(0, n_chunks, first, (zero,) * 8)
        red = [jnp.sum(x, axis=0, keepdims=True) for x in sums]
        dwg_ref[...] = jnp.concatenate(red[0:3], axis=0)
        dbg_ref[...] = red[3]
        dwv_ref[...] = jnp.concatenate(red[4:7], axis=0)
        dbv_ref[...] = red[7]

        def second(c, carry):
            r0 = pl.multiple_of(c * rows, rows)
            for gv, (src, w) in enumerate(((dg_s, wg), (dv_s, wv))):
                ext = _anti_ext(lambda r, n: src[pl.ds(r, n), :], r0, rows, CONV_HALO, s)
                du = ext * w[2:3] + _up(ext, 1) * w[1:2] + _up(ext, 2) * w[0:1]
                du_ref[gv, pl.ds(r0, rows), :] = du[:rows].astype(BF16)
            return carry

        lax.fori_loop(0, n_chunks, second, 0)

    w_out = pl.BlockSpec((None, 3, LANES), lambda hh, j: (hh, 0, j))
    b_out = pl.BlockSpec((None, 1, LANES), lambda hh, j: (hh, 0, j))
    return pl.pallas_call(
        body, name=name, grid=(2, n_strips),
        in_specs=[u_spec, pl.BlockSpec((None, s, LANES), lambda hh, j: (hh, 0, j)), cwg, cwv, cbg, cbv],
        out_specs=[u_spec, w_out, w_out, b_out, b_out],
        out_shape=[jax.ShapeDtypeStruct((2, 2, s, n2), BF16),
                   jax.ShapeDtypeStruct((2, 3, n2), F32), jax.ShapeDtypeStruct((2, 3, n2), F32),
                   jax.ShapeDtypeStruct((2, 1, n2), F32), jax.ShapeDtypeStruct((2, 1, n2), F32)],
        scratch_shapes=[pltpu.VMEM((s, LANES), F32), pltpu.VMEM((s, LANES), F32)],
        compiler_params=_cparams(2),
    )(u, da, cw, cw, cb, cb)


def _dot_nt(a, b):
    return lax.dot_general(a, b, _DIMS["nt"], preferred_element_type=F32)


def _dot_tn(a, b):
    return lax.dot_general(a, b, _DIMS["tn"], preferred_element_type=F32)


def _band_mask(b, ATTN_QROWS):
    qi = lax.broadcasted_iota(jnp.int32, (ATTN_QROWS, ATTN_QROWS + ATTN_BLOCK), 0)
    kj = lax.broadcasted_iota(jnp.int32, (ATTN_QROWS, ATTN_QROWS + ATTN_BLOCK), 1)
    band = jnp.logical_and(kj >= qi, kj <= qi + ATTN_BLOCK)
    return jnp.logical_and(band, jnp.logical_or(b > 0, kj >= ATTN_BLOCK))


def _to_residues(nat, rm, d, seq, pad):
    seg = seq + pad
    for r in range(d):
        if pad:
            rm[pl.ds(r * seg, pad), :] = jnp.zeros((pad, LANES), rm.dtype)
        rows = nat[pl.ds(r, seq, stride=d), :] if d > 1 else nat[...]
        rm[pl.ds(r * seg + pad, seq), :] = rows.astype(rm.dtype)


def _rows_loop(s, rows, fn):
    def step(c, carry):
        fn(pl.ds(pl.multiple_of(c * rows, rows), rows))
        return carry

    lax.fori_loop(0, s // rows, step, 0)


def _attn_fwd(qkv, *, name, comm=None):
    _, s, dm = qkv.shape
    heads, scale = dm // HEAD_DIM, 1.0 / math.sqrt(HEAD_DIM)
    pad_rows = s + ATTN_BLOCK * max(DILATIONS)
    ATTN_QROWS, ATTN_UNROLL = ATTN_FWD_TILE

    def body(q_ref, k_ref, v_ref, o_ref, ob_ref, lse_ref, nat, rq, rk, rv, ro, rl, o_tmp, l_tmp, o_acc, m_acc, s_acc):
        for d in DILATIONS:
            seq = s // d
            nb, seg = seq // ATTN_QROWS, seq + ATTN_BLOCK
            for src, dst, pad in ((q_ref, rq, 0), (k_ref, rk, ATTN_BLOCK), (v_ref, rv, ATTN_BLOCK)):
                if d == 1:
                    _to_residues(src, dst, d, seq, pad)
                else:
                    nat[...] = src[...].astype(F32)
                    _to_residues(nat, dst, d, seq, pad)

            def block(idx, carry):
                r, b = idx // nb, idx % nb
                qrows = pl.ds(pl.multiple_of(r * seq + b * ATTN_QROWS, ATTN_BLOCK), ATTN_QROWS)
                krows = pl.ds(pl.multiple_of(r * seg + b * ATTN_QROWS, ATTN_BLOCK), ATTN_QROWS + ATTN_BLOCK)
                sc = jnp.where(_band_mask(b, ATTN_QROWS),_dot_nt(rq[qrows, :], rk[krows, :]) * scale, NEG_INF)
                m = jnp.max(sc, axis=-1, keepdims=True)
                p = jnp.exp(sc - m)
                den = jnp.sum(p, axis=-1, keepdims=True)
                ro[qrows, :] = jnp.dot(p.astype(BF16), rv[krows, :], preferred_element_type=F32) / den
                rl[qrows, :] = jnp.broadcast_to(m + jnp.log(den), (ATTN_QROWS, LANES))
                return carry

            lax.fori_loop(0, d * nb, block, 0, unroll=ATTN_UNROLL)
            if d == 1:
                def first(rows):
                    o_acc[rows, :] = ro[rows, :]
                    m_acc[rows, :] = rl[rows, :]
                    s_acc[rows, :] = jnp.ones((rows.size, LANES), F32)

                _rows_loop(s, 64, first)
            else:
                for r in range(d):
                    o_tmp[pl.ds(r, seq, stride=d), :] = ro[pl.ds(r * seq, seq), :]
                    l_tmp[pl.ds(r, seq, stride=d), :] = rl[pl.ds(r * seq, seq), :]

                def merge(rows):
                    m_old, l_new = m_acc[rows, :], l_tmp[rows, :]
                    m_new = jnp.maximum(m_old, l_new)
                    w_old, w_new = jnp.exp(m_old - m_new), jnp.exp(l_new - m_new)
                    o_acc[rows, :] = o_acc[rows, :] * w_old + o_tmp[rows, :] * w_new
                    s_acc[rows, :] = s_acc[rows, :] * w_old + w_new
                    m_acc[rows, :] = m_new

                _rows_loop(s, 64, merge)

        def finish(rows):
            tot = s_acc[rows, :]
            o = o_acc[rows, :] / tot
            o_ref[rows, :] = o
            ob_ref[rows, :] = o.astype(BF16)
            lse_ref[rows, :] = m_acc[rows, :] + jnp.log(tot)

        _rows_loop(s, 64, finish)

    head = lambda i: pl.BlockSpec((None, s, HEAD_DIM), lambda h, i=i: (i, 0, h))
    out = pl.BlockSpec((s, HEAD_DIM), lambda h: (0, h))
    nat_f32 = pltpu.VMEM((s, LANES), F32)
    return _pallas(
        body, name=name, grid=(heads,),
        in_specs=[head(0), head(1), head(2)], out_specs=[out, out, out],
        out_shape=[jax.ShapeDtypeStruct((s, dm), F32), jax.ShapeDtypeStruct((s, dm), BF16),
                   jax.ShapeDtypeStruct((s, dm), F32)],
        scratch_shapes=[nat_f32, pltpu.VMEM((s, LANES), BF16), pltpu.VMEM((pad_rows, LANES), BF16),
                        pltpu.VMEM((pad_rows, LANES), BF16), nat_f32, nat_f32, nat_f32, nat_f32, nat_f32, nat_f32, nat_f32],
        args=[qkv, qkv, qkv], comm=comm)


def _attn_delta(do, o, *, name):
    s, dm = o.shape
    heads = dm // HEAD_DIM
    ts = _tile(s, 256, 16)

    def body(do_ref, o_ref, dl_ref):
        for h in range(heads):
            hs = slice(h * HEAD_DIM, (h + 1) * HEAD_DIM)
            row = jnp.sum(do_ref[:, hs].astype(F32) * o_ref[:, hs], axis=-1, keepdims=True)
            dl_ref[:, hs] = jnp.broadcast_to(row, (ts, HEAD_DIM))

    blk = pl.BlockSpec((ts, dm), lambda i: (i, 0))
    return pl.pallas_call(
        body, name=name, grid=(s // ts,),
        in_specs=[blk, blk], out_specs=blk,
        out_shape=jax.ShapeDtypeStruct((s, dm), F32),
        compiler_params=_cparams(1),
    )(do, o)


def _attn_bwd(qkv, do, lse, delta, *, name):
    _, s, dm = qkv.shape
    heads, scale = dm // HEAD_DIM, 1.0 / math.sqrt(HEAD_DIM)
    pad_rows = s + ATTN_BLOCK * max(DILATIONS)
    ATTN_QROWS, ATTN_UNROLL = ATTN_BWD_TILE

    def body(q_ref, k_ref, v_ref, do_ref, l_ref, dl_ref, out_ref,
             nat, rq, rdo, rk, rv, rl, rdl, rdq, kc, kp, vc, vp, aq, ak, av):
        for d in DILATIONS:
            seq = s // d
            nb, seg = seq // ATTN_QROWS, seq + ATTN_BLOCK
            for src, dst, pad in ((q_ref, rq, 0), (do_ref, rdo, 0), (k_ref, rk, ATTN_BLOCK), (v_ref, rv, ATTN_BLOCK)):
                if d == 1:
                    _to_residues(src, dst, d, seq, pad)
                else:
                    nat[...] = src[...].astype(F32)
                    _to_residues(nat, dst, d, seq, pad)
            if d == 1:
                lse_rows, dl_rows = l_ref, dl_ref
            else:
                lse_rows, dl_rows = rl, rdl
                _to_residues(l_ref, rl, d, seq, 0)
                _to_residues(dl_ref, rdl, d, seq, 0)
            def clear(rows):
                kp[rows, :] = jnp.zeros((rows.size, LANES), F32)
                vp[rows, :] = jnp.zeros((rows.size, LANES), F32)

            _rows_loop(d * seg, ATTN_BLOCK, clear)

            def block(idx, carry):
                r, b = idx // nb, idx % nb
                qrows = pl.ds(pl.multiple_of(r * seq + b * ATTN_QROWS, ATTN_BLOCK), ATTN_QROWS)
                krow = pl.multiple_of(r * seg + b * ATTN_QROWS, ATTN_BLOCK)
                krows = pl.ds(krow, ATTN_QROWS + ATTN_BLOCK)
                before, own = pl.ds(krow, ATTN_BLOCK), pl.ds(krow + ATTN_BLOCK, ATTN_QROWS)
                qb, dob, kw, vw = rq[qrows, :], rdo[qrows, :], rk[krows, :], rv[krows, :]
                lse_b = jnp.concatenate([lse_rows[qrows, :]] * (ATTN_QROWS // LANES + 1), axis=1)
                dl_b = jnp.concatenate([dl_rows[qrows, :]] * (ATTN_QROWS // LANES + 1), axis=1)
                sc = jnp.where(_band_mask(b, ATTN_QROWS),_dot_nt(qb, kw) * scale, NEG_INF)
                p = jnp.exp(sc - lse_b)
                ds = (p * (_dot_nt(dob, vw) - dl_b) * scale).astype(BF16)
                rdq[qrows, :] = jnp.dot(ds, kw, preferred_element_type=F32)
                dk, dv = _dot_tn(ds, qb), _dot_tn(p.astype(BF16), dob)
                kp[before, :] = dk[:ATTN_BLOCK]
                kc[own, :] = dk[ATTN_BLOCK:]
                vp[before, :] = dv[:ATTN_BLOCK]
                vc[own, :] = dv[ATTN_BLOCK:]
                return carry

            lax.fori_loop(0, d * nb, block, 0, unroll=ATTN_UNROLL)
            for r in range(d):
                keys = pl.ds(r * seg + ATTN_BLOCK, seq)
                rows = pl.ds(r, seq, stride=d) if d > 1 else pl.ds(0, seq)
                for acc, val in ((aq, rdq[pl.ds(r * seq, seq), :]), (ak, kc[keys, :] + kp[keys, :]),
                                 (av, vc[keys, :] + vp[keys, :])):
                    if d == DILATIONS[0]:
                        acc[rows, :] = val
                    else:
                        acc[rows, :] += val

        def finish(rows):
            for i, acc in enumerate((aq, ak, av)):
                out_ref[i, rows, :] = acc[rows, :].astype(BF16)

        _rows_loop(s, 256, finish)

    head = lambda i: pl.BlockSpec((None, s, HEAD_DIM), lambda h, i=i: (i, 0, h))
    col = pl.BlockSpec((s, HEAD_DIM), lambda h: (0, h))
    f32 = lambda rows: pltpu.VMEM((rows, LANES), F32)
    b16 = lambda rows: pltpu.VMEM((rows, LANES), BF16)
    return pl.pallas_call(
        body, name=name, grid=(heads,),
        in_specs=[head(0), head(1), head(2), col, col, col],
        out_specs=pl.BlockSpec((3, s, HEAD_DIM), lambda h: (0, 0, h)),
        out_shape=jax.ShapeDtypeStruct((3, s, dm), BF16),
        scratch_shapes=[f32(s), b16(s), b16(s), b16(pad_rows), b16(pad_rows), f32(s), f32(s),
                        f32(s), f32(pad_rows), f32(pad_rows), f32(pad_rows), f32(pad_rows), f32(s), f32(s), f32(s)],
        compiler_params=_cparams(1),
    )(qkv, qkv, qkv, do, lse, delta)


def _ew(fn, ins, out_dtypes, *, name, tile_bytes=1 << 20):
    first = ins[0][0] if isinstance(ins[0], tuple) else ins[0]
    rows, cols = first.shape[-2], first.shape[-1]
    tr = _tile(rows, max(16, tile_bytes // (4 * cols)), 16)
    n_in = len(ins)

    def body(*refs):
        outs = fn(*[r[...] for r in refs[:n_in]])
        for o_ref, val in zip(refs[n_in:], outs):
            o_ref[...] = val.astype(o_ref.dtype)

    in_specs, args = [], []
    for item in ins:
        if isinstance(item, tuple):
            arr, lead = item
            in_specs.append(pl.BlockSpec((None, tr, cols), lambda i, lead=lead: (lead, i, 0)))
            args.append(arr)
        else:
            in_specs.append(pl.BlockSpec((tr, cols), lambda i: (i, 0)))
            args.append(item)
    blk = pl.BlockSpec((tr, cols), lambda i: (i, 0))
    return pl.pallas_call(
        body, name=name, grid=(rows // tr,),
        in_specs=in_specs, out_specs=[blk] * len(out_dtypes),
        out_shape=[jax.ShapeDtypeStruct((rows, cols), dt) for dt in out_dtypes],
        compiler_params=_cparams(1),
    )(*args)


def _adamw_math(g, w, m, v):
    m2 = ADAM_B1 * m + (1.0 - ADAM_B1) * g
    v2 = ADAM_B2 * v + (1.0 - ADAM_B2) * (g * g)
    m_hat = m2 / (1.0 - ADAM_B1 ** ADAM_STEP)
    v_hat = v2 / (1.0 - ADAM_B2 ** ADAM_STEP)
    delta = -ADAM_LR * (m_hat / (jnp.sqrt(v_hat) + ADAM_EPS) + ADAM_WD * w)
    return g, delta, m2, v2


def _scalars(*vals):
    return jnp.stack([jnp.asarray(v, jnp.int32) for v in vals])


def _adamw_halves(mine, theirs, core, w, m, v, *, name, lead=0, prev=None):
    shape = w.shape
    a_n, rh, cols = mine.shape
    w3, m3, v3 = (t.reshape(-1, 2 * rh, cols) for t in (w, m, v))
    tr, tc = _tile2(rh, cols, 1 << 17, 8)
    n_i = rh // tr

    def body(c_ref, mine_ref, theirs_ref, w_ref, m_ref, v_ref, *rest):
        g = jnp.where(pl.program_id(1) == c_ref[0], mine_ref[...], theirs_ref[...])
        outs = _adamw_math(g, w_ref[...], m_ref[...], v_ref[...])
        for ref, val in zip(rest[-4:], outs):
            ref[...] = val

    def half(mine_rows):
        def index(a, h, i, j, c_ref):
            use = (h == c_ref[0]) if mine_rows else (h != c_ref[0])
            return (a, jnp.where(use, i, 0), jnp.where(use, j, 0))
        return pl.BlockSpec((None, tr, tc), index)

    full = pl.BlockSpec((None, tr, tc), lambda a, h, i, j, c_ref: (lead + a, h * n_i + i, j))
    args = [mine, theirs, w3, m3, v3]
    in_specs = [half(True), half(False), full, full, full]
    aliases = {}
    if prev is not None:
        args += [p.reshape(w3.shape) for p in prev]
        in_specs += [ANY] * 4
        aliases = {6 + k: k for k in range(4)}
    outs = pl.pallas_call(
        body, name=name,
        grid_spec=pltpu.PrefetchScalarGridSpec(
            num_scalar_prefetch=1, grid=(a_n, 2, n_i, cols // tc), in_specs=in_specs, out_specs=[full] * 4),
        out_shape=[jax.ShapeDtypeStruct(w3.shape, F32)] * 4,
        input_output_aliases=aliases,
        compiler_params=_cparams(4),
    )(_scalars(core), *args)
    return [o.reshape(shape) for o in outs]


def _cast_into(src, buf, lead, slot, *, name, buf_shape=None, dtype=BF16):
    a_n, rows, cols = src.shape
    tr = _tile(rows, max(16, (1 << 21) // (4 * cols)), 16)

    def body(slot_ref, src_ref, *rest):
        rest[-1][...] = src_ref[...].astype(rest[-1].dtype)

    in_specs = [pl.BlockSpec((None, tr, cols), lambda a, i, slot_ref: (a, i, 0))]
    args = [src]
    aliases = {}
    if buf is not None:
        in_specs.append(ANY)
        args.append(buf)
        aliases = {2: 0}
        buf_shape, dtype = buf.shape, buf.dtype
    return pl.pallas_call(
        body, name=name,
        grid_spec=pltpu.PrefetchScalarGridSpec(
            num_scalar_prefetch=1, grid=(a_n, rows // tr), in_specs=in_specs,
            out_specs=pl.BlockSpec((None, None, tr, cols), lambda a, i, slot_ref: (lead + a, slot_ref[0], i, 0))),
        out_shape=jax.ShapeDtypeStruct(buf_shape, dtype),
        input_output_aliases=aliases,
        compiler_params=_cparams(2),
    )(_scalars(slot), *args)


def _place():
    x, y, c = lax.axis_index("x"), lax.axis_index("y"), lax.axis_index("c")
    chips = [(1 - x, y), (x, 1 - y), (1 - x, 1 - y)]
    return x, y, c, chips


def _remote(src, dst, send_sem, recv_sem, dev):
    return pltpu.make_async_remote_copy(src_ref=src, dst_ref=dst, send_sem=send_sem, recv_sem=recv_sem,
                                        device_id=dev, device_id_type=MESH)


def _gather_comm(bufs, pieces):
    def plan(refs, send_sems, recv_sems):
        x, y, c, chips = _place()

        def region(p, q, core):
            t, a0, a1, part, n_parts = pieces[p]
            rh, cw = bufs[t].shape[2] // 2, bufs[t].shape[3] // n_parts
            return refs[t].at[pl.ds(a0, a1 - a0), q, pl.ds(core * rh, rh), pl.ds(part * cw, cw)]

        def ici(p, j, q):
            cx, cy = chips[j]
            return _remote(region(p, q, c), region(p, q, c), send_sems.at[6 * p + j], recv_sems.at[6 * p + j], (cx, cy, c))

        def d2d(p, j, core):
            cx, cy = chips[j]
            rows = region(p, 2 * cx + cy, core)
            return _remote(rows, rows, send_sems.at[6 * p + 3 + j], recv_sems.at[6 * p + 3 + j], (x, y, 1 - c))

        return 2 * x + y, c, [2 * cx + cy for cx, cy in chips], ici, d2d

    todo = [(j, p) for j in range(3) for p in range(len(pieces))]

    def start(refs, send_sems, recv_sems):
        q_me, _, _, ici, _ = plan(refs, send_sems, recv_sems)
        for j, p in todo:
            ici(p, j, q_me).start()

    def finish(refs, send_sems, recv_sems):
        q_me, c, q_of, ici, d2d = plan(refs, send_sems, recv_sems)
        for j, p in todo:
            ici(p, j, q_of[j]).wait_recv()
            d2d(p, j, c).start()
        for j, p in todo:
            d2d(p, j, 1 - c).wait_recv()
        for j, p in todo:
            ici(p, j, q_me).wait_send()
            d2d(p, j, c).wait_send()

    return _Comm(bufs, 6 * len(pieces), start, finish)


def _chips_comm(parts, lands):
    n = len(parts)

    def copy(refs, send_sems, recv_sems, t, j, q_src, q_dst):
        x, y, c, chips = _place()
        cx, cy = chips[j]
        return _remote(refs[t].at[:, q_src], refs[n + t].at[q_dst], send_sems.at[3 * t + j], recv_sems.at[3 * t + j],
                       (cx, cy, c))

    todo = [(j, t) for j in range(3) for t in range(n)]

    def qs():
        x, y, _, chips = _place()
        return 2 * x + y, [2 * cx + cy for cx, cy in chips]

    def start(refs, send_sems, recv_sems):
        q_me, q_of = qs()
        for j, t in todo:
            copy(refs, send_sems, recv_sems, t, j, q_of[j], q_me).start()

    def finish(refs, send_sems, recv_sems):
        q_me, q_of = qs()
        for j, t in todo:
            copy(refs, send_sems, recv_sems, t, j, q_me, q_of[j]).wait_recv()
        for j, t in todo:
            copy(refs, send_sems, recv_sems, t, j, q_of[j], q_me).wait_send()

    return _Comm(list(parts) + list(lands), 3 * n, start, finish)


def _comm_call(comm, *, name):
    k = len(comm.operands)

    def body(*refs):
        operands, (send_sems, recv_sems) = refs[k:2 * k], refs[2 * k:]
        comm.start(operands, send_sems, recv_sems)
        comm.finish(operands, send_sems, recv_sems)

    return pl.pallas_call(
        body, name=name, in_specs=[ANY] * k, out_specs=[ANY] * k,
        out_shape=[jax.ShapeDtypeStruct(t.shape, t.dtype) for t in comm.operands],
        input_output_aliases={i: i for i in range(k)},
        scratch_shapes=[pltpu.SemaphoreType.DMA((comm.n_sems,)), pltpu.SemaphoreType.DMA((comm.n_sems,))],
    )(*comm.operands)


def _rs_sibling(grads, *, name):
    n = len(grads)
    halves = [jax.ShapeDtypeStruct(g.shape[:2] + (g.shape[2] // 2, g.shape[3]), g.dtype) for g in grads]

    def body(*refs):
        src, theirs = refs[:n], refs[n:2 * n]
        send_sems, recv_sems = refs[2 * n:]
        x, y, c, _ = _place()
        ops = []
        for t in range(n):
            rh = grads[t].shape[2] // 2
            give = _remote(src[t].at[:, :, pl.ds((1 - c) * rh, rh), :], theirs[t], send_sems.at[t], recv_sems.at[t],
                           (x, y, 1 - c))
            give.start()
            ops.append(give)
        for op in ops:
            op.wait()

    return pl.pallas_call(
        body, name=name, in_specs=[ANY] * n, out_specs=[ANY] * n, out_shape=halves,
        scratch_shapes=[pltpu.SemaphoreType.DMA((n,)), pltpu.SemaphoreType.DMA((n,))],
    )(*grads)


def _rs_add(grad, theirs, core, slot, *, name):
    a_n, _, rh, cols = theirs.shape
    tr, tc = _tile2(rh, cols, 1 << 18, 16)
    n_i = rh // tr

    def body(s_ref, g_ref, t_ref, p_ref, y_ref):
        part = (g_ref[...].astype(F32) + t_ref[...].astype(F32)).astype(BF16)
        p_ref[...] = part

        @pl.when(pl.program_id(3) == s_ref[1])
        def _():
            y_ref[...] = part

    blk = (None, None, tr, tc)
    return pl.pallas_call(
        body, name=name,
        grid_spec=pltpu.PrefetchScalarGridSpec(
            num_scalar_prefetch=1, grid=(a_n, n_i, cols // tc, N_CHIPS),
            in_specs=[pl.BlockSpec(blk, lambda a, i, j, q, s: (a, q, s[0] * n_i + i, j)),
                      pl.BlockSpec(blk, lambda a, i, j, q, s: (a, q, i, j))],
            out_specs=[pl.BlockSpec(blk, lambda a, i, j, q, s: (a, q, i, j)),
                       pl.BlockSpec(blk, lambda a, i, j, q, s: (s[1], a, i, j))]),
        out_shape=[jax.ShapeDtypeStruct(theirs.shape, BF16),
                   jax.ShapeDtypeStruct((N_CHIPS, a_n, rh, cols), BF16)],
        compiler_params=_cparams(4),
    )(_scalars(core, slot), grad, theirs)


def _rs_finish(halves, *, name):
    n = len(halves)

    def body(*refs):
        src, dst = refs[:n], refs[n:2 * n]
        send_sems, recv_sems = refs[2 * n:]
        x, y, c, _ = _place()
        ops = []
        for t in range(n):
            give = _remote(src[t], dst[t], send_sems.at[t], recv_sems.at[t], (x, y, 1 - c))
            give.start()
            ops.append(give)
        for op in ops:
            op.wait()

    return pl.pallas_call(
        body, name=name, in_specs=[ANY] * n, out_specs=[ANY] * n,
        out_shape=[jax.ShapeDtypeStruct(h.shape, h.dtype) for h in halves],
        scratch_shapes=[pltpu.SemaphoreType.DMA((n,)), pltpu.SemaphoreType.DMA((n,))],
    )(*halves)


def _all_reduce_small(vec, *, name):
    rows = vec.shape[0]

    def body(v_ref, o_ref, land, send_sems, recv_sems):
        x, y, c, _ = _place()
        me = 4 * x + 2 * y + c
        land[me] = v_ref[...]
        flips = [(fx, fy, fc) for fx in (0, 1) for fy in (0, 1) for fc in (0, 1)][1:]
        sent = []
        for k, (fx, fy, fc) in enumerate(flips):
            cp = _remote(v_ref, land.at[me], send_sems.at[k], recv_sems.at[k], (x ^ fx, y ^ fy, c ^ fc))
            cp.start()
            sent.append(cp)
        for k, (fx, fy, fc) in enumerate(flips):
            peer = 4 * (x ^ fx) + 2 * (y ^ fy) + (c ^ fc)
            _remote(v_ref, land.at[peer], send_sems.at[k], recv_sems.at[k], (x ^ fx, y ^ fy, c ^ fc)).wait_recv()
        for cp in sent:
            cp.wait_send()
        total = land[0]
        for dev in range(1, 8):
            total = total + land[dev]
        o_ref[...] = total

    whole = pl.BlockSpec(memory_space=pltpu.VMEM)
    return pl.pallas_call(
        body, name=name, in_specs=[whole], out_specs=whole,
        out_shape=jax.ShapeDtypeStruct(vec.shape, F32),
        scratch_shapes=[pltpu.VMEM((8, rows, LANES), F32), pltpu.SemaphoreType.DMA((7,)), pltpu.SemaphoreType.DMA((7,))],
        compiler_params=pltpu.CompilerParams(vmem_limit_bytes=VMEM_LIMIT),
    )(vec)


def _pack(parts, mult=16):
    flat = jnp.concatenate([p.reshape(-1).astype(F32) for p in parts])
    rows = -(-flat.shape[0] // (LANES * mult)) * mult
    return jnp.pad(flat, (0, rows * LANES - flat.shape[0])).reshape(rows, LANES)


def _unpack(vec, shapes):
    flat, out, pos = vec.reshape(-1), [], 0
    for shp in shapes:
        size = math.prod(shp)
        out.append(flat[pos:pos + size].reshape(shp))
        pos += size
    return out


def kernel(x, pool_w_in, pool_w_grp, pool_scale, pool_w_out, attn_w_q, attn_w_o, shared_w_k, shared_w_v, ffn_w_up, ffn_conv_w, ffn_conv_b, ffn_w_down, ln1_g, ln1_b, ln2_g, ln2_b, loss_target, m_pool_w_in, m_pool_w_grp, m_pool_scale, m_pool_w_out, m_attn_w_q, m_attn_w_o, m_shared_w_k, m_shared_w_v, m_ffn_w_up, m_ffn_conv_w, m_ffn_conv_b, m_ffn_w_down, m_ln1_g, m_ln1_b, m_ln2_g, m_ln2_b, v_pool_w_in, v_pool_w_grp, v_pool_scale, v_pool_w_out, v_attn_w_q, v_attn_w_o, v_shared_w_k, v_shared_w_v, v_ffn_w_up, v_ffn_conv_w, v_ffn_conv_b, v_ffn_w_down, v_ln1_g, v_ln1_b, v_ln2_g, v_ln2_b):
    s, d = x.shape[1], x.shape[2]
    n2 = ffn_w_up.shape[2]
    fq = ffn_w_down.shape[1]
    assert 2 * fq == n2 and d % N_CHIPS == 0
    g_n, cg = pool_w_grp.shape[1], pool_w_grp.shape[3]
    xs, tgt = x[0], loss_target[0]
    q_me = 2 * lax.axis_index("x") + lax.axis_index("y")
    core = lax.axis_index("c")
    rq = d // N_CHIPS

    six_g = None
    for i, w_ in enumerate((pool_w_in[0], pool_w_out[0], attn_w_q[0], shared_w_k, shared_w_v, attn_w_o[0])):
        six_g = _cast_into(w_[None], six_g, i, q_me, name=f"cast_w{i}", buf_shape=(6, N_CHIPS, rq, d))
    grp_g = _cast_into(pool_w_grp[0], None, 0, q_me, name="cast_grp", buf_shape=(g_n, N_CHIPS, cg // N_CHIPS, cg))
    up_t, m_up_t, v_up_t = (jnp.swapaxes(t, 1, 2) for t in (ffn_w_up, m_ffn_w_up, v_ffn_w_up))
    up_g = _cast_into(up_t, None, 0, q_me, name="cast_up", buf_shape=(2, N_CHIPS, n2, d))
    dn_g = _cast_into(ffn_w_down, None, 0, q_me, name="cast_down", buf_shape=(2, N_CHIPS, fq, d))
    small = _pack([ffn_conv_w, pool_scale])
    small_g = _cast_into(small[None], None, 0, q_me, name="cast_small", buf_shape=(1, N_CHIPS) + small.shape, dtype=F32)
    bufs = [six_g, grp_g, up_g, dn_g, small_g]
    SIX, GRP, UP, DN, SMALL = range(5)
    first = [SIX, GRP, SMALL]
    landed = _comm_call(_gather_comm([bufs[t] for t in first], [(0, IW_IN, IW_OUT + 1, 0, 1), (1, 0, g_n, 0, 1),
                                                                (2, 0, 1, 0, 1)]), name="gather_first")
    for t, arr in zip(first, landed):
        bufs[t] = arr
    small_q = bufs[SMALL].reshape(N_CHIPS, -1)
    n_cw = 2 * 3 * n2
    conv_w = small_q[:, :n_cw].reshape(N_CHIPS, 2, 3, n2).transpose(1, 0, 2, 3)
    scale_full = small_q[:, n_cw:n_cw + d // N_CHIPS].reshape(1, d)
    conv_b = ffn_conv_b.reshape(2, N_CHIPS, 1, n2)

    def w6():
        return bufs[SIX].reshape(6, d, d)

    def up8():
        return bufs[UP].reshape(2 * N_CHIPS, n2, d)

    def dn4():
        return bufs[DN].reshape(4, n2, d)

    def gathered(pieces):
        used = sorted({pc[0] for pc in pieces})
        local = [(used.index(t), a0, a1, part, n_parts) for t, a0, a1, part, n_parts in pieces]
        return _gather_comm([bufs[t] for t in used], local), used

    def store(used, operands):
        for t, arr in zip(used, operands):
            bufs[t] = arr

    def ffn_fwd(l, hb, carry_up, carry_down, carry_act=None):
        comm, used = gathered(carry_up)
        u, landed = _mm(hb[None], up8(), mode="nt", name=f"ffn{l}_up", out_dtype=BF16, n_q=4, tm=_tile(s, 512), tn=n2,
                        tk=d, qa=_q0, qb=lambda q, qr: 4 * l + _perm(q), comm=comm)
        store(used, landed)
        u = u.reshape(2, 2, s, n2)
        if carry_act:
            comm, used = gathered(carry_act)
            act, landed = _act_fwd(u, conv_w[l], conv_b[l], name=f"ffn{l}_act", comm=comm)
            store(used, landed)
        else:
            act = _act_fwd(u, conv_w[l], conv_b[l], name=f"ffn{l}_act")
        comm, used = gathered(carry_down) if carry_down else (None, None)
        ff = _mm(act, dn4(), mode="nn", name=f"ffn{l}_down", out_dtype=F32, n_q=1, n_qr=2, tm=_tile(s, 1024),
                 tn=_tile(d, 1024), tk=n2, qa=_qr, qb=lambda q, qr: 2 * l + qr, qo=lambda q: 0, comm=comm)
        if comm is not None:
            ff, landed = ff
            store(used, landed)
        return u, act, ff[0]

    def up0_parts(*parts):
        return gathered([(UP, 0, 1, part, 8) for part in parts])

    comm, used = up0_parts(0, 1)
    p, landed = _mm_dd(xs[None], w6(), IW_IN, mode="nn", name="pool_in", out_dtype=F32, comm=comm)
    store(used, landed)
    comm, used = up0_parts(2)
    pooled, landed = _pool(p[0], backward=False, name="pool_fwd", comm=comm)
    store(used, landed)
    comm, used = up0_parts(3)
    (mg, mixed), landed = _grp_fwd(pooled, bufs[GRP].reshape(g_n, cg, cg), scale_full, name="pool_grp", comm=comm)
    store(used, landed)
    comm, used = up0_parts(4, 5)
    mix0, landed = _mm_dd(mixed[None], w6(), IW_OUT, mode="nn", name="pool_out", out_dtype=F32, comm=comm)
    store(used, landed)
    comm, used = up0_parts(6, 7)
    (h1, h1b, xh1, rs1), landed = _ln_fwd(xs, mix0[0], ln1_g[0:1], ln1_b[0:1], name="ln1_0", comm=comm)
    store(used, landed)
    u0, act0, ff0 = ffn_fwd(0, h1b, [(DN, 0, 1, 0, 1)], [(SIX, IW_O, IW_O + 1, 0, 1)],
                            carry_act=[(SIX, IW_Q, IW_V + 1, 0, 1)])
    h2, h2b, xh2, rs2 = _ln_fwd(h1, ff0, ln2_g[0:1], ln2_b[0:1], name="ln2_0")

    qkv = _mm_dd(h2b[None], w6(), IW_Q, mode="nn", name="attn_qkv", out_dtype=BF16, n_q=3)
    comm, used = gathered([(UP, 1, 2, 0, 1)])
    (o, ob, lse), landed = _attn_fwd(qkv, name="attn_fwd", comm=comm)
    store(used, landed)
    mix1 = _mm_dd(ob[None], w6(), IW_O, mode="nn", name="attn_out", out_dtype=F32)[0]
    h3, h3b, xh3, rs3 = _ln_fwd(h2, mix1, ln1_g[1:2], ln1_b[1:2], name="ln1_1")
    u1, act1, ff1 = ffn_fwd(1, h3b, [(DN, 1, 2, 0, 1)], None)
    h4, _, xh4, rs4 = _ln_fwd(h3, ff1, ln2_g[1:2], ln2_b[1:2], name="ln2_1")
    dh4, loss_local = _loss_head(h4, tgt, name="loss_head")

    rs_parts, rs_lands = {}, {}

    def rs_prepare(items):
        theirs = _rs_sibling([g_ for _, g_ in items], name="rs_sibling_" + items[0][0])
        for (nm, g_), t_ in zip(items, theirs):
            rs_parts[nm], rs_lands[nm] = _rs_add(g_, t_, core, q_me, name=f"rs_add_{nm}")

    def rs_exchange(names):
        return _chips_comm([rs_parts[nm] for nm in names], [rs_lands[nm] for nm in names])

    def rs_landed(names, operands):
        for nm, land in zip(names, operands[len(names):]):
            rs_lands[nm] = land

    d_conv_w, d_conv_b = [None, None], [None, None]

    def ffn_bwd(l, dzb, u, act, hb, carry_dact):
        da = _mm(dzb[None], dn4(), mode="nt", name=f"ffn{l}_dact", out_dtype=F32, n_q=2, tm=_tile(s, 512), tn=n2,
                 tk=d, qa=_q0, qb=lambda q, qr: 2 * l + q,
                 comm=rs_exchange(carry_dact) if carry_dact else None)
        if carry_dact:
            da, landed = da
            rs_landed(carry_dact, landed)
        du, dwg, dwv, dbg, dbv = _act_bwd(u, da, conv_w[l], conv_b[l], name=f"ffn{l}_dconv")
        d_conv_w[l] = jnp.concatenate([dwg, dwv], axis=0)
        d_conv_b[l] = jnp.concatenate([dbg, dbv], axis=0)
        du4 = du.reshape(4, s, n2)
        g_dn = _mm(act, dzb[None], mode="tn", name=f"ffn{l}_gdown", out_dtype=BF16, n_q=2, tm=n2, tn=_tile(d, 512),
                   tk=_tile(s, 2048), qa=_qq, qb=_q0)
        rs_prepare([(f"dn{l}", g_dn.reshape(1, N_CHIPS, fq, d))])
        g_up, landed = _mm(du4, hb[None], mode="tn", name=f"ffn{l}_gup", out_dtype=BF16, n_q=4, tm=n2, tn=_tile(d, 512),
                           tk=_tile(s, 2048), qa=lambda q, qr: _perm(q), qb=_q0, comm=rs_exchange([f"dn{l}"]))
        rs_landed([f"dn{l}"], landed)
        rs_prepare([(f"up{l}", g_up.reshape(1, N_CHIPS, n2, d))])
        dh, landed = _mm(du4, up8(), mode="nn", name=f"ffn{l}_dh", out_dtype=F32, n_q=1, n_qr=4, tm=_tile(s, 1024),
                         tn=_tile(d, 1024), tk=n2, qa=_qr, qb=lambda q, qr: 4 * l + _perm(qr), qo=lambda q: 0,
                         comm=rs_exchange([f"up{l}"]))
        rs_landed([f"up{l}"], landed)
        return dh[0]

    dz4, dz4b, dg_ln2_1, db_ln2_1 = _ln_bwd(None, dh4, xh4, rs4, ln2_g[1:2], name="dln2_1")
    dh3 = ffn_bwd(1, dz4b, u1, act1, h3b, None)
    dz3, dz3b, dg_ln1_1, db_ln1_1 = _ln_bwd(dz4, dh3, xh3, rs3, ln1_g[1:2], name="dln1_1")
    g_wo = _mm_wgrad(ob[None], dz3b[None], name="attn_gwo")
    do = _mm_dd(dz3b[None], w6(), IW_O, mode="nt", name="attn_do", out_dtype=BF16)[0]
    delta = _attn_delta(do, o, name="attn_delta")
    dqkv = _attn_bwd(qkv, do, lse, delta, name="attn_bwd")
    g_wq = _mm_wgrad(h2b[None], dqkv, name="attn_gwq", qb=lambda q, qr: 0)
    g_wk = _mm_wgrad(h2b[None], dqkv, name="attn_gwk", qb=lambda q, qr: 1)
    g_wv = _mm_wgrad(h2b[None], dqkv, name="attn_gwv", qb=lambda q, qr: 2)
    rs_prepare([(nm, g_.reshape(1, N_CHIPS, rq, d)) for nm, g_ in
                (("wo", g_wo), ("wq", g_wq), ("wk", g_wk), ("wv", g_wv))])
    dh2, landed = _mm_dd(dqkv, w6(), IW_Q, mode="nt", name="attn_dh", out_dtype=F32, n_qr=3, qa=_qr,
                         comm=rs_exchange(["wo", "wq", "wk"]))
    rs_landed(["wo", "wq", "wk"], landed)
    dz2, dz2b, dg_ln2_0, db_ln2_0 = _ln_bwd(dz3, dh2[0], xh2, rs2, ln2_g[0:1], name="dln2_0")
    dh1 = ffn_bwd(0, dz2b, u0, act0, h1b, ["wv"])
    dz1, dz1b, dg_ln1_0, db_ln1_0 = _ln_bwd(dz2, dh1, xh1, rs1, ln1_g[0:1], name="dln1_0")
    g_wout = _mm_wgrad(mixed[None], dz1b[None], name="pool_gwout")
    rs_prepare([("wout", g_wout.reshape(1, N_CHIPS, rq, d))])
    dmixed, landed = _mm_dd(dz1b[None], w6(), IW_OUT, mode="nt", name="pool_dmixed", out_dtype=F32,
                            comm=rs_exchange(["wout"]))
    rs_landed(["wout"], landed)
    dmg, d_scale = _grp_bwd_pre(dmixed[0], mg, scale_full, name="pool_dscale")
    g_wgrp = _grp_mm(pooled, dmg, mode="tn", name="pool_gwgrp", out_dtype=BF16, tm=1024)
    dpooled = _grp_mm(dmg, None, mode="nt", name="pool_dpooled", out_dtype=F32, tm=1024, w=bufs[GRP].reshape(g_n, cg, cg))
    dp = _pool(dpooled, backward=True, name="pool_bwd")
    g_win = _mm_wgrad(xs[None], dp[None], name="pool_gwin")
    last = ["win", "wgrp"]
    rs_prepare([("win", g_win.reshape(1, N_CHIPS, rq, d)), ("wgrp", g_wgrp.reshape(g_n, N_CHIPS, cg // N_CHIPS, cg))])
    dx_mm, landed = _mm_dd(dp[None], w6(), IW_IN, mode="nt", name="pool_dx", out_dtype=F32, comm=rs_exchange(last))
    rs_landed(last, landed)
    (grad_x,) = _ew(lambda a, b: (ALPHA * a + b,), [dz1, dx_mm[0]], [F32], name="grad_x")

    rs_names = ["win", "wgrp", "wout", "wq", "wo", "wk", "wv", "up0", "up1", "dn0", "dn1"]
    finished = {}
    for nm in rs_names:
        y4 = rs_lands[nm]
        y3 = y4.reshape(N_CHIPS, -1, y4.shape[-1])
        (tot,) = _ew(lambda a0, a1, a2, a3: (((a0.astype(F32) + a1.astype(F32)) + a2.astype(F32)) + a3.astype(F32),),
                     [(y3, 0), (y3, 1), (y3, 2), (y3, 3)], [F32], name=f"rs_sum_{nm}")
        finished[nm] = tot.reshape(y4.shape[1:])
    others = dict(zip(rs_names, _rs_finish([finished[nm] for nm in rs_names], name="rs_finish")))
    results = {}
    for nm, key, w, m, v in (("pool_w_in", "win", pool_w_in, m_pool_w_in, v_pool_w_in),
                             ("pool_w_grp", "wgrp", pool_w_grp, m_pool_w_grp, v_pool_w_grp),
                             ("pool_w_out", "wout", pool_w_out, m_pool_w_out, v_pool_w_out),
                             ("attn_w_q", "wq", attn_w_q, m_attn_w_q, v_attn_w_q),
                             ("attn_w_o", "wo", attn_w_o, m_attn_w_o, v_attn_w_o),
                             ("shared_w_k", "wk", shared_w_k, m_shared_w_k, v_shared_w_k),
                             ("shared_w_v", "wv", shared_w_v, m_shared_w_v, v_shared_w_v)):
        results[nm] = _adamw_halves(finished[key], others[key], core, w, m, v, name=f"adamw_{nm}")
    for nm, key, w, m, v in (("ffn_w_up", "up", up_t, m_up_t, v_up_t),
                             ("ffn_w_down", "dn", ffn_w_down, m_ffn_w_down, v_ffn_w_down)):
        res = None
        for l in (1, 0):
            res = _adamw_halves(finished[f"{key}{l}"], others[f"{key}{l}"], core, w, m, v, name=f"adamw_{nm}{l}",
                                lead=l, prev=res)
        results[nm] = res
    results["ffn_w_up"] = [jnp.swapaxes(t, 1, 2) for t in results["ffn_w_up"]]

    ln_grads = [jnp.concatenate([a, b], axis=0) for a, b in
                ((dg_ln1_0, dg_ln1_1), (db_ln1_0, db_ln1_1), (dg_ln2_0, dg_ln2_1), (db_ln2_0, db_ln2_1))]
    small_shapes = [(2, N_CHIPS, 3, n2), (2, N_CHIPS, n2)] + [(2, d)] * 4 + [(1, d)]
    vec = _pack([jnp.stack(d_conv_w), jnp.stack(d_conv_b)] + ln_grads + [d_scale], mult=8)
    tot = _unpack(_all_reduce_small(vec, name="allreduce_small"), small_shapes)
    g_cw = lax.dynamic_index_in_dim(tot[0], q_me, axis=1, keepdims=False)
    g_cb = tot[1].reshape(2, N_CHIPS * n2)
    g_scale = lax.dynamic_slice_in_dim(tot[6], q_me * rq, rq, axis=1)
    small_names = ["ffn_conv_w", "ffn_conv_b", "ln1_g", "ln1_b", "ln2_g", "ln2_b", "pool_scale"]
    small_g = [g_cw, g_cb, tot[2], tot[3], tot[4], tot[5], g_scale]
    small_w = [ffn_conv_w, ffn_conv_b, ln1_g, ln1_b, ln2_g, ln2_b, pool_scale]
    small_m = [m_ffn_conv_w, m_ffn_conv_b, m_ln1_g, m_ln1_b, m_ln2_g, m_ln2_b, m_pool_scale]
    small_v = [v_ffn_conv_w, v_ffn_conv_b, v_ln1_g, v_ln1_b, v_ln2_g, v_ln2_b, v_pool_scale]
    packed = _ew(_adamw_math, [_pack(small_g, 8), _pack(small_w, 8), _pack(small_m, 8), _pack(small_v, 8)], [F32] * 4,
                 name="adamw_small")
    shapes = [w.shape for w in small_w]
    unpacked = [_unpack(pk, shapes) for pk in packed]
    for i, nm in enumerate(small_names):
        results[nm] = [unpacked[k][i] for k in range(4)]

    loss = lax.psum(loss_local[0, 0], ("x", "y", "c"))
    order = ["pool_w_in", "pool_w_grp", "pool_scale", "pool_w_out", "attn_w_q", "attn_w_o", "shared_w_k", "shared_w_v",
             "ffn_w_up", "ffn_conv_w", "ffn_conv_b", "ffn_w_down", "ln1_g", "ln1_b", "ln2_g", "ln2_b"]
    outs = [loss, grad_x[None]]
    for k in range(4):
        outs += [results[nm][k] for nm in order]
    return tuple(outs)
```
